```python
import jax, jax.numpy as jnp
from jax import lax
import numpy as np


D_MODEL = 1024
BATCH = 16
SEQ = 2048
DEPTH = 2

CHUNK = 64
D_MIX = D_MODEL
A_DIM = D_MIX // 4
A_HEADS = 4
A_HEAD_DIM = A_DIM // A_HEADS
GMLP_BLOCK = 128
B_HEADS = 4
B_DV = (D_MIX // 2) // B_HEADS
B_DK = B_DV // 2
B_KDIM = B_HEADS * B_DK
B_VDIM = B_HEADS * B_DV
GLA_RANK = 16
GLA_TAU = 16.0
C_DIM = D_MIX - A_DIM - B_VDIM
CONV_WIDTH = 3
P_SIZES = (2 * A_DIM, B_KDIM, B_KDIM, B_VDIM, GLA_RANK, B_VDIM, C_DIM, C_DIM, C_DIM)
P_IN = sum(P_SIZES)
P_SPLITS = tuple(int(s) for s in np.cumsum(P_SIZES)[:-1])
N_EXPERTS = 32
N_GROUPS = 4
EXPERTS_PER_GROUP = N_EXPERTS // N_GROUPS
TOP_K = 2
D_FF_EXPERT = 512
EXPERT_BLOCK = 128
EPS = 1e-6

kernel_name = 'hybrid_gmlp_gla_shortconv_groupmoe_adaln'


def rmsnorm(x, g):
    xf = x.astype(jnp.float32)
    y = xf * lax.rsqrt(jnp.mean(xf * xf, axis=-1, keepdims=True) + EPS)
    return (y * g.astype(jnp.float32)).astype(x.dtype)


def spatial_gating(u, v, ln_g, ln_b, w_s, b_s):
    Bn, T, _ = v.shape
    vf = v.astype(jnp.float32)
    mu = jnp.mean(vf, axis=-1, keepdims=True)
    var = jnp.mean((vf - mu) ** 2, axis=-1, keepdims=True)
    vn = ((vf - mu) * lax.rsqrt(var + EPS) * ln_g.astype(jnp.float32) + ln_b.astype(jnp.float32)).astype(v.dtype)
    vn = vn.reshape(Bn, T // GMLP_BLOCK, GMLP_BLOCK, A_HEADS, A_HEAD_DIM)
    pos_chunk = jnp.arange(GMLP_BLOCK) // CHUNK
    mask = pos_chunk[:, None] >= pos_chunk[None, :]
    w = jnp.where(mask, w_s, 0).astype(vn.dtype)
    sv = jnp.einsum('hts,bnshd->bnthd', w, vn) + b_s.T[:, :, None].astype(vn.dtype)
    return u * sv.reshape(Bn, T, A_DIM)


def gla_chunked(q, k, v, glog):
    Bn, T, H, DK = q.shape
    DV = v.shape[-1]
    nC = T // CHUNK
    qf = q.astype(jnp.float32).reshape(Bn, nC, CHUNK, H, DK)
    kf = k.astype(jnp.float32).reshape(Bn, nC, CHUNK, H, DK)
    vf = v.astype(jnp.float32).reshape(Bn, nC, CHUNK, H, DV)
    b = jnp.cumsum(glog.astype(jnp.float32).reshape(Bn, nC, CHUNK, H, DK), axis=2)
    b_ref = b[:, :, CHUNK // 2:CHUNK // 2 + 1]
    b_last = b[:, :, CHUNK - 1:CHUNK]
    scores = jnp.einsum('bnihk,bnjhk->bnhij', qf * jnp.exp(b - b_ref), kf * jnp.exp(b_ref - b))
    causal = jnp.tril(jnp.ones((CHUNK, CHUNK), dtype=bool))
    scores = jnp.where(causal, scores, 0.0)
    o_intra = jnp.einsum('bnhij,bnjhv->bnihv', scores, vf)
    kv = jnp.einsum('bnjhk,bnjhv->bnhkv', kf * jnp.exp(b_last - b), vf)
    decay = jnp.exp(b_last[:, :, 0])

    def step(S, inp):
        dec, kv_c = inp
        return dec[..., None] * S + kv_c, S

    S0 = jnp.zeros((Bn, H, DK, DV), jnp.float32)
    _, S_prev = lax.scan(step, S0, (decay.transpose(1, 0, 2, 3), kv.transpose(1, 0, 2, 3, 4)))
    S_prev = S_prev.transpose(1, 0, 2, 3, 4)
    o_inter = jnp.einsum('bnihk,bnhkv->bnihv', qf * jnp.exp(b), S_prev)
    return (o_intra + o_inter).reshape(Bn, T, H, DV).astype(q.dtype)


def short_conv_mixer(gate_b, gate_c, xin, conv_w, conv_b):
    T = xin.shape[1]
    z = gate_c * xin
    zp = jnp.pad(z, ((0, 0), (CONV_WIDTH - 1, 0), (0, 0)))
    y = conv_b
    for j in range(CONV_WIDTH):
        y = y + conv_w[j] * zp[:, j:j + T]
    return gate_b * y


def hybrid_mixer(h, w_in, w_out, ln_g, ln_b, w_s, b_s, w_gate, b_gate, gla_g, conv_w, conv_b):
    Bn, T, _ = h.shape
    p = h @ w_in
    a_uv, q, k, v, g_lr, og, cb, cc, cx = jnp.split(p, P_SPLITS, axis=-1)
    u, vv = jnp.split(jax.nn.gelu(a_uv, approximate=False), 2, axis=-1)
    y_a = spatial_gating(u, vv, ln_g, ln_b, w_s, b_s)
    glog = jax.nn.log_sigmoid((g_lr @ w_gate + b_gate).astype(jnp.float32)) / GLA_TAU
    o = gla_chunked(q.reshape(Bn, T, B_HEADS, B_DK) * (B_DK ** -0.5),
                    k.reshape(Bn, T, B_HEADS, B_DK),
                    v.reshape(Bn, T, B_HEADS, B_DV),
                    glog.reshape(Bn, T, B_HEADS, B_DK))
    of = o.astype(jnp.float32)
    on = of * lax.rsqrt(jnp.mean(of * of, axis=-1, keepdims=True) + EPS)
    y_b = (on.reshape(Bn, T, B_VDIM) * gla_g.astype(jnp.float32)).astype(h.dtype) * jax.nn.silu(og)
    y_c = short_conv_mixer(cb, cc, cx, conv_w, conv_b)
    return jnp.concatenate([y_a, y_b, y_c], axis=-1) @ w_out


def moe(h, router_w, router_b, w1, w3, w2):
    Bn, T, D = h.shape
    N = Bn * T
    xt = h.reshape(N, D)
    logits = xt.astype(jnp.float32) @ router_w.astype(jnp.float32) + router_b.astype(jnp.float32)
    probs = jax.nn.softmax(logits, axis=-1)
    pg = probs.reshape(N, N_GROUPS, EXPERTS_PER_GROUP)
    gscore = jnp.sum(lax.top_k(pg, 2)[0], axis=-1)
    gidx = lax.top_k(gscore, 1)[1]
    in_group = jnp.take_along_axis(pg, gidx[:, :, None], axis=1)[:, 0]
    top_p, local = lax.top_k(in_group, TOP_K)
    experts = gidx * EXPERTS_PER_GROUP + local
    weights = top_p / jnp.sum(top_p, axis=-1, keepdims=True)
    A = N * TOP_K
    flat_e = experts.reshape(A)
    order = jnp.argsort(flat_e)
    sorted_e = flat_e[order]
    counts = jnp.bincount(flat_e, length=N_EXPERTS)
    padded = ((counts + EXPERT_BLOCK - 1) // EXPERT_BLOCK) * EXPERT_BLOCK
    pad_end = jnp.cumsum(padded)
    pad_start = pad_end - padded
    start = jnp.cumsum(counts) - counts
    dest = pad_start[sorted_e] + (jnp.arange(A) - start[sorted_e])
    n_blocks = -(-A // EXPERT_BLOCK) + N_EXPERTS
    xs = jnp.zeros((n_blocks * EXPERT_BLOCK, D), h.dtype).at[dest].set(xt[order // TOP_K])
    block_e = jnp.minimum(jnp.searchsorted(pad_end, jnp.arange(n_blocks) * EXPERT_BLOCK, side='right'),
                          N_EXPERTS - 1)

    def expert_block(args):
        xb, e = args
        return (jax.nn.silu(xb @ w1[e]) * (xb @ w3[e])) @ w2[e]

    ys = lax.map(expert_block, (xs.reshape(n_blocks, EXPERT_BLOCK, D), block_e)).reshape(-1, D)
    y_assign = jnp.zeros((A, D), h.dtype).at[order].set(ys[dest]).reshape(N, TOP_K, D)
    out = jnp.einsum('nk,nkd->nd', weights.astype(h.dtype), y_assign)
    return out.reshape(Bn, T, D)


def setup_inputs(seed: int = 0) -> dict:
    key = jax.random.key(seed)
    ks = jax.random.split(key, 24)

    def nrm(k, shape, scale):
        return jax.random.normal(k, shape, jnp.float32) * scale

    return {
        'x': nrm(ks[0], (BATCH, SEQ, D_MODEL), 1.0),
        'c': nrm(ks[1], (BATCH, D_MODEL), 1.0),
        'w_ada': nrm(ks[2], (DEPTH, D_MODEL, 6 * D_MODEL), 0.5 * D_MODEL ** -0.5),
        'b_ada': nrm(ks[3], (DEPTH, 6 * D_MODEL), 0.02),
        'g_norm1': 1.0 + nrm(ks[4], (DEPTH, D_MODEL), 0.02),
        'g_norm2': 1.0 + nrm(ks[5], (DEPTH, D_MODEL), 0.02),
        'w_in': nrm(ks[6], (DEPTH, D_MODEL, P_IN), D_MODEL ** -0.5),
        'w_out': nrm(ks[7], (DEPTH, D_MIX, D_MODEL), D_MIX ** -0.5),
        'gmlp_ln_g': 1.0 + nrm(ks[8], (DEPTH, A_DIM), 0.02),
        'gmlp_ln_b': nrm(ks[9], (DEPTH, A_DIM), 0.02),
        'gmlp_ws': nrm(ks[10], (DEPTH, A_HEADS, GMLP_BLOCK, GMLP_BLOCK), GMLP_BLOCK ** -0.5),
        'gmlp_bs': 1.0 + nrm(ks[11], (DEPTH, A_HEADS, GMLP_BLOCK), 0.02),
        'gla_w_gate': nrm(ks[12], (DEPTH, GLA_RANK, B_KDIM), GLA_RANK ** -0.5),
        'gla_b_gate': nrm(ks[13], (DEPTH, B_KDIM), 0.1),
        'gla_norm_g': 1.0 + nrm(ks[14], (DEPTH, B_VDIM), 0.02),
        'conv_w': nrm(ks[15], (DEPTH, CONV_WIDTH, C_DIM), CONV_WIDTH ** -0.5),
        'conv_b': nrm(ks[16], (DEPTH, C_DIM), 0.02),
        'router_w': nrm(ks[17], (D_MODEL, N_EXPERTS), D_MODEL ** -0.5),
        'router_b': nrm(ks[18], (N_EXPERTS,), 0.01),
        'exp_w1': nrm(ks[19], (DEPTH, N_EXPERTS, D_MODEL, D_FF_EXPERT), D_MODEL ** -0.5),
        'exp_w3': nrm(ks[20], (DEPTH, N_EXPERTS, D_MODEL, D_FF_EXPERT), D_MODEL ** -0.5),
        'exp_w2': nrm(ks[21], (DEPTH, N_EXPERTS, D_FF_EXPERT, D_MODEL), D_FF_EXPERT ** -0.5),
        'g_final': 1.0 + nrm(ks[22], (D_MODEL,), 0.02),
    }


def reference(x, c, w_ada, b_ada, g_norm1, g_norm2, w_in, w_out, gmlp_ln_g, gmlp_ln_b,
              gmlp_ws, gmlp_bs, gla_w_gate, gla_b_gate, gla_norm_g, conv_w, conv_b,
              router_w, router_b, exp_w1, exp_w3, exp_w2, g_final):
    for l in range(DEPTH):
        mod = jax.nn.silu(c) @ w_ada[l] + b_ada[l]
        sh1, sc1, gt1, sh2, sc2, gt2 = jnp.split(mod[:, None, :], 6, axis=-1)
        h = rmsnorm(x, g_norm1[l]) * (1.0 + sc1) + sh1
        x = x + gt1 * hybrid_mixer(h, w_in[l], w_out[l], gmlp_ln_g[l], gmlp_ln_b[l], gmlp_ws[l],
                                   gmlp_bs[l], gla_w_gate[l], gla_b_gate[l], gla_norm_g[l],
                                   conv_w[l], conv_b[l])
        h = rmsnorm(x, g_norm2[l]) * (1.0 + sc2) + sh2
        x = x + gt2 * moe(h, router_w, router_b, exp_w1[l], exp_w3[l], exp_w2[l])
    return rmsnorm(x, g_final)
```

```python
import functools

import jax
import jax.numpy as jnp
from jax import lax
from jax.experimental import pallas as pl
from jax.experimental.pallas import tpu as pltpu

CHUNK = 64
GMLP_BLOCK = 128
A_DIM = 256
A_HEADS = 4
B_HEADS = 4
B_DK = 64
B_DV = 128
B_KDIM = B_HEADS * B_DK
B_VDIM = B_HEADS * B_DV
GLA_RANK = 16
GLA_TAU = 16.0
C_DIM = 256
N_EXPERTS = 32
N_GROUPS = 4
EXPERTS_PER_GROUP = N_EXPERTS // N_GROUPS
TOP_K = 2
EPS = 1e-6

LANES = 128
OFF_AU, OFF_AV, OFF_Q, OFF_K, OFF_V = 0, 256, 512, 768, 1024
OFF_OG, OFF_CB, OFF_CC, OFF_CX, OFF_GLR = 1536, 2048, 2304, 2560, 2816
P_PAD = OFF_GLR + LANES

MIX_ROWS = 512
ROUTE_ROWS = 512
MOVE_ROWS = 512
FFN_ROWS = 256
VMEM_LIMIT = 56 * 1024 * 1024

_NT = (((1,), (1,)), ((), ()))
_TN = (((0,), (0,)), ((), ()))


def _dot(a, b, dims=None):
    if dims is None:
        return jnp.dot(a, b, preferred_element_type=jnp.float32)
    return lax.dot_general(a, b, dims, preferred_element_type=jnp.float32)


def _bf(x):
    return x.astype(jnp.bfloat16)


def _split_bf16(x):
    hi = _bf(x)
    lo = _bf(x - hi.astype(jnp.float32))
    return hi, lo


def _rms(x):
    return x * lax.rsqrt(jnp.mean(x * x, axis=-1, keepdims=True) + EPS)


def _mod_kernel(c_ref, w_ref, b_ref, o_ref):
    c = c_ref[...]
    s = c * jax.nn.sigmoid(c)
    s_hi, s_lo = _split_bf16(s)
    w = w_ref[...]
    w_hi, w_lo = _split_bf16(w)
    acc = _dot(s_hi, w_hi) + _dot(s_hi, w_lo) + _dot(s_lo, w_hi)
    o_ref[...] = acc + b_ref[...]


def _modulation(c, w_ada, b_ada):
    depth, d, six_d = w_ada.shape
    bn = c.shape[0]
    cb = 1024
    return pl.pallas_call(
        _mod_kernel,
        grid=(depth, six_d // cb),
        in_specs=[
            pl.BlockSpec((bn, d), lambda l, j: (0, 0)),
            pl.BlockSpec((None, d, cb), lambda l, j: (l, 0, j)),
            pl.BlockSpec((None, 1, cb), lambda l, j: (l, 0, j)),
        ],
        out_specs=pl.BlockSpec((None, bn, cb), lambda l, j: (l, 0, j)),
        out_shape=jax.ShapeDtypeStruct((depth, bn, six_d), jnp.float32),
        compiler_params=pltpu.CompilerParams(
            dimension_semantics=("arbitrary", "arbitrary"), vmem_limit_bytes=VMEM_LIMIT),
        name="adaln_mod",
    )(c, w_ada, b_ada.reshape(depth, 1, six_d))


def _mixer_kernel(x_ref, mod_ref, g1_ref, win_ref, wout_ref, lng_ref, lnb_ref, ws_ref, bsm_ref,
                  wg_ref, bg_ref, glag_ref, cw_ref, cb_ref, o_ref,
                  h_ref, p_ref, y_ref, s_ref, zc_ref, wsm_ref):
    tb = pl.program_id(1)
    rows_total = x_ref.shape[0]

    @pl.when(tb == 0)
    def _():
        s_ref[...] = jnp.zeros_like(s_ref)
        zc_ref[...] = jnp.zeros_like(zc_ref)
        tt = lax.broadcasted_iota(jnp.int32, (GMLP_BLOCK, GMLP_BLOCK), 0) // CHUNK
        ss = lax.broadcasted_iota(jnp.int32, (GMLP_BLOCK, GMLP_BLOCK), 1) // CHUNK
        for h in range(A_HEADS):
            wsm_ref[h] = _bf(jnp.where(tt >= ss, ws_ref[h], 0.0))

    x = x_ref[...]
    m = mod_ref[...]
    sh1, sc1, gt1 = m[0:1], m[1:2], m[2:3]
    h = (_rms(x) * g1_ref[...]) * (1.0 + sc1) + sh1
    h_ref[...] = _bf(h)
    p_ref[...] = _dot(h_ref[...], win_ref[...])

    lane256 = lax.broadcasted_iota(jnp.int32, (CHUNK, B_KDIM), 1)
    head_of_lane = lane256 // B_DK
    r64 = lax.broadcasted_iota(jnp.int32, (CHUNK, CHUNK), 0)
    c64 = lax.broadcasted_iota(jnp.int32, (CHUNK, CHUNK), 1)
    tri = _bf(jnp.where(r64 >= c64, 1.0, 0.0))
    causal4 = jnp.concatenate([r64 >= c64] * B_HEADS, axis=0)
    eye256 = (lax.broadcasted_iota(jnp.int32, (B_KDIM, B_KDIM), 0)
              == lax.broadcasted_iota(jnp.int32, (B_KDIM, B_KDIM), 1))
    a_head_of_lane = lax.broadcasted_iota(jnp.int32, (GMLP_BLOCK, A_DIM), 1) // (A_DIM // A_HEADS)
    row128 = lax.broadcasted_iota(jnp.int32, (GMLP_BLOCK, C_DIM), 0)
    sqrt_half = 0.7071067811865476

    def gelu(v):
        return 0.5 * v * (1.0 + lax.erf(v * sqrt_half))

    def block(j, carry):
        r0 = pl.multiple_of(j * GMLP_BLOCK, GMLP_BLOCK)
        rows = pl.ds(r0, GMLP_BLOCK)

        u = gelu(p_ref[rows, OFF_AU:OFF_AU + A_DIM])
        vv = gelu(p_ref[rows, OFF_AV:OFF_AV + A_DIM])
        mu = jnp.mean(vv, axis=-1, keepdims=True)
        var = jnp.mean((vv - mu) ** 2, axis=-1, keepdims=True)
        vn = _bf((vv - mu) * lax.rsqrt(var + EPS) * lng_ref[...] + lnb_ref[...])
        sv = jnp.zeros((GMLP_BLOCK, A_DIM), jnp.float32)
        for hh in range(A_HEADS):
            sv = jnp.where(a_head_of_lane == hh, _dot(wsm_ref[hh], vn), sv)
        y_ref[rows, 0:A_DIM] = _bf(u * (sv + bsm_ref[...]))

        z = _dot(_bf(p_ref[rows, OFF_GLR:OFF_GLR + LANES]), wg_ref[...]) + bg_ref[...]
        glog = (jnp.minimum(z, 0.0) - jnp.log1p(jnp.exp(-jnp.abs(z)))) / GLA_TAU
        for c in range(GMLP_BLOCK // CHUNK):
            rc = pl.ds(pl.multiple_of(r0 + c * CHUNK, CHUNK), CHUNK)
            g = glog[c * CHUNK:(c + 1) * CHUNK]
            g_hi = _bf(g)
            g_r1 = g - g_hi.astype(jnp.float32)
            g_mid = _bf(g_r1)
            g_lo = _bf(g_r1 - g_mid.astype(jnp.float32))
            b = _dot(tri, g_hi) + _dot(tri, g_mid) + _dot(tri, g_lo)
            b_mid = b[CHUNK // 2:CHUNK // 2 + 1]
            b_last = b[CHUNK - 1:CHUNK]
            q = p_ref[rc, OFF_Q:OFF_Q + B_KDIM] * (B_DK ** -0.5)
            k = p_ref[rc, OFF_K:OFF_K + B_KDIM]
            vb = _bf(p_ref[rc, OFF_V:OFF_V + B_VDIM])
            qs = q * jnp.exp(b - b_mid)
            ks = _bf(k * jnp.exp(b_mid - b))
            kd = _bf(k * jnp.exp(b_last - b))
            qb = q * jnp.exp(b)
            qs_st = _bf(jnp.concatenate(
                [jnp.where(head_of_lane == hh, qs, 0.0) for hh in range(B_HEADS)], axis=0))
            qb_st = _bf(jnp.concatenate(
                [jnp.where(head_of_lane == hh, qb, 0.0) for hh in range(B_HEADS)], axis=0))
            scores = _bf(jnp.where(causal4, _dot(qs_st, ks, _NT), 0.0))
            state = s_ref[...]
            o_inter = _dot(qb_st, _bf(state))
            kv_all = _dot(kd, vb, _TN)
            outs, kvs = [], []
            for hh in range(B_HEADS):
                rs = slice(hh * CHUNK, (hh + 1) * CHUNK)
                cs = slice(hh * B_DV, (hh + 1) * B_DV)
                o_h = _dot(scores[rs], vb[:, cs]) + o_inter[rs]
                outs.append(_rms(o_h))
                kvs.append(kv_all[rs, cs])
            decay_row = jnp.broadcast_to(jnp.exp(b_last), (B_KDIM, B_KDIM))
            decay_col = jnp.sum(jnp.where(eye256, decay_row, 0.0), axis=1, keepdims=True)
            s_ref[...] = decay_col * state + jnp.concatenate(kvs, axis=0)
            on = jnp.concatenate(outs, axis=1) * glag_ref[...]
            og = p_ref[rc, OFF_OG:OFF_OG + B_VDIM]
            y_ref[rc, A_DIM:A_DIM + B_VDIM] = _bf(on * (og * jax.nn.sigmoid(og)))

        zz = p_ref[rows, OFF_CC:OFF_CC + C_DIM] * p_ref[rows, OFF_CX:OFF_CX + C_DIM]
        prev = zc_ref[...]
        z1 = jnp.where(row128 == 0, prev[7:8], pltpu.roll(zz, 1, 0))
        z2 = jnp.where(row128 == 0, prev[6:7], jnp.where(row128 == 1, prev[7:8], pltpu.roll(zz, 2, 0)))
        cw = cw_ref[...]
        yc = cb_ref[...] + cw[0:1] * z2
        yc = yc + cw[1:2] * z1
        yc = yc + cw[2:3] * zz
        y_ref[rows, A_DIM + B_VDIM:A_DIM + B_VDIM + C_DIM] = _bf(p_ref[rows, OFF_CB:OFF_CB + C_DIM] * yc)
        zc_ref[...] = zz[GMLP_BLOCK - 8:GMLP_BLOCK]
        return carry

    lax.fori_loop(0, rows_total // GMLP_BLOCK, block, 0)
    o_ref[...] = x + gt1 * _dot(y_ref[...], wout_ref[...])


def _mixer(x, mod, g1, w_in, w_out, ln_g, ln_b, w_s, b_s, w_gate, b_gate, gla_g, conv_w, conv_b):
    bn, t, d = x.shape
    tbk = min(MIX_ROWS, t)
    glr0 = 2 * A_DIM + 2 * B_KDIM + B_VDIM
    w_in_p = _bf(jnp.concatenate(
        [w_in[:, :glr0], w_in[:, glr0 + GLA_RANK:], w_in[:, glr0:glr0 + GLA_RANK],
         jnp.zeros((d, LANES - GLA_RANK), w_in.dtype)], axis=1))
    w_gate_p = _bf(jnp.concatenate(
        [w_gate, jnp.zeros((LANES - GLA_RANK, B_KDIM), w_gate.dtype)], axis=0))
    bsm = jnp.repeat(b_s.T, A_DIM // A_HEADS, axis=1)

    def whole(shape):
        return pl.BlockSpec(shape, lambda b, i: (0,) * len(shape))

    return pl.pallas_call(
        _mixer_kernel,
        grid=(bn, t // tbk),
        in_specs=[
            pl.BlockSpec((None, tbk, d), lambda b, i: (b, i, 0)),
            pl.BlockSpec((None, 6, d), lambda b, i: (b, 0, 0)),
            whole((1, d)),
            whole((d, P_PAD)),
            whole((d, d)),
            whole((1, A_DIM)),
            whole((1, A_DIM)),
            whole((A_HEADS, GMLP_BLOCK, GMLP_BLOCK)),
            whole((GMLP_BLOCK, A_DIM)),
            whole((LANES, B_KDIM)),
            whole((1, B_KDIM)),
            whole((1, B_VDIM)),
            whole((3, C_DIM)),
            whole((1, C_DIM)),
        ],
        out_specs=pl.BlockSpec((None, tbk, d), lambda b, i: (b, i, 0)),
        out_shape=jax.ShapeDtypeStruct(x.shape, x.dtype),
        scratch_shapes=[
            pltpu.VMEM((tbk, d), jnp.bfloat16),
            pltpu.VMEM((tbk, P_PAD), jnp.float32),
            pltpu.VMEM((tbk, d), jnp.bfloat16),
            pltpu.VMEM((B_KDIM, B_DV), jnp.float32),
            pltpu.VMEM((8, C_DIM), jnp.float32),
            pltpu.VMEM((A_HEADS, GMLP_BLOCK, GMLP_BLOCK), jnp.bfloat16),
        ],
        compiler_params=pltpu.CompilerParams(
            dimension_semantics=("arbitrary", "arbitrary"), vmem_limit_bytes=VMEM_LIMIT),
        name="mixer",
    )(x, mod, g1.reshape(1, d), w_in_p, _bf(w_out), ln_g.reshape(1, -1), ln_b.reshape(1, -1), w_s, bsm,
      w_gate_p, b_gate.reshape(1, -1), gla_g.reshape(1, -1), conv_w, conv_b.reshape(1, -1))


def _router_kernel(x_ref, mod_ref, g2_ref, rwt_ref, rb_ref, h_ref, e_ref, r_ref, w_ref, cnt_ref, carry_ref):
    first = jnp.logical_and(pl.program_id(0) == 0, pl.program_id(1) == 0)
    tr = x_ref.shape[0]

    @pl.when(first)
    def _():
        carry_ref[...] = jnp.zeros_like(carry_ref)

    m = mod_ref[...]
    sh2, sc2 = m[3:4], m[4:5]
    h = (_rms(x_ref[...]) * g2_ref[...]) * (1.0 + sc2) + sh2
    h_ref[...] = h

    h_hi, h_lo = _split_bf16(h)
    w_hi, w_lo = _split_bf16(rwt_ref[...])
    logits = (_dot(w_hi, h_hi, _NT) + _dot(w_hi, h_lo, _NT) + _dot(w_lo, h_hi, _NT)) + rb_ref[...]
    ex = jnp.exp(logits - jnp.max(logits, axis=0, keepdims=True))
    probs = ex / jnp.sum(ex, axis=0, keepdims=True)

    idx8 = lax.broadcasted_iota(jnp.int32, (EXPERTS_PER_GROUP, tr), 0)
    neg = jnp.float32(-1.0)
    best = None
    for g in range(N_GROUPS):
        pg = probs[g * EXPERTS_PER_GROUP:(g + 1) * EXPERTS_PER_GROUP]
        m1 = jnp.max(pg, axis=0, keepdims=True)
        i1 = jnp.min(jnp.where(pg == m1, idx8, EXPERTS_PER_GROUP), axis=0, keepdims=True)
        pg2 = jnp.where(idx8 == i1, neg, pg)
        m2 = jnp.max(pg2, axis=0, keepdims=True)
        i2 = jnp.min(jnp.where(pg2 == m2, idx8, EXPERTS_PER_GROUP), axis=0, keepdims=True)
        cand = (m1 + m2, m1, m2, i1 + g * EXPERTS_PER_GROUP, i2 + g * EXPERTS_PER_GROUP)
        if best is None:
            best = cand
        else:
            better = cand[0] > best[0]
            best = tuple(jnp.where(better, a, b) for a, b in zip(cand, best))
    _, p1, p2, e0, e1 = best
    denom = p1 + p2
    w0, w1 = p1 / denom, p2 / denom

    eidx = lax.broadcasted_iota(jnp.int32, (N_EXPERTS, tr), 0)
    hit0, hit1 = eidx == e0, eidx == e1
    onehot = jnp.where(jnp.logical_or(hit0, hit1), 1.0, 0.0)
    before = (lax.broadcasted_iota(jnp.int32, (tr, tr), 0) < lax.broadcasted_iota(jnp.int32, (tr, tr), 1))
    rank = _dot(_bf(onehot), _bf(jnp.where(before, 1.0, 0.0))) + carry_ref[...]
    r0 = jnp.sum(jnp.where(hit0, rank, 0.0), axis=0, keepdims=True)
    r1 = jnp.sum(jnp.where(hit1, rank, 0.0), axis=0, keepdims=True)
    carry_ref[...] = carry_ref[...] + jnp.sum(onehot, axis=1, keepdims=True)
    cnt_ref[...] = jnp.broadcast_to(carry_ref[...], cnt_ref.shape).astype(jnp.int32)

    e_ref[0:1, :] = e0
    e_ref[1:2, :] = e1
    r_ref[0:1, :] = r0.astype(jnp.int32)
    r_ref[1:2, :] = r1.astype(jnp.int32)

    eye = lax.broadcasted_iota(jnp.int32, (tr, tr), 0) == lax.broadcasted_iota(jnp.int32, (tr, tr), 1)
    w0c = jnp.sum(jnp.where(eye, jnp.broadcast_to(w0, (tr, tr)), 0.0), axis=1, keepdims=True)
    w1c = jnp.sum(jnp.where(eye, jnp.broadcast_to(w1, (tr, tr)), 0.0), axis=1, keepdims=True)
    lane8 = lax.broadcasted_iota(jnp.int32, (tr, 8), 1)
    w_ref[...] = jnp.where(lane8 == 0, w0c, jnp.where(lane8 == 1, w1c, 0.0))


def _router(x, mod, g2, router_w, router_b):
    bn, t, d = x.shape
    tr = min(ROUTE_ROWS, t)
    n = bn * t
    nt = t // tr

    def whole(shape):
        return pl.BlockSpec(shape, lambda b, i: (0,) * len(shape))

    return pl.pallas_call(
        _router_kernel,
        grid=(bn, nt),
        in_specs=[
            pl.BlockSpec((None, tr, d), lambda b, i: (b, i, 0)),
            pl.BlockSpec((None, 6, d), lambda b, i: (b, 0, 0)),
            whole((1, d)),
            whole((N_EXPERTS, d)),
            whole((N_EXPERTS, 1)),
        ],
        out_specs=[
            pl.BlockSpec((tr, d), lambda b, i: (b * nt + i, 0)),
            pl.BlockSpec((TOP_K, tr), lambda b, i: (0, b * nt + i)),
            pl.BlockSpec((TOP_K, tr), lambda b, i: (0, b * nt + i)),
            pl.BlockSpec((tr, 8), lambda b, i: (b * nt + i, 0)),
            whole((N_EXPERTS, LANES)),
        ],
        out_shape=[
            jax.ShapeDtypeStruct((n, d), jnp.float32),
            jax.ShapeDtypeStruct((TOP_K, n), jnp.int32),
            jax.ShapeDtypeStruct((TOP_K, n), jnp.int32),
            jax.ShapeDtypeStruct((n, 8), jnp.float32),
            jax.ShapeDtypeStruct((N_EXPERTS, LANES), jnp.int32),
        ],
        scratch_shapes=[pltpu.VMEM((N_EXPERTS, 1), jnp.float32)],
        compiler_params=pltpu.CompilerParams(
            dimension_semantics=("arbitrary", "arbitrary"), vmem_limit_bytes=VMEM_LIMIT),
        name="router",
    )(x, mod, g2.reshape(1, d), router_w.T, router_b.reshape(N_EXPERTS, 1))


def _row_copy(src_ref, src_row, dst_ref, dst_row, sem):
    return pltpu.make_async_copy(src_ref.at[pl.ds(src_row, 1)], dst_ref.at[pl.ds(dst_row, 1)], sem)


def _dispatch_kernel(start_ref, e_ref, r_ref, h_ref, xs_in_ref, xs_ref, sem):
    del xs_in_ref
    tm = e_ref.shape[1]
    base = pl.program_id(0) * tm

    def issue(i, carry):
        for k in range(TOP_K):
            dst = start_ref[e_ref[k, i]] + r_ref[k, i]
            _row_copy(h_ref, base + i, xs_ref, dst, sem).start()
        return carry

    lax.fori_loop(0, tm, issue, 0)

    def drain(i, carry):
        _row_copy(h_ref, 0, xs_ref, 0, sem).wait()
        return carry

    lax.fori_loop(0, TOP_K * tm, drain, 0)


def _dispatch(pad_start, e, r, h2, n_rows):
    n, d = h2.shape
    tm = min(MOVE_ROWS, n)
    xs0 = jnp.zeros((n_rows, d), h2.dtype)
    return pl.pallas_call(
        _dispatch_kernel,
        grid_spec=pltpu.PrefetchScalarGridSpec(
            num_scalar_prefetch=1,
            grid=(n // tm,),
            in_specs=[
                pl.BlockSpec((TOP_K, tm), lambda i, s: (0, i), memory_space=pltpu.SMEM),
                pl.BlockSpec((TOP_K, tm), lambda i, s: (0, i), memory_space=pltpu.SMEM),
                pl.BlockSpec(memory_space=pl.ANY),
                pl.BlockSpec(memory_space=pl.ANY),
            ],
            out_specs=pl.BlockSpec(memory_space=pl.ANY),
            scratch_shapes=[pltpu.SemaphoreType.DMA],
        ),
        out_shape=jax.ShapeDtypeStruct((n_rows, d), h2.dtype),
        input_output_aliases={4: 0},
        compiler_params=pltpu.CompilerParams(dimension_semantics=("arbitrary",)),
        name="dispatch",
    )(pad_start, e, r, h2, xs0)


def _ffn_kernel(be_ref, nb_ref, xs_ref, w1_ref, w3_ref, w2_ref, ys_ref):
    del be_ref
    used = pl.program_id(0) < nb_ref[0]

    @pl.when(used)
    def _():
        xb = _bf(xs_ref[...])
        a = _dot(xb, w1_ref[...])
        g = _dot(xb, w3_ref[...])
        hm = _bf((a * jax.nn.sigmoid(a)) * g)
        ys_ref[...] = _dot(hm, w2_ref[...])

    @pl.when(jnp.logical_not(used))
    def _():
        ys_ref[...] = jnp.zeros_like(ys_ref)


def _expert_ffn(block_e, n_used, xs, w1, w3, w2):
    n_rows, d = xs.shape
    n_blocks = n_rows // FFN_ROWS
    f = w1.shape[-1]

    def row_block(i, be, nb):
        return (jnp.minimum(i, nb[0] - 1), 0)

    return pl.pallas_call(
        _ffn_kernel,
        grid_spec=pltpu.PrefetchScalarGridSpec(
            num_scalar_prefetch=2,
            grid=(n_blocks,),
            in_specs=[
                pl.BlockSpec((FFN_ROWS, d), row_block),
                pl.BlockSpec((None, d, f), lambda i, be, nb: (be[i], 0, 0)),
                pl.BlockSpec((None, d, f), lambda i, be, nb: (be[i], 0, 0)),
                pl.BlockSpec((None, f, d), lambda i, be, nb: (be[i], 0, 0)),
            ],
            out_specs=pl.BlockSpec((FFN_ROWS, d), lambda i, be, nb: (i, 0)),
        ),
        out_shape=jax.ShapeDtypeStruct((n_rows, d), jnp.float32),
        compiler_params=pltpu.CompilerParams(
            dimension_semantics=("arbitrary",), vmem_limit_bytes=VMEM_LIMIT),
        name="expert_ffn",
    )(block_e, n_used, xs, _bf(w1), _bf(w3), _bf(w2))


def _combine_kernel(start_ref, e_ref, r_ref, x_ref, mod_ref, w_ref, gf_ref, ys_ref, o_ref, ybuf, sem,
                    *, final_norm):
    tm = x_ref.shape[0]

    def issue(i, carry):
        for k in range(TOP_K):
            src = start_ref[e_ref[k, i]] + r_ref[k, i]
            _row_copy(ys_ref, src, ybuf.at[k], i, sem).start()
        return carry

    lax.fori_loop(0, tm, issue, 0)

    def drain(i, carry):
        _row_copy(ys_ref, 0, ybuf.at[0], 0, sem).wait()
        return carry

    lax.fori_loop(0, TOP_K * tm, drain, 0)

    w = w_ref[...]
    gt2 = mod_ref[...][5:6]
    out = x_ref[...] + gt2 * (w[:, 0:1] * ybuf[0] + w[:, 1:2] * ybuf[1])
    if final_norm:
        out = _rms(out) * gf_ref[...]
    o_ref[...] = out


def _combine(pad_start, e, r, x, mod, wcol, g_final, ys, final_norm):
    bn, t, d = x.shape
    tm = min(MOVE_ROWS, t)
    nt = t // tm
    return pl.pallas_call(
        functools.partial(_combine_kernel, final_norm=final_norm),
        grid_spec=pltpu.PrefetchScalarGridSpec(
            num_scalar_prefetch=1,
            grid=(bn, nt),
            in_specs=[
                pl.BlockSpec((TOP_K, tm), lambda b, i, s: (0, b * nt + i), memory_space=pltpu.SMEM),
                pl.BlockSpec((TOP_K, tm), lambda b, i, s: (0, b * nt + i), memory_space=pltpu.SMEM),
                pl.BlockSpec((None, tm, d), lambda b, i, s: (b, i, 0)),
                pl.BlockSpec((None, 6, d), lambda b, i, s: (b, 0, 0)),
                pl.BlockSpec((tm, 8), lambda b, i, s: (b * nt + i, 0)),
                pl.BlockSpec((1, d), lambda b, i, s: (0, 0)),
                pl.BlockSpec(memory_space=pl.ANY),
            ],
            out_specs=pl.BlockSpec((None, tm, d), lambda b, i, s: (b, i, 0)),
            scratch_shapes=[pltpu.VMEM((TOP_K, tm, d), jnp.float32), pltpu.SemaphoreType.DMA],
        ),
        out_shape=jax.ShapeDtypeStruct(x.shape, x.dtype),
        compiler_params=pltpu.CompilerParams(
            dimension_semantics=("arbitrary", "arbitrary"), vmem_limit_bytes=VMEM_LIMIT),
        name="combine",
    )(pad_start, e, r, x, mod, wcol, g_final.reshape(1, d), ys)


def _moe_layer(x, mod, g2, router_w, router_b, w1, w3, w2, g_final, final_norm):
    bn, t, d = x.shape
    n = bn * t
    h2, e, r, wcol, cnt = _router(x, mod, g2, router_w, router_b)
    counts = cnt[:, 0]
    padded = ((counts + FFN_ROWS - 1) // FFN_ROWS) * FFN_ROWS
    pad_end = jnp.cumsum(padded)
    pad_start = (pad_end - padded).astype(jnp.int32)
    n_blocks = -(-(n * TOP_K) // FFN_ROWS) + N_EXPERTS
    block_e = jnp.minimum(
        jnp.searchsorted(pad_end, jnp.arange(n_blocks, dtype=jnp.int32) * FFN_ROWS, side="right"),
        N_EXPERTS - 1).astype(jnp.int32)
    n_used = (pad_end[-1:] // FFN_ROWS).astype(jnp.int32)
    xs = _dispatch(pad_start, e, r, h2, n_blocks * FFN_ROWS)
    ys = _expert_ffn(block_e, n_used, xs, w1, w3, w2)
    return _combine(pad_start, e, r, x, mod, wcol, g_final, ys, final_norm)


def kernel(x, c, w_ada, b_ada, g_norm1, g_norm2, w_in, w_out, gmlp_ln_g, gmlp_ln_b, gmlp_ws, gmlp_bs,
           gla_w_gate, gla_b_gate, gla_norm_g, conv_w, conv_b, router_w, router_b, exp_w1, exp_w3, exp_w2,
           g_final):
    depth = w_ada.shape[0]
    bn, t, d = x.shape
    mod_all = _modulation(c, w_ada, b_ada).reshape(depth, bn, 6, d)
    for l in range(depth):
        mod = mod_all[l]
        x = _mixer(x, mod, g_norm1[l], w_in[l], w_out[l], gmlp_ln_g[l], gmlp_ln_b[l], gmlp_ws[l],
                   gmlp_bs[l], gla_w_gate[l], gla_b_gate[l], gla_norm_g[l], conv_w[l], conv_b[l])
        x = _moe_layer(x, mod, g_norm2[l], router_w, router_b, exp_w1[l], exp_w3[l], exp_w2[l],
                       g_final, final_norm=(l == depth - 1))
    return x
```

```python
import functools

import jax
import jax.numpy as jnp
from jax import lax
from jax.experimental import pallas as pl
from jax.experimental.pallas import tpu as pltpu

CHUNK = 64
GMLP_BLOCK = 128
A_DIM = 256
A_HEADS = 4
B_HEADS = 4
B_DK = 64
B_DV = 128
B_KDIM = B_HEADS * B_DK
B_VDIM = B_HEADS * B_DV
GLA_RANK = 16
GLA_TAU = 16.0
C_DIM = 256
N_EXPERTS = 32
N_GROUPS = 4
EXPERTS_PER_GROUP = N_EXPERTS // N_GROUPS
TOP_K = 2
EPS = 1e-6

LANES = 128
OFF_AU, OFF_AV, OFF_Q, OFF_K, OFF_V = 0, 256, 512, 768, 1024
OFF_OG, OFF_CB, OFF_CC, OFF_CX, OFF_GLR = 1536, 2048, 2304, 2560, 2816
P_PAD = OFF_GLR + LANES

MIX_ROWS = 512
ROUTE_ROWS = 512
MOVE_ROWS = 512
FFN_ROWS = 256
DMA_UNROLL = 8
VMEM_LIMIT = 56 * 1024 * 1024

_NT = (((1,), (1,)), ((), ()))
_TN = (((0,), (0,)), ((), ()))


def _dot(a, b, dims=None):
    if dims is None:
        return jnp.dot(a, b, preferred_element_type=jnp.float32)
    return lax.dot_general(a, b, dims, preferred_element_type=jnp.float32)


def _bf(x):
    return x.astype(jnp.bfloat16)


def _split_bf16(x):
    hi = _bf(x)
    lo = _bf(x - hi.astype(jnp.float32))
    return hi, lo


def _rms(x):
    return x * lax.rsqrt(jnp.mean(x * x, axis=-1, keepdims=True) + EPS)


def _mod_kernel(c_ref, w_ref, b_ref, o_ref):
    c = c_ref[...]
    s = c * jax.nn.sigmoid(c)
    s_hi, s_lo = _split_bf16(s)
    w = w_ref[...]
    w_hi, w_lo = _split_bf16(w)
    acc = _dot(s_hi, w_hi) + _dot(s_hi, w_lo) + _dot(s_lo, w_hi)
    o_ref[...] = acc + b_ref[...]


def _modulation(c, w_ada, b_ada):
    depth, d, six_d = w_ada.shape
    bn = c.shape[0]
    cb = 1024
    return pl.pallas_call(
        _mod_kernel,
        grid=(depth, six_d // cb),
        in_specs=[
            pl.BlockSpec((bn, d), lambda l, j: (0, 0)),
            pl.BlockSpec((None, d, cb), lambda l, j: (l, 0, j)),
            pl.BlockSpec((None, 1, cb), lambda l, j: (l, 0, j)),
        ],
        out_specs=pl.BlockSpec((None, bn, cb), lambda l, j: (l, 0, j)),
        out_shape=jax.ShapeDtypeStruct((depth, bn, six_d), jnp.float32),
        compiler_params=pltpu.CompilerParams(
            dimension_semantics=("arbitrary", "arbitrary"), vmem_limit_bytes=VMEM_LIMIT),
        name="adaln_mod",
    )(c, w_ada, b_ada.reshape(depth, 1, six_d))


def _mixer_kernel(x_ref, mod_ref, g1_ref, win_ref, wout_ref, lng_ref, lnb_ref, ws_ref, bsm_ref,
                  wg_ref, bg_ref, glag_ref, cw_ref, cb_ref, o_ref,
                  h_ref, p_ref, y_ref, s_ref, zc_ref, wsm_ref):
    tb = pl.program_id(1)
    rows_total = x_ref.shape[0]

    @pl.when(tb == 0)
    def _():
        s_ref[...] = jnp.zeros_like(s_ref)
        zc_ref[...] = jnp.zeros_like(zc_ref)
        tt = lax.broadcasted_iota(jnp.int32, (GMLP_BLOCK, GMLP_BLOCK), 0) // CHUNK
        ss = lax.broadcasted_iota(jnp.int32, (GMLP_BLOCK, GMLP_BLOCK), 1) // CHUNK
        for h in range(A_HEADS):
            wsm_ref[h] = _bf(jnp.where(tt >= ss, ws_ref[h], 0.0))

    x = x_ref[...]
    m = mod_ref[...]
    sh1, sc1, gt1 = m[0:1], m[1:2], m[2:3]
    h = (_rms(x) * g1_ref[...]) * (1.0 + sc1) + sh1
    h_ref[...] = _bf(h)
    p_ref[...] = _dot(h_ref[...], win_ref[...])

    lane256 = lax.broadcasted_iota(jnp.int32, (CHUNK, B_KDIM), 1)
    head_of_lane = lane256 // B_DK
    r64 = lax.broadcasted_iota(jnp.int32, (CHUNK, CHUNK), 0)
    c64 = lax.broadcasted_iota(jnp.int32, (CHUNK, CHUNK), 1)
    tri = _bf(jnp.where(r64 >= c64, 1.0, 0.0))
    causal4 = jnp.concatenate([r64 >= c64] * B_HEADS, axis=0)
    eye256 = (lax.broadcasted_iota(jnp.int32, (B_KDIM, B_KDIM), 0)
              == lax.broadcasted_iota(jnp.int32, (B_KDIM, B_KDIM), 1))
    a_head_of_lane = lax.broadcasted_iota(jnp.int32, (GMLP_BLOCK, A_DIM), 1) // (A_DIM // A_HEADS)
    row128 = lax.broadcasted_iota(jnp.int32, (GMLP_BLOCK, C_DIM), 0)
    sqrt_half = 0.7071067811865476

    def gelu(v):
        return 0.5 * v * (1.0 + lax.erf(v * sqrt_half))

    def block(j, carry):
        r0 = pl.multiple_of(j * GMLP_BLOCK, GMLP_BLOCK)
        rows = pl.ds(r0, GMLP_BLOCK)

        u = gelu(p_ref[rows, OFF_AU:OFF_AU + A_DIM])
        vv = gelu(p_ref[rows, OFF_AV:OFF_AV + A_DIM])
        mu = jnp.mean(vv, axis=-1, keepdims=True)
        var = jnp.mean((vv - mu) ** 2, axis=-1, keepdims=True)
        vn = _bf((vv - mu) * lax.rsqrt(var + EPS) * lng_ref[...] + lnb_ref[...])
        sv = jnp.zeros((GMLP_BLOCK, A_DIM), jnp.float32)
        for hh in range(A_HEADS):
            sv = jnp.where(a_head_of_lane == hh, _dot(wsm_ref[hh], vn), sv)
        y_ref[rows, 0:A_DIM] = _bf(u * (sv + bsm_ref[...]))

        z = _dot(_bf(p_ref[rows, OFF_GLR:OFF_GLR + LANES]), wg_ref[...]) + bg_ref[...]
        glog = (jnp.minimum(z, 0.0) - jnp.log1p(jnp.exp(-jnp.abs(z)))) / GLA_TAU
        for c in range(GMLP_BLOCK // CHUNK):
            rc = pl.ds(pl.multiple_of(r0 + c * CHUNK, CHUNK), CHUNK)
            g = glog[c * CHUNK:(c + 1) * CHUNK]
            g_hi = _bf(g)
            g_r1 = g - g_hi.astype(jnp.float32)
            g_mid = _bf(g_r1)
            g_lo = _bf(g_r1 - g_mid.astype(jnp.float32))
            b = _dot(tri, g_hi) + _dot(tri, g_mid) + _dot(tri, g_lo)
            b_mid = b[CHUNK // 2:CHUNK // 2 + 1]
            b_last = b[CHUNK - 1:CHUNK]
            q = p_ref[rc, OFF_Q:OFF_Q + B_KDIM] * (B_DK ** -0.5)
            k = p_ref[rc, OFF_K:OFF_K + B_KDIM]
            vb = _bf(p_ref[rc, OFF_V:OFF_V + B_VDIM])
            qs = q * jnp.exp(b - b_mid)
            ks = _bf(k * jnp.exp(b_mid - b))
            kd = _bf(k * jnp.exp(b_last - b))
            qb = q * jnp.exp(b)
            qs_st = _bf(jnp.concatenate(
                [jnp.where(head_of_lane == hh, qs, 0.0) for hh in range(B_HEADS)], axis=0))
            qb_st = _bf(jnp.concatenate(
                [jnp.where(head_of_lane == hh, qb, 0.0) for hh in range(B_HEADS)], axis=0))
            scores = _bf(jnp.where(causal4, _dot(qs_st, ks, _NT), 0.0))
            state = s_ref[...]
            o_inter = _dot(qb_st, _bf(state))
            kv_all = _dot(kd, vb, _TN)
            outs, kvs = [], []
            for hh in range(B_HEADS):
                rs = slice(hh * CHUNK, (hh + 1) * CHUNK)
                cs = slice(hh * B_DV, (hh + 1) * B_DV)
                o_h = _dot(scores[rs], vb[:, cs]) + o_inter[rs]
                outs.append(_rms(o_h))
                kvs.append(kv_all[rs, cs])
            decay_row = jnp.broadcast_to(jnp.exp(b_last), (B_KDIM, B_KDIM))
            decay_col = jnp.sum(jnp.where(eye256, decay_row, 0.0), axis=1, keepdims=True)
            s_ref[...] = decay_col * state + jnp.concatenate(kvs, axis=0)
            on = jnp.concatenate(outs, axis=1) * glag_ref[...]
            og = p_ref[rc, OFF_OG:OFF_OG + B_VDIM]
            y_ref[rc, A_DIM:A_DIM + B_VDIM] = _bf(on * (og * jax.nn.sigmoid(og)))

        zz = p_ref[rows, OFF_CC:OFF_CC + C_DIM] * p_ref[rows, OFF_CX:OFF_CX + C_DIM]
        prev = zc_ref[...]
        z1 = jnp.where(row128 == 0, prev[7:8], pltpu.roll(zz, 1, 0))
        z2 = jnp.where(row128 == 0, prev[6:7], jnp.where(row128 == 1, prev[7:8], pltpu.roll(zz, 2, 0)))
        cw = cw_ref[...]
        yc = cb_ref[...] + cw[0:1] * z2
        yc = yc + cw[1:2] * z1
        yc = yc + cw[2:3] * zz
        y_ref[rows, A_DIM + B_VDIM:A_DIM + B_VDIM + C_DIM] = _bf(p_ref[rows, OFF_CB:OFF_CB + C_DIM] * yc)
        zc_ref[...] = zz[GMLP_BLOCK - 8:GMLP_BLOCK]
        return carry

    lax.fori_loop(0, rows_total // GMLP_BLOCK, block, 0)
    o_ref[...] = x + gt1 * _dot(y_ref[...], wout_ref[...])


def _mixer(x, mod, g1, w_in, w_out, ln_g, ln_b, w_s, b_s, w_gate, b_gate, gla_g, conv_w, conv_b):
    bn, t, d = x.shape
    tbk = min(MIX_ROWS, t)
    glr0 = 2 * A_DIM + 2 * B_KDIM + B_VDIM
    w_in_p = _bf(jnp.concatenate(
        [w_in[:, :glr0], w_in[:, glr0 + GLA_RANK:], w_in[:, glr0:glr0 + GLA_RANK],
         jnp.zeros((d, LANES - GLA_RANK), w_in.dtype)], axis=1))
    w_gate_p = _bf(jnp.concatenate(
        [w_gate, jnp.zeros((LANES - GLA_RANK, B_KDIM), w_gate.dtype)], axis=0))
    bsm = jnp.repeat(b_s.T, A_DIM // A_HEADS, axis=1)

    def whole(shape):
        return pl.BlockSpec(shape, lambda b, i: (0,) * len(shape))

    return pl.pallas_call(
        _mixer_kernel,
        grid=(bn, t // tbk),
        in_specs=[
            pl.BlockSpec((None, tbk, d), lambda b, i: (b, i, 0)),
            pl.BlockSpec((None, 6, d), lambda b, i: (b, 0, 0)),
            whole((1, d)),
            whole((d, P_PAD)),
            whole((d, d)),
            whole((1, A_DIM)),
            whole((1, A_DIM)),
            whole((A_HEADS, GMLP_BLOCK, GMLP_BLOCK)),
            whole((GMLP_BLOCK, A_DIM)),
            whole((LANES, B_KDIM)),
            whole((1, B_KDIM)),
            whole((1, B_VDIM)),
            whole((3, C_DIM)),
            whole((1, C_DIM)),
        ],
        out_specs=pl.BlockSpec((None, tbk, d), lambda b, i: (b, i, 0)),
        out_shape=jax.ShapeDtypeStruct(x.shape, x.dtype),
        scratch_shapes=[
            pltpu.VMEM((tbk, d), jnp.bfloat16),
            pltpu.VMEM((tbk, P_PAD), jnp.float32),
            pltpu.VMEM((tbk, d), jnp.bfloat16),
            pltpu.VMEM((B_KDIM, B_DV), jnp.float32),
            pltpu.VMEM((8, C_DIM), jnp.float32),
            pltpu.VMEM((A_HEADS, GMLP_BLOCK, GMLP_BLOCK), jnp.bfloat16),
        ],
        compiler_params=pltpu.CompilerParams(
            dimension_semantics=("arbitrary", "arbitrary"), vmem_limit_bytes=VMEM_LIMIT),
        name="mixer",
    )(x, mod, g1.reshape(1, d), w_in_p, _bf(w_out), ln_g.reshape(1, -1), ln_b.reshape(1, -1), w_s, bsm,
      w_gate_p, b_gate.reshape(1, -1), gla_g.reshape(1, -1), conv_w, conv_b.reshape(1, -1))


def _router_kernel(x_ref, mod_ref, g2_ref, rwt_ref, rb_ref, h_ref, e_ref, r_ref, w_ref, cnt_ref, carry_ref):
    first = jnp.logical_and(pl.program_id(0) == 0, pl.program_id(1) == 0)
    tr = x_ref.shape[0]

    @pl.when(first)
    def _():
        carry_ref[...] = jnp.zeros_like(carry_ref)

    m = mod_ref[...]
    sh2, sc2 = m[3:4], m[4:5]
    h = (_rms(x_ref[...]) * g2_ref[...]) * (1.0 + sc2) + sh2
    h_ref[...] = h

    h_hi, h_lo = _split_bf16(h)
    w_hi, w_lo = _split_bf16(rwt_ref[...])
    logits = (_dot(w_hi, h_hi, _NT) + _dot(w_hi, h_lo, _NT) + _dot(w_lo, h_hi, _NT)) + rb_ref[...]
    ex = jnp.exp(logits - jnp.max(logits, axis=0, keepdims=True))
    probs = ex / jnp.sum(ex, axis=0, keepdims=True)

    idx8 = lax.broadcasted_iota(jnp.int32, (EXPERTS_PER_GROUP, tr), 0)
    neg = jnp.float32(-1.0)
    best = None
    for g in range(N_GROUPS):
        pg = probs[g * EXPERTS_PER_GROUP:(g + 1) * EXPERTS_PER_GROUP]
        m1 = jnp.max(pg, axis=0, keepdims=True)
        i1 = jnp.min(jnp.where(pg == m1, idx8, EXPERTS_PER_GROUP), axis=0, keepdims=True)
        pg2 = jnp.where(idx8 == i1, neg, pg)
        m2 = jnp.max(pg2, axis=0, keepdims=True)
        i2 = jnp.min(jnp.where(pg2 == m2, idx8, EXPERTS_PER_GROUP), axis=0, keepdims=True)
        cand = (m1 + m2, m1, m2, i1 + g * EXPERTS_PER_GROUP, i2 + g * EXPERTS_PER_GROUP)
        if best is None:
            best = cand
        else:
            better = cand[0] > best[0]
            best = tuple(jnp.where(better, a, b) for a, b in zip(cand, best))
    _, p1, p2, e0, e1 = best
    denom = p1 + p2
    w0, w1 = p1 / denom, p2 / denom

    eidx = lax.broadcasted_iota(jnp.int32, (N_EXPERTS, tr), 0)
    hit0, hit1 = eidx == e0, eidx == e1
    onehot = jnp.where(jnp.logical_or(hit0, hit1), 1.0, 0.0)
    before = (lax.broadcasted_iota(jnp.int32, (tr, tr), 0) < lax.broadcasted_iota(jnp.int32, (tr, tr), 1))
    rank = _dot(_bf(onehot), _bf(jnp.where(before, 1.0, 0.0))) + carry_ref[...]
    r0 = jnp.sum(jnp.where(hit0, rank, 0.0), axis=0, keepdims=True)
    r1 = jnp.sum(jnp.where(hit1, rank, 0.0), axis=0, keepdims=True)
    carry_ref[...] = carry_ref[...] + jnp.sum(onehot, axis=1, keepdims=True)
    cnt_ref[...] = jnp.broadcast_to(carry_ref[...], cnt_ref.shape).astype(jnp.int32)

    e_ref[0:1, :] = e0
    e_ref[1:2, :] = e1
    r_ref[0:1, :] = r0.astype(jnp.int32)
    r_ref[1:2, :] = r1.astype(jnp.int32)

    eye = lax.broadcasted_iota(jnp.int32, (tr, tr), 0) == lax.broadcasted_iota(jnp.int32, (tr, tr), 1)
    w0c = jnp.sum(jnp.where(eye, jnp.broadcast_to(w0, (tr, tr)), 0.0), axis=1, keepdims=True)
    w1c = jnp.sum(jnp.where(eye, jnp.broadcast_to(w1, (tr, tr)), 0.0), axis=1, keepdims=True)
    lane8 = lax.broadcasted_iota(jnp.int32, (tr, 8), 1)
    w_ref[...] = jnp.where(lane8 == 0, w0c, jnp.where(lane8 == 1, w1c, 0.0))


def _router(x, mod, g2, router_w, router_b):
    bn, t, d = x.shape
    tr = min(ROUTE_ROWS, t)
    n = bn * t
    nt = t // tr

    def whole(shape):
        return pl.BlockSpec(shape, lambda b, i: (0,) * len(shape))

    return pl.pallas_call(
        _router_kernel,
        grid=(bn, nt),
        in_specs=[
            pl.BlockSpec((None, tr, d), lambda b, i: (b, i, 0)),
            pl.BlockSpec((None, 6, d), lambda b, i: (b, 0, 0)),
            whole((1, d)),
            whole((N_EXPERTS, d)),
            whole((N_EXPERTS, 1)),
        ],
        out_specs=[
            pl.BlockSpec((tr, d), lambda b, i: (b * nt + i, 0)),
            pl.BlockSpec((TOP_K, tr), lambda b, i: (0, b * nt + i)),
            pl.BlockSpec((TOP_K, tr), lambda b, i: (0, b * nt + i)),
            pl.BlockSpec((tr, 8), lambda b, i: (b * nt + i, 0)),
            whole((N_EXPERTS, LANES)),
        ],
        out_shape=[
            jax.ShapeDtypeStruct((n, d), jnp.float32),
            jax.ShapeDtypeStruct((TOP_K, n), jnp.int32),
            jax.ShapeDtypeStruct((TOP_K, n), jnp.int32),
            jax.ShapeDtypeStruct((n, 8), jnp.float32),
            jax.ShapeDtypeStruct((N_EXPERTS, LANES), jnp.int32),
        ],
        scratch_shapes=[pltpu.VMEM((N_EXPERTS, 1), jnp.float32)],
        compiler_params=pltpu.CompilerParams(
            dimension_semantics=("arbitrary", "arbitrary"), vmem_limit_bytes=VMEM_LIMIT),
        name="router",
    )(x, mod, g2.reshape(1, d), router_w.T, router_b.reshape(N_EXPERTS, 1))


def _row_copy(src_ref, src_row, dst_ref, dst_row, sem):
    return pltpu.make_async_copy(src_ref.at[pl.ds(src_row, 1)], dst_ref.at[pl.ds(dst_row, 1)], sem)


def _zero_unwritten_rows(start_ref, cnt_ref, nused_ref, zero_ref, xs_ref, sem):
    copies = []
    for ex in range(N_EXPERTS):
        cnt = cnt_ref[ex]
        seg = start_ref[ex]
        pos = seg + cnt
        end8 = seg + ((cnt + 7) // 8) * 8
        for j in range(7):
            copies.append((pos + j < end8, _row_copy(zero_ref, 0, xs_ref, pos + j, sem)))
        seg_end = seg + ((cnt + FFN_ROWS - 1) // FFN_ROWS) * FFN_ROWS
        groups = (seg_end - end8) // 8
        at = end8
        bit = FFN_ROWS // 16
        while bit >= 1:
            rows = bit * 8
            pred = (groups & bit) != 0
            copies.append((pred, pltpu.make_async_copy(
                zero_ref.at[pl.ds(0, rows)], xs_ref.at[pl.ds(pl.multiple_of(at, 8), rows)], sem)))
            at = at + jnp.where(pred, rows, 0)
            bit //= 2
    n_blocks = xs_ref.shape[0] // FFN_ROWS
    for blk in range(n_blocks - N_EXPERTS, n_blocks):
        copies.append((blk >= nused_ref[0], pltpu.make_async_copy(
            zero_ref, xs_ref.at[pl.ds(blk * FFN_ROWS, FFN_ROWS)], sem)))
    for pred, cp in copies:
        pl.when(pred)(cp.start)
    for pred, cp in copies:
        pl.when(pred)(cp.wait)


def _dispatch_kernel(start_ref, cnt_ref, nused_ref, e_ref, r_ref, h_ref, xs_ref, zero_ref, sem, zsem):
    tm = h_ref.shape[0]

    @pl.when(pl.program_id(0) == 0)
    def _():
        zero_ref[...] = jnp.zeros_like(zero_ref)
        _zero_unwritten_rows(start_ref, cnt_ref, nused_ref, zero_ref, xs_ref, zsem)

    def issue(i, carry):
        for k in range(TOP_K):
            dst = start_ref[e_ref[k, i]] + r_ref[k, i]
            _row_copy(h_ref, i, xs_ref, dst, sem).start()
        return carry

    lax.fori_loop(0, tm, issue, 0, unroll=DMA_UNROLL)

    def drain(i, carry):
        _row_copy(h_ref, 0, xs_ref, 0, sem).wait()
        return carry

    lax.fori_loop(0, TOP_K * tm, drain, 0, unroll=DMA_UNROLL)


def _dispatch(pad_start, counts, n_used, e, r, h2, n_rows):
    n, d = h2.shape
    tm = min(MOVE_ROWS, n)
    return pl.pallas_call(
        _dispatch_kernel,
        grid_spec=pltpu.PrefetchScalarGridSpec(
            num_scalar_prefetch=3,
            grid=(n // tm,),
            in_specs=[
                pl.BlockSpec((TOP_K, tm), lambda i, *_: (0, i), memory_space=pltpu.SMEM),
                pl.BlockSpec((TOP_K, tm), lambda i, *_: (0, i), memory_space=pltpu.SMEM),
                pl.BlockSpec((tm, d), lambda i, *_: (i, 0)),
            ],
            out_specs=pl.BlockSpec(memory_space=pl.ANY),
            scratch_shapes=[pltpu.VMEM((FFN_ROWS, d), h2.dtype), pltpu.SemaphoreType.DMA,
                            pltpu.SemaphoreType.DMA],
        ),
        out_shape=jax.ShapeDtypeStruct((n_rows, d), h2.dtype),
        compiler_params=pltpu.CompilerParams(
            dimension_semantics=("arbitrary",), vmem_limit_bytes=VMEM_LIMIT),
        name="dispatch",
    )(pad_start, counts, n_used, e, r, h2)


def _ffn_kernel(be_ref, nb_ref, xs_ref, w1_ref, w3_ref, w2_ref, ys_ref, w1b_ref, w3b_ref, w2b_ref):
    i = pl.program_id(0)
    used = i < nb_ref[0]
    new_expert = jnp.logical_or(i == 0, be_ref[i] != be_ref[jnp.maximum(i - 1, 0)])

    @pl.when(jnp.logical_and(used, new_expert))
    def _():
        w1b_ref[...] = _bf(w1_ref[...])
        w3b_ref[...] = _bf(w3_ref[...])
        w2b_ref[...] = _bf(w2_ref[...])

    @pl.when(used)
    def _():
        xb = _bf(xs_ref[...])
        a = _dot(xb, w1b_ref[...])
        g = _dot(xb, w3b_ref[...])
        hm = _bf((a * jax.nn.sigmoid(a)) * g)
        ys_ref[...] = _dot(hm, w2b_ref[...])

    @pl.when(jnp.logical_not(used))
    def _():
        ys_ref[...] = jnp.zeros_like(ys_ref)


def _expert_ffn(block_e, n_used, xs, w1, w3, w2):
    n_rows, d = xs.shape
    n_blocks = n_rows // FFN_ROWS
    f = w1.shape[-1]

    def row_block(i, be, nb):
        return (jnp.minimum(i, nb[0] - 1), 0)

    return pl.pallas_call(
        _ffn_kernel,
        grid_spec=pltpu.PrefetchScalarGridSpec(
            num_scalar_prefetch=2,
            grid=(n_blocks,),
            in_specs=[
                pl.BlockSpec((FFN_ROWS, d), row_block),
                pl.BlockSpec((None, d, f), lambda i, be, nb: (be[i], 0, 0)),
                pl.BlockSpec((None, d, f), lambda i, be, nb: (be[i], 0, 0)),
                pl.BlockSpec((None, f, d), lambda i, be, nb: (be[i], 0, 0)),
            ],
            out_specs=pl.BlockSpec((FFN_ROWS, d), lambda i, be, nb: (i, 0)),
            scratch_shapes=[pltpu.VMEM((d, f), jnp.bfloat16), pltpu.VMEM((d, f), jnp.bfloat16),
                            pltpu.VMEM((f, d), jnp.bfloat16)],
        ),
        out_shape=jax.ShapeDtypeStruct((n_rows, d), jnp.float32),
        compiler_params=pltpu.CompilerParams(
            dimension_semantics=("arbitrary",), vmem_limit_bytes=VMEM_LIMIT),
        name="expert_ffn",
    )(block_e, n_used, xs, w1, w3, w2)


def _combine_kernel(start_ref, e_ref, r_ref, x_ref, mod_ref, w_ref, gf_ref, ys_ref, o_ref, ybuf, sem,
                    *, final_norm):
    tm = x_ref.shape[0]

    def issue(i, carry):
        for k in range(TOP_K):
            src = start_ref[e_ref[k, i]] + r_ref[k, i]
            _row_copy(ys_ref, src, ybuf.at[k], i, sem).start()
        return carry

    lax.fori_loop(0, tm, issue, 0, unroll=DMA_UNROLL)

    def drain(i, carry):
        _row_copy(ys_ref, 0, ybuf.at[0], 0, sem).wait()
        return carry

    lax.fori_loop(0, TOP_K * tm, drain, 0, unroll=DMA_UNROLL)

    w = w_ref[...]
    gt2 = mod_ref[...][5:6]
    out = x_ref[...] + gt2 * (w[:, 0:1] * ybuf[0] + w[:, 1:2] * ybuf[1])
    if final_norm:
        out = _rms(out) * gf_ref[...]
    o_ref[...] = out


def _combine(pad_start, e, r, x, mod, wcol, g_final, ys, final_norm):
    bn, t, d = x.shape
    tm = min(MOVE_ROWS, t)
    nt = t // tm
    return pl.pallas_call(
        functools.partial(_combine_kernel, final_norm=final_norm),
        grid_spec=pltpu.PrefetchScalarGridSpec(
            num_scalar_prefetch=1,
            grid=(bn, nt),
            in_specs=[
                pl.BlockSpec((TOP_K, tm), lambda b, i, s: (0, b * nt + i), memory_space=pltpu.SMEM),
                pl.BlockSpec((TOP_K, tm), lambda b, i, s: (0, b * nt + i), memory_space=pltpu.SMEM),
                pl.BlockSpec((None, tm, d), lambda b, i, s: (b, i, 0)),
                pl.BlockSpec((None, 6, d), lambda b, i, s: (b, 0, 0)),
                pl.BlockSpec((tm, 8), lambda b, i, s: (b * nt + i, 0)),
                pl.BlockSpec((1, d), lambda b, i, s: (0, 0)),
                pl.BlockSpec(memory_space=pl.ANY),
            ],
            out_specs=pl.BlockSpec((None, tm, d), lambda b, i, s: (b, i, 0)),
            scratch_shapes=[pltpu.VMEM((TOP_K, tm, d), jnp.float32), pltpu.SemaphoreType.DMA],
        ),
        out_shape=jax.ShapeDtypeStruct(x.shape, x.dtype),
        compiler_params=pltpu.CompilerParams(
            dimension_semantics=("arbitrary", "arbitrary"), vmem_limit_bytes=VMEM_LIMIT),
        name="combine",
    )(pad_start, e, r, x, mod, wcol, g_final.reshape(1, d), ys)


def _moe_layer(x, mod, g2, router_w, router_b, w1, w3, w2, g_final, final_norm):
    bn, t, d = x.shape
    n = bn * t
    h2, e, r, wcol, cnt = _router(x, mod, g2, router_w, router_b)
    counts = cnt[:, 0]
    padded = ((counts + FFN_ROWS - 1) // FFN_ROWS) * FFN_ROWS
    pad_end = jnp.cumsum(padded)
    pad_start = (pad_end - padded).astype(jnp.int32)
    n_blocks = -(-(n * TOP_K) // FFN_ROWS) + N_EXPERTS
    blk_first_row = jnp.arange(n_blocks, dtype=jnp.int32) * FFN_ROWS
    block_e = jnp.minimum(
        jnp.sum((pad_end[None, :] <= blk_first_row[:, None]).astype(jnp.int32), axis=1), N_EXPERTS - 1)
    n_used = (pad_end[-1:] // FFN_ROWS).astype(jnp.int32)
    xs = _dispatch(pad_start, counts, n_used, e, r, h2, n_blocks * FFN_ROWS)
    ys = _expert_ffn(block_e, n_used, xs, w1, w3, w2)
    return _combine(pad_start, e, r, x, mod, wcol, g_final, ys, final_norm)


def kernel(x, c, w_ada, b_ada, g_norm1, g_norm2, w_in, w_out, gmlp_ln_g, gmlp_ln_b, gmlp_ws, gmlp_bs,
           gla_w_gate, gla_b_gate, gla_norm_g, conv_w, conv_b, router_w, router_b, exp_w1, exp_w3, exp_w2,
           g_final):
    depth = w_ada.shape[0]
    bn, t, d = x.shape
    mod_all = _modulation(c, w_ada, b_ada).reshape(depth, bn, 6, d)
    for l in range(depth):
        mod = mod_all[l]
        x = _mixer(x, mod, g_norm1[l], w_in[l], w_out[l], gmlp_ln_g[l], gmlp_ln_b[l], gmlp_ws[l],
                   gmlp_bs[l], gla_w_gate[l], gla_b_gate[l], gla_norm_g[l], conv_w[l], conv_b[l])
        x = _moe_layer(x, mod, g_norm2[l], router_w, router_b, exp_w1[l], exp_w3[l], exp_w2[l],
                       g_final, final_norm=(l == depth - 1))
    return x
```

```python
import functools

import jax
import jax.numpy as jnp
from jax import lax
from jax.experimental import pallas as pl
from jax.experimental.pallas import tpu as pltpu

CHUNK = 64
GMLP_BLOCK = 128
A_DIM = 256
A_HEADS = 4
B_HEADS = 4
B_DK = 64
B_DV = 128
B_KDIM = B_HEADS * B_DK
B_VDIM = B_HEADS * B_DV
GLA_RANK = 16
GLA_TAU = 16.0
C_DIM = 256
N_EXPERTS = 32
N_GROUPS = 4
EXPERTS_PER_GROUP = N_EXPERTS // N_GROUPS
TOP_K = 2
EPS = 1e-6

LANES = 128
SUBLANES = 8
OFF_AU, OFF_AV, OFF_Q, OFF_K, OFF_V = 0, 256, 512, 768, 1024
OFF_OG, OFF_CB, OFF_CC, OFF_CX, OFF_GLR = 1536, 2048, 2304, 2560, 2816
P_PAD = OFF_GLR + LANES

MIX_ROWS = 512
MOVE_ROWS = 512
FFN_ROWS = 256
VMEM_LIMIT = 56 * 1024 * 1024

_NT = (((1,), (1,)), ((), ()))
_TN = (((0,), (0,)), ((), ()))


def _dot(a, b, dims=None):
    if dims is None:
        return jnp.dot(a, b, preferred_element_type=jnp.float32)
    return lax.dot_general(a, b, dims, preferred_element_type=jnp.float32)


def _bf(x):
    return x.astype(jnp.bfloat16)


def _split_bf16(x):
    hi = _bf(x)
    lo = _bf(x - hi.astype(jnp.float32))
    return hi, lo


def _rms(x):
    return x * lax.rsqrt(jnp.mean(x * x, axis=-1, keepdims=True) + EPS)


def _row_copy(src_ref, src_row, dst_ref, dst_row, sem):
    return pltpu.make_async_copy(src_ref.at[pl.ds(src_row, 1)], dst_ref.at[pl.ds(dst_row, 1)], sem)


def _mod_kernel(c_ref, w_ref, b_ref, o_ref):
    c = c_ref[...]
    s = c * jax.nn.sigmoid(c)
    s_hi, s_lo = _split_bf16(s)
    w_hi, w_lo = _split_bf16(w_ref[...])
    acc = _dot(s_hi, w_hi) + _dot(s_hi, w_lo) + _dot(s_lo, w_hi)
    o_ref[...] = acc + b_ref[...]


def _modulation(c, w_ada, b_ada):
    depth, d, six_d = w_ada.shape
    bn = c.shape[0]
    cb = 1024
    return pl.pallas_call(
        _mod_kernel,
        grid=(depth, six_d // cb),
        in_specs=[
            pl.BlockSpec((bn, d), lambda l, j: (0, 0)),
            pl.BlockSpec((None, d, cb), lambda l, j: (l, 0, j)),
            pl.BlockSpec((None, 1, cb), lambda l, j: (l, 0, j)),
        ],
        out_specs=pl.BlockSpec((None, bn, cb), lambda l, j: (l, 0, j)),
        out_shape=jax.ShapeDtypeStruct((depth, bn, six_d), jnp.float32),
        compiler_params=pltpu.CompilerParams(
            dimension_semantics=("arbitrary", "arbitrary"), vmem_limit_bytes=VMEM_LIMIT),
        name="adaln_mod",
    )(c, w_ada, b_ada.reshape(depth, 1, six_d))


def _route(h2, rwt_ref, rb_ref, carry, cap):
    tr = h2.shape[0]
    h_hi, h_lo = _split_bf16(h2)
    w_hi, w_lo = _split_bf16(rwt_ref[...])
    logits = (_dot(w_hi, h_hi, _NT) + _dot(w_hi, h_lo, _NT) + _dot(w_lo, h_hi, _NT)) + rb_ref[...]
    ex = jnp.exp(logits - jnp.max(logits, axis=0, keepdims=True))
    probs = ex / jnp.sum(ex, axis=0, keepdims=True)

    idx8 = lax.broadcasted_iota(jnp.int32, (EXPERTS_PER_GROUP, tr), 0)
    best = None
    for g in range(N_GROUPS):
        pg = probs[g * EXPERTS_PER_GROUP:(g + 1) * EXPERTS_PER_GROUP]
        m1 = jnp.max(pg, axis=0, keepdims=True)
        i1 = jnp.min(jnp.where(pg == m1, idx8, EXPERTS_PER_GROUP), axis=0, keepdims=True)
        pg2 = jnp.where(idx8 == i1, -1.0, pg)
        m2 = jnp.max(pg2, axis=0, keepdims=True)
        i2 = jnp.min(jnp.where(pg2 == m2, idx8, EXPERTS_PER_GROUP), axis=0, keepdims=True)
        cand = (m1 + m2, m1, m2, i1 + g * EXPERTS_PER_GROUP, i2 + g * EXPERTS_PER_GROUP)
        if best is None:
            best = cand
        else:
            better = cand[0] > best[0]
            best = tuple(jnp.where(better, a, b) for a, b in zip(cand, best))
    _, p1, p2, e0, e1 = best
    denom = p1 + p2
    w0, w1 = p1 / denom, p2 / denom

    eidx = lax.broadcasted_iota(jnp.int32, (N_EXPERTS, tr), 0)
    hit0, hit1 = eidx == e0, eidx == e1
    onehot = jnp.where(jnp.logical_or(hit0, hit1), 1.0, 0.0)
    before = (lax.broadcasted_iota(jnp.int32, (tr, tr), 0) < lax.broadcasted_iota(jnp.int32, (tr, tr), 1))
    rank = _dot(_bf(onehot), _bf(jnp.where(before, 1.0, 0.0))) + carry
    r0 = jnp.sum(jnp.where(hit0, rank, 0.0), axis=0, keepdims=True).astype(jnp.int32)
    r1 = jnp.sum(jnp.where(hit1, rank, 0.0), axis=0, keepdims=True).astype(jnp.int32)
    carry = carry + jnp.sum(onehot, axis=1, keepdims=True)

    eye = lax.broadcasted_iota(jnp.int32, (tr, tr), 0) == lax.broadcasted_iota(jnp.int32, (tr, tr), 1)
    w0c = jnp.sum(jnp.where(eye, jnp.broadcast_to(w0, (tr, tr)), 0.0), axis=1, keepdims=True)
    w1c = jnp.sum(jnp.where(eye, jnp.broadcast_to(w1, (tr, tr)), 0.0), axis=1, keepdims=True)
    lane8 = lax.broadcasted_iota(jnp.int32, (tr, 8), 1)
    wcol = jnp.where(lane8 == 0, w0c, jnp.where(lane8 == 1, w1c, 0.0))
    return e0 * cap + r0, e1 * cap + r1, wcol, carry


def _zero_segment_padding(cnt_ref, cap, zero_ref, xs_ref, sem):
    copies = []
    for ex in range(N_EXPERTS):
        cnt = cnt_ref[ex, 0]
        seg = ex * cap
        pos = seg + cnt
        end8 = seg + ((cnt + SUBLANES - 1) // SUBLANES) * SUBLANES
        for j in range(SUBLANES - 1):
            copies.append((pos + j < end8, _row_copy(zero_ref, 0, xs_ref, pos + j, sem)))
        seg_end = seg + ((cnt + FFN_ROWS - 1) // FFN_ROWS) * FFN_ROWS
        groups = (seg_end - end8) // SUBLANES
        at = end8
        bit = FFN_ROWS // (2 * SUBLANES)
        while bit >= 1:
            rows = bit * SUBLANES
            pred = (groups & bit) != 0
            copies.append((pred, pltpu.make_async_copy(
                zero_ref.at[pl.ds(0, rows)], xs_ref.at[pl.ds(pl.multiple_of(at, SUBLANES), rows)], sem)))
            at = at + jnp.where(pred, rows, 0)
            bit //= 2
    for pred, cp in copies:
        pl.when(pred)(cp.start)
    for pred, cp in copies:
        pl.when(pred)(cp.wait)


def _mixer_kernel(x_ref, mod_ref, g1_ref, win_ref, wout_ref, lng_ref, lnb_ref, ws_ref, bsm_ref,
                  wg_ref, bg_ref, glag_ref, cw_ref, cb_ref, g2_ref, rwt_ref, rb_ref,
                  o_ref, dest_ref, w_ref, cnt_ref, xs_ref,
                  h_ref, p_ref, y_ref, s_ref, zc_ref, wsm_ref, h2_ref, carry_ref, dvm_ref, d0s_ref, d1s_ref,
                  cvm_ref, csm_ref, zero_ref, sem, sem_s, sem_z, *, cap):
    bi, ti = pl.program_id(0), pl.program_id(1)
    tm = x_ref.shape[0]
    n_blk = tm // GMLP_BLOCK

    @pl.when(jnp.logical_and(bi == 0, ti == 0))
    def _():
        carry_ref[...] = jnp.zeros_like(carry_ref)
        tt = lax.broadcasted_iota(jnp.int32, (GMLP_BLOCK, GMLP_BLOCK), 0) // CHUNK
        ss = lax.broadcasted_iota(jnp.int32, (GMLP_BLOCK, GMLP_BLOCK), 1) // CHUNK
        for h in range(A_HEADS):
            wsm_ref[h] = _bf(jnp.where(tt >= ss, ws_ref[h], 0.0))

    @pl.when(ti == 0)
    def _():
        s_ref[...] = jnp.zeros_like(s_ref)
        zc_ref[...] = jnp.zeros_like(zc_ref)

    m = mod_ref[...]
    sh1, sc1, gt1, sh2, sc2 = m[0:1], m[1:2], m[2:3], m[3:4], m[4:5]
    h_ref[...] = _bf((_rms(x_ref[...]) * g1_ref[...]) * (1.0 + sc1) + sh1)

    lane256 = lax.broadcasted_iota(jnp.int32, (CHUNK, B_KDIM), 1)
    head_of_lane = lane256 // B_DK
    r64 = lax.broadcasted_iota(jnp.int32, (CHUNK, CHUNK), 0)
    c64 = lax.broadcasted_iota(jnp.int32, (CHUNK, CHUNK), 1)
    tri = _bf(jnp.where(r64 >= c64, 1.0, 0.0))
    causal4 = jnp.concatenate([r64 >= c64] * B_HEADS, axis=0)
    eye256 = (lax.broadcasted_iota(jnp.int32, (B_KDIM, B_KDIM), 0)
              == lax.broadcasted_iota(jnp.int32, (B_KDIM, B_KDIM), 1))
    a_head_of_lane = lax.broadcasted_iota(jnp.int32, (GMLP_BLOCK, A_DIM), 1) // (A_DIM // A_HEADS)
    row128 = lax.broadcasted_iota(jnp.int32, (GMLP_BLOCK, C_DIM), 0)
    sqrt_half = 0.7071067811865476

    def gelu(v):
        return 0.5 * v * (1.0 + lax.erf(v * sqrt_half))

    def project(j):
        rows = slice(j * GMLP_BLOCK, (j + 1) * GMLP_BLOCK)
        p_ref[rows, :] = _dot(h_ref[rows, :], win_ref[...])

    def block(j, state, zprev):
        r0 = j * GMLP_BLOCK
        rows = slice(r0, r0 + GMLP_BLOCK)

        u = gelu(p_ref[rows, OFF_AU:OFF_AU + A_DIM])
        vv = gelu(p_ref[rows, OFF_AV:OFF_AV + A_DIM])
        mu = jnp.mean(vv, axis=-1, keepdims=True)
        var = jnp.mean((vv - mu) ** 2, axis=-1, keepdims=True)
        vn = _bf((vv - mu) * lax.rsqrt(var + EPS) * lng_ref[...] + lnb_ref[...])
        sv = jnp.zeros((GMLP_BLOCK, A_DIM), jnp.float32)
        for hh in range(A_HEADS):
            sv = jnp.where(a_head_of_lane == hh, _dot(wsm_ref[hh], vn), sv)
        y_ref[rows, 0:A_DIM] = _bf(u * (sv + bsm_ref[...]))

        z = _dot(_bf(p_ref[rows, OFF_GLR:OFF_GLR + LANES]), wg_ref[...]) + bg_ref[...]
        glog = (jnp.minimum(z, 0.0) - jnp.log1p(jnp.exp(-jnp.abs(z)))) / GLA_TAU
        for c in range(GMLP_BLOCK // CHUNK):
            rc = slice(r0 + c * CHUNK, r0 + (c + 1) * CHUNK)
            g = glog[c * CHUNK:(c + 1) * CHUNK]
            g_hi = _bf(g)
            g_r1 = g - g_hi.astype(jnp.float32)
            g_mid = _bf(g_r1)
            g_lo = _bf(g_r1 - g_mid.astype(jnp.float32))
            b = _dot(tri, g_hi) + _dot(tri, g_mid) + _dot(tri, g_lo)
            b_mid = b[CHUNK // 2:CHUNK // 2 + 1]
            b_last = b[CHUNK - 1:CHUNK]
            q = p_ref[rc, OFF_Q:OFF_Q + B_KDIM] * (B_DK ** -0.5)
            k = p_ref[rc, OFF_K:OFF_K + B_KDIM]
            vb = _bf(p_ref[rc, OFF_V:OFF_V + B_VDIM])
            qs = q * jnp.exp(b - b_mid)
            ks = _bf(k * jnp.exp(b_mid - b))
            kd = _bf(k * jnp.exp(b_last - b))
            qb = q * jnp.exp(b)
            qs_st = _bf(jnp.concatenate(
                [jnp.where(head_of_lane == hh, qs, 0.0) for hh in range(B_HEADS)], axis=0))
            qb_st = _bf(jnp.concatenate(
                [jnp.where(head_of_lane == hh, qb, 0.0) for hh in range(B_HEADS)], axis=0))
            scores = _bf(jnp.where(causal4, _dot(qs_st, ks, _NT), 0.0))
            o_inter = _dot(qb_st, _bf(state))
            kv_all = _dot(kd, vb, _TN)
            outs, kvs = [], []
            for hh in range(B_HEADS):
                rs = slice(hh * CHUNK, (hh + 1) * CHUNK)
                cs = slice(hh * B_DV, (hh + 1) * B_DV)
                o_h = _dot(scores[rs], vb[:, cs]) + o_inter[rs]
                outs.append(_rms(o_h))
                kvs.append(kv_all[rs, cs])
            decay_row = jnp.broadcast_to(jnp.exp(b_last), (B_KDIM, B_KDIM))
            decay_col = jnp.sum(jnp.where(eye256, decay_row, 0.0), axis=1, keepdims=True)
            state = decay_col * state + jnp.concatenate(kvs, axis=0)
            on = jnp.concatenate(outs, axis=1) * glag_ref[...]
            og = p_ref[rc, OFF_OG:OFF_OG + B_VDIM]
            y_ref[rc, A_DIM:A_DIM + B_VDIM] = _bf(on * (og * jax.nn.sigmoid(og)))

        zz = p_ref[rows, OFF_CC:OFF_CC + C_DIM] * p_ref[rows, OFF_CX:OFF_CX + C_DIM]
        z1 = jnp.where(row128 == 0, zprev[7:8], pltpu.roll(zz, 1, 0))
        z2 = jnp.where(row128 == 0, zprev[6:7], jnp.where(row128 == 1, zprev[7:8], pltpu.roll(zz, 2, 0)))
        cw = cw_ref[...]
        yc = cb_ref[...] + cw[0:1] * z2
        yc = yc + cw[1:2] * z1
        yc = yc + cw[2:3] * zz
        y_ref[rows, A_DIM + B_VDIM:A_DIM + B_VDIM + C_DIM] = _bf(p_ref[rows, OFF_CB:OFF_CB + C_DIM] * yc)
        return state, zz[GMLP_BLOCK - SUBLANES:GMLP_BLOCK]

    state, zprev, carry = s_ref[...], zc_ref[...], carry_ref[...]
    project(0)
    for j in range(n_blk):
        if j + 1 < n_blk:
            project(j + 1)
        state, zprev = block(j, state, zprev)
        rows = slice(j * GMLP_BLOCK, (j + 1) * GMLP_BLOCK)
        xo = x_ref[rows, :] + gt1 * _dot(y_ref[rows, :], wout_ref[...])
        o_ref[rows, :] = xo
        h2 = (_rms(xo) * g2_ref[...]) * (1.0 + sc2) + sh2
        h2_ref[rows, :] = h2
        d0, d1, wcol, carry = _route(h2, rwt_ref, rb_ref, carry, cap)
        dvm_ref[0:1, rows] = d0
        dvm_ref[1:2, rows] = d1
        w_ref[rows, :] = wcol
    s_ref[...] = state
    zc_ref[...] = zprev
    carry_ref[...] = carry
    cnt = jnp.broadcast_to(carry, cnt_ref.shape).astype(jnp.int32)
    cnt_ref[...] = cnt
    dest_ref[...] = dvm_ref[0:TOP_K, :]

    to_smem = [pltpu.make_async_copy(dvm_ref.at[0], d0s_ref, sem_s),
               pltpu.make_async_copy(dvm_ref.at[1], d1s_ref, sem_s)]
    for cp in to_smem:
        cp.start()
    for cp in to_smem:
        cp.wait()

    def issue(g, c):
        for jj in range(SUBLANES):
            i = pl.multiple_of(g * SUBLANES, SUBLANES) + jj
            _row_copy(h2_ref, i, xs_ref, d0s_ref[i], sem).start()
            _row_copy(h2_ref, i, xs_ref, d1s_ref[i], sem).start()
        return c

    lax.fori_loop(0, tm // SUBLANES, issue, 0)

    def drain(g, c):
        for _ in range(TOP_K * SUBLANES):
            _row_copy(h2_ref, 0, xs_ref, 0, sem).wait()
        return c

    lax.fori_loop(0, tm // SUBLANES, drain, 0)

    @pl.when(jnp.logical_and(bi == pl.num_programs(0) - 1, ti == pl.num_programs(1) - 1))
    def _():
        cvm_ref[...] = cnt
        zero_ref[...] = jnp.zeros_like(zero_ref)
        cp = pltpu.make_async_copy(cvm_ref, csm_ref, sem_s)
        cp.start()
        cp.wait()
        _zero_segment_padding(csm_ref, cap, zero_ref, xs_ref, sem_z)


def _mixer(x, mod, g1, g2, w_in, w_out, ln_g, ln_b, w_s, b_s, w_gate, b_gate, gla_g, conv_w, conv_b,
           router_w, router_b):
    bn, t, d = x.shape
    n = bn * t
    tm = min(MIX_ROWS, t)
    nt = t // tm
    cap = n
    glr0 = 2 * A_DIM + 2 * B_KDIM + B_VDIM
    w_in_p = _bf(jnp.concatenate(
        [w_in[:, :glr0], w_in[:, glr0 + GLA_RANK:], w_in[:, glr0:glr0 + GLA_RANK],
         jnp.zeros((d, LANES - GLA_RANK), w_in.dtype)], axis=1))
    w_gate_p = _bf(jnp.concatenate(
        [w_gate, jnp.zeros((LANES - GLA_RANK, B_KDIM), w_gate.dtype)], axis=0))
    bsm = jnp.repeat(b_s.T, A_DIM // A_HEADS, axis=1)

    def whole(shape):
        return pl.BlockSpec(shape, lambda b, i: (0,) * len(shape))

    return pl.pallas_call(
        functools.partial(_mixer_kernel, cap=cap),
        grid=(bn, nt),
        in_specs=[
            pl.BlockSpec((None, tm, d), lambda b, i: (b, i, 0)),
            pl.BlockSpec((None, 6, d), lambda b, i: (b, 0, 0)),
            whole((1, d)),
            whole((d, P_PAD)),
            whole((d, d)),
            whole((1, A_DIM)),
            whole((1, A_DIM)),
            whole((A_HEADS, GMLP_BLOCK, GMLP_BLOCK)),
            whole((GMLP_BLOCK, A_DIM)),
            whole((LANES, B_KDIM)),
            whole((1, B_KDIM)),
            whole((1, B_VDIM)),
            whole((3, C_DIM)),
            whole((1, C_DIM)),
            whole((1, d)),
            whole((N_EXPERTS, d)),
            whole((N_EXPERTS, 1)),
        ],
        out_specs=[
            pl.BlockSpec((None, tm, d), lambda b, i: (b, i, 0)),
            pl.BlockSpec((TOP_K, tm), lambda b, i: (0, b * nt + i)),
            pl.BlockSpec((tm, 8), lambda b, i: (b * nt + i, 0)),
            whole((N_EXPERTS, LANES)),
            pl.BlockSpec(memory_space=pl.ANY),
        ],
        out_shape=[
            jax.ShapeDtypeStruct(x.shape, x.dtype),
            jax.ShapeDtypeStruct((TOP_K, n), jnp.int32),
            jax.ShapeDtypeStruct((n, 8), jnp.float32),
            jax.ShapeDtypeStruct((N_EXPERTS, LANES), jnp.int32),
            jax.ShapeDtypeStruct((N_EXPERTS * cap, d), jnp.float32),
        ],
        scratch_shapes=[
            pltpu.VMEM((tm, d), jnp.bfloat16),
            pltpu.VMEM((tm, P_PAD), jnp.float32),
            pltpu.VMEM((tm, d), jnp.bfloat16),
            pltpu.VMEM((B_KDIM, B_DV), jnp.float32),
            pltpu.VMEM((SUBLANES, C_DIM), jnp.float32),
            pltpu.VMEM((A_HEADS, GMLP_BLOCK, GMLP_BLOCK), jnp.bfloat16),
            pltpu.VMEM((tm, d), jnp.float32),
            pltpu.VMEM((N_EXPERTS, 1), jnp.float32),
            pltpu.VMEM((SUBLANES, tm), jnp.int32),
            pltpu.SMEM((tm,), jnp.int32),
            pltpu.SMEM((tm,), jnp.int32),
            pltpu.VMEM((N_EXPERTS, LANES), jnp.int32),
            pltpu.SMEM((N_EXPERTS, LANES), jnp.int32),
            pltpu.VMEM((FFN_ROWS // 2, d), jnp.float32),
            pltpu.SemaphoreType.DMA,
            pltpu.SemaphoreType.DMA,
            pltpu.SemaphoreType.DMA,
        ],
        compiler_params=pltpu.CompilerParams(
            dimension_semantics=("arbitrary", "arbitrary"), vmem_limit_bytes=VMEM_LIMIT),
        name="mixer",
    )(x, mod, g1.reshape(1, d), w_in_p, _bf(w_out), ln_g.reshape(1, -1), ln_b.reshape(1, -1), w_s, bsm,
      w_gate_p, b_gate.reshape(1, -1), gla_g.reshape(1, -1), conv_w, conv_b.reshape(1, -1),
      g2.reshape(1, d), router_w.T, router_b.reshape(N_EXPERTS, 1))


def _ffn_kernel(be_ref, rb_ref, nb_ref, xs_ref, w1_ref, w3_ref, w2_ref, ys_ref, w1b_ref, w3b_ref, w2b_ref):
    del rb_ref
    i = pl.program_id(0)
    used = i < nb_ref[0]
    new_expert = jnp.logical_or(i == 0, be_ref[i] != be_ref[jnp.maximum(i - 1, 0)])

    @pl.when(jnp.logical_and(used, new_expert))
    def _():
        w1b_ref[...] = _bf(w1_ref[...])
        w3b_ref[...] = _bf(w3_ref[...])
        w2b_ref[...] = _bf(w2_ref[...])

    @pl.when(used)
    def _():
        xb = _bf(xs_ref[...])
        a = _dot(xb, w1b_ref[...])
        g = _dot(xb, w3b_ref[...])
        hm = _bf((a * jax.nn.sigmoid(a)) * g)
        ys_ref[...] = _dot(hm, w2b_ref[...])


def _expert_ffn(layer, block_e, row_block, n_used, xs, w1, w3, w2):
    n_rows, d = xs.shape
    n_blocks = block_e.shape[0]
    f = w1.shape[-1]
    return pl.pallas_call(
        _ffn_kernel,
        grid_spec=pltpu.PrefetchScalarGridSpec(
            num_scalar_prefetch=3,
            grid=(n_blocks,),
            in_specs=[
                pl.BlockSpec((FFN_ROWS, d), lambda i, be, rb, nb: (rb[i], 0)),
                pl.BlockSpec((None, None, d, f), lambda i, be, rb, nb: (layer, be[i], 0, 0)),
                pl.BlockSpec((None, None, d, f), lambda i, be, rb, nb: (layer, be[i], 0, 0)),
                pl.BlockSpec((None, None, f, d), lambda i, be, rb, nb: (layer, be[i], 0, 0)),
            ],
            out_specs=pl.BlockSpec((FFN_ROWS, d), lambda i, be, rb, nb: (rb[i], 0)),
            scratch_shapes=[pltpu.VMEM((d, f), jnp.bfloat16), pltpu.VMEM((d, f), jnp.bfloat16),
                            pltpu.VMEM((f, d), jnp.bfloat16)],
        ),
        out_shape=jax.ShapeDtypeStruct((n_rows, d), jnp.float32),
        compiler_params=pltpu.CompilerParams(
            dimension_semantics=("arbitrary",), vmem_limit_bytes=VMEM_LIMIT),
        name="expert_ffn",
    )(block_e, row_block, n_used, xs, w1, w3, w2)


def _combine_kernel(d0_ref, d1_ref, x_ref, mod_ref, w_ref, gf_ref, ys_ref, o_ref, ybuf, sem, *, final_norm):
    tm = x_ref.shape[0]

    def issue(g, c):
        for jj in range(SUBLANES):
            i = pl.multiple_of(g * SUBLANES, SUBLANES) + jj
            _row_copy(ys_ref, d0_ref[i], ybuf.at[0], i, sem).start()
            _row_copy(ys_ref, d1_ref[i], ybuf.at[1], i, sem).start()
        return c

    lax.fori_loop(0, tm // SUBLANES, issue, 0)

    def drain(g, c):
        for _ in range(TOP_K * SUBLANES):
            _row_copy(ys_ref, 0, ybuf.at[0], 0, sem).wait()
        return c

    lax.fori_loop(0, tm // SUBLANES, drain, 0)

    w = w_ref[...]
    gt2 = mod_ref[...][5:6]
    out = x_ref[...] + gt2 * (w[:, 0:1] * ybuf[0] + w[:, 1:2] * ybuf[1])
    if final_norm:
        out = _rms(out) * gf_ref[...]
    o_ref[...] = out


def _combine(d0, d1, x, mod, wcol, g_final, ys, final_norm):
    bn, t, d = x.shape
    tm = min(MOVE_ROWS, t)
    nt = t // tm
    return pl.pallas_call(
        functools.partial(_combine_kernel, final_norm=final_norm),
        grid=(bn, nt),
        in_specs=[
            pl.BlockSpec((tm,), lambda b, i: (b * nt + i,), memory_space=pltpu.SMEM),
            pl.BlockSpec((tm,), lambda b, i: (b * nt + i,), memory_space=pltpu.SMEM),
            pl.BlockSpec((None, tm, d), lambda b, i: (b, i, 0)),
            pl.BlockSpec((None, 6, d), lambda b, i: (b, 0, 0)),
            pl.BlockSpec((tm, 8), lambda b, i: (b * nt + i, 0)),
            pl.BlockSpec((1, d), lambda b, i: (0, 0)),
            pl.BlockSpec(memory_space=pl.ANY),
        ],
        out_specs=pl.BlockSpec((None, tm, d), lambda b, i: (b, i, 0)),
        out_shape=jax.ShapeDtypeStruct(x.shape, x.dtype),
        scratch_shapes=[pltpu.VMEM((TOP_K, tm, d), jnp.float32), pltpu.SemaphoreType.DMA],
        compiler_params=pltpu.CompilerParams(
            dimension_semantics=("arbitrary", "arbitrary"), vmem_limit_bytes=VMEM_LIMIT),
        name="combine",
    )(d0, d1, x, mod, wcol, g_final.reshape(1, d), ys)


def _block_map(counts, cap, n_blocks):
    blocks_per_expert = (counts + FFN_ROWS - 1) // FFN_ROWS
    ends = jnp.cumsum(blocks_per_expert)
    n_used = ends[-1:]
    step = jnp.minimum(jnp.arange(n_blocks, dtype=jnp.int32), n_used - 1)
    block_e = jnp.minimum(jnp.sum((ends[None, :] <= step[:, None]).astype(jnp.int32), axis=1), N_EXPERTS - 1)
    first = (ends - blocks_per_expert)[block_e]
    row_block = block_e * (cap // FFN_ROWS) + (step - first)
    return block_e.astype(jnp.int32), row_block.astype(jnp.int32), n_used.astype(jnp.int32)


def kernel(x, c, w_ada, b_ada, g_norm1, g_norm2, w_in, w_out, gmlp_ln_g, gmlp_ln_b, gmlp_ws, gmlp_bs,
           gla_w_gate, gla_b_gate, gla_norm_g, conv_w, conv_b, router_w, router_b, exp_w1, exp_w3, exp_w2,
           g_final):
    depth = w_ada.shape[0]
    bn, t, d = x.shape
    n = bn * t
    n_blocks = -(-(n * TOP_K) // FFN_ROWS) + N_EXPERTS
    mod_all = _modulation(c, w_ada, b_ada).reshape(depth, bn, 6, d)
    for l in range(depth):
        mod = mod_all[l]
        x, dest, wcol, cnt, xs = _mixer(
            x, mod, g_norm1[l], g_norm2[l], w_in[l], w_out[l], gmlp_ln_g[l], gmlp_ln_b[l], gmlp_ws[l],
            gmlp_bs[l], gla_w_gate[l], gla_b_gate[l], gla_norm_g[l], conv_w[l], conv_b[l], router_w, router_b)
        block_e, row_block, n_used = _block_map(cnt[:, 0], n, n_blocks)
        ys = _expert_ffn(l, block_e, row_block, n_used, xs, exp_w1, exp_w3, exp_w2)
        x = _combine(dest[0], dest[1], x, mod, wcol, g_final, ys, final_norm=(l == depth - 1))
    return x
```

```python
import functools

import jax
import jax.numpy as jnp
from jax import lax
from jax.experimental import pallas as pl
from jax.experimental.pallas import tpu as pltpu

CHUNK = 64
GMLP_BLOCK = 128
A_DIM = 256
A_HEADS = 4
B_HEADS = 4
B_DK = 64
B_DV = 128
B_KDIM = B_HEADS * B_DK
B_VDIM = B_HEADS * B_DV
GLA_RANK = 16
GLA_TAU = 16.0
C_DIM = 256
N_EXPERTS = 32
N_GROUPS = 4
EXPERTS_PER_GROUP = N_EXPERTS // N_GROUPS
TOP_K = 2
EPS = 1e-6

LANES = 128
SUBLANES = 8
OFF_AU, OFF_AV, OFF_Q, OFF_K, OFF_V = 0, 256, 512, 768, 1024
OFF_OG, OFF_CB, OFF_CC, OFF_CX, OFF_GLR = 1536, 2048, 2304, 2560, 2816
P_PAD = OFF_GLR + LANES

MIX_ROWS = 512
MOVE_ROWS = 512
FFN_ROWS = 512
VMEM_LIMIT = 56 * 1024 * 1024

_NT = (((1,), (1,)), ((), ()))
_TN = (((0,), (0,)), ((), ()))


def _dot(a, b, dims=None):
    if dims is None:
        return jnp.dot(a, b, preferred_element_type=jnp.float32)
    return lax.dot_general(a, b, dims, preferred_element_type=jnp.float32)


def _bf(x):
    return x.astype(jnp.bfloat16)


def _split_bf16(x):
    hi = _bf(x)
    lo = _bf(x - hi.astype(jnp.float32))
    return hi, lo


def _rms(x):
    return x * lax.rsqrt(jnp.mean(x * x, axis=-1, keepdims=True) + EPS)


def _row_copy(src_ref, src_row, dst_ref, dst_row, sem):
    return pltpu.make_async_copy(src_ref.at[pl.ds(src_row, 1)], dst_ref.at[pl.ds(dst_row, 1)], sem)


def _mod_kernel(c_ref, w_ref, b_ref, o_ref):
    c = c_ref[...]
    s = c * jax.nn.sigmoid(c)
    s_hi, s_lo = _split_bf16(s)
    w_hi, w_lo = _split_bf16(w_ref[...])
    acc = _dot(s_hi, w_hi) + _dot(s_hi, w_lo) + _dot(s_lo, w_hi)
    o_ref[...] = acc + b_ref[...]


def _modulation(c, w_ada, b_ada):
    depth, d, six_d = w_ada.shape
    bn = c.shape[0]
    cb = 1024
    return pl.pallas_call(
        _mod_kernel,
        grid=(depth, six_d // cb),
        in_specs=[
            pl.BlockSpec((bn, d), lambda l, j: (0, 0)),
            pl.BlockSpec((None, d, cb), lambda l, j: (l, 0, j)),
            pl.BlockSpec((None, 1, cb), lambda l, j: (l, 0, j)),
        ],
        out_specs=pl.BlockSpec((None, bn, cb), lambda l, j: (l, 0, j)),
        out_shape=jax.ShapeDtypeStruct((depth, bn, six_d), jnp.float32),
        compiler_params=pltpu.CompilerParams(
            dimension_semantics=("arbitrary", "arbitrary"), vmem_limit_bytes=VMEM_LIMIT),
        name="adaln_mod",
    )(c, w_ada, b_ada.reshape(depth, 1, six_d))


def _route(h2, rwt_ref, rb_ref, carry, cap):
    tr = h2.shape[0]
    h_hi, h_lo = _split_bf16(h2)
    w_hi, w_lo = _split_bf16(rwt_ref[...])
    logits = (_dot(w_hi, h_hi, _NT) + _dot(w_hi, h_lo, _NT) + _dot(w_lo, h_hi, _NT)) + rb_ref[...]
    ex = jnp.exp(logits - jnp.max(logits, axis=0, keepdims=True))
    probs = ex / jnp.sum(ex, axis=0, keepdims=True)

    idx8 = lax.broadcasted_iota(jnp.int32, (EXPERTS_PER_GROUP, tr), 0)
    best = None
    for g in range(N_GROUPS):
        pg = probs[g * EXPERTS_PER_GROUP:(g + 1) * EXPERTS_PER_GROUP]
        m1 = jnp.max(pg, axis=0, keepdims=True)
        i1 = jnp.min(jnp.where(pg == m1, idx8, EXPERTS_PER_GROUP), axis=0, keepdims=True)
        pg2 = jnp.where(idx8 == i1, -1.0, pg)
        m2 = jnp.max(pg2, axis=0, keepdims=True)
        i2 = jnp.min(jnp.where(pg2 == m2, idx8, EXPERTS_PER_GROUP), axis=0, keepdims=True)
        cand = (m1 + m2, m1, m2, i1 + g * EXPERTS_PER_GROUP, i2 + g * EXPERTS_PER_GROUP)
        if best is None:
            best = cand
        else:
            better = cand[0] > best[0]
            best = tuple(jnp.where(better, a, b) for a, b in zip(cand, best))
    _, p1, p2, e0, e1 = best
    denom = p1 + p2
    w0, w1 = p1 / denom, p2 / denom

    eidx = lax.broadcasted_iota(jnp.int32, (N_EXPERTS, tr), 0)
    hit0, hit1 = eidx == e0, eidx == e1
    onehot = jnp.where(jnp.logical_or(hit0, hit1), 1.0, 0.0)
    before = (lax.broadcasted_iota(jnp.int32, (tr, tr), 0) < lax.broadcasted_iota(jnp.int32, (tr, tr), 1))
    rank = _dot(_bf(onehot), _bf(jnp.where(before, 1.0, 0.0))) + carry
    r0 = jnp.sum(jnp.where(hit0, rank, 0.0), axis=0, keepdims=True).astype(jnp.int32)
    r1 = jnp.sum(jnp.where(hit1, rank, 0.0), axis=0, keepdims=True).astype(jnp.int32)
    carry = carry + jnp.sum(onehot, axis=1, keepdims=True)

    eye = lax.broadcasted_iota(jnp.int32, (tr, tr), 0) == lax.broadcasted_iota(jnp.int32, (tr, tr), 1)
    w0c = jnp.sum(jnp.where(eye, jnp.broadcast_to(w0, (tr, tr)), 0.0), axis=1, keepdims=True)
    w1c = jnp.sum(jnp.where(eye, jnp.broadcast_to(w1, (tr, tr)), 0.0), axis=1, keepdims=True)
    lane8 = lax.broadcasted_iota(jnp.int32, (tr, 8), 1)
    wcol = jnp.where(lane8 == 0, w0c, jnp.where(lane8 == 1, w1c, 0.0))
    return e0 * cap + r0, e1 * cap + r1, wcol, carry


def _zero_segment_padding(cnt_ref, cap, zero_ref, xs_ref, sem):
    copies = []
    for ex in range(N_EXPERTS):
        cnt = cnt_ref[ex, 0]
        seg = ex * cap
        pos = seg + cnt
        end8 = seg + ((cnt + SUBLANES - 1) // SUBLANES) * SUBLANES
        for j in range(SUBLANES - 1):
            copies.append((pos + j < end8, _row_copy(zero_ref, 0, xs_ref, pos + j, sem)))
        seg_end = seg + ((cnt + FFN_ROWS - 1) // FFN_ROWS) * FFN_ROWS
        groups = (seg_end - end8) // SUBLANES
        at = end8
        bit = FFN_ROWS // (2 * SUBLANES)
        while bit >= 1:
            rows = bit * SUBLANES
            pred = (groups & bit) != 0
            copies.append((pred, pltpu.make_async_copy(
                zero_ref.at[pl.ds(0, rows)], xs_ref.at[pl.ds(pl.multiple_of(at, SUBLANES), rows)], sem)))
            at = at + jnp.where(pred, rows, 0)
            bit //= 2
    for pred, cp in copies:
        pl.when(pred)(cp.start)
    for pred, cp in copies:
        pl.when(pred)(cp.wait)


def _mixer_kernel(x_ref, mod_ref, g1_ref, win_ref, wout_ref, lng_ref, lnb_ref, ws_ref, bsm_ref,
                  wg_ref, bg_ref, glag_ref, cw_ref, cb_ref, g2_ref, rwt_ref, rb_ref,
                  o_ref, dest_ref, w_ref, cnt_ref, xs_ref,
                  h_ref, pa_ref, pb_ref, y0_ref, y1_ref, y2_ref, y3_ref, s_ref, zc_ref, wsm_ref, h2_ref, carry_ref, dvm_ref, d0s_ref, d1s_ref,
                  cvm_ref, csm_ref, zero_ref, sem, sem_s, sem_z, *, cap):
    bi, ti = pl.program_id(0), pl.program_id(1)
    tm = x_ref.shape[0]
    n_blk = tm // GMLP_BLOCK
    step = bi * pl.num_programs(1) + ti
    slot = step % 2
    spare_row0 = N_EXPERTS * cap

    @pl.when(step == 0)
    def _():
        carry_ref[...] = jnp.zeros_like(carry_ref)
        h2_ref[...] = jnp.zeros_like(h2_ref)

        def fill(i, c):
            d0s_ref[i] = spare_row0 + i
            d1s_ref[i] = spare_row0 + tm + i
            return c

        lax.fori_loop(0, tm, fill, 0)
        tt = lax.broadcasted_iota(jnp.int32, (GMLP_BLOCK, GMLP_BLOCK), 0) // CHUNK
        ss = lax.broadcasted_iota(jnp.int32, (GMLP_BLOCK, GMLP_BLOCK), 1) // CHUNK
        for h in range(A_HEADS):
            wsm_ref[h] = _bf(jnp.where(tt >= ss, ws_ref[h], 0.0))

    @pl.when(ti == 0)
    def _():
        s_ref[...] = jnp.zeros_like(s_ref)
        zc_ref[...] = jnp.zeros_like(zc_ref)

    m = mod_ref[...]
    sh1, sc1, gt1, sh2, sc2 = m[0:1], m[1:2], m[2:3], m[3:4], m[4:5]
    h_ref[...] = _bf((_rms(x_ref[...]) * g1_ref[...]) * (1.0 + sc1) + sh1)

    lane256 = lax.broadcasted_iota(jnp.int32, (CHUNK, B_KDIM), 1)
    head_of_lane = lane256 // B_DK
    r64 = lax.broadcasted_iota(jnp.int32, (CHUNK, CHUNK), 0)
    c64 = lax.broadcasted_iota(jnp.int32, (CHUNK, CHUNK), 1)
    tri = _bf(jnp.where(r64 >= c64, 1.0, 0.0))
    causal4 = jnp.concatenate([r64 >= c64] * B_HEADS, axis=0)
    eye256 = (lax.broadcasted_iota(jnp.int32, (B_KDIM, B_KDIM), 0)
              == lax.broadcasted_iota(jnp.int32, (B_KDIM, B_KDIM), 1))
    a_head_of_lane = lax.broadcasted_iota(jnp.int32, (GMLP_BLOCK, A_DIM), 1) // (A_DIM // A_HEADS)
    row128 = lax.broadcasted_iota(jnp.int32, (GMLP_BLOCK, C_DIM), 0)
    sqrt_half = 0.7071067811865476

    def gelu(v):
        return 0.5 * v * (1.0 + lax.erf(v * sqrt_half))

    p_halves = (pa_ref, pb_ref)
    y_blocks = (y0_ref, y1_ref, y2_ref, y3_ref)
    assert n_blk == len(y_blocks)

    def project(half):
        rows = slice(half * (tm // 2), (half + 1) * (tm // 2))
        p_halves[half][...] = _dot(h_ref[rows, :], win_ref[...])

    def block(j, state, zprev):
        p_ref, y_ref = p_halves[j // (n_blk // 2)], y_blocks[j]
        r0 = (j % (n_blk // 2)) * GMLP_BLOCK
        rows = slice(r0, r0 + GMLP_BLOCK)

        u = gelu(p_ref[rows, OFF_AU:OFF_AU + A_DIM])
        vv = gelu(p_ref[rows, OFF_AV:OFF_AV + A_DIM])
        mu = jnp.mean(vv, axis=-1, keepdims=True)
        var = jnp.mean((vv - mu) ** 2, axis=-1, keepdims=True)
        vn = _bf((vv - mu) * lax.rsqrt(var + EPS) * lng_ref[...] + lnb_ref[...])
        sv = jnp.zeros((GMLP_BLOCK, A_DIM), jnp.float32)
        for hh in range(A_HEADS):
            sv = jnp.where(a_head_of_lane == hh, _dot(wsm_ref[hh], vn), sv)
        y_ref[:, 0:A_DIM] = _bf(u * (sv + bsm_ref[...]))

        z = _dot(_bf(p_ref[rows, OFF_GLR:OFF_GLR + LANES]), wg_ref[...]) + bg_ref[...]
        glog = (jnp.minimum(z, 0.0) - jnp.log1p(jnp.exp(-jnp.abs(z)))) / GLA_TAU
        for c in range(GMLP_BLOCK // CHUNK):
            rc = slice(r0 + c * CHUNK, r0 + (c + 1) * CHUNK)
            g = glog[c * CHUNK:(c + 1) * CHUNK]
            g_hi = _bf(g)
            g_r1 = g - g_hi.astype(jnp.float32)
            g_mid = _bf(g_r1)
            g_lo = _bf(g_r1 - g_mid.astype(jnp.float32))
            b = _dot(tri, g_hi) + _dot(tri, g_mid) + _dot(tri, g_lo)
            b_mid = b[CHUNK // 2:CHUNK // 2 + 1]
            b_last = b[CHUNK - 1:CHUNK]
            q = p_ref[rc, OFF_Q:OFF_Q + B_KDIM] * (B_DK ** -0.5)
            k = p_ref[rc, OFF_K:OFF_K + B_KDIM]
            vb = _bf(p_ref[rc, OFF_V:OFF_V + B_VDIM])
            qs = q * jnp.exp(b - b_mid)
            ks = _bf(k * jnp.exp(b_mid - b))
            kd = _bf(k * jnp.exp(b_last - b))
            qb = q * jnp.exp(b)
            qs_st = _bf(jnp.concatenate(
                [jnp.where(head_of_lane == hh, qs, 0.0) for hh in range(B_HEADS)], axis=0))
            qb_st = _bf(jnp.concatenate(
                [jnp.where(head_of_lane == hh, qb, 0.0) for hh in range(B_HEADS)], axis=0))
            scores = _bf(jnp.where(causal4, _dot(qs_st, ks, _NT), 0.0))
            o_inter = _dot(qb_st, _bf(state))
            kv_all = _dot(kd, vb, _TN)
            outs, kvs = [], []
            for hh in range(B_HEADS):
                rs = slice(hh * CHUNK, (hh + 1) * CHUNK)
                cs = slice(hh * B_DV, (hh + 1) * B_DV)
                o_h = _dot(scores[rs], vb[:, cs]) + o_inter[rs]
                outs.append(_rms(o_h))
                kvs.append(kv_all[rs, cs])
            decay_row = jnp.broadcast_to(jnp.exp(b_last), (B_KDIM, B_KDIM))
            decay_col = jnp.sum(jnp.where(eye256, decay_row, 0.0), axis=1, keepdims=True)
            state = decay_col * state + jnp.concatenate(kvs, axis=0)
            on = jnp.concatenate(outs, axis=1) * glag_ref[...]
            og = p_ref[rc, OFF_OG:OFF_OG + B_VDIM]
            y_ref[c * CHUNK:(c + 1) * CHUNK, A_DIM:A_DIM + B_VDIM] = _bf(on * (og * jax.nn.sigmoid(og)))

        zz = p_ref[rows, OFF_CC:OFF_CC + C_DIM] * p_ref[rows, OFF_CX:OFF_CX + C_DIM]
        z1 = jnp.where(row128 == 0, zprev[7:8], pltpu.roll(zz, 1, 0))
        z2 = jnp.where(row128 == 0, zprev[6:7], jnp.where(row128 == 1, zprev[7:8], pltpu.roll(zz, 2, 0)))
        cw = cw_ref[...]
        yc = cb_ref[...] + cw[0:1] * z2
        yc = yc + cw[1:2] * z1
        yc = yc + cw[2:3] * zz
        y_ref[:, A_DIM + B_VDIM:A_DIM + B_VDIM + C_DIM] = _bf(p_ref[rows, OFF_CB:OFF_CB + C_DIM] * yc)
        return state, zz[GMLP_BLOCK - SUBLANES:GMLP_BLOCK]

    state, zprev, carry = s_ref[...], zc_ref[...], carry_ref[...]
    h2_prev = h2_ref.at[1 - slot]
    project(0)
    for j in range(n_blk):
        if j == 0:
            project(1)
        state, zprev = block(j, state, zprev)
        rows = slice(j * GMLP_BLOCK, (j + 1) * GMLP_BLOCK)
        xo = x_ref[rows, :] + gt1 * _dot(y_blocks[j][...], wout_ref[...])
        o_ref[rows, :] = xo
        h2 = (_rms(xo) * g2_ref[...]) * (1.0 + sc2) + sh2
        h2_ref[slot, rows, :] = h2
        d0, d1, wcol, carry = _route(h2, rwt_ref, rb_ref, carry, cap)
        dvm_ref[0:1, rows] = d0
        dvm_ref[1:2, rows] = d1
        w_ref[rows, :] = wcol
        for i in range(j * GMLP_BLOCK, (j + 1) * GMLP_BLOCK):
            _row_copy(h2_prev, i, xs_ref, d0s_ref[i], sem).start()
            _row_copy(h2_prev, i, xs_ref, d1s_ref[i], sem).start()
    s_ref[...] = state
    zc_ref[...] = zprev
    carry_ref[...] = carry
    cnt = jnp.broadcast_to(carry, cnt_ref.shape).astype(jnp.int32)
    cnt_ref[...] = cnt
    dest_ref[...] = dvm_ref[0:TOP_K, :]

    def drain(g, c):
        for _ in range(TOP_K * SUBLANES):
            _row_copy(h2_prev, 0, xs_ref, 0, sem).wait()
        return c

    lax.fori_loop(0, tm // SUBLANES, drain, 0)

    to_smem = [pltpu.make_async_copy(dvm_ref.at[0], d0s_ref, sem_s),
               pltpu.make_async_copy(dvm_ref.at[1], d1s_ref, sem_s)]
    for cp in to_smem:
        cp.start()
    for cp in to_smem:
        cp.wait()

    @pl.when(step == pl.num_programs(0) * pl.num_programs(1) - 1)
    def _():
        h2_last = h2_ref.at[slot]

        def issue(g, c):
            for jj in range(SUBLANES):
                i = pl.multiple_of(g * SUBLANES, SUBLANES) + jj
                _row_copy(h2_last, i, xs_ref, d0s_ref[i], sem).start()
                _row_copy(h2_last, i, xs_ref, d1s_ref[i], sem).start()
            return c

        lax.fori_loop(0, tm // SUBLANES, issue, 0)
        lax.fori_loop(0, tm // SUBLANES, drain, 0)
        cvm_ref[...] = cnt
        zero_ref[...] = jnp.zeros_like(zero_ref)
        cp = pltpu.make_async_copy(cvm_ref, csm_ref, sem_s)
        cp.start()
        cp.wait()
        _zero_segment_padding(csm_ref, cap, zero_ref, xs_ref, sem_z)


def _mixer(x, mod, g1, g2, w_in, w_out, ln_g, ln_b, w_s, b_s, w_gate, b_gate, gla_g, conv_w, conv_b,
           router_w, router_b):
    bn, t, d = x.shape
    n = bn * t
    tm = min(MIX_ROWS, t)
    nt = t // tm
    cap = n
    glr0 = 2 * A_DIM + 2 * B_KDIM + B_VDIM
    w_in_p = _bf(jnp.concatenate(
        [w_in[:, :glr0], w_in[:, glr0 + GLA_RANK:], w_in[:, glr0:glr0 + GLA_RANK],
         jnp.zeros((d, LANES - GLA_RANK), w_in.dtype)], axis=1))
    w_gate_p = _bf(jnp.concatenate(
        [w_gate, jnp.zeros((LANES - GLA_RANK, B_KDIM), w_gate.dtype)], axis=0))
    bsm = jnp.repeat(b_s.T, A_DIM // A_HEADS, axis=1)

    def whole(shape):
        return pl.BlockSpec(shape, lambda b, i: (0,) * len(shape))

    return pl.pallas_call(
        functools.partial(_mixer_kernel, cap=cap),
        grid=(bn, nt),
        in_specs=[
            pl.BlockSpec((None, tm, d), lambda b, i: (b, i, 0)),
            pl.BlockSpec((None, 6, d), lambda b, i: (b, 0, 0)),
            whole((1, d)),
            whole((d, P_PAD)),
            whole((d, d)),
            whole((1, A_DIM)),
            whole((1, A_DIM)),
            whole((A_HEADS, GMLP_BLOCK, GMLP_BLOCK)),
            whole((GMLP_BLOCK, A_DIM)),
            whole((LANES, B_KDIM)),
            whole((1, B_KDIM)),
            whole((1, B_VDIM)),
            whole((3, C_DIM)),
            whole((1, C_DIM)),
            whole((1, d)),
            whole((N_EXPERTS, d)),
            whole((N_EXPERTS, 1)),
        ],
        out_specs=[
            pl.BlockSpec((None, tm, d), lambda b, i: (b, i, 0)),
            pl.BlockSpec((TOP_K, tm), lambda b, i: (0, b * nt + i)),
            pl.BlockSpec((tm, 8), lambda b, i: (b * nt + i, 0)),
            whole((N_EXPERTS, LANES)),
            pl.BlockSpec(memory_space=pl.ANY),
        ],
        out_shape=[
            jax.ShapeDtypeStruct(x.shape, x.dtype),
            jax.ShapeDtypeStruct((TOP_K, n), jnp.int32),
            jax.ShapeDtypeStruct((n, 8), jnp.float32),
            jax.ShapeDtypeStruct((N_EXPERTS, LANES), jnp.int32),
            jax.ShapeDtypeStruct((N_EXPERTS * cap + TOP_K * tm, d), jnp.float32),
        ],
        scratch_shapes=[
            pltpu.VMEM((tm, d), jnp.bfloat16),
            pltpu.VMEM((tm // 2, P_PAD), jnp.float32),
            pltpu.VMEM((tm // 2, P_PAD), jnp.float32),
            pltpu.VMEM((GMLP_BLOCK, d), jnp.bfloat16),
            pltpu.VMEM((GMLP_BLOCK, d), jnp.bfloat16),
            pltpu.VMEM((GMLP_BLOCK, d), jnp.bfloat16),
            pltpu.VMEM((GMLP_BLOCK, d), jnp.bfloat16),
            pltpu.VMEM((B_KDIM, B_DV), jnp.float32),
            pltpu.VMEM((SUBLANES, C_DIM), jnp.float32),
            pltpu.VMEM((A_HEADS, GMLP_BLOCK, GMLP_BLOCK), jnp.bfloat16),
            pltpu.VMEM((2, tm, d), jnp.float32),
            pltpu.VMEM((N_EXPERTS, 1), jnp.float32),
            pltpu.VMEM((SUBLANES, tm), jnp.int32),
            pltpu.SMEM((tm,), jnp.int32),
            pltpu.SMEM((tm,), jnp.int32),
            pltpu.VMEM((N_EXPERTS, LANES), jnp.int32),
            pltpu.SMEM((N_EXPERTS, LANES), jnp.int32),
            pltpu.VMEM((FFN_ROWS // 2, d), jnp.float32),
            pltpu.SemaphoreType.DMA,
            pltpu.SemaphoreType.DMA,
            pltpu.SemaphoreType.DMA,
        ],
        compiler_params=pltpu.CompilerParams(
            dimension_semantics=("arbitrary", "arbitrary"), vmem_limit_bytes=VMEM_LIMIT),
        name="mixer",
    )(x, mod, g1.reshape(1, d), w_in_p, _bf(w_out), ln_g.reshape(1, -1), ln_b.reshape(1, -1), w_s, bsm,
      w_gate_p, b_gate.reshape(1, -1), gla_g.reshape(1, -1), conv_w, conv_b.reshape(1, -1),
      g2.reshape(1, d), router_w.T, router_b.reshape(N_EXPERTS, 1))


def _ffn_kernel(be_ref, rb_ref, nb_ref, xs_ref, w1_ref, w3_ref, w2_ref, ys_ref, w1b_ref, w3b_ref, w2b_ref):
    del rb_ref
    i = pl.program_id(0)
    used = i < nb_ref[0]
    new_expert = jnp.logical_or(i == 0, be_ref[i] != be_ref[jnp.maximum(i - 1, 0)])

    @pl.when(jnp.logical_and(used, new_expert))
    def _():
        w1b_ref[...] = _bf(w1_ref[...])
        w3b_ref[...] = _bf(w3_ref[...])
        w2b_ref[...] = _bf(w2_ref[...])

    @pl.when(used)
    def _():
        xb = _bf(xs_ref[...])
        a = _dot(xb, w1b_ref[...])
        g = _dot(xb, w3b_ref[...])
        hm = _bf((a * jax.nn.sigmoid(a)) * g)
        ys_ref[...] = _dot(hm, w2b_ref[...])


def _expert_ffn(layer, block_e, row_block, n_used, xs, w1, w3, w2):
    n_rows, d = xs.shape
    n_blocks = block_e.shape[0]
    f = w1.shape[-1]
    return pl.pallas_call(
        _ffn_kernel,
        grid_spec=pltpu.PrefetchScalarGridSpec(
            num_scalar_prefetch=3,
            grid=(n_blocks,),
            in_specs=[
                pl.BlockSpec((FFN_ROWS, d), lambda i, be, rb, nb: (rb[i], 0)),
                pl.BlockSpec((None, None, d, f), lambda i, be, rb, nb: (layer, be[i], 0, 0)),
                pl.BlockSpec((None, None, d, f), lambda i, be, rb, nb: (layer, be[i], 0, 0)),
                pl.BlockSpec((None, None, f, d), lambda i, be, rb, nb: (layer, be[i], 0, 0)),
            ],
            out_specs=pl.BlockSpec((FFN_ROWS, d), lambda i, be, rb, nb: (rb[i], 0)),
            scratch_shapes=[pltpu.VMEM((d, f), jnp.bfloat16), pltpu.VMEM((d, f), jnp.bfloat16),
                            pltpu.VMEM((f, d), jnp.bfloat16)],
        ),
        out_shape=jax.ShapeDtypeStruct((n_rows, d), jnp.float32),
        compiler_params=pltpu.CompilerParams(
            dimension_semantics=("arbitrary",), vmem_limit_bytes=VMEM_LIMIT),
        name="expert_ffn",
    )(block_e, row_block, n_used, xs, w1, w3, w2)


def _combine_kernel(d0_ref, d1_ref, x_ref, mod_ref, w_ref, gf_ref, ys_ref, o_ref, ybuf, sem, *, final_norm):
    tm = x_ref.shape[0]

    def issue(g, c):
        for jj in range(SUBLANES):
            i = pl.multiple_of(g * SUBLANES, SUBLANES) + jj
            _row_copy(ys_ref, d0_ref[i], ybuf.at[0], i, sem).start()
            _row_copy(ys_ref, d1_ref[i], ybuf.at[1], i, sem).start()
        return c

    lax.fori_loop(0, tm // SUBLANES, issue, 0)

    def drain(g, c):
        for _ in range(TOP_K * SUBLANES):
            _row_copy(ys_ref, 0, ybuf.at[0], 0, sem).wait()
        return c

    lax.fori_loop(0, tm // SUBLANES, drain, 0)

    w = w_ref[...]
    gt2 = mod_ref[...][5:6]
    out = x_ref[...] + gt2 * (w[:, 0:1] * ybuf[0] + w[:, 1:2] * ybuf[1])
    if final_norm:
        out = _rms(out) * gf_ref[...]
    o_ref[...] = out


def _combine(d0, d1, x, mod, wcol, g_final, ys, final_norm):
    bn, t, d = x.shape
    tm = min(MOVE_ROWS, t)
    nt = t // tm
    return pl.pallas_call(
        functools.partial(_combine_kernel, final_norm=final_norm),
        grid=(bn, nt),
        in_specs=[
            pl.BlockSpec((tm,), lambda b, i: (b * nt + i,), memory_space=pltpu.SMEM),
            pl.BlockSpec((tm,), lambda b, i: (b * nt + i,), memory_space=pltpu.SMEM),
            pl.BlockSpec((None, tm, d), lambda b, i: (b, i, 0)),
            pl.BlockSpec((None, 6, d), lambda b, i: (b, 0, 0)),
            pl.BlockSpec((tm, 8), lambda b, i: (b * nt + i, 0)),
            pl.BlockSpec((1, d), lambda b, i: (0, 0)),
            pl.BlockSpec(memory_space=pl.ANY),
        ],
        out_specs=pl.BlockSpec((None, tm, d), lambda b, i: (b, i, 0)),
        out_shape=jax.ShapeDtypeStruct(x.shape, x.dtype),
        scratch_shapes=[pltpu.VMEM((TOP_K, tm, d), jnp.float32), pltpu.SemaphoreType.DMA],
        compiler_params=pltpu.CompilerParams(
            dimension_semantics=("arbitrary", "arbitrary"), vmem_limit_bytes=VMEM_LIMIT),
        name="combine",
    )(d0, d1, x, mod, wcol, g_final.reshape(1, d), ys)


def _block_map(counts, cap, n_blocks):
    blocks_per_expert = (counts + FFN_ROWS - 1) // FFN_ROWS
    ends = jnp.cumsum(blocks_per_expert)
    n_used = ends[-1:]
    step = jnp.minimum(jnp.arange(n_blocks, dtype=jnp.int32), n_used - 1)
    block_e = jnp.minimum(jnp.sum((ends[None, :] <= step[:, None]).astype(jnp.int32), axis=1), N_EXPERTS - 1)
    first = (ends - blocks_per_expert)[block_e]
    row_block = block_e * (cap // FFN_ROWS) + (step - first)
    return block_e.astype(jnp.int32), row_block.astype(jnp.int32), n_used.astype(jnp.int32)


def kernel(x, c, w_ada, b_ada, g_norm1, g_norm2, w_in, w_out, gmlp_ln_g, gmlp_ln_b, gmlp_ws, gmlp_bs,
           gla_w_gate, gla_b_gate, gla_norm_g, conv_w, conv_b, router_w, router_b, exp_w1, exp_w3, exp_w2,
           g_final):
    depth = w_ada.shape[0]
    bn, t, d = x.shape
    n = bn * t
    n_blocks = -(-(n * TOP_K) // FFN_ROWS) + N_EXPERTS
    mod_all = _modulation(c, w_ada, b_ada).reshape(depth, bn, 6, d)
    for l in range(depth):
        mod = mod_all[l]
        x, dest, wcol, cnt, xs = _mixer(
            x, mod, g_norm1[l], g_norm2[l], w_in[l], w_out[l], gmlp_ln_g[l], gmlp_ln_b[l], gmlp_ws[l],
            gmlp_bs[l], gla_w_gate[l], gla_b_gate[l], gla_norm_g[l], conv_w[l], conv_b[l], router_w, router_b)
        block_e, row_block, n_used = _block_map(cnt[:, 0], n, n_blocks)
        ys = _expert_ffn(l, block_e, row_block, n_used, xs, exp_w1, exp_w3, exp_w2)
        x = _combine(dest[0], dest[1], x, mod, wcol, g_final, ys, final_norm=(l == depth - 1))
    return x
```

```python
import functools

import jax
import jax.numpy as jnp
from jax import lax
from jax.experimental import pallas as pl
from jax.experimental.pallas import tpu as pltpu

CHUNK = 64
GMLP_BLOCK = 128
A_DIM = 256
A_HEADS = 4
B_HEADS = 4
B_DK = 64
B_DV = 128
B_KDIM = B_HEADS * B_DK
B_VDIM = B_HEADS * B_DV
GLA_RANK = 16
GLA_TAU = 16.0
C_DIM = 256
N_EXPERTS = 32
N_GROUPS = 4
EXPERTS_PER_GROUP = N_EXPERTS // N_GROUPS
TOP_K = 2
EPS = 1e-6

LANES = 128
SUBLANES = 8
OFF_AU, OFF_AV, OFF_Q, OFF_K, OFF_V = 0, 256, 512, 768, 1024
OFF_OG, OFF_CB, OFF_CC, OFF_CX, OFF_GLR = 1536, 2048, 2304, 2560, 2816
P_PAD = OFF_GLR + LANES

MIX_ROWS = 512
MOVE_ROWS = 512
FFN_ROWS = 512
VMEM_LIMIT = 56 * 1024 * 1024

_NT = (((1,), (1,)), ((), ()))
_TN = (((0,), (0,)), ((), ()))


def _dot(a, b, dims=None):
    if dims is None:
        return jnp.dot(a, b, preferred_element_type=jnp.float32)
    return lax.dot_general(a, b, dims, preferred_element_type=jnp.float32)


def _bf(x):
    return x.astype(jnp.bfloat16)


def _split_bf16(x):
    hi = _bf(x)
    lo = _bf(x - hi.astype(jnp.float32))
    return hi, lo


def _rms(x):
    return x * lax.rsqrt(jnp.mean(x * x, axis=-1, keepdims=True) + EPS)


def _pack_rows(v):
    half = v.shape[1] // 2
    hi = pltpu.bitcast(_bf(v[:, :half]).astype(jnp.float32), jnp.uint32)
    lo = pltpu.bitcast(_bf(v[:, half:]).astype(jnp.float32), jnp.uint32)
    return hi | (lo >> 16)


def _unpack_rows(w):
    hi = pltpu.bitcast(w & jnp.uint32(0xFFFF0000), jnp.float32)
    lo = pltpu.bitcast(w << 16, jnp.float32)
    return _bf(hi), _bf(lo)


def _row_copy(src_ref, src_row, dst_ref, dst_row, sem):
    return pltpu.make_async_copy(src_ref.at[pl.ds(src_row, 1)], dst_ref.at[pl.ds(dst_row, 1)], sem)


def _mod_kernel(c_ref, w_ref, b_ref, o_ref):
    c = c_ref[...]
    s = c * jax.nn.sigmoid(c)
    s_hi, s_lo = _split_bf16(s)
    w_hi, w_lo = _split_bf16(w_ref[...])
    acc = _dot(s_hi, w_hi) + _dot(s_hi, w_lo) + _dot(s_lo, w_hi)
    o_ref[...] = acc + b_ref[...]


def _modulation(c, w_ada, b_ada):
    depth, d, six_d = w_ada.shape
    bn = c.shape[0]
    cb = 1024
    return pl.pallas_call(
        _mod_kernel,
        grid=(depth, six_d // cb),
        in_specs=[
            pl.BlockSpec((bn, d), lambda l, j: (0, 0)),
            pl.BlockSpec((None, d, cb), lambda l, j: (l, 0, j)),
            pl.BlockSpec((None, 1, cb), lambda l, j: (l, 0, j)),
        ],
        out_specs=pl.BlockSpec((None, bn, cb), lambda l, j: (l, 0, j)),
        out_shape=jax.ShapeDtypeStruct((depth, bn, six_d), jnp.float32),
        compiler_params=pltpu.CompilerParams(
            dimension_semantics=("arbitrary", "arbitrary"), vmem_limit_bytes=VMEM_LIMIT),
        name="adaln_mod",
    )(c, w_ada, b_ada.reshape(depth, 1, six_d))


def _route(h2, rwt_ref, rb_ref, carry, cap):
    tr = h2.shape[0]
    h_hi, h_lo = _split_bf16(h2)
    w_hi, w_lo = _split_bf16(rwt_ref[...])
    logits = (_dot(w_hi, h_hi, _NT) + _dot(w_hi, h_lo, _NT) + _dot(w_lo, h_hi, _NT)) + rb_ref[...]
    ex = jnp.exp(logits - jnp.max(logits, axis=0, keepdims=True))
    probs = ex / jnp.sum(ex, axis=0, keepdims=True)

    idx8 = lax.broadcasted_iota(jnp.int32, (EXPERTS_PER_GROUP, tr), 0)
    best = None
    for g in range(N_GROUPS):
        pg = probs[g * EXPERTS_PER_GROUP:(g + 1) * EXPERTS_PER_GROUP]
        m1 = jnp.max(pg, axis=0, keepdims=True)
        i1 = jnp.min(jnp.where(pg == m1, idx8, EXPERTS_PER_GROUP), axis=0, keepdims=True)
        pg2 = jnp.where(idx8 == i1, -1.0, pg)
        m2 = jnp.max(pg2, axis=0, keepdims=True)
        i2 = jnp.min(jnp.where(pg2 == m2, idx8, EXPERTS_PER_GROUP), axis=0, keepdims=True)
        cand = (m1 + m2, m1, m2, i1 + g * EXPERTS_PER_GROUP, i2 + g * EXPERTS_PER_GROUP)
        if best is None:
            best = cand
        else:
            better = cand[0] > best[0]
            best = tuple(jnp.where(better, a, b) for a, b in zip(cand, best))
    _, p1, p2, e0, e1 = best
    denom = p1 + p2
    w0, w1 = p1 / denom, p2 / denom

    eidx = lax.broadcasted_iota(jnp.int32, (N_EXPERTS, tr), 0)
    hit0, hit1 = eidx == e0, eidx == e1
    onehot = jnp.where(jnp.logical_or(hit0, hit1), 1.0, 0.0)
    before = (lax.broadcasted_iota(jnp.int32, (tr, tr), 0) < lax.broadcasted_iota(jnp.int32, (tr, tr), 1))
    rank = _dot(_bf(onehot), _bf(jnp.where(before, 1.0, 0.0))) + carry
    r0 = jnp.sum(jnp.where(hit0, rank, 0.0), axis=0, keepdims=True).astype(jnp.int32)
    r1 = jnp.sum(jnp.where(hit1, rank, 0.0), axis=0, keepdims=True).astype(jnp.int32)
    carry = carry + jnp.sum(onehot, axis=1, keepdims=True)

    eye = lax.broadcasted_iota(jnp.int32, (tr, tr), 0) == lax.broadcasted_iota(jnp.int32, (tr, tr), 1)
    w0c = jnp.sum(jnp.where(eye, jnp.broadcast_to(w0, (tr, tr)), 0.0), axis=1, keepdims=True)
    w1c = jnp.sum(jnp.where(eye, jnp.broadcast_to(w1, (tr, tr)), 0.0), axis=1, keepdims=True)
    lane8 = lax.broadcasted_iota(jnp.int32, (tr, 8), 1)
    wcol = jnp.where(lane8 == 0, w0c, jnp.where(lane8 == 1, w1c, 0.0))
    return e0 * cap + r0, e1 * cap + r1, wcol, carry


def _zero_segment_padding(cnt_ref, cap, zero_ref, xs_ref, sem):
    copies = []
    for ex in range(N_EXPERTS):
        cnt = cnt_ref[ex, 0]
        seg = ex * cap
        pos = seg + cnt
        end8 = seg + ((cnt + SUBLANES - 1) // SUBLANES) * SUBLANES
        for j in range(SUBLANES - 1):
            copies.append((pos + j < end8, _row_copy(zero_ref, 0, xs_ref, pos + j, sem)))
        seg_end = seg + ((cnt + FFN_ROWS - 1) // FFN_ROWS) * FFN_ROWS
        groups = (seg_end - end8) // SUBLANES
        at = end8
        bit = FFN_ROWS // (2 * SUBLANES)
        while bit >= 1:
            rows = bit * SUBLANES
            pred = (groups & bit) != 0
            copies.append((pred, pltpu.make_async_copy(
                zero_ref.at[pl.ds(0, rows)], xs_ref.at[pl.ds(pl.multiple_of(at, SUBLANES), rows)], sem)))
            at = at + jnp.where(pred, rows, 0)
            bit //= 2
    for pred, cp in copies:
        pl.when(pred)(cp.start)
    for pred, cp in copies:
        pl.when(pred)(cp.wait)


def _mixer_kernel(x_ref, mod_ref, g1_ref, win_ref, wout_ref, lng_ref, lnb_ref, ws_ref, bsm_ref,
                  wg_ref, bg_ref, glag_ref, cw_ref, cb_ref, g2_ref, rwt_ref, rb_ref,
                  o_ref, dest_ref, w_ref, cnt_ref, xs_ref,
                  h_ref, pa_ref, pb_ref, y0_ref, y1_ref, y2_ref, y3_ref, s_ref, zc_ref, wsm_ref, h2_ref,
                  carry_ref, dvm_ref, d0s_ref, d1s_ref, cvm_ref, csm_ref, zero_ref, sem, sem_s, sem_z, *, cap):
    bi, ti = pl.program_id(0), pl.program_id(1)
    tm = x_ref.shape[0]
    n_blk = tm // GMLP_BLOCK
    step = bi * pl.num_programs(1) + ti
    slot = step % 2
    spare_row0 = N_EXPERTS * cap

    @pl.when(step == 0)
    def _():
        carry_ref[...] = jnp.zeros_like(carry_ref)
        h2_ref[...] = jnp.zeros_like(h2_ref)

        def fill(i, c):
            d0s_ref[i] = spare_row0 + i
            d1s_ref[i] = spare_row0 + tm + i
            return c

        lax.fori_loop(0, tm, fill, 0)
        tt = lax.broadcasted_iota(jnp.int32, (GMLP_BLOCK, GMLP_BLOCK), 0) // CHUNK
        ss = lax.broadcasted_iota(jnp.int32, (GMLP_BLOCK, GMLP_BLOCK), 1) // CHUNK
        for h in range(A_HEADS):
            wsm_ref[h] = _bf(jnp.where(tt >= ss, ws_ref[h], 0.0))

    @pl.when(ti == 0)
    def _():
        s_ref[...] = jnp.zeros_like(s_ref)
        zc_ref[...] = jnp.zeros_like(zc_ref)

    m = mod_ref[...]
    sh1, sc1, gt1, sh2, sc2 = m[0:1], m[1:2], m[2:3], m[3:4], m[4:5]
    h_ref[...] = _bf((_rms(x_ref[...]) * g1_ref[...]) * (1.0 + sc1) + sh1)

    lane256 = lax.broadcasted_iota(jnp.int32, (CHUNK, B_KDIM), 1)
    head_of_lane = lane256 // B_DK
    r64 = lax.broadcasted_iota(jnp.int32, (CHUNK, CHUNK), 0)
    c64 = lax.broadcasted_iota(jnp.int32, (CHUNK, CHUNK), 1)
    tri = _bf(jnp.where(r64 >= c64, 1.0, 0.0))
    causal4 = jnp.concatenate([r64 >= c64] * B_HEADS, axis=0)
    eye256 = (lax.broadcasted_iota(jnp.int32, (B_KDIM, B_KDIM), 0)
              == lax.broadcasted_iota(jnp.int32, (B_KDIM, B_KDIM), 1))
    a_head_of_lane = lax.broadcasted_iota(jnp.int32, (GMLP_BLOCK, A_DIM), 1) // (A_DIM // A_HEADS)
    row128 = lax.broadcasted_iota(jnp.int32, (GMLP_BLOCK, C_DIM), 0)
    sqrt_half = 0.7071067811865476

    def gelu(v):
        return 0.5 * v * (1.0 + lax.erf(v * sqrt_half))

    p_halves = (pa_ref, pb_ref)
    y_blocks = (y0_ref, y1_ref, y2_ref, y3_ref)
    assert n_blk == len(y_blocks)

    def project(half):
        rows = slice(half * (tm // 2), (half + 1) * (tm // 2))
        p_halves[half][...] = _dot(h_ref[rows, :], win_ref[...])

    def block(j, state, zprev):
        p_ref, y_ref = p_halves[j // (n_blk // 2)], y_blocks[j]
        r0 = (j % (n_blk // 2)) * GMLP_BLOCK
        rows = slice(r0, r0 + GMLP_BLOCK)

        u = gelu(p_ref[rows, OFF_AU:OFF_AU + A_DIM])
        vv = gelu(p_ref[rows, OFF_AV:OFF_AV + A_DIM])
        mu = jnp.mean(vv, axis=-1, keepdims=True)
        var = jnp.mean((vv - mu) ** 2, axis=-1, keepdims=True)
        vn = _bf((vv - mu) * lax.rsqrt(var + EPS) * lng_ref[...] + lnb_ref[...])
        sv = jnp.zeros((GMLP_BLOCK, A_DIM), jnp.float32)
        for hh in range(A_HEADS):
            sv = jnp.where(a_head_of_lane == hh, _dot(wsm_ref[hh], vn), sv)
        y_ref[:, 0:A_DIM] = _bf(u * (sv + bsm_ref[...]))

        z = _dot(_bf(p_ref[rows, OFF_GLR:OFF_GLR + LANES]), wg_ref[...]) + bg_ref[...]
        glog = (jnp.minimum(z, 0.0) - jnp.log1p(jnp.exp(-jnp.abs(z)))) / GLA_TAU
        for c in range(GMLP_BLOCK // CHUNK):
            rc = slice(r0 + c * CHUNK, r0 + (c + 1) * CHUNK)
            g = glog[c * CHUNK:(c + 1) * CHUNK]
            g_hi = _bf(g)
            g_r1 = g - g_hi.astype(jnp.float32)
            g_mid = _bf(g_r1)
            g_lo = _bf(g_r1 - g_mid.astype(jnp.float32))
            b = _dot(tri, g_hi) + _dot(tri, g_mid) + _dot(tri, g_lo)
            b_mid = b[CHUNK // 2:CHUNK // 2 + 1]
            b_last = b[CHUNK - 1:CHUNK]
            q = p_ref[rc, OFF_Q:OFF_Q + B_KDIM] * (B_DK ** -0.5)
            k = p_ref[rc, OFF_K:OFF_K + B_KDIM]
            vb = _bf(p_ref[rc, OFF_V:OFF_V + B_VDIM])
            qs = q * jnp.exp(b - b_mid)
            ks = _bf(k * jnp.exp(b_mid - b))
            kd = _bf(k * jnp.exp(b_last - b))
            qb = q * jnp.exp(b)
            qs_st = _bf(jnp.concatenate(
                [jnp.where(head_of_lane == hh, qs, 0.0) for hh in range(B_HEADS)], axis=0))
            qb_st = _bf(jnp.concatenate(
                [jnp.where(head_of_lane == hh, qb, 0.0) for hh in range(B_HEADS)], axis=0))
            scores = _bf(jnp.where(causal4, _dot(qs_st, ks, _NT), 0.0))
            o_inter = _dot(qb_st, _bf(state))
            kv_all = _dot(kd, vb, _TN)
            outs, kvs = [], []
            for hh in range(B_HEADS):
                rs = slice(hh * CHUNK, (hh + 1) * CHUNK)
                cs = slice(hh * B_DV, (hh + 1) * B_DV)
                o_h = _dot(scores[rs], vb[:, cs]) + o_inter[rs]
                outs.append(_rms(o_h))
                kvs.append(kv_all[rs, cs])
            decay_row = jnp.broadcast_to(jnp.exp(b_last), (B_KDIM, B_KDIM))
            decay_col = jnp.sum(jnp.where(eye256, decay_row, 0.0), axis=1, keepdims=True)
            state = decay_col * state + jnp.concatenate(kvs, axis=0)
            on = jnp.concatenate(outs, axis=1) * glag_ref[...]
            og = p_ref[rc, OFF_OG:OFF_OG + B_VDIM]
            y_ref[c * CHUNK:(c + 1) * CHUNK, A_DIM:A_DIM + B_VDIM] = _bf(on * (og * jax.nn.sigmoid(og)))

        zz = p_ref[rows, OFF_CC:OFF_CC + C_DIM] * p_ref[rows, OFF_CX:OFF_CX + C_DIM]
        z1 = jnp.where(row128 == 0, zprev[7:8], pltpu.roll(zz, 1, 0))
        z2 = jnp.where(row128 == 0, zprev[6:7], jnp.where(row128 == 1, zprev[7:8], pltpu.roll(zz, 2, 0)))
        cw = cw_ref[...]
        yc = cb_ref[...] + cw[0:1] * z2
        yc = yc + cw[1:2] * z1
        yc = yc + cw[2:3] * zz
        y_ref[:, A_DIM + B_VDIM:A_DIM + B_VDIM + C_DIM] = _bf(p_ref[rows, OFF_CB:OFF_CB + C_DIM] * yc)
        return state, zz[GMLP_BLOCK - SUBLANES:GMLP_BLOCK]

    state, zprev, carry = s_ref[...], zc_ref[...], carry_ref[...]
    h2_prev = h2_ref.at[1 - slot]
    project(0)
    for j in range(n_blk):
        if j == 0:
            project(1)
        state, zprev = block(j, state, zprev)
        rows = slice(j * GMLP_BLOCK, (j + 1) * GMLP_BLOCK)
        xo = x_ref[rows, :] + gt1 * _dot(y_blocks[j][...], wout_ref[...])
        o_ref[rows, :] = xo
        h2 = (_rms(xo) * g2_ref[...]) * (1.0 + sc2) + sh2
        h2_ref[slot, rows, :] = _pack_rows(h2)
        d0, d1, wcol, carry = _route(h2, rwt_ref, rb_ref, carry, cap)
        dvm_ref[0:1, rows] = d0
        dvm_ref[1:2, rows] = d1
        w_ref[rows, :] = wcol
        for i in range(j * GMLP_BLOCK, (j + 1) * GMLP_BLOCK):
            _row_copy(h2_prev, i, xs_ref, d0s_ref[i], sem).start()
            _row_copy(h2_prev, i, xs_ref, d1s_ref[i], sem).start()
    s_ref[...] = state
    zc_ref[...] = zprev
    carry_ref[...] = carry
    cnt = jnp.broadcast_to(carry, cnt_ref.shape).astype(jnp.int32)
    cnt_ref[...] = cnt
    dest_ref[...] = dvm_ref[0:TOP_K, :]

    def drain(g, c):
        for _ in range(TOP_K * SUBLANES):
            _row_copy(h2_prev, 0, xs_ref, 0, sem).wait()
        return c

    lax.fori_loop(0, tm // SUBLANES, drain, 0)

    to_smem = [pltpu.make_async_copy(dvm_ref.at[0], d0s_ref, sem_s),
               pltpu.make_async_copy(dvm_ref.at[1], d1s_ref, sem_s)]
    for cp in to_smem:
        cp.start()
    for cp in to_smem:
        cp.wait()

    @pl.when(step == pl.num_programs(0) * pl.num_programs(1) - 1)
    def _():
        h2_last = h2_ref.at[slot]

        def issue(g, c):
            for jj in range(SUBLANES):
                i = pl.multiple_of(g * SUBLANES, SUBLANES) + jj
                _row_copy(h2_last, i, xs_ref, d0s_ref[i], sem).start()
                _row_copy(h2_last, i, xs_ref, d1s_ref[i], sem).start()
            return c

        lax.fori_loop(0, tm // SUBLANES, issue, 0)
        lax.fori_loop(0, tm // SUBLANES, drain, 0)
        cvm_ref[...] = cnt
        zero_ref[...] = jnp.zeros_like(zero_ref)
        cp = pltpu.make_async_copy(cvm_ref, csm_ref, sem_s)
        cp.start()
        cp.wait()
        _zero_segment_padding(csm_ref, cap, zero_ref, xs_ref, sem_z)


def _mixer(x, mod, g1, g2, w_in, w_out, ln_g, ln_b, w_s, b_s, w_gate, b_gate, gla_g, conv_w, conv_b,
           router_w, router_b):
    bn, t, d = x.shape
    n = bn * t
    tm = min(MIX_ROWS, t)
    nt = t // tm
    cap = n
    glr0 = 2 * A_DIM + 2 * B_KDIM + B_VDIM
    w_in_p = _bf(jnp.concatenate(
        [w_in[:, :glr0], w_in[:, glr0 + GLA_RANK:], w_in[:, glr0:glr0 + GLA_RANK],
         jnp.zeros((d, LANES - GLA_RANK), w_in.dtype)], axis=1))
    w_gate_p = _bf(jnp.concatenate(
        [w_gate, jnp.zeros((LANES - GLA_RANK, B_KDIM), w_gate.dtype)], axis=0))
    bsm = jnp.repeat(b_s.T, A_DIM // A_HEADS, axis=1)

    def whole(shape):
        return pl.BlockSpec(shape, lambda b, i: (0,) * len(shape))

    return pl.pallas_call(
        functools.partial(_mixer_kernel, cap=cap),
        grid=(bn, nt),
        in_specs=[
            pl.BlockSpec((None, tm, d), lambda b, i: (b, i, 0)),
            pl.BlockSpec((None, 6, d), lambda b, i: (b, 0, 0)),
            whole((1, d)),
            whole((d, P_PAD)),
            whole((d, d)),
            whole((1, A_DIM)),
            whole((1, A_DIM)),
            whole((A_HEADS, GMLP_BLOCK, GMLP_BLOCK)),
            whole((GMLP_BLOCK, A_DIM)),
            whole((LANES, B_KDIM)),
            whole((1, B_KDIM)),
            whole((1, B_VDIM)),
            whole((3, C_DIM)),
            whole((1, C_DIM)),
            whole((1, d)),
            whole((N_EXPERTS, d)),
            whole((N_EXPERTS, 1)),
        ],
        out_specs=[
            pl.BlockSpec((None, tm, d), lambda b, i: (b, i, 0)),
            pl.BlockSpec((TOP_K, tm), lambda b, i: (0, b * nt + i)),
            pl.BlockSpec((tm, 8), lambda b, i: (b * nt + i, 0)),
            whole((N_EXPERTS, LANES)),
            pl.BlockSpec(memory_space=pl.ANY),
        ],
        out_shape=[
            jax.ShapeDtypeStruct(x.shape, x.dtype),
            jax.ShapeDtypeStruct((TOP_K, n), jnp.int32),
            jax.ShapeDtypeStruct((n, 8), jnp.float32),
            jax.ShapeDtypeStruct((N_EXPERTS, LANES), jnp.int32),
            jax.ShapeDtypeStruct((N_EXPERTS * cap + TOP_K * tm, d // 2), jnp.uint32),
        ],
        scratch_shapes=[
            pltpu.VMEM((tm, d), jnp.bfloat16),
            pltpu.VMEM((tm // 2, P_PAD), jnp.float32),
            pltpu.VMEM((tm // 2, P_PAD), jnp.float32),
            pltpu.VMEM((GMLP_BLOCK, d), jnp.bfloat16),
            pltpu.VMEM((GMLP_BLOCK, d), jnp.bfloat16),
            pltpu.VMEM((GMLP_BLOCK, d), jnp.bfloat16),
            pltpu.VMEM((GMLP_BLOCK, d), jnp.bfloat16),
            pltpu.VMEM((B_KDIM, B_DV), jnp.float32),
            pltpu.VMEM((SUBLANES, C_DIM), jnp.float32),
            pltpu.VMEM((A_HEADS, GMLP_BLOCK, GMLP_BLOCK), jnp.bfloat16),
            pltpu.VMEM((2, tm, d // 2), jnp.uint32),
            pltpu.VMEM((N_EXPERTS, 1), jnp.float32),
            pltpu.VMEM((SUBLANES, tm), jnp.int32),
            pltpu.SMEM((tm,), jnp.int32),
            pltpu.SMEM((tm,), jnp.int32),
            pltpu.VMEM((N_EXPERTS, LANES), jnp.int32),
            pltpu.SMEM((N_EXPERTS, LANES), jnp.int32),
            pltpu.VMEM((FFN_ROWS // 2, d // 2), jnp.uint32),
            pltpu.SemaphoreType.DMA,
            pltpu.SemaphoreType.DMA,
            pltpu.SemaphoreType.DMA,
        ],
        compiler_params=pltpu.CompilerParams(
            dimension_semantics=("arbitrary", "arbitrary"), vmem_limit_bytes=VMEM_LIMIT),
        name="mixer",
    )(x, mod, g1.reshape(1, d), w_in_p, _bf(w_out), ln_g.reshape(1, -1), ln_b.reshape(1, -1), w_s, bsm,
      w_gate_p, b_gate.reshape(1, -1), gla_g.reshape(1, -1), conv_w, conv_b.reshape(1, -1),
      g2.reshape(1, d), router_w.T, router_b.reshape(N_EXPERTS, 1))


def _ffn_kernel(be_ref, rb_ref, nb_ref, xs_ref, w1_ref, w3_ref, w2_ref, ys_ref, w1b_ref, w3b_ref, w2b_ref):
    del rb_ref
    i = pl.program_id(0)
    used = i < nb_ref[0]
    new_expert = jnp.logical_or(i == 0, be_ref[i] != be_ref[jnp.maximum(i - 1, 0)])

    @pl.when(jnp.logical_and(used, new_expert))
    def _():
        w1b_ref[...] = _bf(w1_ref[...])
        w3b_ref[...] = _bf(w3_ref[...])
        w2b_ref[...] = _bf(w2_ref[...])

    @pl.when(used)
    def _():
        x_a, x_b = _unpack_rows(xs_ref[...])
        half = x_a.shape[1]
        a = _dot(x_a, w1b_ref[0:half, :]) + _dot(x_b, w1b_ref[half:, :])
        g = _dot(x_a, w3b_ref[0:half, :]) + _dot(x_b, w3b_ref[half:, :])
        hm = _bf((a * jax.nn.sigmoid(a)) * g)
        ys_ref[...] = _dot(hm, w2b_ref[...])


def _expert_ffn(layer, block_e, row_block, n_used, xs, w1, w3, w2):
    n_rows = xs.shape[0]
    n_blocks = block_e.shape[0]
    d, f = w1.shape[-2], w1.shape[-1]
    return pl.pallas_call(
        _ffn_kernel,
        grid_spec=pltpu.PrefetchScalarGridSpec(
            num_scalar_prefetch=3,
            grid=(n_blocks,),
            in_specs=[
                pl.BlockSpec((FFN_ROWS, d // 2), lambda i, be, rb, nb: (rb[i], 0)),
                pl.BlockSpec((None, None, d, f), lambda i, be, rb, nb: (layer, be[i], 0, 0)),
                pl.BlockSpec((None, None, d, f), lambda i, be, rb, nb: (layer, be[i], 0, 0)),
                pl.BlockSpec((None, None, f, d), lambda i, be, rb, nb: (layer, be[i], 0, 0)),
            ],
            out_specs=pl.BlockSpec((FFN_ROWS, d), lambda i, be, rb, nb: (rb[i], 0)),
            scratch_shapes=[pltpu.VMEM((d, f), jnp.bfloat16), pltpu.VMEM((d, f), jnp.bfloat16),
                            pltpu.VMEM((f, d), jnp.bfloat16)],
        ),
        out_shape=jax.ShapeDtypeStruct((n_rows, d), jnp.float32),
        compiler_params=pltpu.CompilerParams(
            dimension_semantics=("arbitrary",), vmem_limit_bytes=VMEM_LIMIT),
        name="expert_ffn",
    )(block_e, row_block, n_used, xs, w1, w3, w2)


def _combine_kernel(d0_ref, d1_ref, d0n_ref, d1n_ref, x_ref, mod_ref, w_ref, gf_ref, ys_ref, o_ref, ybuf, sems,
                    *, final_norm):
    tm = x_ref.shape[0]
    step = pl.program_id(0) * pl.num_programs(1) + pl.program_id(1)
    last = step == pl.num_programs(0) * pl.num_programs(1) - 1
    slot = step % 2

    def drain(sem):
        def body(g, c):
            for _ in range(TOP_K * SUBLANES):
                _row_copy(ys_ref, 0, ybuf.at[0, 0], 0, sem).wait()
            return c
        lax.fori_loop(0, tm // SUBLANES, body, 0)

    @pl.when(step == 0)
    def _():
        def issue(g, c):
            for jj in range(SUBLANES):
                i = pl.multiple_of(g * SUBLANES, SUBLANES) + jj
                _row_copy(ys_ref, d0_ref[i], ybuf.at[0, 0], i, sems.at[0]).start()
                _row_copy(ys_ref, d1_ref[i], ybuf.at[0, 1], i, sems.at[0]).start()
            return c
        lax.fori_loop(0, tm // SUBLANES, issue, 0)

    nxt, sem_nxt = ybuf.at[1 - slot], sems.at[1 - slot]
    for i in range(tm):
        _row_copy(ys_ref, d0n_ref[i], nxt.at[0], i, sem_nxt).start()
        _row_copy(ys_ref, d1n_ref[i], nxt.at[1], i, sem_nxt).start()

    drain(sems.at[slot])
    w = w_ref[...]
    gt2 = mod_ref[...][5:6]
    out = x_ref[...] + gt2 * (w[:, 0:1] * ybuf[slot, 0] + w[:, 1:2] * ybuf[slot, 1])
    if final_norm:
        out = _rms(out) * gf_ref[...]
    o_ref[...] = out

    @pl.when(last)
    def _():
        drain(sems.at[1 - slot])


def _combine(d0, d1, x, mod, wcol, g_final, ys, final_norm):
    bn, t, d = x.shape
    tm = min(MOVE_ROWS, t)
    nt = t // tm
    n_tiles = bn * nt

    def this_tile(b, i):
        return (b * nt + i,)

    def next_tile(b, i):
        return (jnp.minimum(b * nt + i + 1, n_tiles - 1),)

    return pl.pallas_call(
        functools.partial(_combine_kernel, final_norm=final_norm),
        grid=(bn, nt),
        in_specs=[
            pl.BlockSpec((tm,), this_tile, memory_space=pltpu.SMEM),
            pl.BlockSpec((tm,), this_tile, memory_space=pltpu.SMEM),
            pl.BlockSpec((tm,), next_tile, memory_space=pltpu.SMEM),
            pl.BlockSpec((tm,), next_tile, memory_space=pltpu.SMEM),
            pl.BlockSpec((None, tm, d), lambda b, i: (b, i, 0)),
            pl.BlockSpec((None, 6, d), lambda b, i: (b, 0, 0)),
            pl.BlockSpec((tm, 8), lambda b, i: (b * nt + i, 0)),
            pl.BlockSpec((1, d), lambda b, i: (0, 0)),
            pl.BlockSpec(memory_space=pl.ANY),
        ],
        out_specs=pl.BlockSpec((None, tm, d), lambda b, i: (b, i, 0)),
        out_shape=jax.ShapeDtypeStruct(x.shape, x.dtype),
        scratch_shapes=[pltpu.VMEM((2, TOP_K, tm, d), jnp.float32), pltpu.SemaphoreType.DMA((2,))],
        compiler_params=pltpu.CompilerParams(
            dimension_semantics=("arbitrary", "arbitrary"), vmem_limit_bytes=VMEM_LIMIT),
        name="combine",
    )(d0, d1, d0, d1, x, mod, wcol, g_final.reshape(1, d), ys)


def _block_map(counts, cap, n_blocks):
    blocks_per_expert = (counts + FFN_ROWS - 1) // FFN_ROWS
    ends = jnp.cumsum(blocks_per_expert)
    n_used = ends[-1:]
    step = jnp.minimum(jnp.arange(n_blocks, dtype=jnp.int32), n_used - 1)
    block_e = jnp.minimum(jnp.sum((ends[None, :] <= step[:, None]).astype(jnp.int32), axis=1), N_EXPERTS - 1)
    first = (ends - blocks_per_expert)[block_e]
    row_block = block_e * (cap // FFN_ROWS) + (step - first)
    return block_e.astype(jnp.int32), row_block.astype(jnp.int32), n_used.astype(jnp.int32)


def kernel(x, c, w_ada, b_ada, g_norm1, g_norm2, w_in, w_out, gmlp_ln_g, gmlp_ln_b, gmlp_ws, gmlp_bs,
           gla_w_gate, gla_b_gate, gla_norm_g, conv_w, conv_b, router_w, router_b, exp_w1, exp_w3, exp_w2,
           g_final):
    depth = w_ada.shape[0]
    bn, t, d = x.shape
    n = bn * t
    n_blocks = -(-(n * TOP_K) // FFN_ROWS) + N_EXPERTS
    mod_all = _modulation(c, w_ada, b_ada).reshape(depth, bn, 6, d)
    for l in range(depth):
        mod = mod_all[l]
        x, dest, wcol, cnt, xs = _mixer(
            x, mod, g_norm1[l], g_norm2[l], w_in[l], w_out[l], gmlp_ln_g[l], gmlp_ln_b[l], gmlp_ws[l],
            gmlp_bs[l], gla_w_gate[l], gla_b_gate[l], gla_norm_g[l], conv_w[l], conv_b[l], router_w, router_b)
        block_e, row_block, n_used = _block_map(cnt[:, 0], n, n_blocks)
        ys = _expert_ffn(l, block_e, row_block, n_used, xs, exp_w1, exp_w3, exp_w2)
        x = _combine(dest[0], dest[1], x, mod, wcol, g_final, ys, final_norm=(l == depth - 1))
    return x
```

```python
import functools

import jax
import jax.numpy as jnp
from jax import lax
from jax.experimental import pallas as pl
from jax.experimental.pallas import tpu as pltpu

CHUNK = 64
GMLP_BLOCK = 128
A_DIM = 256
A_HEADS = 4
B_HEADS = 4
B_DK = 64
B_DV = 128
B_KDIM = B_HEADS * B_DK
B_VDIM = B_HEADS * B_DV
GLA_RANK = 16
GLA_TAU = 16.0
C_DIM = 256
N_EXPERTS = 32
N_GROUPS = 4
EXPERTS_PER_GROUP = N_EXPERTS // N_GROUPS
TOP_K = 2
EPS = 1e-6

LANES = 128
SUBLANES = 8
OFF_AU, OFF_AV, OFF_Q, OFF_K, OFF_V = 0, 256, 512, 768, 1024
OFF_OG, OFF_CB, OFF_CC, OFF_CX, OFF_GLR = 1536, 2048, 2304, 2560, 2816
P_PAD = OFF_GLR + LANES

MIX_ROWS = 512
COPY_POINTS_PER_BLOCK = 10
MOVE_ROWS = 512
FFN_ROWS = 512
VMEM_LIMIT = 56 * 1024 * 1024

_NT = (((1,), (1,)), ((), ()))
_TN = (((0,), (0,)), ((), ()))


def _dot(a, b, dims=None):
    if dims is None:
        return jnp.dot(a, b, preferred_element_type=jnp.float32)
    return lax.dot_general(a, b, dims, preferred_element_type=jnp.float32)


def _bf(x):
    return x.astype(jnp.bfloat16)


def _split_bf16(x):
    hi = _bf(x)
    lo = _bf(x - hi.astype(jnp.float32))
    return hi, lo


def _rms(x):
    return x * lax.rsqrt(jnp.mean(x * x, axis=-1, keepdims=True) + EPS)


def _pack_rows(v):
    half = v.shape[1] // 2
    hi = pltpu.bitcast(_bf(v[:, :half]).astype(jnp.float32), jnp.uint32)
    lo = pltpu.bitcast(_bf(v[:, half:]).astype(jnp.float32), jnp.uint32)
    return hi | (lo >> 16)


def _unpack_rows(w):
    hi = pltpu.bitcast(w & jnp.uint32(0xFFFF0000), jnp.float32)
    lo = pltpu.bitcast(w << 16, jnp.float32)
    return _bf(hi), _bf(lo)


def _row_copy(src_ref, src_row, dst_ref, dst_row, sem):
    return pltpu.make_async_copy(src_ref.at[pl.ds(src_row, 1)], dst_ref.at[pl.ds(dst_row, 1)], sem)


def _mod_kernel(c_ref, w_ref, b_ref, o_ref):
    c = c_ref[...]
    s = c * jax.nn.sigmoid(c)
    s_hi, s_lo = _split_bf16(s)
    w_hi, w_lo = _split_bf16(w_ref[...])
    acc = _dot(s_hi, w_hi) + _dot(s_hi, w_lo) + _dot(s_lo, w_hi)
    o_ref[...] = acc + b_ref[...]


def _modulation(c, w_ada, b_ada):
    depth, d, six_d = w_ada.shape
    bn = c.shape[0]
    cb = 1024
    return pl.pallas_call(
        _mod_kernel,
        grid=(depth, six_d // cb),
        in_specs=[
            pl.BlockSpec((bn, d), lambda l, j: (0, 0)),
            pl.BlockSpec((None, d, cb), lambda l, j: (l, 0, j)),
            pl.BlockSpec((None, 1, cb), lambda l, j: (l, 0, j)),
        ],
        out_specs=pl.BlockSpec((None, bn, cb), lambda l, j: (l, 0, j)),
        out_shape=jax.ShapeDtypeStruct((depth, bn, six_d), jnp.float32),
        compiler_params=pltpu.CompilerParams(
            dimension_semantics=("arbitrary", "arbitrary"), vmem_limit_bytes=VMEM_LIMIT),
        name="adaln_mod",
    )(c, w_ada, b_ada.reshape(depth, 1, six_d))


def _route(h2, rwt_ref, rb_ref, carry, cap):
    tr = h2.shape[0]
    h_hi, h_lo = _split_bf16(h2)
    w_hi, w_lo = _split_bf16(rwt_ref[...])
    logits = (_dot(w_hi, h_hi, _NT) + _dot(w_hi, h_lo, _NT) + _dot(w_lo, h_hi, _NT)) + rb_ref[...]
    ex = jnp.exp(logits - jnp.max(logits, axis=0, keepdims=True))
    probs = ex / jnp.sum(ex, axis=0, keepdims=True)

    idx8 = lax.broadcasted_iota(jnp.int32, (EXPERTS_PER_GROUP, tr), 0)
    best = None
    for g in range(N_GROUPS):
        pg = probs[g * EXPERTS_PER_GROUP:(g + 1) * EXPERTS_PER_GROUP]
        m1 = jnp.max(pg, axis=0, keepdims=True)
        i1 = jnp.min(jnp.where(pg == m1, idx8, EXPERTS_PER_GROUP), axis=0, keepdims=True)
        pg2 = jnp.where(idx8 == i1, -1.0, pg)
        m2 = jnp.max(pg2, axis=0, keepdims=True)
        i2 = jnp.min(jnp.where(pg2 == m2, idx8, EXPERTS_PER_GROUP), axis=0, keepdims=True)
        cand = (m1 + m2, m1, m2, i1 + g * EXPERTS_PER_GROUP, i2 + g * EXPERTS_PER_GROUP)
        if best is None:
            best = cand
        else:
            better = cand[0] > best[0]
            best = tuple(jnp.where(better, a, b) for a, b in zip(cand, best))
    _, p1, p2, e0, e1 = best
    denom = p1 + p2
    w0, w1 = p1 / denom, p2 / denom

    eidx = lax.broadcasted_iota(jnp.int32, (N_EXPERTS, tr), 0)
    hit0, hit1 = eidx == e0, eidx == e1
    onehot = jnp.where(jnp.logical_or(hit0, hit1), 1.0, 0.0)
    before = (lax.broadcasted_iota(jnp.int32, (tr, tr), 0) < lax.broadcasted_iota(jnp.int32, (tr, tr), 1))
    rank = _dot(_bf(onehot), _bf(jnp.where(before, 1.0, 0.0))) + carry
    r0 = jnp.sum(jnp.where(hit0, rank, 0.0), axis=0, keepdims=True).astype(jnp.int32)
    r1 = jnp.sum(jnp.where(hit1, rank, 0.0), axis=0, keepdims=True).astype(jnp.int32)
    carry = carry + jnp.sum(onehot, axis=1, keepdims=True)

    eye = lax.broadcasted_iota(jnp.int32, (tr, tr), 0) == lax.broadcasted_iota(jnp.int32, (tr, tr), 1)
    w0c = jnp.sum(jnp.where(eye, jnp.broadcast_to(w0, (tr, tr)), 0.0), axis=1, keepdims=True)
    w1c = jnp.sum(jnp.where(eye, jnp.broadcast_to(w1, (tr, tr)), 0.0), axis=1, keepdims=True)
    lane8 = lax.broadcasted_iota(jnp.int32, (tr, 8), 1)
    wcol = jnp.where(lane8 == 0, w0c, jnp.where(lane8 == 1, w1c, 0.0))
    return e0 * cap + r0, e1 * cap + r1, wcol, carry


def _zero_segment_padding(cnt_ref, cap, zero_ref, xs_ref, sem):
    copies = []
    for ex in range(N_EXPERTS):
        cnt = cnt_ref[ex, 0]
        seg = ex * cap
        pos = seg + cnt
        end8 = seg + ((cnt + SUBLANES - 1) // SUBLANES) * SUBLANES
        for j in range(SUBLANES - 1):
            copies.append((pos + j < end8, _row_copy(zero_ref, 0, xs_ref, pos + j, sem)))
        seg_end = seg + ((cnt + FFN_ROWS - 1) // FFN_ROWS) * FFN_ROWS
        groups = (seg_end - end8) // SUBLANES
        at = end8
        bit = FFN_ROWS // (2 * SUBLANES)
        while bit >= 1:
            rows = bit * SUBLANES
            pred = (groups & bit) != 0
            copies.append((pred, pltpu.make_async_copy(
                zero_ref.at[pl.ds(0, rows)], xs_ref.at[pl.ds(pl.multiple_of(at, SUBLANES), rows)], sem)))
            at = at + jnp.where(pred, rows, 0)
            bit //= 2
    for pred, cp in copies:
        pl.when(pred)(cp.start)
    for pred, cp in copies:
        pl.when(pred)(cp.wait)


N_MIXER_INPUTS = 17
N_MIXER_OUTPUTS = 5


def _mixer_kernel(*refs, cap, fuse_prev):
    ins, rest = refs[:N_MIXER_INPUTS], refs[N_MIXER_INPUTS:]
    if fuse_prev:
        dp0_ref, dp1_ref, dp0n_ref, dp1n_ref, wp_ref, modp_ref, ysp_ref = rest[:7]
        rest = rest[7:]
    outs, scratch = rest[:N_MIXER_OUTPUTS], rest[N_MIXER_OUTPUTS:]
    (x_ref, mod_ref, g1_ref, win_ref, wout_ref, lng_ref, lnb_ref, ws_ref, bsm_ref,
     wg_ref, bg_ref, glag_ref, cw_ref, cb_ref, g2_ref, rwt_ref, rb_ref) = ins
    o_ref, dest_ref, w_ref, cnt_ref, xs_ref = outs
    (h_ref, pa_ref, pb_ref, y0_ref, y1_ref, y2_ref, y3_ref, s_ref, zc_ref, wsm_ref, h2_ref,
     carry_ref, dvm_ref, d0s_ref, d1s_ref, cvm_ref, csm_ref, zero_ref, sem, sem_s, sem_z,
     ybuf, xin_ref, sem_g) = scratch

    bi, ti = pl.program_id(0), pl.program_id(1)
    tm = x_ref.shape[0]
    n_blk = tm // GMLP_BLOCK
    step = bi * pl.num_programs(1) + ti
    last_step = step == pl.num_programs(0) * pl.num_programs(1) - 1
    slot = step % 2
    spare_row0 = N_EXPERTS * cap

    def drain_rows(src_ref, dst_ref, s):
        def body(g, c):
            for _ in range(TOP_K * SUBLANES):
                _row_copy(src_ref, 0, dst_ref, 0, s).wait()
            return c
        lax.fori_loop(0, tm // SUBLANES, body, 0)

    if fuse_prev:
        @pl.when(step == 0)
        def _():
            def issue(g, c):
                for jj in range(SUBLANES):
                    i = pl.multiple_of(g * SUBLANES, SUBLANES) + jj
                    _row_copy(ysp_ref, dp0_ref[i], ybuf.at[0, 0], i, sem_g.at[0]).start()
                    _row_copy(ysp_ref, dp1_ref[i], ybuf.at[0, 1], i, sem_g.at[0]).start()
                return c
            lax.fori_loop(0, tm // SUBLANES, issue, 0)

        drain_rows(ysp_ref, ybuf.at[0, 0], sem_g.at[slot])
        wp = wp_ref[...]
        xin_ref[...] = x_ref[...] + modp_ref[...][5:6] * (
            wp[:, 0:1] * ybuf[slot, 0] + wp[:, 1:2] * ybuf[slot, 1])
        x_in = xin_ref
    else:
        x_in = x_ref

    @pl.when(step == 0)
    def _():
        carry_ref[...] = jnp.zeros_like(carry_ref)
        h2_ref[...] = jnp.zeros_like(h2_ref)

        def fill(i, c):
            d0s_ref[i] = spare_row0 + i
            d1s_ref[i] = spare_row0 + tm + i
            return c

        lax.fori_loop(0, tm, fill, 0)
        tt = lax.broadcasted_iota(jnp.int32, (GMLP_BLOCK, GMLP_BLOCK), 0) // CHUNK
        ss = lax.broadcasted_iota(jnp.int32, (GMLP_BLOCK, GMLP_BLOCK), 1) // CHUNK
        for h in range(A_HEADS):
            wsm_ref[h] = _bf(jnp.where(tt >= ss, ws_ref[h], 0.0))

    @pl.when(ti == 0)
    def _():
        s_ref[...] = jnp.zeros_like(s_ref)
        zc_ref[...] = jnp.zeros_like(zc_ref)

    m = mod_ref[...]
    sh1, sc1, gt1, sh2, sc2 = m[0:1], m[1:2], m[2:3], m[3:4], m[4:5]
    h_ref[...] = _bf((_rms(x_in[...]) * g1_ref[...]) * (1.0 + sc1) + sh1)

    lane256 = lax.broadcasted_iota(jnp.int32, (CHUNK, B_KDIM), 1)
    head_of_lane = lane256 // B_DK
    r64 = lax.broadcasted_iota(jnp.int32, (CHUNK, CHUNK), 0)
    c64 = lax.broadcasted_iota(jnp.int32, (CHUNK, CHUNK), 1)
    tri = _bf(jnp.where(r64 >= c64, 1.0, 0.0))
    causal4 = jnp.concatenate([r64 >= c64] * B_HEADS, axis=0)
    eye256 = (lax.broadcasted_iota(jnp.int32, (B_KDIM, B_KDIM), 0)
              == lax.broadcasted_iota(jnp.int32, (B_KDIM, B_KDIM), 1))
    a_head_of_lane = lax.broadcasted_iota(jnp.int32, (GMLP_BLOCK, A_DIM), 1) // (A_DIM // A_HEADS)
    row128 = lax.broadcasted_iota(jnp.int32, (GMLP_BLOCK, C_DIM), 0)
    sqrt_half = 0.7071067811865476

    def gelu(v):
        return 0.5 * v * (1.0 + lax.erf(v * sqrt_half))

    h2_prev = h2_ref.at[1 - slot]
    row_copies = []
    for i in range(tm):
        row_copies.append(lambda i=i: _row_copy(h2_prev, i, xs_ref, d0s_ref[i], sem).start())
        row_copies.append(lambda i=i: _row_copy(h2_prev, i, xs_ref, d1s_ref[i], sem).start())
        if fuse_prev:
            row_copies.append(lambda i=i: _row_copy(
                ysp_ref, dp0n_ref[i], ybuf.at[1 - slot, 0], i, sem_g.at[1 - slot]).start())
            row_copies.append(lambda i=i: _row_copy(
                ysp_ref, dp1n_ref[i], ybuf.at[1 - slot, 1], i, sem_g.at[1 - slot]).start())
    row_copies.reverse()
    copies_per_point = -(-len(row_copies) // (n_blk * COPY_POINTS_PER_BLOCK))

    def start_row_copies():
        for _ in range(min(copies_per_point, len(row_copies))):
            row_copies.pop()()

    p_halves = (pa_ref, pb_ref)
    y_blocks = (y0_ref, y1_ref, y2_ref, y3_ref)
    assert n_blk == len(y_blocks)

    def project(half):
        rows = slice(half * (tm // 2), (half + 1) * (tm // 2))
        p_halves[half][...] = _dot(h_ref[rows, :], win_ref[...])

    def block(j, state, zprev):
        p_ref, y_ref = p_halves[j // (n_blk // 2)], y_blocks[j]
        r0 = (j % (n_blk // 2)) * GMLP_BLOCK
        rows = slice(r0, r0 + GMLP_BLOCK)

        u = gelu(p_ref[rows, OFF_AU:OFF_AU + A_DIM])
        vv = gelu(p_ref[rows, OFF_AV:OFF_AV + A_DIM])
        mu = jnp.mean(vv, axis=-1, keepdims=True)
        var = jnp.mean((vv - mu) ** 2, axis=-1, keepdims=True)
        vn = _bf((vv - mu) * lax.rsqrt(var + EPS) * lng_ref[...] + lnb_ref[...])
        sv = jnp.zeros((GMLP_BLOCK, A_DIM), jnp.float32)
        for hh in range(A_HEADS):
            sv = jnp.where(a_head_of_lane == hh, _dot(wsm_ref[hh], vn), sv)
        y_ref[:, 0:A_DIM] = _bf(u * (sv + bsm_ref[...]))
        start_row_copies()

        z = _dot(_bf(p_ref[rows, OFF_GLR:OFF_GLR + LANES]), wg_ref[...]) + bg_ref[...]
        glog = (jnp.minimum(z, 0.0) - jnp.log1p(jnp.exp(-jnp.abs(z)))) / GLA_TAU
        for c in range(GMLP_BLOCK // CHUNK):
            rc = slice(r0 + c * CHUNK, r0 + (c + 1) * CHUNK)
            g = glog[c * CHUNK:(c + 1) * CHUNK]
            g_hi = _bf(g)
            g_r1 = g - g_hi.astype(jnp.float32)
            g_mid = _bf(g_r1)
            g_lo = _bf(g_r1 - g_mid.astype(jnp.float32))
            b = _dot(tri, g_hi) + _dot(tri, g_mid) + _dot(tri, g_lo)
            b_mid = b[CHUNK // 2:CHUNK // 2 + 1]
            b_last = b[CHUNK - 1:CHUNK]
            q = p_ref[rc, OFF_Q:OFF_Q + B_KDIM] * (B_DK ** -0.5)
            k = p_ref[rc, OFF_K:OFF_K + B_KDIM]
            vb = _bf(p_ref[rc, OFF_V:OFF_V + B_VDIM])
            qs = q * jnp.exp(b - b_mid)
            ks = _bf(k * jnp.exp(b_mid - b))
            kd = _bf(k * jnp.exp(b_last - b))
            qb = q * jnp.exp(b)
            qs_st = _bf(jnp.concatenate(
                [jnp.where(head_of_lane == hh, qs, 0.0) for hh in range(B_HEADS)], axis=0))
            qb_st = _bf(jnp.concatenate(
                [jnp.where(head_of_lane == hh, qb, 0.0) for hh in range(B_HEADS)], axis=0))
            scores = _bf(jnp.where(causal4, _dot(qs_st, ks, _NT), 0.0))
            start_row_copies()
            o_inter = _dot(qb_st, _bf(state))
            kv_all = _dot(kd, vb, _TN)
            start_row_copies()
            outs, kvs = [], []
            for hh in range(B_HEADS):
                rs = slice(hh * CHUNK, (hh + 1) * CHUNK)
                cs = slice(hh * B_DV, (hh + 1) * B_DV)
                o_h = _dot(scores[rs], vb[:, cs]) + o_inter[rs]
                outs.append(_rms(o_h))
                kvs.append(kv_all[rs, cs])
            decay_row = jnp.broadcast_to(jnp.exp(b_last), (B_KDIM, B_KDIM))
            decay_col = jnp.sum(jnp.where(eye256, decay_row, 0.0), axis=1, keepdims=True)
            state = decay_col * state + jnp.concatenate(kvs, axis=0)
            on = jnp.concatenate(outs, axis=1) * glag_ref[...]
            og = p_ref[rc, OFF_OG:OFF_OG + B_VDIM]
            y_ref[c * CHUNK:(c + 1) * CHUNK, A_DIM:A_DIM + B_VDIM] = _bf(on * (og * jax.nn.sigmoid(og)))
            start_row_copies()

        zz = p_ref[rows, OFF_CC:OFF_CC + C_DIM] * p_ref[rows, OFF_CX:OFF_CX + C_DIM]
        z1 = jnp.where(row128 == 0, zprev[7:8], pltpu.roll(zz, 1, 0))
        z2 = jnp.where(row128 == 0, zprev[6:7], jnp.where(row128 == 1, zprev[7:8], pltpu.roll(zz, 2, 0)))
        cw = cw_ref[...]
        yc = cb_ref[...] + cw[0:1] * z2
        yc = yc + cw[1:2] * z1
        yc = yc + cw[2:3] * zz
        y_ref[:, A_DIM + B_VDIM:A_DIM + B_VDIM + C_DIM] = _bf(p_ref[rows, OFF_CB:OFF_CB + C_DIM] * yc)
        start_row_copies()
        return state, zz[GMLP_BLOCK - SUBLANES:GMLP_BLOCK]

    state, zprev, carry = s_ref[...], zc_ref[...], carry_ref[...]
    project(0)
    for j in range(n_blk):
        if j == 0:
            project(1)
        state, zprev = block(j, state, zprev)
        rows = slice(j * GMLP_BLOCK, (j + 1) * GMLP_BLOCK)
        xo = x_in[rows, :] + gt1 * _dot(y_blocks[j][...], wout_ref[...])
        o_ref[rows, :] = xo
        start_row_copies()
        h2 = (_rms(xo) * g2_ref[...]) * (1.0 + sc2) + sh2
        h2_ref[slot, rows, :] = _pack_rows(h2)
        d0, d1, wcol, carry = _route(h2, rwt_ref, rb_ref, carry, cap)
        dvm_ref[0:1, rows] = d0
        dvm_ref[1:2, rows] = d1
        w_ref[rows, :] = wcol
        start_row_copies()
    while row_copies:
        row_copies.pop()()
    s_ref[...] = state
    zc_ref[...] = zprev
    carry_ref[...] = carry
    cnt = jnp.broadcast_to(carry, cnt_ref.shape).astype(jnp.int32)
    cnt_ref[...] = cnt
    dest_ref[...] = dvm_ref[0:TOP_K, :]

    drain_rows(h2_prev, xs_ref, sem)

    to_smem = [pltpu.make_async_copy(dvm_ref.at[0], d0s_ref, sem_s),
               pltpu.make_async_copy(dvm_ref.at[1], d1s_ref, sem_s)]
    for cp in to_smem:
        cp.start()
    for cp in to_smem:
        cp.wait()

    @pl.when(last_step)
    def _():
        h2_last = h2_ref.at[slot]

        def issue(g, c):
            for jj in range(SUBLANES):
                i = pl.multiple_of(g * SUBLANES, SUBLANES) + jj
                _row_copy(h2_last, i, xs_ref, d0s_ref[i], sem).start()
                _row_copy(h2_last, i, xs_ref, d1s_ref[i], sem).start()
            return c

        lax.fori_loop(0, tm // SUBLANES, issue, 0)
        drain_rows(h2_last, xs_ref, sem)
        if fuse_prev:
            drain_rows(ysp_ref, ybuf.at[0, 0], sem_g.at[1 - slot])
        cvm_ref[...] = cnt
        zero_ref[...] = jnp.zeros_like(zero_ref)
        cp = pltpu.make_async_copy(cvm_ref, csm_ref, sem_s)
        cp.start()
        cp.wait()
        _zero_segment_padding(csm_ref, cap, zero_ref, xs_ref, sem_z)


def _mixer(x, mod, g1, g2, w_in, w_out, ln_g, ln_b, w_s, b_s, w_gate, b_gate, gla_g, conv_w, conv_b,
           router_w, router_b, prev=None):
    bn, t, d = x.shape
    n = bn * t
    tm = min(MIX_ROWS, t)
    nt = t // tm
    cap = n
    glr0 = 2 * A_DIM + 2 * B_KDIM + B_VDIM
    w_in_p = _bf(jnp.concatenate(
        [w_in[:, :glr0], w_in[:, glr0 + GLA_RANK:], w_in[:, glr0:glr0 + GLA_RANK],
         jnp.zeros((d, LANES - GLA_RANK), w_in.dtype)], axis=1))
    w_gate_p = _bf(jnp.concatenate(
        [w_gate, jnp.zeros((LANES - GLA_RANK, B_KDIM), w_gate.dtype)], axis=0))
    bsm = jnp.repeat(b_s.T, A_DIM // A_HEADS, axis=1)

    def whole(shape):
        return pl.BlockSpec(shape, lambda b, i: (0,) * len(shape))

    prev_specs, prev_args = [], ()
    if prev is not None:
        n_tiles = bn * nt

        def this_tile(b, i):
            return (b * nt + i,)

        def next_tile(b, i):
            return (jnp.minimum(b * nt + i + 1, n_tiles - 1),)

        d0p, d1p, wcolp, modp, ysp = prev
        prev_specs = [
            pl.BlockSpec((tm,), this_tile, memory_space=pltpu.SMEM),
            pl.BlockSpec((tm,), this_tile, memory_space=pltpu.SMEM),
            pl.BlockSpec((tm,), next_tile, memory_space=pltpu.SMEM),
            pl.BlockSpec((tm,), next_tile, memory_space=pltpu.SMEM),
            pl.BlockSpec((tm, 8), lambda b, i: (b * nt + i, 0)),
            pl.BlockSpec((None, 6, d), lambda b, i: (b, 0, 0)),
            pl.BlockSpec(memory_space=pl.ANY),
        ]
        prev_args = (d0p, d1p, d0p, d1p, wcolp, modp, ysp)

    return pl.pallas_call(
        functools.partial(_mixer_kernel, cap=cap, fuse_prev=prev is not None),
        grid=(bn, nt),
        in_specs=[
            pl.BlockSpec((None, tm, d), lambda b, i: (b, i, 0)),
            pl.BlockSpec((None, 6, d), lambda b, i: (b, 0, 0)),
            whole((1, d)),
            whole((d, P_PAD)),
            whole((d, d)),
            whole((1, A_DIM)),
            whole((1, A_DIM)),
            whole((A_HEADS, GMLP_BLOCK, GMLP_BLOCK)),
            whole((GMLP_BLOCK, A_DIM)),
            whole((LANES, B_KDIM)),
            whole((1, B_KDIM)),
            whole((1, B_VDIM)),
            whole((3, C_DIM)),
            whole((1, C_DIM)),
            whole((1, d)),
            whole((N_EXPERTS, d)),
            whole((N_EXPERTS, 1)),
        ] + prev_specs,
        out_specs=[
            pl.BlockSpec((None, tm, d), lambda b, i: (b, i, 0)),
            pl.BlockSpec((TOP_K, tm), lambda b, i: (0, b * nt + i)),
            pl.BlockSpec((tm, 8), lambda b, i: (b * nt + i, 0)),
            whole((N_EXPERTS, LANES)),
            pl.BlockSpec(memory_space=pl.ANY),
        ],
        out_shape=[
            jax.ShapeDtypeStruct(x.shape, x.dtype),
            jax.ShapeDtypeStruct((TOP_K, n), jnp.int32),
            jax.ShapeDtypeStruct((n, 8), jnp.float32),
            jax.ShapeDtypeStruct((N_EXPERTS, LANES), jnp.int32),
            jax.ShapeDtypeStruct((N_EXPERTS * cap + TOP_K * tm, d // 2), jnp.uint32),
        ],
        scratch_shapes=[
            pltpu.VMEM((tm, d), jnp.bfloat16),
            pltpu.VMEM((tm // 2, P_PAD), jnp.float32),
            pltpu.VMEM((tm // 2, P_PAD), jnp.float32),
            pltpu.VMEM((GMLP_BLOCK, d), jnp.bfloat16),
            pltpu.VMEM((GMLP_BLOCK, d), jnp.bfloat16),
            pltpu.VMEM((GMLP_BLOCK, d), jnp.bfloat16),
            pltpu.VMEM((GMLP_BLOCK, d), jnp.bfloat16),
            pltpu.VMEM((B_KDIM, B_DV), jnp.float32),
            pltpu.VMEM((SUBLANES, C_DIM), jnp.float32),
            pltpu.VMEM((A_HEADS, GMLP_BLOCK, GMLP_BLOCK), jnp.bfloat16),
            pltpu.VMEM((2, tm, d // 2), jnp.uint32),
            pltpu.VMEM((N_EXPERTS, 1), jnp.float32),
            pltpu.VMEM((SUBLANES, tm), jnp.int32),
            pltpu.SMEM((tm,), jnp.int32),
            pltpu.SMEM((tm,), jnp.int32),
            pltpu.VMEM((N_EXPERTS, LANES), jnp.int32),
            pltpu.SMEM((N_EXPERTS, LANES), jnp.int32),
            pltpu.VMEM((FFN_ROWS // 2, d // 2), jnp.uint32),
            pltpu.SemaphoreType.DMA,
            pltpu.SemaphoreType.DMA,
            pltpu.SemaphoreType.DMA,
            pltpu.VMEM((2, TOP_K, tm, d), jnp.float32),
            pltpu.VMEM((tm, d), jnp.float32),
            pltpu.SemaphoreType.DMA((2,)),
        ],
        compiler_params=pltpu.CompilerParams(
            dimension_semantics=("arbitrary", "arbitrary"), vmem_limit_bytes=VMEM_LIMIT),
        name="mixer",
    )(x, mod, g1.reshape(1, d), w_in_p, _bf(w_out), ln_g.reshape(1, -1), ln_b.reshape(1, -1), w_s, bsm,
      w_gate_p, b_gate.reshape(1, -1), gla_g.reshape(1, -1), conv_w, conv_b.reshape(1, -1),
      g2.reshape(1, d), router_w.T, router_b.reshape(N_EXPERTS, 1), *prev_args)


def _ffn_kernel(be_ref, rb_ref, nb_ref, xs_ref, w1_ref, w3_ref, w2_ref, ys_ref, w1b_ref, w3b_ref, w2b_ref):
    del rb_ref
    i = pl.program_id(0)
    used = i < nb_ref[0]
    new_expert = jnp.logical_or(i == 0, be_ref[i] != be_ref[jnp.maximum(i - 1, 0)])

    @pl.when(jnp.logical_and(used, new_expert))
    def _():
        w1b_ref[...] = _bf(w1_ref[...])
        w3b_ref[...] = _bf(w3_ref[...])
        w2b_ref[...] = _bf(w2_ref[...])

    @pl.when(used)
    def _():
        x_a, x_b = _unpack_rows(xs_ref[...])
        half = x_a.shape[1]
        a = _dot(x_a, w1b_ref[0:half, :]) + _dot(x_b, w1b_ref[half:, :])
        g = _dot(x_a, w3b_ref[0:half, :]) + _dot(x_b, w3b_ref[half:, :])
        hm = _bf((a * jax.nn.sigmoid(a)) * g)
        ys_ref[...] = _dot(hm, w2b_ref[...])


def _expert_ffn(layer, block_e, row_block, n_used, xs, w1, w3, w2):
    n_rows = xs.shape[0]
    n_blocks = block_e.shape[0]
    d, f = w1.shape[-2], w1.shape[-1]
    return pl.pallas_call(
        _ffn_kernel,
        grid_spec=pltpu.PrefetchScalarGridSpec(
            num_scalar_prefetch=3,
            grid=(n_blocks,),
            in_specs=[
                pl.BlockSpec((FFN_ROWS, d // 2), lambda i, be, rb, nb: (rb[i], 0)),
                pl.BlockSpec((None, None, d, f), lambda i, be, rb, nb: (layer, be[i], 0, 0)),
                pl.BlockSpec((None, None, d, f), lambda i, be, rb, nb: (layer, be[i], 0, 0)),
                pl.BlockSpec((None, None, f, d), lambda i, be, rb, nb: (layer, be[i], 0, 0)),
            ],
            out_specs=pl.BlockSpec((FFN_ROWS, d), lambda i, be, rb, nb: (rb[i], 0)),
            scratch_shapes=[pltpu.VMEM((d, f), jnp.bfloat16), pltpu.VMEM((d, f), jnp.bfloat16),
                            pltpu.VMEM((f, d), jnp.bfloat16)],
        ),
        out_shape=jax.ShapeDtypeStruct((n_rows, d), jnp.float32),
        compiler_params=pltpu.CompilerParams(
            dimension_semantics=("arbitrary",), vmem_limit_bytes=VMEM_LIMIT),
        name="expert_ffn",
    )(block_e, row_block, n_used, xs, w1, w3, w2)


def _combine_kernel(d0_ref, d1_ref, d0n_ref, d1n_ref, x_ref, mod_ref, w_ref, gf_ref, ys_ref, o_ref, ybuf, sems,
                    *, final_norm):
    tm = x_ref.shape[0]
    step = pl.program_id(0) * pl.num_programs(1) + pl.program_id(1)
    last = step == pl.num_programs(0) * pl.num_programs(1) - 1
    slot = step % 2

    def drain(sem):
        def body(g, c):
            for _ in range(TOP_K * SUBLANES):
                _row_copy(ys_ref, 0, ybuf.at[0, 0], 0, sem).wait()
            return c
        lax.fori_loop(0, tm // SUBLANES, body, 0)

    @pl.when(step == 0)
    def _():
        def issue(g, c):
            for jj in range(SUBLANES):
                i = pl.multiple_of(g * SUBLANES, SUBLANES) + jj
                _row_copy(ys_ref, d0_ref[i], ybuf.at[0, 0], i, sems.at[0]).start()
                _row_copy(ys_ref, d1_ref[i], ybuf.at[0, 1], i, sems.at[0]).start()
            return c
        lax.fori_loop(0, tm // SUBLANES, issue, 0)

    nxt, sem_nxt = ybuf.at[1 - slot], sems.at[1 - slot]
    for i in range(tm):
        _row_copy(ys_ref, d0n_ref[i], nxt.at[0], i, sem_nxt).start()
        _row_copy(ys_ref, d1n_ref[i], nxt.at[1], i, sem_nxt).start()

    drain(sems.at[slot])
    w = w_ref[...]
    gt2 = mod_ref[...][5:6]
    out = x_ref[...] + gt2 * (w[:, 0:1] * ybuf[slot, 0] + w[:, 1:2] * ybuf[slot, 1])
    if final_norm:
        out = _rms(out) * gf_ref[...]
    o_ref[...] = out

    @pl.when(last)
    def _():
        drain(sems.at[1 - slot])


def _combine(d0, d1, x, mod, wcol, g_final, ys, final_norm):
    bn, t, d = x.shape
    tm = min(MOVE_ROWS, t)
    nt = t // tm
    n_tiles = bn * nt

    def this_tile(b, i):
        return (b * nt + i,)

    def next_tile(b, i):
        return (jnp.minimum(b * nt + i + 1, n_tiles - 1),)

    return pl.pallas_call(
        functools.partial(_combine_kernel, final_norm=final_norm),
        grid=(bn, nt),
        in_specs=[
            pl.BlockSpec((tm,), this_tile, memory_space=pltpu.SMEM),
            pl.BlockSpec((tm,), this_tile, memory_space=pltpu.SMEM),
            pl.BlockSpec((tm,), next_tile, memory_space=pltpu.SMEM),
            pl.BlockSpec((tm,), next_tile, memory_space=pltpu.SMEM),
            pl.BlockSpec((None, tm, d), lambda b, i: (b, i, 0)),
            pl.BlockSpec((None, 6, d), lambda b, i: (b, 0, 0)),
            pl.BlockSpec((tm, 8), lambda b, i: (b * nt + i, 0)),
            pl.BlockSpec((1, d), lambda b, i: (0, 0)),
            pl.BlockSpec(memory_space=pl.ANY),
        ],
        out_specs=pl.BlockSpec((None, tm, d), lambda b, i: (b, i, 0)),
        out_shape=jax.ShapeDtypeStruct(x.shape, x.dtype),
        scratch_shapes=[pltpu.VMEM((2, TOP_K, tm, d), jnp.float32), pltpu.SemaphoreType.DMA((2,))],
        compiler_params=pltpu.CompilerParams(
            dimension_semantics=("arbitrary", "arbitrary"), vmem_limit_bytes=VMEM_LIMIT),
        name="combine",
    )(d0, d1, d0, d1, x, mod, wcol, g_final.reshape(1, d), ys)


def _block_map(counts, cap, n_blocks):
    blocks_per_expert = (counts + FFN_ROWS - 1) // FFN_ROWS
    ends = jnp.cumsum(blocks_per_expert)
    n_used = ends[-1:]
    step = jnp.minimum(jnp.arange(n_blocks, dtype=jnp.int32), n_used - 1)
    block_e = jnp.minimum(jnp.sum((ends[None, :] <= step[:, None]).astype(jnp.int32), axis=1), N_EXPERTS - 1)
    first = (ends - blocks_per_expert)[block_e]
    row_block = block_e * (cap // FFN_ROWS) + (step - first)
    return block_e.astype(jnp.int32), row_block.astype(jnp.int32), n_used.astype(jnp.int32)


def kernel(x, c, w_ada, b_ada, g_norm1, g_norm2, w_in, w_out, gmlp_ln_g, gmlp_ln_b, gmlp_ws, gmlp_bs,
           gla_w_gate, gla_b_gate, gla_norm_g, conv_w, conv_b, router_w, router_b, exp_w1, exp_w3, exp_w2,
           g_final):
    depth = w_ada.shape[0]
    bn, t, d = x.shape
    n = bn * t
    n_blocks = -(-(n * TOP_K) // FFN_ROWS) + N_EXPERTS
    mod_all = _modulation(c, w_ada, b_ada).reshape(depth, bn, 6, d)
    prev = None
    for l in range(depth):
        mod = mod_all[l]
        x, dest, wcol, cnt, xs = _mixer(
            x, mod, g_norm1[l], g_norm2[l], w_in[l], w_out[l], gmlp_ln_g[l], gmlp_ln_b[l], gmlp_ws[l],
            gmlp_bs[l], gla_w_gate[l], gla_b_gate[l], gla_norm_g[l], conv_w[l], conv_b[l], router_w, router_b,
            prev=prev)
        block_e, row_block, n_used = _block_map(cnt[:, 0], n, n_blocks)
        ys = _expert_ffn(l, block_e, row_block, n_used, xs, exp_w1, exp_w3, exp_w2)
        prev = (dest[0], dest[1], wcol, mod, ys)
    d0, d1, wcol, mod, ys = prev
    return _combine(d0, d1, x, mod, wcol, g_final, ys, final_norm=True)
```

```python
import functools

import jax
import jax.numpy as jnp
from jax import lax
from jax.experimental import pallas as pl
from jax.experimental.pallas import tpu as pltpu

CHUNK = 64
GMLP_BLOCK = 128
A_DIM = 256
A_HEADS = 4
B_HEADS = 4
B_DK = 64
B_DV = 128
B_KDIM = B_HEADS * B_DK
B_VDIM = B_HEADS * B_DV
GLA_RANK = 16
GLA_TAU = 16.0
C_DIM = 256
N_EXPERTS = 32
N_GROUPS = 4
EXPERTS_PER_GROUP = N_EXPERTS // N_GROUPS
TOP_K = 2
EPS = 1e-6

LANES = 128
SUBLANES = 8
OFF_AU, OFF_AV, OFF_Q, OFF_K, OFF_V = 0, 256, 512, 768, 1024
OFF_OG, OFF_CB, OFF_CC, OFF_CX, OFF_GLR = 1536, 2048, 2304, 2560, 2816
P_PAD = OFF_GLR + LANES

MIX_ROWS = 512
COPY_POINTS_PER_BLOCK = 10
MOVE_ROWS = 512
FFN_ROWS = 512
VMEM_LIMIT = 56 * 1024 * 1024

_NT = (((1,), (1,)), ((), ()))
_TN = (((0,), (0,)), ((), ()))


def _dot(a, b, dims=None):
    if dims is None:
        return jnp.dot(a, b, preferred_element_type=jnp.float32)
    return lax.dot_general(a, b, dims, preferred_element_type=jnp.float32)


def _bf(x):
    return x.astype(jnp.bfloat16)


def _split_bf16(x):
    hi = _bf(x)
    lo = _bf(x - hi.astype(jnp.float32))
    return hi, lo


def _rms(x):
    return x * lax.rsqrt(jnp.mean(x * x, axis=-1, keepdims=True) + EPS)


def _pack_rows(v):
    half = v.shape[1] // 2
    hi = pltpu.bitcast(_bf(v[:, :half]).astype(jnp.float32), jnp.uint32)
    lo = pltpu.bitcast(_bf(v[:, half:]).astype(jnp.float32), jnp.uint32)
    return hi | (lo >> 16)


def _unpack_rows(w):
    hi = pltpu.bitcast(w & jnp.uint32(0xFFFF0000), jnp.float32)
    lo = pltpu.bitcast(w << 16, jnp.float32)
    return _bf(hi), _bf(lo)


def _row_copy(src_ref, src_row, dst_ref, dst_row, sem):
    return pltpu.make_async_copy(src_ref.at[pl.ds(src_row, 1)], dst_ref.at[pl.ds(dst_row, 1)], sem)


def _aligned(row):
    return row if isinstance(row, int) else pl.multiple_of(row, SUBLANES)


def _tile_copy(src_ref, src_row, dst_ref, dst_row, sem):
    return pltpu.make_async_copy(src_ref.at[pl.ds(_aligned(src_row), SUBLANES)],
                                 dst_ref.at[pl.ds(_aligned(dst_row), SUBLANES)], sem)


def _store_rows_as_tiles(ref, value):
    rows = value.shape[0]
    assert value.shape[1] == SUBLANES * LANES
    for c in range(SUBLANES):
        ref[pl.ds(c, rows, stride=SUBLANES), :] = value[:, c * LANES:(c + 1) * LANES]


def _load_rows_from_tiles(ref, idx, rows):
    return jnp.concatenate(
        [ref[idx + (pl.ds(c, rows, stride=SUBLANES), slice(None))] for c in range(SUBLANES)], axis=1)


def _mod_kernel(c_ref, w_ref, b_ref, o_ref):
    c = c_ref[...]
    s = c * jax.nn.sigmoid(c)
    s_hi, s_lo = _split_bf16(s)
    w_hi, w_lo = _split_bf16(w_ref[...])
    acc = _dot(s_hi, w_hi) + _dot(s_hi, w_lo) + _dot(s_lo, w_hi)
    o_ref[...] = acc + b_ref[...]


def _modulation(c, w_ada, b_ada):
    depth, d, six_d = w_ada.shape
    bn = c.shape[0]
    cb = 1024
    return pl.pallas_call(
        _mod_kernel,
        grid=(depth, six_d // cb),
        in_specs=[
            pl.BlockSpec((bn, d), lambda l, j: (0, 0)),
            pl.BlockSpec((None, d, cb), lambda l, j: (l, 0, j)),
            pl.BlockSpec((None, 1, cb), lambda l, j: (l, 0, j)),
        ],
        out_specs=pl.BlockSpec((None, bn, cb), lambda l, j: (l, 0, j)),
        out_shape=jax.ShapeDtypeStruct((depth, bn, six_d), jnp.float32),
        compiler_params=pltpu.CompilerParams(
            dimension_semantics=("arbitrary", "arbitrary"), vmem_limit_bytes=VMEM_LIMIT),
        name="adaln_mod",
    )(c, w_ada, b_ada.reshape(depth, 1, six_d))


def _route(h2, rwt_ref, rb_ref, carry, cap):
    tr = h2.shape[0]
    h_hi, h_lo = _split_bf16(h2)
    w_hi, w_lo = _split_bf16(rwt_ref[...])
    logits = (_dot(w_hi, h_hi, _NT) + _dot(w_hi, h_lo, _NT) + _dot(w_lo, h_hi, _NT)) + rb_ref[...]
    ex = jnp.exp(logits - jnp.max(logits, axis=0, keepdims=True))
    probs = ex / jnp.sum(ex, axis=0, keepdims=True)

    idx8 = lax.broadcasted_iota(jnp.int32, (EXPERTS_PER_GROUP, tr), 0)
    best = None
    for g in range(N_GROUPS):
        pg = probs[g * EXPERTS_PER_GROUP:(g + 1) * EXPERTS_PER_GROUP]
        m1 = jnp.max(pg, axis=0, keepdims=True)
        i1 = jnp.min(jnp.where(pg == m1, idx8, EXPERTS_PER_GROUP), axis=0, keepdims=True)
        pg2 = jnp.where(idx8 == i1, -1.0, pg)
        m2 = jnp.max(pg2, axis=0, keepdims=True)
        i2 = jnp.min(jnp.where(pg2 == m2, idx8, EXPERTS_PER_GROUP), axis=0, keepdims=True)
        cand = (m1 + m2, m1, m2, i1 + g * EXPERTS_PER_GROUP, i2 + g * EXPERTS_PER_GROUP)
        if best is None:
            best = cand
        else:
            better = cand[0] > best[0]
            best = tuple(jnp.where(better, a, b) for a, b in zip(cand, best))
    _, p1, p2, e0, e1 = best
    denom = p1 + p2
    w0, w1 = p1 / denom, p2 / denom

    eidx = lax.broadcasted_iota(jnp.int32, (N_EXPERTS, tr), 0)
    hit0, hit1 = eidx == e0, eidx == e1
    onehot = jnp.where(jnp.logical_or(hit0, hit1), 1.0, 0.0)
    before = (lax.broadcasted_iota(jnp.int32, (tr, tr), 0) < lax.broadcasted_iota(jnp.int32, (tr, tr), 1))
    rank = _dot(_bf(onehot), _bf(jnp.where(before, 1.0, 0.0))) + carry
    r0 = jnp.sum(jnp.where(hit0, rank, 0.0), axis=0, keepdims=True).astype(jnp.int32)
    r1 = jnp.sum(jnp.where(hit1, rank, 0.0), axis=0, keepdims=True).astype(jnp.int32)
    carry = carry + jnp.sum(onehot, axis=1, keepdims=True)

    eye = lax.broadcasted_iota(jnp.int32, (tr, tr), 0) == lax.broadcasted_iota(jnp.int32, (tr, tr), 1)
    w0c = jnp.sum(jnp.where(eye, jnp.broadcast_to(w0, (tr, tr)), 0.0), axis=1, keepdims=True)
    w1c = jnp.sum(jnp.where(eye, jnp.broadcast_to(w1, (tr, tr)), 0.0), axis=1, keepdims=True)
    lane8 = lax.broadcasted_iota(jnp.int32, (tr, 8), 1)
    wcol = jnp.where(lane8 == 0, w0c, jnp.where(lane8 == 1, w1c, 0.0))
    return e0 * cap + r0, e1 * cap + r1, wcol, carry


def _zero_segment_padding(cnt_ref, cap, zero_ref, xs_ref, sem):
    copies = []
    for ex in range(N_EXPERTS):
        cnt = cnt_ref[ex, 0]
        seg = ex * cap
        pos = seg + cnt
        end8 = seg + ((cnt + SUBLANES - 1) // SUBLANES) * SUBLANES
        for j in range(SUBLANES - 1):
            copies.append((pos + j < end8, _row_copy(zero_ref, 0, xs_ref, pos + j, sem)))
        seg_end = seg + ((cnt + FFN_ROWS - 1) // FFN_ROWS) * FFN_ROWS
        groups = (seg_end - end8) // SUBLANES
        at = end8
        bit = FFN_ROWS // (2 * SUBLANES)
        while bit >= 1:
            rows = bit * SUBLANES
            pred = (groups & bit) != 0
            copies.append((pred, pltpu.make_async_copy(
                zero_ref.at[pl.ds(0, rows)], xs_ref.at[pl.ds(pl.multiple_of(at, SUBLANES), rows)], sem)))
            at = at + jnp.where(pred, rows, 0)
            bit //= 2
    for pred, cp in copies:
        pl.when(pred)(cp.start)
    for pred, cp in copies:
        pl.when(pred)(cp.wait)


N_MIXER_INPUTS = 17
N_MIXER_OUTPUTS = 5


def _mixer_kernel(*refs, cap, fuse_prev):
    ins, rest = refs[:N_MIXER_INPUTS], refs[N_MIXER_INPUTS:]
    if fuse_prev:
        dp0_ref, dp1_ref, dp0n_ref, dp1n_ref, wp_ref, modp_ref, ysp_ref = rest[:7]
        rest = rest[7:]
    outs, scratch = rest[:N_MIXER_OUTPUTS], rest[N_MIXER_OUTPUTS:]
    (x_ref, mod_ref, g1_ref, win_ref, wout_ref, lng_ref, lnb_ref, ws_ref, bsm_ref,
     wg_ref, bg_ref, glag_ref, cw_ref, cb_ref, g2_ref, rwt_ref, rb_ref) = ins
    o_ref, dest_ref, w_ref, cnt_ref, xs_ref = outs
    (h_ref, pa_ref, pb_ref, y0_ref, y1_ref, y2_ref, y3_ref, s_ref, zc_ref, wsm_ref, h2_ref,
     carry_ref, dvm_ref, d0s_ref, d1s_ref, cvm_ref, csm_ref, zero_ref, sem, sem_s, sem_z,
     ybuf, xin_ref, sem_g) = scratch

    bi, ti = pl.program_id(0), pl.program_id(1)
    tm = x_ref.shape[0]
    n_blk = tm // GMLP_BLOCK
    step = bi * pl.num_programs(1) + ti
    last_step = step == pl.num_programs(0) * pl.num_programs(1) - 1
    slot = step % 2
    spare_row0 = N_EXPERTS * cap

    def drain_rows(src_ref, dst_ref, s, copy=_row_copy):
        def body(g, c):
            for _ in range(TOP_K * SUBLANES):
                copy(src_ref, 0, dst_ref, 0, s).wait()
            return c
        lax.fori_loop(0, tm // SUBLANES, body, 0)

    if fuse_prev:
        @pl.when(step == 0)
        def _():
            def issue(g, c):
                for jj in range(SUBLANES):
                    i = pl.multiple_of(g * SUBLANES, SUBLANES) + jj
                    _tile_copy(ysp_ref, dp0_ref[i], ybuf.at[0, 0], i * SUBLANES, sem_g.at[0]).start()
                    _tile_copy(ysp_ref, dp1_ref[i], ybuf.at[0, 1], i * SUBLANES, sem_g.at[0]).start()
                return c
            lax.fori_loop(0, tm // SUBLANES, issue, 0)

        drain_rows(ysp_ref, ybuf.at[0, 0], sem_g.at[slot], _tile_copy)
        wp = wp_ref[...]
        xin_ref[...] = x_ref[...] + modp_ref[...][5:6] * (
            wp[:, 0:1] * _load_rows_from_tiles(ybuf, (slot, 0), tm)
            + wp[:, 1:2] * _load_rows_from_tiles(ybuf, (slot, 1), tm))
        x_in = xin_ref
    else:
        x_in = x_ref

    @pl.when(step == 0)
    def _():
        carry_ref[...] = jnp.zeros_like(carry_ref)
        h2_ref[...] = jnp.zeros_like(h2_ref)

        def fill(i, c):
            d0s_ref[i] = spare_row0 + i
            d1s_ref[i] = spare_row0 + tm + i
            return c

        lax.fori_loop(0, tm, fill, 0)
        tt = lax.broadcasted_iota(jnp.int32, (GMLP_BLOCK, GMLP_BLOCK), 0) // CHUNK
        ss = lax.broadcasted_iota(jnp.int32, (GMLP_BLOCK, GMLP_BLOCK), 1) // CHUNK
        for h in range(A_HEADS):
            wsm_ref[h] = _bf(jnp.where(tt >= ss, ws_ref[h], 0.0))

    @pl.when(ti == 0)
    def _():
        s_ref[...] = jnp.zeros_like(s_ref)
        zc_ref[...] = jnp.zeros_like(zc_ref)

    m = mod_ref[...]
    sh1, sc1, gt1, sh2, sc2 = m[0:1], m[1:2], m[2:3], m[3:4], m[4:5]
    h_ref[...] = _bf((_rms(x_in[...]) * g1_ref[...]) * (1.0 + sc1) + sh1)

    lane256 = lax.broadcasted_iota(jnp.int32, (CHUNK, B_KDIM), 1)
    head_of_lane = lane256 // B_DK
    r64 = lax.broadcasted_iota(jnp.int32, (CHUNK, CHUNK), 0)
    c64 = lax.broadcasted_iota(jnp.int32, (CHUNK, CHUNK), 1)
    tri = _bf(jnp.where(r64 >= c64, 1.0, 0.0))
    causal4 = jnp.concatenate([r64 >= c64] * B_HEADS, axis=0)
    eye256 = (lax.broadcasted_iota(jnp.int32, (B_KDIM, B_KDIM), 0)
              == lax.broadcasted_iota(jnp.int32, (B_KDIM, B_KDIM), 1))
    a_head_of_lane = lax.broadcasted_iota(jnp.int32, (GMLP_BLOCK, A_DIM), 1) // (A_DIM // A_HEADS)
    row128 = lax.broadcasted_iota(jnp.int32, (GMLP_BLOCK, C_DIM), 0)
    sqrt_half = 0.7071067811865476

    def gelu(v):
        return 0.5 * v * (1.0 + lax.erf(v * sqrt_half))

    h2_prev = h2_ref.at[1 - slot]
    row_copies = []
    for i in range(tm):
        row_copies.append(lambda i=i: _row_copy(h2_prev, i, xs_ref, d0s_ref[i], sem).start())
        row_copies.append(lambda i=i: _row_copy(h2_prev, i, xs_ref, d1s_ref[i], sem).start())
        if fuse_prev:
            row_copies.append(lambda i=i: _tile_copy(
                ysp_ref, dp0n_ref[i], ybuf.at[1 - slot, 0], i * SUBLANES, sem_g.at[1 - slot]).start())
            row_copies.append(lambda i=i: _tile_copy(
                ysp_ref, dp1n_ref[i], ybuf.at[1 - slot, 1], i * SUBLANES, sem_g.at[1 - slot]).start())
    row_copies.reverse()
    copies_per_point = -(-len(row_copies) // (n_blk * COPY_POINTS_PER_BLOCK))

    def start_row_copies():
        for _ in range(min(copies_per_point, len(row_copies))):
            row_copies.pop()()

    p_halves = (pa_ref, pb_ref)
    y_blocks = (y0_ref, y1_ref, y2_ref, y3_ref)
    assert n_blk == len(y_blocks)

    def project(half):
        rows = slice(half * (tm // 2), (half + 1) * (tm // 2))
        p_halves[half][...] = _dot(h_ref[rows, :], win_ref[...])

    def block(j, state, zprev):
        p_ref, y_ref = p_halves[j // (n_blk // 2)], y_blocks[j]
        r0 = (j % (n_blk // 2)) * GMLP_BLOCK
        rows = slice(r0, r0 + GMLP_BLOCK)

        u = gelu(p_ref[rows, OFF_AU:OFF_AU + A_DIM])
        vv = gelu(p_ref[rows, OFF_AV:OFF_AV + A_DIM])
        mu = jnp.mean(vv, axis=-1, keepdims=True)
        var = jnp.mean((vv - mu) ** 2, axis=-1, keepdims=True)
        vn = _bf((vv - mu) * lax.rsqrt(var + EPS) * lng_ref[...] + lnb_ref[...])
        sv = jnp.zeros((GMLP_BLOCK, A_DIM), jnp.float32)
        for hh in range(A_HEADS):
            sv = jnp.where(a_head_of_lane == hh, _dot(wsm_ref[hh], vn), sv)
        y_ref[:, 0:A_DIM] = _bf(u * (sv + bsm_ref[...]))
        start_row_copies()

        z = _dot(_bf(p_ref[rows, OFF_GLR:OFF_GLR + LANES]), wg_ref[...]) + bg_ref[...]
        glog = (jnp.minimum(z, 0.0) - jnp.log1p(jnp.exp(-jnp.abs(z)))) / GLA_TAU
        for c in range(GMLP_BLOCK // CHUNK):
            rc = slice(r0 + c * CHUNK, r0 + (c + 1) * CHUNK)
            g = glog[c * CHUNK:(c + 1) * CHUNK]
            g_hi = _bf(g)
            g_r1 = g - g_hi.astype(jnp.float32)
            g_mid = _bf(g_r1)
            g_lo = _bf(g_r1 - g_mid.astype(jnp.float32))
            b = _dot(tri, g_hi) + _dot(tri, g_mid) + _dot(tri, g_lo)
            b_mid = b[CHUNK // 2:CHUNK // 2 + 1]
            b_last = b[CHUNK - 1:CHUNK]
            q = p_ref[rc, OFF_Q:OFF_Q + B_KDIM] * (B_DK ** -0.5)
            k = p_ref[rc, OFF_K:OFF_K + B_KDIM]
            vb = _bf(p_ref[rc, OFF_V:OFF_V + B_VDIM])
            qs = q * jnp.exp(b - b_mid)
            ks = _bf(k * jnp.exp(b_mid - b))
            kd = _bf(k * jnp.exp(b_last - b))
            qb = q * jnp.exp(b)
            qs_st = _bf(jnp.concatenate(
                [jnp.where(head_of_lane == hh, qs, 0.0) for hh in range(B_HEADS)], axis=0))
            qb_st = _bf(jnp.concatenate(
                [jnp.where(head_of_lane == hh, qb, 0.0) for hh in range(B_HEADS)], axis=0))
            scores = _bf(jnp.where(causal4, _dot(qs_st, ks, _NT), 0.0))
            start_row_copies()
            o_inter = _dot(qb_st, _bf(state))
            kv_all = _dot(kd, vb, _TN)
            start_row_copies()
            outs, kvs = [], []
            for hh in range(B_HEADS):
                rs = slice(hh * CHUNK, (hh + 1) * CHUNK)
                cs = slice(hh * B_DV, (hh + 1) * B_DV)
                o_h = _dot(scores[rs], vb[:, cs]) + o_inter[rs]
                outs.append(_rms(o_h))
                kvs.append(kv_all[rs, cs])
            decay_row = jnp.broadcast_to(jnp.exp(b_last), (B_KDIM, B_KDIM))
            decay_col = jnp.sum(jnp.where(eye256, decay_row, 0.0), axis=1, keepdims=True)
            state = decay_col * state + jnp.concatenate(kvs, axis=0)
            on = jnp.concatenate(outs, axis=1) * glag_ref[...]
            og = p_ref[rc, OFF_OG:OFF_OG + B_VDIM]
            y_ref[c * CHUNK:(c + 1) * CHUNK, A_DIM:A_DIM + B_VDIM] = _bf(on * (og * jax.nn.sigmoid(og)))
            start_row_copies()

        zz = p_ref[rows, OFF_CC:OFF_CC + C_DIM] * p_ref[rows, OFF_CX:OFF_CX + C_DIM]
        z1 = jnp.where(row128 == 0, zprev[7:8], pltpu.roll(zz, 1, 0))
        z2 = jnp.where(row128 == 0, zprev[6:7], jnp.where(row128 == 1, zprev[7:8], pltpu.roll(zz, 2, 0)))
        cw = cw_ref[...]
        yc = cb_ref[...] + cw[0:1] * z2
        yc = yc + cw[1:2] * z1
        yc = yc + cw[2:3] * zz
        y_ref[:, A_DIM + B_VDIM:A_DIM + B_VDIM + C_DIM] = _bf(p_ref[rows, OFF_CB:OFF_CB + C_DIM] * yc)
        start_row_copies()
        return state, zz[GMLP_BLOCK - SUBLANES:GMLP_BLOCK]

    state, zprev, carry = s_ref[...], zc_ref[...], carry_ref[...]
    project(0)
    for j in range(n_blk):
        if j == 0:
            project(1)
        state, zprev = block(j, state, zprev)
        rows = slice(j * GMLP_BLOCK, (j + 1) * GMLP_BLOCK)
        xo = x_in[rows, :] + gt1 * _dot(y_blocks[j][...], wout_ref[...])
        o_ref[rows, :] = xo
        start_row_copies()
        h2 = (_rms(xo) * g2_ref[...]) * (1.0 + sc2) + sh2
        h2_ref[slot, rows, :] = _pack_rows(h2)
        d0, d1, wcol, carry = _route(h2, rwt_ref, rb_ref, carry, cap)
        dvm_ref[0:1, rows] = d0
        dvm_ref[1:2, rows] = d1
        w_ref[rows, :] = wcol
        start_row_copies()
    while row_copies:
        row_copies.pop()()
    s_ref[...] = state
    zc_ref[...] = zprev
    carry_ref[...] = carry
    cnt = jnp.broadcast_to(carry, cnt_ref.shape).astype(jnp.int32)
    cnt_ref[...] = cnt
    dest_ref[...] = dvm_ref[0:TOP_K, :]

    drain_rows(h2_prev, xs_ref, sem)

    to_smem = [pltpu.make_async_copy(dvm_ref.at[0], d0s_ref, sem_s),
               pltpu.make_async_copy(dvm_ref.at[1], d1s_ref, sem_s)]
    for cp in to_smem:
        cp.start()
    for cp in to_smem:
        cp.wait()

    @pl.when(last_step)
    def _():
        h2_last = h2_ref.at[slot]

        def issue(g, c):
            for jj in range(SUBLANES):
                i = pl.multiple_of(g * SUBLANES, SUBLANES) + jj
                _row_copy(h2_last, i, xs_ref, d0s_ref[i], sem).start()
                _row_copy(h2_last, i, xs_ref, d1s_ref[i], sem).start()
            return c

        lax.fori_loop(0, tm // SUBLANES, issue, 0)
        drain_rows(h2_last, xs_ref, sem)
        if fuse_prev:
            drain_rows(ysp_ref, ybuf.at[0, 0], sem_g.at[1 - slot], _tile_copy)
        cvm_ref[...] = cnt
        zero_ref[...] = jnp.zeros_like(zero_ref)
        cp = pltpu.make_async_copy(cvm_ref, csm_ref, sem_s)
        cp.start()
        cp.wait()
        _zero_segment_padding(csm_ref, cap, zero_ref, xs_ref, sem_z)


def _mixer(x, mod, g1, g2, w_in, w_out, ln_g, ln_b, w_s, b_s, w_gate, b_gate, gla_g, conv_w, conv_b,
           router_w, router_b, prev=None):
    bn, t, d = x.shape
    n = bn * t
    tm = min(MIX_ROWS, t)
    nt = t // tm
    cap = n
    glr0 = 2 * A_DIM + 2 * B_KDIM + B_VDIM
    w_in_p = _bf(jnp.concatenate(
        [w_in[:, :glr0], w_in[:, glr0 + GLA_RANK:], w_in[:, glr0:glr0 + GLA_RANK],
         jnp.zeros((d, LANES - GLA_RANK), w_in.dtype)], axis=1))
    w_gate_p = _bf(jnp.concatenate(
        [w_gate, jnp.zeros((LANES - GLA_RANK, B_KDIM), w_gate.dtype)], axis=0))
    bsm = jnp.repeat(b_s.T, A_DIM // A_HEADS, axis=1)

    def whole(shape):
        return pl.BlockSpec(shape, lambda b, i: (0,) * len(shape))

    prev_specs, prev_args = [], ()
    if prev is not None:
        n_tiles = bn * nt

        def this_tile(b, i):
            return (b * nt + i,)

        def next_tile(b, i):
            return (jnp.minimum(b * nt + i + 1, n_tiles - 1),)

        d0p, d1p, wcolp, modp, ysp = prev
        prev_specs = [
            pl.BlockSpec((tm,), this_tile, memory_space=pltpu.SMEM),
            pl.BlockSpec((tm,), this_tile, memory_space=pltpu.SMEM),
            pl.BlockSpec((tm,), next_tile, memory_space=pltpu.SMEM),
            pl.BlockSpec((tm,), next_tile, memory_space=pltpu.SMEM),
            pl.BlockSpec((tm, 8), lambda b, i: (b * nt + i, 0)),
            pl.BlockSpec((None, 6, d), lambda b, i: (b, 0, 0)),
            pl.BlockSpec(memory_space=pl.ANY),
        ]
        prev_args = (d0p, d1p, d0p, d1p, wcolp, modp, ysp)

    return pl.pallas_call(
        functools.partial(_mixer_kernel, cap=cap, fuse_prev=prev is not None),
        grid=(bn, nt),
        in_specs=[
            pl.BlockSpec((None, tm, d), lambda b, i: (b, i, 0)),
            pl.BlockSpec((None, 6, d), lambda b, i: (b, 0, 0)),
            whole((1, d)),
            whole((d, P_PAD)),
            whole((d, d)),
            whole((1, A_DIM)),
            whole((1, A_DIM)),
            whole((A_HEADS, GMLP_BLOCK, GMLP_BLOCK)),
            whole((GMLP_BLOCK, A_DIM)),
            whole((LANES, B_KDIM)),
            whole((1, B_KDIM)),
            whole((1, B_VDIM)),
            whole((3, C_DIM)),
            whole((1, C_DIM)),
            whole((1, d)),
            whole((N_EXPERTS, d)),
            whole((N_EXPERTS, 1)),
        ] + prev_specs,
        out_specs=[
            pl.BlockSpec((None, tm, d), lambda b, i: (b, i, 0)),
            pl.BlockSpec((TOP_K, tm), lambda b, i: (0, b * nt + i)),
            pl.BlockSpec((tm, 8), lambda b, i: (b * nt + i, 0)),
            whole((N_EXPERTS, LANES)),
            pl.BlockSpec(memory_space=pl.ANY),
        ],
        out_shape=[
            jax.ShapeDtypeStruct(x.shape, x.dtype),
            jax.ShapeDtypeStruct((TOP_K, n), jnp.int32),
            jax.ShapeDtypeStruct((n, 8), jnp.float32),
            jax.ShapeDtypeStruct((N_EXPERTS, LANES), jnp.int32),
            jax.ShapeDtypeStruct((N_EXPERTS * cap + TOP_K * tm, d // 2), jnp.uint32),
        ],
        scratch_shapes=[
            pltpu.VMEM((tm, d), jnp.bfloat16),
            pltpu.VMEM((tm // 2, P_PAD), jnp.float32),
            pltpu.VMEM((tm // 2, P_PAD), jnp.float32),
            pltpu.VMEM((GMLP_BLOCK, d), jnp.bfloat16),
            pltpu.VMEM((GMLP_BLOCK, d), jnp.bfloat16),
            pltpu.VMEM((GMLP_BLOCK, d), jnp.bfloat16),
            pltpu.VMEM((GMLP_BLOCK, d), jnp.bfloat16),
            pltpu.VMEM((B_KDIM, B_DV), jnp.float32),
            pltpu.VMEM((SUBLANES, C_DIM), jnp.float32),
            pltpu.VMEM((A_HEADS, GMLP_BLOCK, GMLP_BLOCK), jnp.bfloat16),
            pltpu.VMEM((2, tm, d // 2), jnp.uint32),
            pltpu.VMEM((N_EXPERTS, 1), jnp.float32),
            pltpu.VMEM((SUBLANES, tm), jnp.int32),
            pltpu.SMEM((tm,), jnp.int32),
            pltpu.SMEM((tm,), jnp.int32),
            pltpu.VMEM((N_EXPERTS, LANES), jnp.int32),
            pltpu.SMEM((N_EXPERTS, LANES), jnp.int32),
            pltpu.VMEM((FFN_ROWS // 2, d // 2), jnp.uint32),
            pltpu.SemaphoreType.DMA,
            pltpu.SemaphoreType.DMA,
            pltpu.SemaphoreType.DMA,
            pltpu.VMEM((2, TOP_K, tm * SUBLANES, LANES), jnp.float32),
            pltpu.VMEM((tm, d), jnp.float32),
            pltpu.SemaphoreType.DMA((2,)),
        ],
        compiler_params=pltpu.CompilerParams(
            dimension_semantics=("arbitrary", "arbitrary"), vmem_limit_bytes=VMEM_LIMIT),
        name="mixer",
    )(x, mod, g1.reshape(1, d), w_in_p, _bf(w_out), ln_g.reshape(1, -1), ln_b.reshape(1, -1), w_s, bsm,
      w_gate_p, b_gate.reshape(1, -1), gla_g.reshape(1, -1), conv_w, conv_b.reshape(1, -1),
      g2.reshape(1, d), router_w.T, router_b.reshape(N_EXPERTS, 1), *prev_args)


def _ffn_kernel(be_ref, rb_ref, nb_ref, xs_ref, w1_ref, w3_ref, w2_ref, ys_ref, w1b_ref, w3b_ref, w2b_ref):
    del rb_ref
    i = pl.program_id(0)
    used = i < nb_ref[0]
    new_expert = jnp.logical_or(i == 0, be_ref[i] != be_ref[jnp.maximum(i - 1, 0)])

    @pl.when(jnp.logical_and(used, new_expert))
    def _():
        w1b_ref[...] = _bf(w1_ref[...])
        w3b_ref[...] = _bf(w3_ref[...])
        w2b_ref[...] = _bf(w2_ref[...])

    @pl.when(used)
    def _():
        x_a, x_b = _unpack_rows(xs_ref[...])
        half = x_a.shape[1]
        a = _dot(x_a, w1b_ref[0:half, :]) + _dot(x_b, w1b_ref[half:, :])
        g = _dot(x_a, w3b_ref[0:half, :]) + _dot(x_b, w3b_ref[half:, :])
        hm = _bf((a * jax.nn.sigmoid(a)) * g)
        _store_rows_as_tiles(ys_ref, _dot(hm, w2b_ref[...]))


def _expert_ffn(layer, block_e, row_block, n_used, xs, w1, w3, w2):
    n_rows = xs.shape[0]
    n_blocks = block_e.shape[0]
    d, f = w1.shape[-2], w1.shape[-1]
    return pl.pallas_call(
        _ffn_kernel,
        grid_spec=pltpu.PrefetchScalarGridSpec(
            num_scalar_prefetch=3,
            grid=(n_blocks,),
            in_specs=[
                pl.BlockSpec((FFN_ROWS, d // 2), lambda i, be, rb, nb: (rb[i], 0)),
                pl.BlockSpec((None, None, d, f), lambda i, be, rb, nb: (layer, be[i], 0, 0)),
                pl.BlockSpec((None, None, d, f), lambda i, be, rb, nb: (layer, be[i], 0, 0)),
                pl.BlockSpec((None, None, f, d), lambda i, be, rb, nb: (layer, be[i], 0, 0)),
            ],
            out_specs=pl.BlockSpec((FFN_ROWS * SUBLANES, LANES), lambda i, be, rb, nb: (rb[i], 0)),
            scratch_shapes=[pltpu.VMEM((d, f), jnp.bfloat16), pltpu.VMEM((d, f), jnp.bfloat16),
                            pltpu.VMEM((f, d), jnp.bfloat16)],
        ),
        out_shape=jax.ShapeDtypeStruct((n_rows * SUBLANES, LANES), jnp.float32),
        compiler_params=pltpu.CompilerParams(
            dimension_semantics=("arbitrary",), vmem_limit_bytes=VMEM_LIMIT),
        name="expert_ffn",
    )(block_e, row_block, n_used, xs, w1, w3, w2)


def _combine_kernel(d0_ref, d1_ref, d0n_ref, d1n_ref, x_ref, mod_ref, w_ref, gf_ref, ys_ref, o_ref, ybuf, sems,
                    *, final_norm):
    tm = x_ref.shape[0]
    step = pl.program_id(0) * pl.num_programs(1) + pl.program_id(1)
    last = step == pl.num_programs(0) * pl.num_programs(1) - 1
    slot = step % 2

    def drain(sem):
        def body(g, c):
            for _ in range(TOP_K * SUBLANES):
                _tile_copy(ys_ref, 0, ybuf.at[0, 0], 0, sem).wait()
            return c
        lax.fori_loop(0, tm // SUBLANES, body, 0)

    @pl.when(step == 0)
    def _():
        def issue(g, c):
            for jj in range(SUBLANES):
                i = pl.multiple_of(g * SUBLANES, SUBLANES) + jj
                _tile_copy(ys_ref, d0_ref[i], ybuf.at[0, 0], i * SUBLANES, sems.at[0]).start()
                _tile_copy(ys_ref, d1_ref[i], ybuf.at[0, 1], i * SUBLANES, sems.at[0]).start()
            return c
        lax.fori_loop(0, tm // SUBLANES, issue, 0)

    nxt, sem_nxt = ybuf.at[1 - slot], sems.at[1 - slot]
    for i in range(tm):
        _tile_copy(ys_ref, d0n_ref[i], nxt.at[0], i * SUBLANES, sem_nxt).start()
        _tile_copy(ys_ref, d1n_ref[i], nxt.at[1], i * SUBLANES, sem_nxt).start()

    drain(sems.at[slot])
    w = w_ref[...]
    gt2 = mod_ref[...][5:6]
    out = x_ref[...] + gt2 * (w[:, 0:1] * _load_rows_from_tiles(ybuf, (slot, 0), tm)
                              + w[:, 1:2] * _load_rows_from_tiles(ybuf, (slot, 1), tm))
    if final_norm:
        out = _rms(out) * gf_ref[...]
    o_ref[...] = out

    @pl.when(last)
    def _():
        drain(sems.at[1 - slot])


def _combine(d0, d1, x, mod, wcol, g_final, ys, final_norm):
    bn, t, d = x.shape
    tm = min(MOVE_ROWS, t)
    nt = t // tm
    n_tiles = bn * nt

    def this_tile(b, i):
        return (b * nt + i,)

    def next_tile(b, i):
        return (jnp.minimum(b * nt + i + 1, n_tiles - 1),)

    return pl.pallas_call(
        functools.partial(_combine_kernel, final_norm=final_norm),
        grid=(bn, nt),
        in_specs=[
            pl.BlockSpec((tm,), this_tile, memory_space=pltpu.SMEM),
            pl.BlockSpec((tm,), this_tile, memory_space=pltpu.SMEM),
            pl.BlockSpec((tm,), next_tile, memory_space=pltpu.SMEM),
            pl.BlockSpec((tm,), next_tile, memory_space=pltpu.SMEM),
            pl.BlockSpec((None, tm, d), lambda b, i: (b, i, 0)),
            pl.BlockSpec((None, 6, d), lambda b, i: (b, 0, 0)),
            pl.BlockSpec((tm, 8), lambda b, i: (b * nt + i, 0)),
            pl.BlockSpec((1, d), lambda b, i: (0, 0)),
            pl.BlockSpec(memory_space=pl.ANY),
        ],
        out_specs=pl.BlockSpec((None, tm, d), lambda b, i: (b, i, 0)),
        out_shape=jax.ShapeDtypeStruct(x.shape, x.dtype),
        scratch_shapes=[pltpu.VMEM((2, TOP_K, tm * SUBLANES, LANES), jnp.float32), pltpu.SemaphoreType.DMA((2,))],
        compiler_params=pltpu.CompilerParams(
            dimension_semantics=("arbitrary", "arbitrary"), vmem_limit_bytes=VMEM_LIMIT),
        name="combine",
    )(d0, d1, d0, d1, x, mod, wcol, g_final.reshape(1, d), ys)


def _block_map(counts, cap, n_blocks):
    blocks_per_expert = (counts + FFN_ROWS - 1) // FFN_ROWS
    ends = jnp.cumsum(blocks_per_expert)
    n_used = ends[-1:]
    step = jnp.minimum(jnp.arange(n_blocks, dtype=jnp.int32), n_used - 1)
    block_e = jnp.minimum(jnp.sum((ends[None, :] <= step[:, None]).astype(jnp.int32), axis=1), N_EXPERTS - 1)
    first = (ends - blocks_per_expert)[block_e]
    row_block = block_e * (cap // FFN_ROWS) + (step - first)
    return block_e.astype(jnp.int32), row_block.astype(jnp.int32), n_used.astype(jnp.int32)


def kernel(x, c, w_ada, b_ada, g_norm1, g_norm2, w_in, w_out, gmlp_ln_g, gmlp_ln_b, gmlp_ws, gmlp_bs,
           gla_w_gate, gla_b_gate, gla_norm_g, conv_w, conv_b, router_w, router_b, exp_w1, exp_w3, exp_w2,
           g_final):
    depth = w_ada.shape[0]
    bn, t, d = x.shape
    n = bn * t
    n_blocks = -(-(n * TOP_K) // FFN_ROWS) + N_EXPERTS
    mod_all = _modulation(c, w_ada, b_ada).reshape(depth, bn, 6, d)
    prev = None
    for l in range(depth):
        mod = mod_all[l]
        x, dest, wcol, cnt, xs = _mixer(
            x, mod, g_norm1[l], g_norm2[l], w_in[l], w_out[l], gmlp_ln_g[l], gmlp_ln_b[l], gmlp_ws[l],
            gmlp_bs[l], gla_w_gate[l], gla_b_gate[l], gla_norm_g[l], conv_w[l], conv_b[l], router_w, router_b,
            prev=prev)
        block_e, row_block, n_used = _block_map(cnt[:, 0], n, n_blocks)
        ys = _expert_ffn(l, block_e, row_block, n_used, xs, exp_w1, exp_w3, exp_w2)
        prev = (dest[0] * SUBLANES, dest[1] * SUBLANES, wcol, mod, ys)
    d0, d1, wcol, mod, ys = prev
    return _combine(d0, d1, x, mod, wcol, g_final, ys, final_norm=True)
```

```python
import functools

import jax
import jax.numpy as jnp
from jax import lax
from jax.experimental import pallas as pl
from jax.experimental.pallas import tpu as pltpu

CHUNK = 64
GMLP_BLOCK = 128
A_DIM = 256
A_HEADS = 4
B_HEADS = 4
B_DK = 64
B_DV = 128
B_KDIM = B_HEADS * B_DK
B_VDIM = B_HEADS * B_DV
GLA_RANK = 16
GLA_TAU = 16.0
C_DIM = 256
N_EXPERTS = 32
N_GROUPS = 4
EXPERTS_PER_GROUP = N_EXPERTS // N_GROUPS
TOP_K = 2
EPS = 1e-6

LANES = 128
SUBLANES = 8
OFF_AU, OFF_AV, OFF_Q, OFF_K, OFF_V = 0, 256, 512, 768, 1024
OFF_OG, OFF_CB, OFF_CC, OFF_CX, OFF_GLR = 1536, 2048, 2304, 2560, 2816
P_PAD = OFF_GLR + LANES

MIX_ROWS = 512
COPY_POINTS_PER_BLOCK = 10
MOVE_ROWS = 512
FFN_ROWS = 512
VMEM_LIMIT = 56 * 1024 * 1024

_NT = (((1,), (1,)), ((), ()))
_TN = (((0,), (0,)), ((), ()))


def _dot(a, b, dims=None):
    if dims is None:
        return jnp.dot(a, b, preferred_element_type=jnp.float32)
    return lax.dot_general(a, b, dims, preferred_element_type=jnp.float32)


def _bf(x):
    return x.astype(jnp.bfloat16)


def _split_bf16(x):
    hi = _bf(x)
    lo = _bf(x - hi.astype(jnp.float32))
    return hi, lo


def _rms(x):
    return x * lax.rsqrt(jnp.mean(x * x, axis=-1, keepdims=True) + EPS)


def _pack_rows(v):
    half = v.shape[1] // 2
    hi = pltpu.bitcast(_bf(v[:, :half]).astype(jnp.float32), jnp.uint32)
    lo = pltpu.bitcast(_bf(v[:, half:]).astype(jnp.float32), jnp.uint32)
    return hi | (lo >> 16)


def _unpack_rows(w):
    hi = pltpu.bitcast(w & jnp.uint32(0xFFFF0000), jnp.float32)
    lo = pltpu.bitcast(w << 16, jnp.float32)
    return _bf(hi), _bf(lo)


def _row_copy(src_ref, src_row, dst_ref, dst_row, sem):
    return pltpu.make_async_copy(src_ref.at[pl.ds(src_row, 1)], dst_ref.at[pl.ds(dst_row, 1)], sem)


def _aligned(row):
    return row if isinstance(row, int) else pl.multiple_of(row, SUBLANES)


def _tile_copy(src_ref, src_row, dst_ref, dst_row, sem):
    return pltpu.make_async_copy(src_ref.at[pl.ds(_aligned(src_row), SUBLANES)],
                                 dst_ref.at[pl.ds(_aligned(dst_row), SUBLANES)], sem)


def _store_rows_as_tiles(ref, value):
    rows = value.shape[0]
    assert value.shape[1] == SUBLANES * LANES
    for c in range(SUBLANES):
        ref[pl.ds(c, rows, stride=SUBLANES), :] = value[:, c * LANES:(c + 1) * LANES]


def _load_rows_from_tiles(ref, idx, rows):
    return jnp.concatenate(
        [ref[idx + (pl.ds(c, rows, stride=SUBLANES), slice(None))] for c in range(SUBLANES)], axis=1)


def _mod_kernel(c_ref, w_ref, b_ref, o_ref):
    c = c_ref[...]
    s = c * jax.nn.sigmoid(c)
    s_hi, s_lo = _split_bf16(s)
    w_hi, w_lo = _split_bf16(w_ref[...])
    acc = _dot(s_hi, w_hi) + _dot(s_hi, w_lo) + _dot(s_lo, w_hi)
    o_ref[...] = acc + b_ref[...]


def _modulation(c, w_ada, b_ada):
    depth, d, six_d = w_ada.shape
    bn = c.shape[0]
    cb = 1024
    return pl.pallas_call(
        _mod_kernel,
        grid=(depth, six_d // cb),
        in_specs=[
            pl.BlockSpec((bn, d), lambda l, j: (0, 0)),
            pl.BlockSpec((None, d, cb), lambda l, j: (l, 0, j)),
            pl.BlockSpec((None, 1, cb), lambda l, j: (l, 0, j)),
        ],
        out_specs=pl.BlockSpec((None, bn, cb), lambda l, j: (l, 0, j)),
        out_shape=jax.ShapeDtypeStruct((depth, bn, six_d), jnp.float32),
        compiler_params=pltpu.CompilerParams(
            dimension_semantics=("arbitrary", "arbitrary"), vmem_limit_bytes=VMEM_LIMIT),
        name="adaln_mod",
    )(c, w_ada, b_ada.reshape(depth, 1, six_d))


def _route(h2, rwt_ref, rb_ref, carry, cap):
    tr = h2.shape[0]
    h_hi, h_lo = _split_bf16(h2)
    w_hi, w_lo = _split_bf16(rwt_ref[...])
    logits = (_dot(w_hi, h_hi, _NT) + _dot(w_hi, h_lo, _NT) + _dot(w_lo, h_hi, _NT)) + rb_ref[...]
    ex = jnp.exp(logits - jnp.max(logits, axis=0, keepdims=True))
    probs = ex / jnp.sum(ex, axis=0, keepdims=True)

    idx8 = lax.broadcasted_iota(jnp.int32, (EXPERTS_PER_GROUP, tr), 0)
    best = None
    for g in range(N_GROUPS):
        pg = probs[g * EXPERTS_PER_GROUP:(g + 1) * EXPERTS_PER_GROUP]
        m1 = jnp.max(pg, axis=0, keepdims=True)
        i1 = jnp.min(jnp.where(pg == m1, idx8, EXPERTS_PER_GROUP), axis=0, keepdims=True)
        pg2 = jnp.where(idx8 == i1, -1.0, pg)
        m2 = jnp.max(pg2, axis=0, keepdims=True)
        i2 = jnp.min(jnp.where(pg2 == m2, idx8, EXPERTS_PER_GROUP), axis=0, keepdims=True)
        cand = (m1 + m2, m1, m2, i1 + g * EXPERTS_PER_GROUP, i2 + g * EXPERTS_PER_GROUP)
        if best is None:
            best = cand
        else:
            better = cand[0] > best[0]
            best = tuple(jnp.where(better, a, b) for a, b in zip(cand, best))
    _, p1, p2, e0, e1 = best
    denom = p1 + p2
    w0, w1 = p1 / denom, p2 / denom

    eidx = lax.broadcasted_iota(jnp.int32, (N_EXPERTS, tr), 0)
    hit0, hit1 = eidx == e0, eidx == e1
    onehot = jnp.where(jnp.logical_or(hit0, hit1), 1.0, 0.0)
    before = (lax.broadcasted_iota(jnp.int32, (tr, tr), 0) < lax.broadcasted_iota(jnp.int32, (tr, tr), 1))
    rank = _dot(_bf(onehot), _bf(jnp.where(before, 1.0, 0.0))) + carry
    r0 = jnp.sum(jnp.where(hit0, rank, 0.0), axis=0, keepdims=True).astype(jnp.int32)
    r1 = jnp.sum(jnp.where(hit1, rank, 0.0), axis=0, keepdims=True).astype(jnp.int32)
    carry = carry + jnp.sum(onehot, axis=1, keepdims=True)

    eye = lax.broadcasted_iota(jnp.int32, (tr, tr), 0) == lax.broadcasted_iota(jnp.int32, (tr, tr), 1)
    w0c = jnp.sum(jnp.where(eye, jnp.broadcast_to(w0, (tr, tr)), 0.0), axis=1, keepdims=True)
    w1c = jnp.sum(jnp.where(eye, jnp.broadcast_to(w1, (tr, tr)), 0.0), axis=1, keepdims=True)
    lane8 = lax.broadcasted_iota(jnp.int32, (tr, 8), 1)
    wcol = jnp.where(lane8 == 0, w0c, jnp.where(lane8 == 1, w1c, 0.0))
    return e0 * cap + r0, e1 * cap + r1, wcol, carry


def _zero_segment_padding(cnt_ref, cap, zero_ref, xs_ref, sem):
    copies = []
    for ex in range(N_EXPERTS):
        cnt = cnt_ref[ex, 0]
        seg = ex * cap
        pos = seg + cnt
        end8 = seg + ((cnt + SUBLANES - 1) // SUBLANES) * SUBLANES
        for j in range(SUBLANES - 1):
            copies.append((pos + j < end8, _row_copy(zero_ref, 0, xs_ref, pos + j, sem)))
        seg_end = seg + ((cnt + FFN_ROWS - 1) // FFN_ROWS) * FFN_ROWS
        groups = (seg_end - end8) // SUBLANES
        at = end8
        bit = FFN_ROWS // (2 * SUBLANES)
        while bit >= 1:
            rows = bit * SUBLANES
            pred = (groups & bit) != 0
            copies.append((pred, pltpu.make_async_copy(
                zero_ref.at[pl.ds(0, rows)], xs_ref.at[pl.ds(pl.multiple_of(at, SUBLANES), rows)], sem)))
            at = at + jnp.where(pred, rows, 0)
            bit //= 2
    for pred, cp in copies:
        pl.when(pred)(cp.start)
    for pred, cp in copies:
        pl.when(pred)(cp.wait)


N_MIXER_INPUTS = 17
N_MIXER_OUTPUTS = 5


def _mixer_kernel(*refs, cap, fuse_prev):
    ins, rest = refs[:N_MIXER_INPUTS], refs[N_MIXER_INPUTS:]
    if fuse_prev:
        dp0_ref, dp1_ref, dp0n_ref, dp1n_ref, wp_ref, modp_ref, ysp_ref = rest[:7]
        rest = rest[7:]
    outs, scratch = rest[:N_MIXER_OUTPUTS], rest[N_MIXER_OUTPUTS:]
    (x_ref, mod_ref, g1_ref, win_ref, wout_ref, lng_ref, lnb_ref, ws_ref, bsm_ref,
     wg_ref, bg_ref, glag_ref, cw_ref, cb_ref, g2_ref, rwt_ref, rb_ref) = ins
    o_ref, dest_ref, w_ref, cnt_ref, xs_ref = outs
    (h_ref, pa_ref, pb_ref, y0_ref, y1_ref, y2_ref, y3_ref, s_ref, zc_ref, wsm_ref, h2_ref,
     carry_ref, dvm_ref, d0s_ref, d1s_ref, cvm_ref, csm_ref, zero_ref, sem, sem_s, sem_z,
     ybuf, xin_ref, sem_g) = scratch

    bi, ti = pl.program_id(0), pl.program_id(1)
    tm = x_ref.shape[0]
    n_blk = tm // GMLP_BLOCK
    step = bi * pl.num_programs(1) + ti
    last_step = step == pl.num_programs(0) * pl.num_programs(1) - 1
    slot = step % 2
    spare_row0 = N_EXPERTS * cap

    def drain_rows(src_ref, dst_ref, s, copy=_row_copy):
        def body(g, c):
            for _ in range(TOP_K * SUBLANES):
                copy(src_ref, 0, dst_ref, 0, s).wait()
            return c
        lax.fori_loop(0, tm // SUBLANES, body, 0)

    if fuse_prev:
        @pl.when(step == 0)
        def _():
            def issue(g, c):
                for jj in range(SUBLANES):
                    i = pl.multiple_of(g * SUBLANES, SUBLANES) + jj
                    _tile_copy(ysp_ref, dp0_ref[i], ybuf.at[0, 0], i * SUBLANES, sem_g.at[0]).start()
                    _tile_copy(ysp_ref, dp1_ref[i], ybuf.at[0, 1], i * SUBLANES, sem_g.at[0]).start()
                return c
            lax.fori_loop(0, tm // SUBLANES, issue, 0)

        drain_rows(ysp_ref, ybuf.at[0, 0], sem_g.at[slot], _tile_copy)
        wp = wp_ref[...]
        xin_ref[...] = x_ref[...] + modp_ref[...][5:6] * (
            wp[:, 0:1] * _load_rows_from_tiles(ybuf, (slot, 0), tm)
            + wp[:, 1:2] * _load_rows_from_tiles(ybuf, (slot, 1), tm))
        x_in = xin_ref
    else:
        x_in = x_ref

    @pl.when(step == 0)
    def _():
        carry_ref[...] = jnp.zeros_like(carry_ref)
        h2_ref[...] = jnp.zeros_like(h2_ref)

        def fill(i, c):
            d0s_ref[i] = spare_row0 + i
            d1s_ref[i] = spare_row0 + tm + i
            return c

        lax.fori_loop(0, tm, fill, 0)
        tt = lax.broadcasted_iota(jnp.int32, (GMLP_BLOCK, GMLP_BLOCK), 0) // CHUNK
        ss = lax.broadcasted_iota(jnp.int32, (GMLP_BLOCK, GMLP_BLOCK), 1) // CHUNK
        for h in range(A_HEADS):
            wsm_ref[h] = _bf(jnp.where(tt >= ss, ws_ref[h], 0.0))

    @pl.when(ti == 0)
    def _():
        s_ref[...] = jnp.zeros_like(s_ref)
        zc_ref[...] = jnp.zeros_like(zc_ref)

    m = mod_ref[...]
    sh1, sc1, gt1, sh2, sc2 = m[0:1], m[1:2], m[2:3], m[3:4], m[4:5]
    h_ref[...] = _bf((_rms(x_in[...]) * g1_ref[...]) * (1.0 + sc1) + sh1)

    lane256 = lax.broadcasted_iota(jnp.int32, (CHUNK, B_KDIM), 1)
    head_of_lane = lane256 // B_DK
    r64 = lax.broadcasted_iota(jnp.int32, (CHUNK, CHUNK), 0)
    c64 = lax.broadcasted_iota(jnp.int32, (CHUNK, CHUNK), 1)
    tri = _bf(jnp.where(r64 >= c64, 1.0, 0.0))
    causal4 = jnp.concatenate([r64 >= c64] * B_HEADS, axis=0)
    eye256 = (lax.broadcasted_iota(jnp.int32, (B_KDIM, B_KDIM), 0)
              == lax.broadcasted_iota(jnp.int32, (B_KDIM, B_KDIM), 1))
    a_head_of_lane = lax.broadcasted_iota(jnp.int32, (GMLP_BLOCK, A_DIM), 1) // (A_DIM // A_HEADS)
    row128 = lax.broadcasted_iota(jnp.int32, (GMLP_BLOCK, C_DIM), 0)
    sqrt_half = 0.7071067811865476

    def gelu(v):
        return 0.5 * v * (1.0 + lax.erf(v * sqrt_half))

    h2_prev = h2_ref.at[1 - slot]
    row_copies = []
    for i in range(tm):
        row_copies.append(lambda i=i: _row_copy(h2_prev, i, xs_ref, d0s_ref[i], sem).start(priority=0))
        row_copies.append(lambda i=i: _row_copy(h2_prev, i, xs_ref, d1s_ref[i], sem).start(priority=1))
        if fuse_prev:
            row_copies.append(lambda i=i: _tile_copy(
                ysp_ref, dp0n_ref[i], ybuf.at[1 - slot, 0], i * SUBLANES, sem_g.at[1 - slot]).start(priority=0))
            row_copies.append(lambda i=i: _tile_copy(
                ysp_ref, dp1n_ref[i], ybuf.at[1 - slot, 1], i * SUBLANES, sem_g.at[1 - slot]).start(priority=1))
    row_copies.reverse()
    copies_per_point = -(-len(row_copies) // (n_blk * COPY_POINTS_PER_BLOCK))

    def start_row_copies():
        for _ in range(min(copies_per_point, len(row_copies))):
            row_copies.pop()()

    p_halves = (pa_ref, pb_ref)
    y_blocks = (y0_ref, y1_ref, y2_ref, y3_ref)
    assert n_blk == len(y_blocks)

    def project(half):
        rows = slice(half * (tm // 2), (half + 1) * (tm // 2))
        p_halves[half][...] = _dot(h_ref[rows, :], win_ref[...])

    def block(j, state, zprev):
        p_ref, y_ref = p_halves[j // (n_blk // 2)], y_blocks[j]
        r0 = (j % (n_blk // 2)) * GMLP_BLOCK
        rows = slice(r0, r0 + GMLP_BLOCK)

        u = gelu(p_ref[rows, OFF_AU:OFF_AU + A_DIM])
        vv = gelu(p_ref[rows, OFF_AV:OFF_AV + A_DIM])
        mu = jnp.mean(vv, axis=-1, keepdims=True)
        var = jnp.mean((vv - mu) ** 2, axis=-1, keepdims=True)
        vn = _bf((vv - mu) * lax.rsqrt(var + EPS) * lng_ref[...] + lnb_ref[...])
        sv = jnp.zeros((GMLP_BLOCK, A_DIM), jnp.float32)
        for hh in range(A_HEADS):
            sv = jnp.where(a_head_of_lane == hh, _dot(wsm_ref[hh], vn), sv)
        y_ref[:, 0:A_DIM] = _bf(u * (sv + bsm_ref[...]))
        start_row_copies()

        z = _dot(_bf(p_ref[rows, OFF_GLR:OFF_GLR + LANES]), wg_ref[...]) + bg_ref[...]
        glog = (jnp.minimum(z, 0.0) - jnp.log1p(jnp.exp(-jnp.abs(z)))) / GLA_TAU
        for c in range(GMLP_BLOCK // CHUNK):
            rc = slice(r0 + c * CHUNK, r0 + (c + 1) * CHUNK)
            g = glog[c * CHUNK:(c + 1) * CHUNK]
            g_hi = _bf(g)
            g_r1 = g - g_hi.astype(jnp.float32)
            g_mid = _bf(g_r1)
            g_lo = _bf(g_r1 - g_mid.astype(jnp.float32))
            b = _dot(tri, g_hi) + _dot(tri, g_mid) + _dot(tri, g_lo)
            b_mid = b[CHUNK // 2:CHUNK // 2 + 1]
            b_last = b[CHUNK - 1:CHUNK]
            q = p_ref[rc, OFF_Q:OFF_Q + B_KDIM] * (B_DK ** -0.5)
            k = p_ref[rc, OFF_K:OFF_K + B_KDIM]
            vb = _bf(p_ref[rc, OFF_V:OFF_V + B_VDIM])
            qs = q * jnp.exp(b - b_mid)
            ks = _bf(k * jnp.exp(b_mid - b))
            kd = _bf(k * jnp.exp(b_last - b))
            qb = q * jnp.exp(b)
            qs_st = _bf(jnp.concatenate(
                [jnp.where(head_of_lane == hh, qs, 0.0) for hh in range(B_HEADS)], axis=0))
            qb_st = _bf(jnp.concatenate(
                [jnp.where(head_of_lane == hh, qb, 0.0) for hh in range(B_HEADS)], axis=0))
            scores = _bf(jnp.where(causal4, _dot(qs_st, ks, _NT), 0.0))
            start_row_copies()
            o_inter = _dot(qb_st, _bf(state))
            kv_all = _dot(kd, vb, _TN)
            start_row_copies()
            outs, kvs = [], []
            for hh in range(B_HEADS):
                rs = slice(hh * CHUNK, (hh + 1) * CHUNK)
                cs = slice(hh * B_DV, (hh + 1) * B_DV)
                o_h = _dot(scores[rs], vb[:, cs]) + o_inter[rs]
                outs.append(_rms(o_h))
                kvs.append(kv_all[rs, cs])
            decay_row = jnp.broadcast_to(jnp.exp(b_last), (B_KDIM, B_KDIM))
            decay_col = jnp.sum(jnp.where(eye256, decay_row, 0.0), axis=1, keepdims=True)
            state = decay_col * state + jnp.concatenate(kvs, axis=0)
            on = jnp.concatenate(outs, axis=1) * glag_ref[...]
            og = p_ref[rc, OFF_OG:OFF_OG + B_VDIM]
            y_ref[c * CHUNK:(c + 1) * CHUNK, A_DIM:A_DIM + B_VDIM] = _bf(on * (og * jax.nn.sigmoid(og)))
            start_row_copies()

        zz = p_ref[rows, OFF_CC:OFF_CC + C_DIM] * p_ref[rows, OFF_CX:OFF_CX + C_DIM]
        z1 = jnp.where(row128 == 0, zprev[7:8], pltpu.roll(zz, 1, 0))
        z2 = jnp.where(row128 == 0, zprev[6:7], jnp.where(row128 == 1, zprev[7:8], pltpu.roll(zz, 2, 0)))
        cw = cw_ref[...]
        yc = cb_ref[...] + cw[0:1] * z2
        yc = yc + cw[1:2] * z1
        yc = yc + cw[2:3] * zz
        y_ref[:, A_DIM + B_VDIM:A_DIM + B_VDIM + C_DIM] = _bf(p_ref[rows, OFF_CB:OFF_CB + C_DIM] * yc)
        start_row_copies()
        return state, zz[GMLP_BLOCK - SUBLANES:GMLP_BLOCK]

    state, zprev, carry = s_ref[...], zc_ref[...], carry_ref[...]
    project(0)
    for j in range(n_blk):
        if j == 0:
            project(1)
        state, zprev = block(j, state, zprev)
        rows = slice(j * GMLP_BLOCK, (j + 1) * GMLP_BLOCK)
        xo = x_in[rows, :] + gt1 * _dot(y_blocks[j][...], wout_ref[...])
        o_ref[rows, :] = xo
        start_row_copies()
        h2 = (_rms(xo) * g2_ref[...]) * (1.0 + sc2) + sh2
        h2_ref[slot, rows, :] = _pack_rows(h2)
        d0, d1, wcol, carry = _route(h2, rwt_ref, rb_ref, carry, cap)
        dvm_ref[0:1, rows] = d0
        dvm_ref[1:2, rows] = d1
        w_ref[rows, :] = wcol
        start_row_copies()
    while row_copies:
        row_copies.pop()()
    s_ref[...] = state
    zc_ref[...] = zprev
    carry_ref[...] = carry
    cnt = jnp.broadcast_to(carry, cnt_ref.shape).astype(jnp.int32)
    cnt_ref[...] = cnt
    dest_ref[...] = dvm_ref[0:TOP_K, :]

    drain_rows(h2_prev, xs_ref, sem)

    to_smem = [pltpu.make_async_copy(dvm_ref.at[0], d0s_ref, sem_s),
               pltpu.make_async_copy(dvm_ref.at[1], d1s_ref, sem_s)]
    for cp in to_smem:
        cp.start()
    for cp in to_smem:
        cp.wait()

    @pl.when(last_step)
    def _():
        h2_last = h2_ref.at[slot]

        def issue(g, c):
            for jj in range(SUBLANES):
                i = pl.multiple_of(g * SUBLANES, SUBLANES) + jj
                _row_copy(h2_last, i, xs_ref, d0s_ref[i], sem).start()
                _row_copy(h2_last, i, xs_ref, d1s_ref[i], sem).start()
            return c

        lax.fori_loop(0, tm // SUBLANES, issue, 0)
        drain_rows(h2_last, xs_ref, sem)
        if fuse_prev:
            drain_rows(ysp_ref, ybuf.at[0, 0], sem_g.at[1 - slot], _tile_copy)
        cvm_ref[...] = cnt
        zero_ref[...] = jnp.zeros_like(zero_ref)
        cp = pltpu.make_async_copy(cvm_ref, csm_ref, sem_s)
        cp.start()
        cp.wait()
        _zero_segment_padding(csm_ref, cap, zero_ref, xs_ref, sem_z)


def _mixer(x, mod, g1, g2, w_in, w_out, ln_g, ln_b, w_s, b_s, w_gate, b_gate, gla_g, conv_w, conv_b,
           router_w, router_b, prev=None):
    bn, t, d = x.shape
    n = bn * t
    tm = min(MIX_ROWS, t)
    nt = t // tm
    cap = n
    glr0 = 2 * A_DIM + 2 * B_KDIM + B_VDIM
    w_in_p = _bf(jnp.concatenate(
        [w_in[:, :glr0], w_in[:, glr0 + GLA_RANK:], w_in[:, glr0:glr0 + GLA_RANK],
         jnp.zeros((d, LANES - GLA_RANK), w_in.dtype)], axis=1))
    w_gate_p = _bf(jnp.concatenate(
        [w_gate, jnp.zeros((LANES - GLA_RANK, B_KDIM), w_gate.dtype)], axis=0))
    bsm = jnp.repeat(b_s.T, A_DIM // A_HEADS, axis=1)

    def whole(shape):
        return pl.BlockSpec(shape, lambda b, i: (0,) * len(shape))

    prev_specs, prev_args = [], ()
    if prev is not None:
        n_tiles = bn * nt

        def this_tile(b, i):
            return (b * nt + i,)

        def next_tile(b, i):
            return (jnp.minimum(b * nt + i + 1, n_tiles - 1),)

        d0p, d1p, wcolp, modp, ysp = prev
        prev_specs = [
            pl.BlockSpec((tm,), this_tile, memory_space=pltpu.SMEM),
            pl.BlockSpec((tm,), this_tile, memory_space=pltpu.SMEM),
            pl.BlockSpec((tm,), next_tile, memory_space=pltpu.SMEM),
            pl.BlockSpec((tm,), next_tile, memory_space=pltpu.SMEM),
            pl.BlockSpec((tm, 8), lambda b, i: (b * nt + i, 0)),
            pl.BlockSpec((None, 6, d), lambda b, i: (b, 0, 0)),
            pl.BlockSpec(memory_space=pl.ANY),
        ]
        prev_args = (d0p, d1p, d0p, d1p, wcolp, modp, ysp)

    return pl.pallas_call(
        functools.partial(_mixer_kernel, cap=cap, fuse_prev=prev is not None),
        grid=(bn, nt),
        in_specs=[
            pl.BlockSpec((None, tm, d), lambda b, i: (b, i, 0)),
            pl.BlockSpec((None, 6, d), lambda b, i: (b, 0, 0)),
            whole((1, d)),
            whole((d, P_PAD)),
            whole((d, d)),
            whole((1, A_DIM)),
            whole((1, A_DIM)),
            whole((A_HEADS, GMLP_BLOCK, GMLP_BLOCK)),
            whole((GMLP_BLOCK, A_DIM)),
            whole((LANES, B_KDIM)),
            whole((1, B_KDIM)),
            whole((1, B_VDIM)),
            whole((3, C_DIM)),
            whole((1, C_DIM)),
            whole((1, d)),
            whole((N_EXPERTS, d)),
            whole((N_EXPERTS, 1)),
        ] + prev_specs,
        out_specs=[
            pl.BlockSpec((None, tm, d), lambda b, i: (b, i, 0)),
            pl.BlockSpec((TOP_K, tm), lambda b, i: (0, b * nt + i)),
            pl.BlockSpec((tm, 8), lambda b, i: (b * nt + i, 0)),
            whole((N_EXPERTS, LANES)),
            pl.BlockSpec(memory_space=pl.ANY),
        ],
        out_shape=[
            jax.ShapeDtypeStruct(x.shape, x.dtype),
            jax.ShapeDtypeStruct((TOP_K, n), jnp.int32),
            jax.ShapeDtypeStruct((n, 8), jnp.float32),
            jax.ShapeDtypeStruct((N_EXPERTS, LANES), jnp.int32),
            jax.ShapeDtypeStruct((N_EXPERTS * cap + TOP_K * tm, d // 2), jnp.uint32),
        ],
        scratch_shapes=[
            pltpu.VMEM((tm, d), jnp.bfloat16),
            pltpu.VMEM((tm // 2, P_PAD), jnp.float32),
            pltpu.VMEM((tm // 2, P_PAD), jnp.float32),
            pltpu.VMEM((GMLP_BLOCK, d), jnp.bfloat16),
            pltpu.VMEM((GMLP_BLOCK, d), jnp.bfloat16),
            pltpu.VMEM((GMLP_BLOCK, d), jnp.bfloat16),
            pltpu.VMEM((GMLP_BLOCK, d), jnp.bfloat16),
            pltpu.VMEM((B_KDIM, B_DV), jnp.float32),
            pltpu.VMEM((SUBLANES, C_DIM), jnp.float32),
            pltpu.VMEM((A_HEADS, GMLP_BLOCK, GMLP_BLOCK), jnp.bfloat16),
            pltpu.VMEM((2, tm, d // 2), jnp.uint32),
            pltpu.VMEM((N_EXPERTS, 1), jnp.float32),
            pltpu.VMEM((SUBLANES, tm), jnp.int32),
            pltpu.SMEM((tm,), jnp.int32),
            pltpu.SMEM((tm,), jnp.int32),
            pltpu.VMEM((N_EXPERTS, LANES), jnp.int32),
            pltpu.SMEM((N_EXPERTS, LANES), jnp.int32),
            pltpu.VMEM((FFN_ROWS // 2, d // 2), jnp.uint32),
            pltpu.SemaphoreType.DMA,
            pltpu.SemaphoreType.DMA,
            pltpu.SemaphoreType.DMA,
            pltpu.VMEM((2, TOP_K, tm * SUBLANES, LANES), jnp.float32),
            pltpu.VMEM((tm, d), jnp.float32),
            pltpu.SemaphoreType.DMA((2,)),
        ],
        compiler_params=pltpu.CompilerParams(
            dimension_semantics=("arbitrary", "arbitrary"), vmem_limit_bytes=VMEM_LIMIT),
        name="mixer",
    )(x, mod, g1.reshape(1, d), w_in_p, _bf(w_out), ln_g.reshape(1, -1), ln_b.reshape(1, -1), w_s, bsm,
      w_gate_p, b_gate.reshape(1, -1), gla_g.reshape(1, -1), conv_w, conv_b.reshape(1, -1),
      g2.reshape(1, d), router_w.T, router_b.reshape(N_EXPERTS, 1), *prev_args)


def _ffn_kernel(be_ref, rb_ref, nb_ref, xs_ref, w1_ref, w3_ref, w2_ref, ys_ref, w1b_ref, w3b_ref, w2b_ref):
    del rb_ref
    i = pl.program_id(0)
    used = i < nb_ref[0]
    new_expert = jnp.logical_or(i == 0, be_ref[i] != be_ref[jnp.maximum(i - 1, 0)])

    @pl.when(jnp.logical_and(used, new_expert))
    def _():
        w1b_ref[...] = _bf(w1_ref[...])
        w3b_ref[...] = _bf(w3_ref[...])
        w2b_ref[...] = _bf(w2_ref[...])

    @pl.when(used)
    def _():
        x_a, x_b = _unpack_rows(xs_ref[...])
        half = x_a.shape[1]
        a = _dot(x_a, w1b_ref[0:half, :]) + _dot(x_b, w1b_ref[half:, :])
        g = _dot(x_a, w3b_ref[0:half, :]) + _dot(x_b, w3b_ref[half:, :])
        hm = _bf((a * jax.nn.sigmoid(a)) * g)
        _store_rows_as_tiles(ys_ref, _dot(hm, w2b_ref[...]))


def _expert_ffn(layer, block_e, row_block, n_used, xs, w1, w3, w2):
    n_rows = xs.shape[0]
    n_blocks = block_e.shape[0]
    d, f = w1.shape[-2], w1.shape[-1]
    return pl.pallas_call(
        _ffn_kernel,
        grid_spec=pltpu.PrefetchScalarGridSpec(
            num_scalar_prefetch=3,
            grid=(n_blocks,),
            in_specs=[
                pl.BlockSpec((FFN_ROWS, d // 2), lambda i, be, rb, nb: (rb[i], 0)),
                pl.BlockSpec((None, None, d, f), lambda i, be, rb, nb: (layer, be[i], 0, 0)),
                pl.BlockSpec((None, None, d, f), lambda i, be, rb, nb: (layer, be[i], 0, 0)),
                pl.BlockSpec((None, None, f, d), lambda i, be, rb, nb: (layer, be[i], 0, 0)),
            ],
            out_specs=pl.BlockSpec((FFN_ROWS * SUBLANES, LANES), lambda i, be, rb, nb: (rb[i], 0)),
            scratch_shapes=[pltpu.VMEM((d, f), jnp.bfloat16), pltpu.VMEM((d, f), jnp.bfloat16),
                            pltpu.VMEM((f, d), jnp.bfloat16)],
        ),
        out_shape=jax.ShapeDtypeStruct((n_rows * SUBLANES, LANES), jnp.float32),
        compiler_params=pltpu.CompilerParams(
            dimension_semantics=("arbitrary",), vmem_limit_bytes=VMEM_LIMIT),
        name="expert_ffn",
    )(block_e, row_block, n_used, xs, w1, w3, w2)


def _combine_kernel(d0_ref, d1_ref, d0n_ref, d1n_ref, x_ref, mod_ref, w_ref, gf_ref, ys_ref, o_ref, ybuf, sems,
                    *, final_norm):
    tm = x_ref.shape[0]
    step = pl.program_id(0) * pl.num_programs(1) + pl.program_id(1)
    last = step == pl.num_programs(0) * pl.num_programs(1) - 1
    slot = step % 2

    def drain(sem):
        def body(g, c):
            for _ in range(TOP_K * SUBLANES):
                _tile_copy(ys_ref, 0, ybuf.at[0, 0], 0, sem).wait()
            return c
        lax.fori_loop(0, tm // SUBLANES, body, 0)

    @pl.when(step == 0)
    def _():
        def issue(g, c):
            for jj in range(SUBLANES):
                i = pl.multiple_of(g * SUBLANES, SUBLANES) + jj
                _tile_copy(ys_ref, d0_ref[i], ybuf.at[0, 0], i * SUBLANES, sems.at[0]).start()
                _tile_copy(ys_ref, d1_ref[i], ybuf.at[0, 1], i * SUBLANES, sems.at[0]).start()
            return c
        lax.fori_loop(0, tm // SUBLANES, issue, 0)

    nxt, sem_nxt = ybuf.at[1 - slot], sems.at[1 - slot]
    for i in range(tm):
        _tile_copy(ys_ref, d0n_ref[i], nxt.at[0], i * SUBLANES, sem_nxt).start(priority=0)
        _tile_copy(ys_ref, d1n_ref[i], nxt.at[1], i * SUBLANES, sem_nxt).start(priority=1)

    drain(sems.at[slot])
    w = w_ref[...]
    gt2 = mod_ref[...][5:6]
    out = x_ref[...] + gt2 * (w[:, 0:1] * _load_rows_from_tiles(ybuf, (slot, 0), tm)
                              + w[:, 1:2] * _load_rows_from_tiles(ybuf, (slot, 1), tm))
    if final_norm:
        out = _rms(out) * gf_ref[...]
    o_ref[...] = out

    @pl.when(last)
    def _():
        drain(sems.at[1 - slot])


def _combine(d0, d1, x, mod, wcol, g_final, ys, final_norm):
    bn, t, d = x.shape
    tm = min(MOVE_ROWS, t)
    nt = t // tm
    n_tiles = bn * nt

    def this_tile(b, i):
        return (b * nt + i,)

    def next_tile(b, i):
        return (jnp.minimum(b * nt + i + 1, n_tiles - 1),)

    return pl.pallas_call(
        functools.partial(_combine_kernel, final_norm=final_norm),
        grid=(bn, nt),
        in_specs=[
            pl.BlockSpec((tm,), this_tile, memory_space=pltpu.SMEM),
            pl.BlockSpec((tm,), this_tile, memory_space=pltpu.SMEM),
            pl.BlockSpec((tm,), next_tile, memory_space=pltpu.SMEM),
            pl.BlockSpec((tm,), next_tile, memory_space=pltpu.SMEM),
            pl.BlockSpec((None, tm, d), lambda b, i: (b, i, 0)),
            pl.BlockSpec((None, 6, d), lambda b, i: (b, 0, 0)),
            pl.BlockSpec((tm, 8), lambda b, i: (b * nt + i, 0)),
            pl.BlockSpec((1, d), lambda b, i: (0, 0)),
            pl.BlockSpec(memory_space=pl.ANY),
        ],
        out_specs=pl.BlockSpec((None, tm, d), lambda b, i: (b, i, 0)),
        out_shape=jax.ShapeDtypeStruct(x.shape, x.dtype),
        scratch_shapes=[pltpu.VMEM((2, TOP_K, tm * SUBLANES, LANES), jnp.float32), pltpu.SemaphoreType.DMA((2,))],
        compiler_params=pltpu.CompilerParams(
            dimension_semantics=("arbitrary", "arbitrary"), vmem_limit_bytes=VMEM_LIMIT),
        name="combine",
    )(d0, d1, d0, d1, x, mod, wcol, g_final.reshape(1, d), ys)


def _block_map(counts, cap, n_blocks):
    blocks_per_expert = (counts + FFN_ROWS - 1) // FFN_ROWS
    ends = jnp.cumsum(blocks_per_expert)
    n_used = ends[-1:]
    step = jnp.minimum(jnp.arange(n_blocks, dtype=jnp.int32), n_used - 1)
    block_e = jnp.minimum(jnp.sum((ends[None, :] <= step[:, None]).astype(jnp.int32), axis=1), N_EXPERTS - 1)
    first = (ends - blocks_per_expert)[block_e]
    row_block = block_e * (cap // FFN_ROWS) + (step - first)
    return block_e.astype(jnp.int32), row_block.astype(jnp.int32), n_used.astype(jnp.int32)


def kernel(x, c, w_ada, b_ada, g_norm1, g_norm2, w_in, w_out, gmlp_ln_g, gmlp_ln_b, gmlp_ws, gmlp_bs,
           gla_w_gate, gla_b_gate, gla_norm_g, conv_w, conv_b, router_w, router_b, exp_w1, exp_w3, exp_w2,
           g_final):
    depth = w_ada.shape[0]
    bn, t, d = x.shape
    n = bn * t
    n_blocks = -(-(n * TOP_K) // FFN_ROWS) + N_EXPERTS
    mod_all = _modulation(c, w_ada, b_ada).reshape(depth, bn, 6, d)
    prev = None
    for l in range(depth):
        mod = mod_all[l]
        x, dest, wcol, cnt, xs = _mixer(
            x, mod, g_norm1[l], g_norm2[l], w_in[l], w_out[l], gmlp_ln_g[l], gmlp_ln_b[l], gmlp_ws[l],
            gmlp_bs[l], gla_w_gate[l], gla_b_gate[l], gla_norm_g[l], conv_w[l], conv_b[l], router_w, router_b,
            prev=prev)
        block_e, row_block, n_used = _block_map(cnt[:, 0], n, n_blocks)
        ys = _expert_ffn(l, block_e, row_block, n_used, xs, exp_w1, exp_w3, exp_w2)
        prev = (dest[0] * SUBLANES, dest[1] * SUBLANES, wcol, mod, ys)
    d0, d1, wcol, mod, ys = prev
    return _combine(d0, d1, x, mod, wcol, g_final, ys, final_norm=True)
```

```python
import functools

import jax
import jax.numpy as jnp
from jax import lax
from jax.experimental import pallas as pl
from jax.experimental.pallas import tpu as pltpu

CHUNK = 64
GMLP_BLOCK = 128
A_DIM = 256
A_HEADS = 4
B_HEADS = 4
B_DK = 64
B_DV = 128
B_KDIM = B_HEADS * B_DK
B_VDIM = B_HEADS * B_DV
GLA_RANK = 16
GLA_TAU = 16.0
C_DIM = 256
N_EXPERTS = 32
N_GROUPS = 4
EXPERTS_PER_GROUP = N_EXPERTS // N_GROUPS
TOP_K = 2
EPS = 1e-6

LANES = 128
SUBLANES = 8
OFF_AU, OFF_AV, OFF_Q, OFF_K, OFF_V = 0, 256, 512, 768, 1024
OFF_OG, OFF_CB, OFF_CC, OFF_CX, OFF_GLR = 1536, 2048, 2304, 2560, 2816
P_PAD = OFF_GLR + LANES

MIX_ROWS = 512
COPY_POINTS_PER_BLOCK = 18
MOVE_ROWS = 512
FFN_ROWS = 512
VMEM_LIMIT = 56 * 1024 * 1024

_NT = (((1,), (1,)), ((), ()))
_TN = (((0,), (0,)), ((), ()))


def _dot(a, b, dims=None):
    if dims is None:
        return jnp.dot(a, b, preferred_element_type=jnp.float32)
    return lax.dot_general(a, b, dims, preferred_element_type=jnp.float32)


def _bf(x):
    return x.astype(jnp.bfloat16)


def _split_bf16(x):
    hi = _bf(x)
    lo = _bf(x - hi.astype(jnp.float32))
    return hi, lo


def _rms(x):
    return x * lax.rsqrt(jnp.mean(x * x, axis=-1, keepdims=True) + EPS)


def _pack_rows(v):
    half = v.shape[1] // 2
    hi = pltpu.bitcast(_bf(v[:, :half]).astype(jnp.float32), jnp.uint32)
    lo = pltpu.bitcast(_bf(v[:, half:]).astype(jnp.float32), jnp.uint32)
    return hi | (lo >> 16)


def _unpack_rows(w):
    hi = pltpu.bitcast(w & jnp.uint32(0xFFFF0000), jnp.float32)
    lo = pltpu.bitcast(w << 16, jnp.float32)
    return _bf(hi), _bf(lo)


def _row_copy(src_ref, src_row, dst_ref, dst_row, sem):
    return pltpu.make_async_copy(src_ref.at[pl.ds(src_row, 1)], dst_ref.at[pl.ds(dst_row, 1)], sem)


def _aligned(row):
    return row if isinstance(row, int) else pl.multiple_of(row, SUBLANES)


def _tile_copy(src_ref, src_row, dst_ref, dst_row, sem):
    return pltpu.make_async_copy(src_ref.at[pl.ds(_aligned(src_row), SUBLANES)],
                                 dst_ref.at[pl.ds(_aligned(dst_row), SUBLANES)], sem)


def _store_rows_as_tiles(ref, value):
    rows = value.shape[0]
    assert value.shape[1] == SUBLANES * LANES
    for c in range(SUBLANES):
        ref[pl.ds(c, rows, stride=SUBLANES), :] = value[:, c * LANES:(c + 1) * LANES]


def _load_rows_from_tiles(ref, idx, rows):
    return jnp.concatenate(
        [ref[idx + (pl.ds(c, rows, stride=SUBLANES), slice(None))] for c in range(SUBLANES)], axis=1)


def _mod_kernel(c_ref, w_ref, b_ref, o_ref):
    c = c_ref[...]
    s = c * jax.nn.sigmoid(c)
    s_hi, s_lo = _split_bf16(s)
    w_hi, w_lo = _split_bf16(w_ref[...])
    acc = _dot(s_hi, w_hi) + _dot(s_hi, w_lo) + _dot(s_lo, w_hi)
    o_ref[...] = acc + b_ref[...]


def _modulation(c, w_ada, b_ada):
    depth, d, six_d = w_ada.shape
    bn = c.shape[0]
    cb = 1024
    return pl.pallas_call(
        _mod_kernel,
        grid=(depth, six_d // cb),
        in_specs=[
            pl.BlockSpec((bn, d), lambda l, j: (0, 0)),
            pl.BlockSpec((None, d, cb), lambda l, j: (l, 0, j)),
            pl.BlockSpec((None, 1, cb), lambda l, j: (l, 0, j)),
        ],
        out_specs=pl.BlockSpec((None, bn, cb), lambda l, j: (l, 0, j)),
        out_shape=jax.ShapeDtypeStruct((depth, bn, six_d), jnp.float32),
        compiler_params=pltpu.CompilerParams(
            dimension_semantics=("arbitrary", "arbitrary"), vmem_limit_bytes=VMEM_LIMIT),
        name="adaln_mod",
    )(c, w_ada, b_ada.reshape(depth, 1, six_d))


def _route(h2, rwt_ref, rb_ref, carry, cap):
    tr = h2.shape[0]
    h_hi, h_lo = _split_bf16(h2)
    w_hi, w_lo = _split_bf16(rwt_ref[...])
    logits = (_dot(w_hi, h_hi, _NT) + _dot(w_hi, h_lo, _NT) + _dot(w_lo, h_hi, _NT)) + rb_ref[...]
    ex = jnp.exp(logits - jnp.max(logits, axis=0, keepdims=True))
    probs = ex / jnp.sum(ex, axis=0, keepdims=True)

    idx8 = lax.broadcasted_iota(jnp.int32, (EXPERTS_PER_GROUP, tr), 0)
    best = None
    for g in range(N_GROUPS):
        pg = probs[g * EXPERTS_PER_GROUP:(g + 1) * EXPERTS_PER_GROUP]
        m1 = jnp.max(pg, axis=0, keepdims=True)
        i1 = jnp.min(jnp.where(pg == m1, idx8, EXPERTS_PER_GROUP), axis=0, keepdims=True)
        pg2 = jnp.where(idx8 == i1, -1.0, pg)
        m2 = jnp.max(pg2, axis=0, keepdims=True)
        i2 = jnp.min(jnp.where(pg2 == m2, idx8, EXPERTS_PER_GROUP), axis=0, keepdims=True)
        cand = (m1 + m2, m1, m2, i1 + g * EXPERTS_PER_GROUP, i2 + g * EXPERTS_PER_GROUP)
        if best is None:
            best = cand
        else:
            better = cand[0] > best[0]
            best = tuple(jnp.where(better, a, b) for a, b in zip(cand, best))
    _, p1, p2, e0, e1 = best
    denom = p1 + p2
    w0, w1 = p1 / denom, p2 / denom

    eidx = lax.broadcasted_iota(jnp.int32, (N_EXPERTS, tr), 0)
    hit0, hit1 = eidx == e0, eidx == e1
    onehot = jnp.where(jnp.logical_or(hit0, hit1), 1.0, 0.0)
    before = (lax.broadcasted_iota(jnp.int32, (tr, tr), 0) < lax.broadcasted_iota(jnp.int32, (tr, tr), 1))
    rank = _dot(_bf(onehot), _bf(jnp.where(before, 1.0, 0.0))) + carry
    r0 = jnp.sum(jnp.where(hit0, rank, 0.0), axis=0, keepdims=True).astype(jnp.int32)
    r1 = jnp.sum(jnp.where(hit1, rank, 0.0), axis=0, keepdims=True).astype(jnp.int32)
    carry = carry + jnp.sum(onehot, axis=1, keepdims=True)

    eye = lax.broadcasted_iota(jnp.int32, (tr, tr), 0) == lax.broadcasted_iota(jnp.int32, (tr, tr), 1)
    w0c = jnp.sum(jnp.where(eye, jnp.broadcast_to(w0, (tr, tr)), 0.0), axis=1, keepdims=True)
    w1c = jnp.sum(jnp.where(eye, jnp.broadcast_to(w1, (tr, tr)), 0.0), axis=1, keepdims=True)
    lane8 = lax.broadcasted_iota(jnp.int32, (tr, 8), 1)
    wcol = jnp.where(lane8 == 0, w0c, jnp.where(lane8 == 1, w1c, 0.0))
    return e0 * cap + r0, e1 * cap + r1, wcol, carry


def _zero_segment_padding(cnt_ref, cap, zero_ref, xs_ref, sem):
    copies = []
    for ex in range(N_EXPERTS):
        cnt = cnt_ref[ex, 0]
        seg = ex * cap
        pos = seg + cnt
        end8 = seg + ((cnt + SUBLANES - 1) // SUBLANES) * SUBLANES
        for j in range(SUBLANES - 1):
            copies.append((pos + j < end8, _row_copy(zero_ref, 0, xs_ref, pos + j, sem)))
        seg_end = seg + ((cnt + FFN_ROWS - 1) // FFN_ROWS) * FFN_ROWS
        groups = (seg_end - end8) // SUBLANES
        at = end8
        bit = FFN_ROWS // (2 * SUBLANES)
        while bit >= 1:
            rows = bit * SUBLANES
            pred = (groups & bit) != 0
            copies.append((pred, pltpu.make_async_copy(
                zero_ref.at[pl.ds(0, rows)], xs_ref.at[pl.ds(pl.multiple_of(at, SUBLANES), rows)], sem)))
            at = at + jnp.where(pred, rows, 0)
            bit //= 2
    for pred, cp in copies:
        pl.when(pred)(cp.start)
    for pred, cp in copies:
        pl.when(pred)(cp.wait)


N_MIXER_INPUTS = 17
N_MIXER_OUTPUTS = 5


def _mixer_kernel(*refs, cap, fuse_prev):
    ins, rest = refs[:N_MIXER_INPUTS], refs[N_MIXER_INPUTS:]
    if fuse_prev:
        dp0_ref, dp1_ref, dp0n_ref, dp1n_ref, wp_ref, modp_ref, ysp_ref = rest[:7]
        rest = rest[7:]
    outs, scratch = rest[:N_MIXER_OUTPUTS], rest[N_MIXER_OUTPUTS:]
    (x_ref, mod_ref, g1_ref, win_ref, wout_ref, lng_ref, lnb_ref, ws_ref, bsm_ref,
     wg_ref, bg_ref, glag_ref, cw_ref, cb_ref, g2_ref, rwt_ref, rb_ref) = ins
    o_ref, dest_ref, w_ref, cnt_ref, xs_ref = outs
    (h_ref, pa_ref, pb_ref, y0_ref, y1_ref, y2_ref, y3_ref, s_ref, zc_ref, wsm_ref, h2_ref,
     carry_ref, dvm_ref, d0s_ref, d1s_ref, cvm_ref, csm_ref, zero_ref, sem, sem_s, sem_z,
     ybuf, xin_ref, sem_g) = scratch

    bi, ti = pl.program_id(0), pl.program_id(1)
    tm = x_ref.shape[0]
    n_blk = tm // GMLP_BLOCK
    step = bi * pl.num_programs(1) + ti
    last_step = step == pl.num_programs(0) * pl.num_programs(1) - 1
    slot = step % 2
    spare_row0 = N_EXPERTS * cap

    def drain_rows(src_ref, dst_ref, s, copy=_row_copy):
        def body(g, c):
            for _ in range(TOP_K * SUBLANES):
                copy(src_ref, 0, dst_ref, 0, s).wait()
            return c
        lax.fori_loop(0, tm // SUBLANES, body, 0)

    @pl.when(step > 0)
    def _():
        pltpu.make_async_copy(dvm_ref.at[0], d0s_ref, sem_s).wait()
        pltpu.make_async_copy(dvm_ref.at[1], d1s_ref, sem_s).wait()

    if fuse_prev:
        @pl.when(step == 0)
        def _():
            def issue(g, c):
                for jj in range(SUBLANES):
                    i = pl.multiple_of(g * SUBLANES, SUBLANES) + jj
                    _tile_copy(ysp_ref, dp0_ref[i], ybuf.at[0, 0], i * SUBLANES, sem_g.at[0]).start()
                    _tile_copy(ysp_ref, dp1_ref[i], ybuf.at[0, 1], i * SUBLANES, sem_g.at[0]).start()
                return c
            lax.fori_loop(0, tm // SUBLANES, issue, 0)

        drain_rows(ysp_ref, ybuf.at[0, 0], sem_g.at[slot], _tile_copy)
        wp = wp_ref[...]
        xin_ref[...] = x_ref[...] + modp_ref[...][5:6] * (
            wp[:, 0:1] * _load_rows_from_tiles(ybuf, (slot, 0), tm)
            + wp[:, 1:2] * _load_rows_from_tiles(ybuf, (slot, 1), tm))
        x_in = xin_ref
    else:
        x_in = x_ref

    @pl.when(step == 0)
    def _():
        carry_ref[...] = jnp.zeros_like(carry_ref)
        h2_ref[...] = jnp.zeros_like(h2_ref)

        def fill(i, c):
            d0s_ref[i] = spare_row0 + i
            d1s_ref[i] = spare_row0 + tm + i
            return c

        lax.fori_loop(0, tm, fill, 0)
        tt = lax.broadcasted_iota(jnp.int32, (GMLP_BLOCK, GMLP_BLOCK), 0) // CHUNK
        ss = lax.broadcasted_iota(jnp.int32, (GMLP_BLOCK, GMLP_BLOCK), 1) // CHUNK
        for h in range(A_HEADS):
            wsm_ref[h] = _bf(jnp.where(tt >= ss, ws_ref[h], 0.0))

    @pl.when(ti == 0)
    def _():
        s_ref[...] = jnp.zeros_like(s_ref)
        zc_ref[...] = jnp.zeros_like(zc_ref)

    m = mod_ref[...]
    sh1, sc1, gt1, sh2, sc2 = m[0:1], m[1:2], m[2:3], m[3:4], m[4:5]
    h_ref[...] = _bf((_rms(x_in[...]) * g1_ref[...]) * (1.0 + sc1) + sh1)

    lane256 = lax.broadcasted_iota(jnp.int32, (CHUNK, B_KDIM), 1)
    head_of_lane = lane256 // B_DK
    r64 = lax.broadcasted_iota(jnp.int32, (CHUNK, CHUNK), 0)
    c64 = lax.broadcasted_iota(jnp.int32, (CHUNK, CHUNK), 1)
    tri = _bf(jnp.where(r64 >= c64, 1.0, 0.0))
    causal4 = jnp.concatenate([r64 >= c64] * B_HEADS, axis=0)
    eye256 = (lax.broadcasted_iota(jnp.int32, (B_KDIM, B_KDIM), 0)
              == lax.broadcasted_iota(jnp.int32, (B_KDIM, B_KDIM), 1))
    a_head_of_lane = lax.broadcasted_iota(jnp.int32, (GMLP_BLOCK, A_DIM), 1) // (A_DIM // A_HEADS)
    row128 = lax.broadcasted_iota(jnp.int32, (GMLP_BLOCK, C_DIM), 0)
    sqrt_half = 0.7071067811865476

    def gelu(v):
        return 0.5 * v * (1.0 + lax.erf(v * sqrt_half))

    h2_prev = h2_ref.at[1 - slot]
    row_copies = []
    for i in range(tm):
        row_copies.append(lambda i=i: _row_copy(h2_prev, i, xs_ref, d0s_ref[i], sem).start(priority=0))
        row_copies.append(lambda i=i: _row_copy(h2_prev, i, xs_ref, d1s_ref[i], sem).start(priority=1))
        if fuse_prev:
            row_copies.append(lambda i=i: _tile_copy(
                ysp_ref, dp0n_ref[i], ybuf.at[1 - slot, 0], i * SUBLANES, sem_g.at[1 - slot]).start(priority=0))
            row_copies.append(lambda i=i: _tile_copy(
                ysp_ref, dp1n_ref[i], ybuf.at[1 - slot, 1], i * SUBLANES, sem_g.at[1 - slot]).start(priority=1))
    row_copies.reverse()
    copies_per_point = -(-len(row_copies) // (n_blk * COPY_POINTS_PER_BLOCK))

    def start_row_copies():
        for _ in range(min(copies_per_point, len(row_copies))):
            row_copies.pop()()

    p_halves = (pa_ref, pb_ref)
    y_blocks = (y0_ref, y1_ref, y2_ref, y3_ref)
    assert n_blk == len(y_blocks)

    def project(half):
        rows = slice(half * (tm // 2), (half + 1) * (tm // 2))
        p_halves[half][...] = _dot(h_ref[rows, :], win_ref[...])

    def block(j, state, zprev):
        p_ref, y_ref = p_halves[j // (n_blk // 2)], y_blocks[j]
        r0 = (j % (n_blk // 2)) * GMLP_BLOCK
        rows = slice(r0, r0 + GMLP_BLOCK)

        u = gelu(p_ref[rows, OFF_AU:OFF_AU + A_DIM])
        vv = gelu(p_ref[rows, OFF_AV:OFF_AV + A_DIM])
        mu = jnp.mean(vv, axis=-1, keepdims=True)
        var = jnp.mean((vv - mu) ** 2, axis=-1, keepdims=True)
        vn = _bf((vv - mu) * lax.rsqrt(var + EPS) * lng_ref[...] + lnb_ref[...])
        sv = jnp.zeros((GMLP_BLOCK, A_DIM), jnp.float32)
        for hh in range(A_HEADS):
            sv = jnp.where(a_head_of_lane == hh, _dot(wsm_ref[hh], vn), sv)
        y_ref[:, 0:A_DIM] = _bf(u * (sv + bsm_ref[...]))
        start_row_copies()

        z = _dot(_bf(p_ref[rows, OFF_GLR:OFF_GLR + LANES]), wg_ref[...]) + bg_ref[...]
        glog = (jnp.minimum(z, 0.0) - jnp.log1p(jnp.exp(-jnp.abs(z)))) / GLA_TAU
        for c in range(GMLP_BLOCK // CHUNK):
            rc = slice(r0 + c * CHUNK, r0 + (c + 1) * CHUNK)
            g = glog[c * CHUNK:(c + 1) * CHUNK]
            g_hi = _bf(g)
            g_r1 = g - g_hi.astype(jnp.float32)
            g_mid = _bf(g_r1)
            g_lo = _bf(g_r1 - g_mid.astype(jnp.float32))
            b = _dot(tri, g_hi) + _dot(tri, g_mid) + _dot(tri, g_lo)
            b_mid = b[CHUNK // 2:CHUNK // 2 + 1]
            b_last = b[CHUNK - 1:CHUNK]
            q = p_ref[rc, OFF_Q:OFF_Q + B_KDIM] * (B_DK ** -0.5)
            k = p_ref[rc, OFF_K:OFF_K + B_KDIM]
            vb = _bf(p_ref[rc, OFF_V:OFF_V + B_VDIM])
            qs = q * jnp.exp(b - b_mid)
            ks = _bf(k * jnp.exp(b_mid - b))
            kd = _bf(k * jnp.exp(b_last - b))
            qb = q * jnp.exp(b)
            qs_st = _bf(jnp.concatenate(
                [jnp.where(head_of_lane == hh, qs, 0.0) for hh in range(B_HEADS)], axis=0))
            qb_st = _bf(jnp.concatenate(
                [jnp.where(head_of_lane == hh, qb, 0.0) for hh in range(B_HEADS)], axis=0))
            scores = _bf(jnp.where(causal4, _dot(qs_st, ks, _NT), 0.0))
            start_row_copies()
            o_inter = _dot(qb_st, _bf(state))
            kv_all = _dot(kd, vb, _TN)
            start_row_copies()
            outs, kvs = [], []
            for hh in range(B_HEADS):
                rs = slice(hh * CHUNK, (hh + 1) * CHUNK)
                cs = slice(hh * B_DV, (hh + 1) * B_DV)
                o_h = _dot(scores[rs], vb[:, cs]) + o_inter[rs]
                outs.append(_rms(o_h))
                kvs.append(kv_all[rs, cs])
                start_row_copies()
            decay_row = jnp.broadcast_to(jnp.exp(b_last), (B_KDIM, B_KDIM))
            decay_col = jnp.sum(jnp.where(eye256, decay_row, 0.0), axis=1, keepdims=True)
            state = decay_col * state + jnp.concatenate(kvs, axis=0)
            on = jnp.concatenate(outs, axis=1) * glag_ref[...]
            og = p_ref[rc, OFF_OG:OFF_OG + B_VDIM]
            y_ref[c * CHUNK:(c + 1) * CHUNK, A_DIM:A_DIM + B_VDIM] = _bf(on * (og * jax.nn.sigmoid(og)))
            start_row_copies()

        zz = p_ref[rows, OFF_CC:OFF_CC + C_DIM] * p_ref[rows, OFF_CX:OFF_CX + C_DIM]
        z1 = jnp.where(row128 == 0, zprev[7:8], pltpu.roll(zz, 1, 0))
        z2 = jnp.where(row128 == 0, zprev[6:7], jnp.where(row128 == 1, zprev[7:8], pltpu.roll(zz, 2, 0)))
        cw = cw_ref[...]
        yc = cb_ref[...] + cw[0:1] * z2
        yc = yc + cw[1:2] * z1
        yc = yc + cw[2:3] * zz
        y_ref[:, A_DIM + B_VDIM:A_DIM + B_VDIM + C_DIM] = _bf(p_ref[rows, OFF_CB:OFF_CB + C_DIM] * yc)
        start_row_copies()
        return state, zz[GMLP_BLOCK - SUBLANES:GMLP_BLOCK]

    state, zprev, carry = s_ref[...], zc_ref[...], carry_ref[...]
    project(0)
    for j in range(n_blk):
        if j == 0:
            project(1)
        state, zprev = block(j, state, zprev)
        rows = slice(j * GMLP_BLOCK, (j + 1) * GMLP_BLOCK)
        xo = x_in[rows, :] + gt1 * _dot(y_blocks[j][...], wout_ref[...])
        o_ref[rows, :] = xo
        start_row_copies()
        h2 = (_rms(xo) * g2_ref[...]) * (1.0 + sc2) + sh2
        h2_ref[slot, rows, :] = _pack_rows(h2)
        d0, d1, wcol, carry = _route(h2, rwt_ref, rb_ref, carry, cap)
        dvm_ref[0:1, rows] = d0
        dvm_ref[1:2, rows] = d1
        w_ref[rows, :] = wcol
        start_row_copies()
    while row_copies:
        row_copies.pop()()
    s_ref[...] = state
    zc_ref[...] = zprev
    carry_ref[...] = carry
    cnt = jnp.broadcast_to(carry, cnt_ref.shape).astype(jnp.int32)
    cnt_ref[...] = cnt
    dest_ref[...] = dvm_ref[0:TOP_K, :]

    drain_rows(h2_prev, xs_ref, sem)

    to_smem = [pltpu.make_async_copy(dvm_ref.at[0], d0s_ref, sem_s),
               pltpu.make_async_copy(dvm_ref.at[1], d1s_ref, sem_s)]
    for cp in to_smem:
        cp.start()

    @pl.when(last_step)
    def _():
        for cp in to_smem:
            cp.wait()
        h2_last = h2_ref.at[slot]

        def issue(g, c):
            for jj in range(SUBLANES):
                i = pl.multiple_of(g * SUBLANES, SUBLANES) + jj
                _row_copy(h2_last, i, xs_ref, d0s_ref[i], sem).start()
                _row_copy(h2_last, i, xs_ref, d1s_ref[i], sem).start()
            return c

        lax.fori_loop(0, tm // SUBLANES, issue, 0)
        drain_rows(h2_last, xs_ref, sem)
        if fuse_prev:
            drain_rows(ysp_ref, ybuf.at[0, 0], sem_g.at[1 - slot], _tile_copy)
        cvm_ref[...] = cnt
        zero_ref[...] = jnp.zeros_like(zero_ref)
        cp = pltpu.make_async_copy(cvm_ref, csm_ref, sem_s)
        cp.start()
        cp.wait()
        _zero_segment_padding(csm_ref, cap, zero_ref, xs_ref, sem_z)


def _mixer(x, mod, g1, g2, w_in, w_out, ln_g, ln_b, w_s, b_s, w_gate, b_gate, gla_g, conv_w, conv_b,
           router_w, router_b, prev=None):
    bn, t, d = x.shape
    n = bn * t
    tm = min(MIX_ROWS, t)
    nt = t // tm
    cap = n
    glr0 = 2 * A_DIM + 2 * B_KDIM + B_VDIM
    w_in_p = _bf(jnp.concatenate(
        [w_in[:, :glr0], w_in[:, glr0 + GLA_RANK:], w_in[:, glr0:glr0 + GLA_RANK],
         jnp.zeros((d, LANES - GLA_RANK), w_in.dtype)], axis=1))
    w_gate_p = _bf(jnp.concatenate(
        [w_gate, jnp.zeros((LANES - GLA_RANK, B_KDIM), w_gate.dtype)], axis=0))
    bsm = jnp.repeat(b_s.T, A_DIM // A_HEADS, axis=1)

    def whole(shape):
        return pl.BlockSpec(shape, lambda b, i: (0,) * len(shape))

    prev_specs, prev_args = [], ()
    if prev is not None:
        n_tiles = bn * nt

        def this_tile(b, i):
            return (b * nt + i,)

        def next_tile(b, i):
            return (jnp.minimum(b * nt + i + 1, n_tiles - 1),)

        d0p, d1p, wcolp, modp, ysp = prev
        prev_specs = [
            pl.BlockSpec((tm,), this_tile, memory_space=pltpu.SMEM),
            pl.BlockSpec((tm,), this_tile, memory_space=pltpu.SMEM),
            pl.BlockSpec((tm,), next_tile, memory_space=pltpu.SMEM),
            pl.BlockSpec((tm,), next_tile, memory_space=pltpu.SMEM),
            pl.BlockSpec((tm, 8), lambda b, i: (b * nt + i, 0)),
            pl.BlockSpec((None, 6, d), lambda b, i: (b, 0, 0)),
            pl.BlockSpec(memory_space=pl.ANY),
        ]
        prev_args = (d0p, d1p, d0p, d1p, wcolp, modp, ysp)

    return pl.pallas_call(
        functools.partial(_mixer_kernel, cap=cap, fuse_prev=prev is not None),
        grid=(bn, nt),
        in_specs=[
            pl.BlockSpec((None, tm, d), lambda b, i: (b, i, 0)),
            pl.BlockSpec((None, 6, d), lambda b, i: (b, 0, 0)),
            whole((1, d)),
            whole((d, P_PAD)),
            whole((d, d)),
            whole((1, A_DIM)),
            whole((1, A_DIM)),
            whole((A_HEADS, GMLP_BLOCK, GMLP_BLOCK)),
            whole((GMLP_BLOCK, A_DIM)),
            whole((LANES, B_KDIM)),
            whole((1, B_KDIM)),
            whole((1, B_VDIM)),
            whole((3, C_DIM)),
            whole((1, C_DIM)),
            whole((1, d)),
            whole((N_EXPERTS, d)),
            whole((N_EXPERTS, 1)),
        ] + prev_specs,
        out_specs=[
            pl.BlockSpec((None, tm, d), lambda b, i: (b, i, 0)),
            pl.BlockSpec((TOP_K, tm), lambda b, i: (0, b * nt + i)),
            pl.BlockSpec((tm, 8), lambda b, i: (b * nt + i, 0)),
            whole((N_EXPERTS, LANES)),
            pl.BlockSpec(memory_space=pl.ANY),
        ],
        out_shape=[
            jax.ShapeDtypeStruct(x.shape, x.dtype),
            jax.ShapeDtypeStruct((TOP_K, n), jnp.int32),
            jax.ShapeDtypeStruct((n, 8), jnp.float32),
            jax.ShapeDtypeStruct((N_EXPERTS, LANES), jnp.int32),
            jax.ShapeDtypeStruct((N_EXPERTS * cap + TOP_K * tm, d // 2), jnp.uint32),
        ],
        scratch_shapes=[
            pltpu.VMEM((tm, d), jnp.bfloat16),
            pltpu.VMEM((tm // 2, P_PAD), jnp.float32),
            pltpu.VMEM((tm // 2, P_PAD), jnp.float32),
            pltpu.VMEM((GMLP_BLOCK, d), jnp.bfloat16),
            pltpu.VMEM((GMLP_BLOCK, d), jnp.bfloat16),
            pltpu.VMEM((GMLP_BLOCK, d), jnp.bfloat16),
            pltpu.VMEM((GMLP_BLOCK, d), jnp.bfloat16),
            pltpu.VMEM((B_KDIM, B_DV), jnp.float32),
            pltpu.VMEM((SUBLANES, C_DIM), jnp.float32),
            pltpu.VMEM((A_HEADS, GMLP_BLOCK, GMLP_BLOCK), jnp.bfloat16),
            pltpu.VMEM((2, tm, d // 2), jnp.uint32),
            pltpu.VMEM((N_EXPERTS, 1), jnp.float32),
            pltpu.VMEM((SUBLANES, tm), jnp.int32),
            pltpu.SMEM((tm,), jnp.int32),
            pltpu.SMEM((tm,), jnp.int32),
            pltpu.VMEM((N_EXPERTS, LANES), jnp.int32),
            pltpu.SMEM((N_EXPERTS, LANES), jnp.int32),
            pltpu.VMEM((FFN_ROWS // 2, d // 2), jnp.uint32),
            pltpu.SemaphoreType.DMA,
            pltpu.SemaphoreType.DMA,
            pltpu.SemaphoreType.DMA,
            pltpu.VMEM((2, TOP_K, tm * SUBLANES, LANES), jnp.float32),
            pltpu.VMEM((tm, d), jnp.float32),
            pltpu.SemaphoreType.DMA((2,)),
        ],
        compiler_params=pltpu.CompilerParams(
            dimension_semantics=("arbitrary", "arbitrary"), vmem_limit_bytes=VMEM_LIMIT),
        name="mixer",
    )(x, mod, g1.reshape(1, d), w_in_p, _bf(w_out), ln_g.reshape(1, -1), ln_b.reshape(1, -1), w_s, bsm,
      w_gate_p, b_gate.reshape(1, -1), gla_g.reshape(1, -1), conv_w, conv_b.reshape(1, -1),
      g2.reshape(1, d), router_w.T, router_b.reshape(N_EXPERTS, 1), *prev_args)


def _ffn_kernel(be_ref, rb_ref, nb_ref, xs_ref, w1_ref, w3_ref, w2_ref, ys_ref, w1b_ref, w3b_ref, w2b_ref):
    del rb_ref
    i = pl.program_id(0)
    used = i < nb_ref[0]
    new_expert = jnp.logical_or(i == 0, be_ref[i] != be_ref[jnp.maximum(i - 1, 0)])

    @pl.when(jnp.logical_and(used, new_expert))
    def _():
        w1b_ref[...] = _bf(w1_ref[...])
        w3b_ref[...] = _bf(w3_ref[...])
        w2b_ref[...] = _bf(w2_ref[...])

    @pl.when(used)
    def _():
        x_a, x_b = _unpack_rows(xs_ref[...])
        half = x_a.shape[1]
        a = _dot(x_a, w1b_ref[0:half, :]) + _dot(x_b, w1b_ref[half:, :])
        g = _dot(x_a, w3b_ref[0:half, :]) + _dot(x_b, w3b_ref[half:, :])
        hm = _bf((a * jax.nn.sigmoid(a)) * g)
        _store_rows_as_tiles(ys_ref, _dot(hm, w2b_ref[...]))


def _expert_ffn(layer, block_e, row_block, n_used, xs, w1, w3, w2):
    n_rows = xs.shape[0]
    n_blocks = block_e.shape[0]
    d, f = w1.shape[-2], w1.shape[-1]
    return pl.pallas_call(
        _ffn_kernel,
        grid_spec=pltpu.PrefetchScalarGridSpec(
            num_scalar_prefetch=3,
            grid=(n_blocks,),
            in_specs=[
                pl.BlockSpec((FFN_ROWS, d // 2), lambda i, be, rb, nb: (rb[i], 0)),
                pl.BlockSpec((None, None, d, f), lambda i, be, rb, nb: (layer, be[i], 0, 0)),
                pl.BlockSpec((None, None, d, f), lambda i, be, rb, nb: (layer, be[i], 0, 0)),
                pl.BlockSpec((None, None, f, d), lambda i, be, rb, nb: (layer, be[i], 0, 0)),
            ],
            out_specs=pl.BlockSpec((FFN_ROWS * SUBLANES, LANES), lambda i, be, rb, nb: (rb[i], 0)),
            scratch_shapes=[pltpu.VMEM((d, f), jnp.bfloat16), pltpu.VMEM((d, f), jnp.bfloat16),
                            pltpu.VMEM((f, d), jnp.bfloat16)],
        ),
        out_shape=jax.ShapeDtypeStruct((n_rows * SUBLANES, LANES), jnp.float32),
        compiler_params=pltpu.CompilerParams(
            dimension_semantics=("arbitrary",), vmem_limit_bytes=VMEM_LIMIT),
        name="expert_ffn",
    )(block_e, row_block, n_used, xs, w1, w3, w2)


def _combine_kernel(d0_ref, d1_ref, d0n_ref, d1n_ref, x_ref, mod_ref, w_ref, gf_ref, ys_ref, o_ref, ybuf, sems,
                    *, final_norm):
    tm = x_ref.shape[0]
    step = pl.program_id(0) * pl.num_programs(1) + pl.program_id(1)
    last = step == pl.num_programs(0) * pl.num_programs(1) - 1
    slot = step % 2

    def drain(sem):
        def body(g, c):
            for _ in range(TOP_K * SUBLANES):
                _tile_copy(ys_ref, 0, ybuf.at[0, 0], 0, sem).wait()
            return c
        lax.fori_loop(0, tm // SUBLANES, body, 0)

    @pl.when(step == 0)
    def _():
        def issue(g, c):
            for jj in range(SUBLANES):
                i = pl.multiple_of(g * SUBLANES, SUBLANES) + jj
                _tile_copy(ys_ref, d0_ref[i], ybuf.at[0, 0], i * SUBLANES, sems.at[0]).start()
                _tile_copy(ys_ref, d1_ref[i], ybuf.at[0, 1], i * SUBLANES, sems.at[0]).start()
            return c
        lax.fori_loop(0, tm // SUBLANES, issue, 0)

    nxt, sem_nxt = ybuf.at[1 - slot], sems.at[1 - slot]
    for i in range(tm):
        _tile_copy(ys_ref, d0n_ref[i], nxt.at[0], i * SUBLANES, sem_nxt).start(priority=0)
        _tile_copy(ys_ref, d1n_ref[i], nxt.at[1], i * SUBLANES, sem_nxt).start(priority=1)

    drain(sems.at[slot])
    w = w_ref[...]
    gt2 = mod_ref[...][5:6]
    out = x_ref[...] + gt2 * (w[:, 0:1] * _load_rows_from_tiles(ybuf, (slot, 0), tm)
                              + w[:, 1:2] * _load_rows_from_tiles(ybuf, (slot, 1), tm))
    if final_norm:
        out = _rms(out) * gf_ref[...]
    o_ref[...] = out

    @pl.when(last)
    def _():
        drain(sems.at[1 - slot])


def _combine(d0, d1, x, mod, wcol, g_final, ys, final_norm):
    bn, t, d = x.shape
    tm = min(MOVE_ROWS, t)
    nt = t // tm
    n_tiles = bn * nt

    def this_tile(b, i):
        return (b * nt + i,)

    def next_tile(b, i):
        return (jnp.minimum(b * nt + i + 1, n_tiles - 1),)

    return pl.pallas_call(
        functools.partial(_combine_kernel, final_norm=final_norm),
        grid=(bn, nt),
        in_specs=[
            pl.BlockSpec((tm,), this_tile, memory_space=pltpu.SMEM),
            pl.BlockSpec((tm,), this_tile, memory_space=pltpu.SMEM),
            pl.BlockSpec((tm,), next_tile, memory_space=pltpu.SMEM),
            pl.BlockSpec((tm,), next_tile, memory_space=pltpu.SMEM),
            pl.BlockSpec((None, tm, d), lambda b, i: (b, i, 0)),
            pl.BlockSpec((None, 6, d), lambda b, i: (b, 0, 0)),
            pl.BlockSpec((tm, 8), lambda b, i: (b * nt + i, 0)),
            pl.BlockSpec((1, d), lambda b, i: (0, 0)),
            pl.BlockSpec(memory_space=pl.ANY),
        ],
        out_specs=pl.BlockSpec((None, tm, d), lambda b, i: (b, i, 0)),
        out_shape=jax.ShapeDtypeStruct(x.shape, x.dtype),
        scratch_shapes=[pltpu.VMEM((2, TOP_K, tm * SUBLANES, LANES), jnp.float32), pltpu.SemaphoreType.DMA((2,))],
        compiler_params=pltpu.CompilerParams(
            dimension_semantics=("arbitrary", "arbitrary"), vmem_limit_bytes=VMEM_LIMIT),
        name="combine",
    )(d0, d1, d0, d1, x, mod, wcol, g_final.reshape(1, d), ys)


def _block_map(counts, cap, n_blocks):
    blocks_per_expert = (counts + FFN_ROWS - 1) // FFN_ROWS
    ends = jnp.cumsum(blocks_per_expert)
    n_used = ends[-1:]
    step = jnp.minimum(jnp.arange(n_blocks, dtype=jnp.int32), n_used - 1)
    block_e = jnp.minimum(jnp.sum((ends[None, :] <= step[:, None]).astype(jnp.int32), axis=1), N_EXPERTS - 1)
    first = (ends - blocks_per_expert)[block_e]
    row_block = block_e * (cap // FFN_ROWS) + (step - first)
    return block_e.astype(jnp.int32), row_block.astype(jnp.int32), n_used.astype(jnp.int32)


def kernel(x, c, w_ada, b_ada, g_norm1, g_norm2, w_in, w_out, gmlp_ln_g, gmlp_ln_b, gmlp_ws, gmlp_bs,
           gla_w_gate, gla_b_gate, gla_norm_g, conv_w, conv_b, router_w, router_b, exp_w1, exp_w3, exp_w2,
           g_final):
    depth = w_ada.shape[0]
    bn, t, d = x.shape
    n = bn * t
    n_blocks = -(-(n * TOP_K) // FFN_ROWS) + N_EXPERTS
    mod_all = _modulation(c, w_ada, b_ada).reshape(depth, bn, 6, d)
    prev = None
    for l in range(depth):
        mod = mod_all[l]
        x, dest, wcol, cnt, xs = _mixer(
            x, mod, g_norm1[l], g_norm2[l], w_in[l], w_out[l], gmlp_ln_g[l], gmlp_ln_b[l], gmlp_ws[l],
            gmlp_bs[l], gla_w_gate[l], gla_b_gate[l], gla_norm_g[l], conv_w[l], conv_b[l], router_w, router_b,
            prev=prev)
        block_e, row_block, n_used = _block_map(cnt[:, 0], n, n_blocks)
        ys = _expert_ffn(l, block_e, row_block, n_used, xs, exp_w1, exp_w3, exp_w2)
        prev = (dest[0] * SUBLANES, dest[1] * SUBLANES, wcol, mod, ys)
    d0, d1, wcol, mod, ys = prev
    return _combine(d0, d1, x, mod, wcol, g_final, ys, final_norm=True)
```

```python
import functools

import jax
import jax.numpy as jnp
from jax import lax
from jax.experimental import pallas as pl
from jax.experimental.pallas import tpu as pltpu

CHUNK = 64
GMLP_BLOCK = 128
A_DIM = 256
A_HEADS = 4
B_HEADS = 4
B_DK = 64
B_DV = 128
B_KDIM = B_HEADS * B_DK
B_VDIM = B_HEADS * B_DV
GLA_RANK = 16
GLA_TAU = 16.0
C_DIM = 256
N_EXPERTS = 32
N_GROUPS = 4
EXPERTS_PER_GROUP = N_EXPERTS // N_GROUPS
TOP_K = 2
EPS = 1e-6

LANES = 128
SUBLANES = 8
OFF_AU, OFF_AV, OFF_Q, OFF_K, OFF_V = 0, 256, 512, 768, 1024
OFF_OG, OFF_CB, OFF_CC, OFF_CX, OFF_GLR = 1536, 2048, 2304, 2560, 2816
P_PAD = OFF_GLR + LANES

MIX_ROWS = 512
COPY_POINTS_PER_BLOCK = 18
MOVE_ROWS = 512
FFN_ROWS = 512
VMEM_LIMIT = 56 * 1024 * 1024

_NT = (((1,), (1,)), ((), ()))
_TN = (((0,), (0,)), ((), ()))


def _dot(a, b, dims=None):
    if dims is None:
        return jnp.dot(a, b, preferred_element_type=jnp.float32)
    return lax.dot_general(a, b, dims, preferred_element_type=jnp.float32)


def _bf(x):
    return x.astype(jnp.bfloat16)


def _split_bf16(x):
    hi = _bf(x)
    lo = _bf(x - hi.astype(jnp.float32))
    return hi, lo


def _rms(x):
    return x * lax.rsqrt(jnp.mean(x * x, axis=-1, keepdims=True) + EPS)


def _pack_rows(v):
    half = v.shape[1] // 2
    hi = pltpu.bitcast(_bf(v[:, :half]).astype(jnp.float32), jnp.uint32)
    lo = pltpu.bitcast(_bf(v[:, half:]).astype(jnp.float32), jnp.uint32)
    return hi | (lo >> 16)


def _unpack_rows(w):
    hi = pltpu.bitcast(w & jnp.uint32(0xFFFF0000), jnp.float32)
    lo = pltpu.bitcast(w << 16, jnp.float32)
    return _bf(hi), _bf(lo)


def _row_copy(src_ref, src_row, dst_ref, dst_row, sem):
    return pltpu.make_async_copy(src_ref.at[pl.ds(src_row, 1)], dst_ref.at[pl.ds(dst_row, 1)], sem)


def _aligned(row):
    return row if isinstance(row, int) else pl.multiple_of(row, SUBLANES)


def _tile_copy(src_ref, src_row, dst_ref, dst_row, sem):
    return pltpu.make_async_copy(src_ref.at[pl.ds(_aligned(src_row), SUBLANES)],
                                 dst_ref.at[pl.ds(_aligned(dst_row), SUBLANES)], sem)


def _store_rows_as_tiles(ref, value):
    rows = value.shape[0]
    assert value.shape[1] == SUBLANES * LANES
    for c in range(SUBLANES):
        ref[pl.ds(c, rows, stride=SUBLANES), :] = value[:, c * LANES:(c + 1) * LANES]


def _load_rows_from_tiles(ref, idx, rows):
    return jnp.concatenate(
        [ref[idx + (pl.ds(c, rows, stride=SUBLANES), slice(None))] for c in range(SUBLANES)], axis=1)


def _mod_kernel(c_ref, w_ref, b_ref, o_ref):
    c = c_ref[...]
    s = c * jax.nn.sigmoid(c)
    s_hi, s_lo = _split_bf16(s)
    w_hi, w_lo = _split_bf16(w_ref[...])
    acc = _dot(s_hi, w_hi) + _dot(s_hi, w_lo) + _dot(s_lo, w_hi)
    o_ref[...] = acc + b_ref[...]


def _modulation(c, w_ada, b_ada):
    depth, d, six_d = w_ada.shape
    bn = c.shape[0]
    cb = 1024
    return pl.pallas_call(
        _mod_kernel,
        grid=(depth, six_d // cb),
        in_specs=[
            pl.BlockSpec((bn, d), lambda l, j: (0, 0)),
            pl.BlockSpec((None, d, cb), lambda l, j: (l, 0, j)),
            pl.BlockSpec((None, 1, cb), lambda l, j: (l, 0, j)),
        ],
        out_specs=pl.BlockSpec((None, bn, cb), lambda l, j: (l, 0, j)),
        out_shape=jax.ShapeDtypeStruct((depth, bn, six_d), jnp.float32),
        compiler_params=pltpu.CompilerParams(
            dimension_semantics=("arbitrary", "arbitrary"), vmem_limit_bytes=VMEM_LIMIT),
        name="adaln_mod",
    )(c, w_ada, b_ada.reshape(depth, 1, six_d))


def _route(h2, rwt_ref, rb_ref, carry, cap):
    tr = h2.shape[0]
    h_hi, h_lo = _split_bf16(h2)
    w_hi, w_lo = _split_bf16(rwt_ref[...])
    logits = (_dot(w_hi, h_hi, _NT) + _dot(w_hi, h_lo, _NT) + _dot(w_lo, h_hi, _NT)) + rb_ref[...]
    ex = jnp.exp(logits - jnp.max(logits, axis=0, keepdims=True))
    probs = ex / jnp.sum(ex, axis=0, keepdims=True)

    idx8 = lax.broadcasted_iota(jnp.int32, (EXPERTS_PER_GROUP, tr), 0)
    best = None
    for g in range(N_GROUPS):
        pg = probs[g * EXPERTS_PER_GROUP:(g + 1) * EXPERTS_PER_GROUP]
        m1 = jnp.max(pg, axis=0, keepdims=True)
        i1 = jnp.min(jnp.where(pg == m1, idx8, EXPERTS_PER_GROUP), axis=0, keepdims=True)
        pg2 = jnp.where(idx8 == i1, -1.0, pg)
        m2 = jnp.max(pg2, axis=0, keepdims=True)
        i2 = jnp.min(jnp.where(pg2 == m2, idx8, EXPERTS_PER_GROUP), axis=0, keepdims=True)
        cand = (m1 + m2, m1, m2, i1 + g * EXPERTS_PER_GROUP, i2 + g * EXPERTS_PER_GROUP)
        if best is None:
            best = cand
        else:
            better = cand[0] > best[0]
            best = tuple(jnp.where(better, a, b) for a, b in zip(cand, best))
    _, p1, p2, e0, e1 = best
    denom = p1 + p2
    w0, w1 = p1 / denom, p2 / denom

    eidx = lax.broadcasted_iota(jnp.int32, (N_EXPERTS, tr), 0)
    hit0, hit1 = eidx == e0, eidx == e1
    onehot = jnp.where(jnp.logical_or(hit0, hit1), 1.0, 0.0)
    before = (lax.broadcasted_iota(jnp.int32, (tr, tr), 0) < lax.broadcasted_iota(jnp.int32, (tr, tr), 1))
    rank = _dot(_bf(onehot), _bf(jnp.where(before, 1.0, 0.0))) + carry
    r0 = jnp.sum(jnp.where(hit0, rank, 0.0), axis=0, keepdims=True).astype(jnp.int32)
    r1 = jnp.sum(jnp.where(hit1, rank, 0.0), axis=0, keepdims=True).astype(jnp.int32)
    carry = carry + jnp.sum(onehot, axis=1, keepdims=True)

    eye = lax.broadcasted_iota(jnp.int32, (tr, tr), 0) == lax.broadcasted_iota(jnp.int32, (tr, tr), 1)
    w0c = jnp.sum(jnp.where(eye, jnp.broadcast_to(w0, (tr, tr)), 0.0), axis=1, keepdims=True)
    w1c = jnp.sum(jnp.where(eye, jnp.broadcast_to(w1, (tr, tr)), 0.0), axis=1, keepdims=True)
    lane8 = lax.broadcasted_iota(jnp.int32, (tr, 8), 1)
    wcol = jnp.where(lane8 == 0, w0c, jnp.where(lane8 == 1, w1c, 0.0))
    return e0 * cap + r0, e1 * cap + r1, wcol, carry


def _zero_segment_padding(cnt_ref, cap, zero_ref, xs_ref, sem):
    copies = []
    for ex in range(N_EXPERTS):
        cnt = cnt_ref[ex, 0]
        seg = ex * cap
        pos = seg + cnt
        end8 = seg + ((cnt + SUBLANES - 1) // SUBLANES) * SUBLANES
        for j in range(SUBLANES - 1):
            copies.append((pos + j < end8, _row_copy(zero_ref, 0, xs_ref, pos + j, sem)))
        seg_end = seg + ((cnt + FFN_ROWS - 1) // FFN_ROWS) * FFN_ROWS
        groups = (seg_end - end8) // SUBLANES
        at = end8
        bit = FFN_ROWS // (2 * SUBLANES)
        while bit >= 1:
            rows = bit * SUBLANES
            pred = (groups & bit) != 0
            copies.append((pred, pltpu.make_async_copy(
                zero_ref.at[pl.ds(0, rows)], xs_ref.at[pl.ds(pl.multiple_of(at, SUBLANES), rows)], sem)))
            at = at + jnp.where(pred, rows, 0)
            bit //= 2
    for pred, cp in copies:
        pl.when(pred)(cp.start)
    for pred, cp in copies:
        pl.when(pred)(cp.wait)


N_MIXER_INPUTS = 17
N_MIXER_OUTPUTS = 5


def _mixer_kernel(*refs, cap, fuse_prev):
    ins, rest = refs[:N_MIXER_INPUTS], refs[N_MIXER_INPUTS:]
    if fuse_prev:
        dp0_ref, dp1_ref, dp0n_ref, dp1n_ref, wp_ref, modp_ref, ysp_ref = rest[:7]
        rest = rest[7:]
    outs, scratch = rest[:N_MIXER_OUTPUTS], rest[N_MIXER_OUTPUTS:]
    (x_ref, mod_ref, g1_ref, win_ref, wout_ref, lng_ref, lnb_ref, ws_ref, bsm_ref,
     wg_ref, bg_ref, glag_ref, cw_ref, cb_ref, g2_ref, rwt_ref, rb_ref) = ins
    o_ref, dest_ref, w_ref, cnt_ref, xs_ref = outs
    (h_ref, pa_ref, pb_ref, y0_ref, y1_ref, y2_ref, y3_ref, s_ref, zc_ref, wsm_ref, h2_ref,
     carry_ref, dvm_ref, d0s_ref, d1s_ref, cvm_ref, csm_ref, zero_ref, sem, sem_s, sem_z,
     ybuf, xin_ref, sem_g) = scratch

    bi, ti = pl.program_id(0), pl.program_id(1)
    tm = x_ref.shape[0]
    n_blk = tm // GMLP_BLOCK
    step = bi * pl.num_programs(1) + ti
    last_step = step == pl.num_programs(0) * pl.num_programs(1) - 1
    slot = step % 2
    spare_row0 = N_EXPERTS * cap

    def drain_rows(src_ref, dst_ref, s, copy=_row_copy):
        def body(g, c):
            for _ in range(TOP_K * SUBLANES):
                copy(src_ref, 0, dst_ref, 0, s).wait()
            return c
        lax.fori_loop(0, tm // SUBLANES, body, 0)

    @pl.when(step > 0)
    def _():
        pltpu.make_async_copy(dvm_ref.at[0], d0s_ref, sem_s).wait()
        pltpu.make_async_copy(dvm_ref.at[1], d1s_ref, sem_s).wait()

    if fuse_prev:
        @pl.when(step == 0)
        def _():
            def issue(g, c):
                for jj in range(SUBLANES):
                    i = pl.multiple_of(g * SUBLANES, SUBLANES) + jj
                    _tile_copy(ysp_ref, dp0_ref[i], ybuf.at[0, 0], i * SUBLANES, sem_g.at[0]).start()
                    _tile_copy(ysp_ref, dp1_ref[i], ybuf.at[0, 1], i * SUBLANES, sem_g.at[0]).start()
                return c
            lax.fori_loop(0, tm // SUBLANES, issue, 0)

        drain_rows(ysp_ref, ybuf.at[0, 0], sem_g.at[slot], _tile_copy)
        wp = wp_ref[...]
        xin_ref[...] = x_ref[...] + modp_ref[...][5:6] * (
            wp[:, 0:1] * _load_rows_from_tiles(ybuf, (slot, 0), tm)
            + wp[:, 1:2] * _load_rows_from_tiles(ybuf, (slot, 1), tm))
        x_in = xin_ref
    else:
        x_in = x_ref

    @pl.when(step == 0)
    def _():
        carry_ref[...] = jnp.zeros_like(carry_ref)
        h2_ref[...] = jnp.zeros_like(h2_ref)

        def fill(i, c):
            d0s_ref[i] = spare_row0 + i
            d1s_ref[i] = spare_row0 + tm + i
            return c

        lax.fori_loop(0, tm, fill, 0)
        tt = lax.broadcasted_iota(jnp.int32, (GMLP_BLOCK, GMLP_BLOCK), 0) // CHUNK
        ss = lax.broadcasted_iota(jnp.int32, (GMLP_BLOCK, GMLP_BLOCK), 1) // CHUNK
        for h in range(A_HEADS):
            wsm_ref[h] = _bf(jnp.where(tt >= ss, ws_ref[h], 0.0))

    @pl.when(ti == 0)
    def _():
        s_ref[...] = jnp.zeros_like(s_ref)
        zc_ref[...] = jnp.zeros_like(zc_ref)

    m = mod_ref[...]
    sh1, sc1, gt1, sh2, sc2 = m[0:1], m[1:2], m[2:3], m[3:4], m[4:5]
    h_ref[...] = _bf((_rms(x_in[...]) * g1_ref[...]) * (1.0 + sc1) + sh1)

    lane256 = lax.broadcasted_iota(jnp.int32, (CHUNK, B_KDIM), 1)
    head_of_lane = lane256 // B_DK
    r64 = lax.broadcasted_iota(jnp.int32, (CHUNK, CHUNK), 0)
    c64 = lax.broadcasted_iota(jnp.int32, (CHUNK, CHUNK), 1)
    tri = _bf(jnp.where(r64 >= c64, 1.0, 0.0))
    causal4 = jnp.concatenate([r64 >= c64] * B_HEADS, axis=0)
    eye256 = (lax.broadcasted_iota(jnp.int32, (B_KDIM, B_KDIM), 0)
              == lax.broadcasted_iota(jnp.int32, (B_KDIM, B_KDIM), 1))
    a_head_of_lane = lax.broadcasted_iota(jnp.int32, (GMLP_BLOCK, A_DIM), 1) // (A_DIM // A_HEADS)
    row128 = lax.broadcasted_iota(jnp.int32, (GMLP_BLOCK, C_DIM), 0)
    sqrt_half = 0.7071067811865476

    def gelu(v):
        return 0.5 * v * (1.0 + lax.erf(v * sqrt_half))

    h2_prev = h2_ref.at[1 - slot]
    row_copies = []
    for i in range(tm):
        row_copies.append(lambda i=i: _row_copy(h2_prev, i, xs_ref, d0s_ref[i], sem).start(priority=0))
        row_copies.append(lambda i=i: _row_copy(h2_prev, i, xs_ref, d1s_ref[i], sem).start(priority=1))
        if fuse_prev:
            row_copies.append(lambda i=i: _tile_copy(
                ysp_ref, dp0n_ref[i], ybuf.at[1 - slot, 0], i * SUBLANES, sem_g.at[1 - slot]).start(priority=0))
            row_copies.append(lambda i=i: _tile_copy(
                ysp_ref, dp1n_ref[i], ybuf.at[1 - slot, 1], i * SUBLANES, sem_g.at[1 - slot]).start(priority=1))
    row_copies.reverse()
    copies_per_point = -(-len(row_copies) // (n_blk * COPY_POINTS_PER_BLOCK))

    def start_row_copies():
        for _ in range(min(copies_per_point, len(row_copies))):
            row_copies.pop()()

    p_halves = (pa_ref, pb_ref)
    y_blocks = (y0_ref, y1_ref, y2_ref, y3_ref)
    assert n_blk == len(y_blocks)

    def project(half):
        rows = slice(half * (tm // 2), (half + 1) * (tm // 2))
        p_halves[half][...] = _dot(h_ref[rows, :], win_ref[...])

    def block(j, state, zprev):
        p_ref, y_ref = p_halves[j // (n_blk // 2)], y_blocks[j]
        r0 = (j % (n_blk // 2)) * GMLP_BLOCK
        rows = slice(r0, r0 + GMLP_BLOCK)

        u = gelu(p_ref[rows, OFF_AU:OFF_AU + A_DIM])
        vv = gelu(p_ref[rows, OFF_AV:OFF_AV + A_DIM])
        mu = jnp.mean(vv, axis=-1, keepdims=True)
        var = jnp.mean((vv - mu) ** 2, axis=-1, keepdims=True)
        vn = _bf((vv - mu) * lax.rsqrt(var + EPS) * lng_ref[...] + lnb_ref[...])
        sv = jnp.zeros((GMLP_BLOCK, A_DIM), jnp.float32)
        for hh in range(A_HEADS):
            sv = jnp.where(a_head_of_lane == hh, _dot(wsm_ref[hh], vn), sv)
        y_ref[:, 0:A_DIM] = _bf(u * (sv + bsm_ref[...]))
        start_row_copies()

        z = _dot(_bf(p_ref[rows, OFF_GLR:OFF_GLR + LANES]), wg_ref[...]) + bg_ref[...]
        glog = (jnp.minimum(z, 0.0) - jnp.log1p(jnp.exp(-jnp.abs(z)))) / GLA_TAU
        def chunk_head(c):
            rc = slice(r0 + c * CHUNK, r0 + (c + 1) * CHUNK)
            g = glog[c * CHUNK:(c + 1) * CHUNK]
            g_hi = _bf(g)
            g_r1 = g - g_hi.astype(jnp.float32)
            g_mid = _bf(g_r1)
            g_lo = _bf(g_r1 - g_mid.astype(jnp.float32))
            b = _dot(tri, g_hi) + _dot(tri, g_mid) + _dot(tri, g_lo)
            b_mid = b[CHUNK // 2:CHUNK // 2 + 1]
            b_last = b[CHUNK - 1:CHUNK]
            q = p_ref[rc, OFF_Q:OFF_Q + B_KDIM] * (B_DK ** -0.5)
            k = p_ref[rc, OFF_K:OFF_K + B_KDIM]
            vb = _bf(p_ref[rc, OFF_V:OFF_V + B_VDIM])
            qs = q * jnp.exp(b - b_mid)
            ks = _bf(k * jnp.exp(b_mid - b))
            kd = _bf(k * jnp.exp(b_last - b))
            qb = q * jnp.exp(b)
            qs_st = _bf(jnp.concatenate(
                [jnp.where(head_of_lane == hh, qs, 0.0) for hh in range(B_HEADS)], axis=0))
            qb_st = _bf(jnp.concatenate(
                [jnp.where(head_of_lane == hh, qb, 0.0) for hh in range(B_HEADS)], axis=0))
            scores = _bf(jnp.where(causal4, _dot(qs_st, ks, _NT), 0.0))
            start_row_copies()
            kv_all = _dot(kd, vb, _TN)
            start_row_copies()
            o_intra, kvs = [], []
            for hh in range(B_HEADS):
                rs = slice(hh * CHUNK, (hh + 1) * CHUNK)
                cs = slice(hh * B_DV, (hh + 1) * B_DV)
                o_intra.append(_dot(scores[rs], vb[:, cs]))
                kvs.append(kv_all[rs, cs])
                start_row_copies()
            decay_row = jnp.broadcast_to(jnp.exp(b_last), (B_KDIM, B_KDIM))
            decay_col = jnp.sum(jnp.where(eye256, decay_row, 0.0), axis=1, keepdims=True)
            og = p_ref[rc, OFF_OG:OFF_OG + B_VDIM]
            return qb_st, o_intra, jnp.concatenate(kvs, axis=0), decay_col, og * jax.nn.sigmoid(og)

        heads = [chunk_head(c) for c in range(GMLP_BLOCK // CHUNK)]
        for c, (qb_st, o_intra, kv, decay_col, out_gate) in enumerate(heads):
            o_inter = _dot(qb_st, _bf(state))
            state = decay_col * state + kv
            outs = [_rms(o_intra[hh] + o_inter[hh * CHUNK:(hh + 1) * CHUNK]) for hh in range(B_HEADS)]
            on = jnp.concatenate(outs, axis=1) * glag_ref[...]
            y_ref[c * CHUNK:(c + 1) * CHUNK, A_DIM:A_DIM + B_VDIM] = _bf(on * out_gate)
            start_row_copies()

        zz = p_ref[rows, OFF_CC:OFF_CC + C_DIM] * p_ref[rows, OFF_CX:OFF_CX + C_DIM]
        z1 = jnp.where(row128 == 0, zprev[7:8], pltpu.roll(zz, 1, 0))
        z2 = jnp.where(row128 == 0, zprev[6:7], jnp.where(row128 == 1, zprev[7:8], pltpu.roll(zz, 2, 0)))
        cw = cw_ref[...]
        yc = cb_ref[...] + cw[0:1] * z2
        yc = yc + cw[1:2] * z1
        yc = yc + cw[2:3] * zz
        y_ref[:, A_DIM + B_VDIM:A_DIM + B_VDIM + C_DIM] = _bf(p_ref[rows, OFF_CB:OFF_CB + C_DIM] * yc)
        start_row_copies()
        return state, zz[GMLP_BLOCK - SUBLANES:GMLP_BLOCK]

    state, zprev, carry = s_ref[...], zc_ref[...], carry_ref[...]
    project(0)
    for j in range(n_blk):
        if j == 0:
            project(1)
        state, zprev = block(j, state, zprev)
        rows = slice(j * GMLP_BLOCK, (j + 1) * GMLP_BLOCK)
        xo = x_in[rows, :] + gt1 * _dot(y_blocks[j][...], wout_ref[...])
        o_ref[rows, :] = xo
        start_row_copies()
        h2 = (_rms(xo) * g2_ref[...]) * (1.0 + sc2) + sh2
        h2_ref[slot, rows, :] = _pack_rows(h2)
        d0, d1, wcol, carry = _route(h2, rwt_ref, rb_ref, carry, cap)
        dvm_ref[0:1, rows] = d0
        dvm_ref[1:2, rows] = d1
        w_ref[rows, :] = wcol
        start_row_copies()
    while row_copies:
        row_copies.pop()()
    s_ref[...] = state
    zc_ref[...] = zprev
    carry_ref[...] = carry
    cnt = jnp.broadcast_to(carry, cnt_ref.shape).astype(jnp.int32)
    cnt_ref[...] = cnt
    dest_ref[...] = dvm_ref[0:TOP_K, :]

    drain_rows(h2_prev, xs_ref, sem)

    to_smem = [pltpu.make_async_copy(dvm_ref.at[0], d0s_ref, sem_s),
               pltpu.make_async_copy(dvm_ref.at[1], d1s_ref, sem_s)]
    for cp in to_smem:
        cp.start()

    @pl.when(last_step)
    def _():
        for cp in to_smem:
            cp.wait()
        h2_last = h2_ref.at[slot]

        def issue(g, c):
            for jj in range(SUBLANES):
                i = pl.multiple_of(g * SUBLANES, SUBLANES) + jj
                _row_copy(h2_last, i, xs_ref, d0s_ref[i], sem).start()
                _row_copy(h2_last, i, xs_ref, d1s_ref[i], sem).start()
            return c

        lax.fori_loop(0, tm // SUBLANES, issue, 0)
        drain_rows(h2_last, xs_ref, sem)
        if fuse_prev:
            drain_rows(ysp_ref, ybuf.at[0, 0], sem_g.at[1 - slot], _tile_copy)
        cvm_ref[...] = cnt
        zero_ref[...] = jnp.zeros_like(zero_ref)
        cp = pltpu.make_async_copy(cvm_ref, csm_ref, sem_s)
        cp.start()
        cp.wait()
        _zero_segment_padding(csm_ref, cap, zero_ref, xs_ref, sem_z)


def _mixer(x, mod, g1, g2, w_in, w_out, ln_g, ln_b, w_s, b_s, w_gate, b_gate, gla_g, conv_w, conv_b,
           router_w, router_b, prev=None):
    bn, t, d = x.shape
    n = bn * t
    tm = min(MIX_ROWS, t)
    nt = t // tm
    cap = n
    glr0 = 2 * A_DIM + 2 * B_KDIM + B_VDIM
    w_in_p = _bf(jnp.concatenate(
        [w_in[:, :glr0], w_in[:, glr0 + GLA_RANK:], w_in[:, glr0:glr0 + GLA_RANK],
         jnp.zeros((d, LANES - GLA_RANK), w_in.dtype)], axis=1))
    w_gate_p = _bf(jnp.concatenate(
        [w_gate, jnp.zeros((LANES - GLA_RANK, B_KDIM), w_gate.dtype)], axis=0))
    bsm = jnp.repeat(b_s.T, A_DIM // A_HEADS, axis=1)

    def whole(shape):
        return pl.BlockSpec(shape, lambda b, i: (0,) * len(shape))

    prev_specs, prev_args = [], ()
    if prev is not None:
        n_tiles = bn * nt

        def this_tile(b, i):
            return (b * nt + i,)

        def next_tile(b, i):
            return (jnp.minimum(b * nt + i + 1, n_tiles - 1),)

        d0p, d1p, wcolp, modp, ysp = prev
        prev_specs = [
            pl.BlockSpec((tm,), this_tile, memory_space=pltpu.SMEM),
            pl.BlockSpec((tm,), this_tile, memory_space=pltpu.SMEM),
            pl.BlockSpec((tm,), next_tile, memory_space=pltpu.SMEM),
            pl.BlockSpec((tm,), next_tile, memory_space=pltpu.SMEM),
            pl.BlockSpec((tm, 8), lambda b, i: (b * nt + i, 0)),
            pl.BlockSpec((None, 6, d), lambda b, i: (b, 0, 0)),
            pl.BlockSpec(memory_space=pl.ANY),
        ]
        prev_args = (d0p, d1p, d0p, d1p, wcolp, modp, ysp)

    return pl.pallas_call(
        functools.partial(_mixer_kernel, cap=cap, fuse_prev=prev is not None),
        grid=(bn, nt),
        in_specs=[
            pl.BlockSpec((None, tm, d), lambda b, i: (b, i, 0)),
            pl.BlockSpec((None, 6, d), lambda b, i: (b, 0, 0)),
            whole((1, d)),
            whole((d, P_PAD)),
            whole((d, d)),
            whole((1, A_DIM)),
            whole((1, A_DIM)),
            whole((A_HEADS, GMLP_BLOCK, GMLP_BLOCK)),
            whole((GMLP_BLOCK, A_DIM)),
            whole((LANES, B_KDIM)),
            whole((1, B_KDIM)),
            whole((1, B_VDIM)),
            whole((3, C_DIM)),
            whole((1, C_DIM)),
            whole((1, d)),
            whole((N_EXPERTS, d)),
            whole((N_EXPERTS, 1)),
        ] + prev_specs,
        out_specs=[
            pl.BlockSpec((None, tm, d), lambda b, i: (b, i, 0)),
            pl.BlockSpec((TOP_K, tm), lambda b, i: (0, b * nt + i)),
            pl.BlockSpec((tm, 8), lambda b, i: (b * nt + i, 0)),
            whole((N_EXPERTS, LANES)),
            pl.BlockSpec(memory_space=pl.ANY),
        ],
        out_shape=[
            jax.ShapeDtypeStruct(x.shape, x.dtype),
            jax.ShapeDtypeStruct((TOP_K, n), jnp.int32),
            jax.ShapeDtypeStruct((n, 8), jnp.float32),
            jax.ShapeDtypeStruct((N_EXPERTS, LANES), jnp.int32),
            jax.ShapeDtypeStruct((N_EXPERTS * cap + TOP_K * tm, d // 2), jnp.uint32),
        ],
        scratch_shapes=[
            pltpu.VMEM((tm, d), jnp.bfloat16),
            pltpu.VMEM((tm // 2, P_PAD), jnp.float32),
            pltpu.VMEM((tm // 2, P_PAD), jnp.float32),
            pltpu.VMEM((GMLP_BLOCK, d), jnp.bfloat16),
            pltpu.VMEM((GMLP_BLOCK, d), jnp.bfloat16),
            pltpu.VMEM((GMLP_BLOCK, d), jnp.bfloat16),
            pltpu.VMEM((GMLP_BLOCK, d), jnp.bfloat16),
            pltpu.VMEM((B_KDIM, B_DV), jnp.float32),
            pltpu.VMEM((SUBLANES, C_DIM), jnp.float32),
            pltpu.VMEM((A_HEADS, GMLP_BLOCK, GMLP_BLOCK), jnp.bfloat16),
            pltpu.VMEM((2, tm, d // 2), jnp.uint32),
            pltpu.VMEM((N_EXPERTS, 1), jnp.float32),
            pltpu.VMEM((SUBLANES, tm), jnp.int32),
            pltpu.SMEM((tm,), jnp.int32),
            pltpu.SMEM((tm,), jnp.int32),
            pltpu.VMEM((N_EXPERTS, LANES), jnp.int32),
            pltpu.SMEM((N_EXPERTS, LANES), jnp.int32),
            pltpu.VMEM((FFN_ROWS // 2, d // 2), jnp.uint32),
            pltpu.SemaphoreType.DMA,
            pltpu.SemaphoreType.DMA,
            pltpu.SemaphoreType.DMA,
            pltpu.VMEM((2, TOP_K, tm * SUBLANES, LANES), jnp.float32),
            pltpu.VMEM((tm, d), jnp.float32),
            pltpu.SemaphoreType.DMA((2,)),
        ],
        compiler_params=pltpu.CompilerParams(
            dimension_semantics=("arbitrary", "arbitrary"), vmem_limit_bytes=VMEM_LIMIT),
        name="mixer",
    )(x, mod, g1.reshape(1, d), w_in_p, _bf(w_out), ln_g.reshape(1, -1), ln_b.reshape(1, -1), w_s, bsm,
      w_gate_p, b_gate.reshape(1, -1), gla_g.reshape(1, -1), conv_w, conv_b.reshape(1, -1),
      g2.reshape(1, d), router_w.T, router_b.reshape(N_EXPERTS, 1), *prev_args)


def _ffn_kernel(be_ref, rb_ref, nb_ref, xs_ref, w1_ref, w3_ref, w2_ref, ys_ref, w1b_ref, w3b_ref, w2b_ref):
    del rb_ref
    i = pl.program_id(0)
    used = i < nb_ref[0]
    new_expert = jnp.logical_or(i == 0, be_ref[i] != be_ref[jnp.maximum(i - 1, 0)])

    @pl.when(jnp.logical_and(used, new_expert))
    def _():
        w1b_ref[...] = _bf(w1_ref[...])
        w3b_ref[...] = _bf(w3_ref[...])
        w2b_ref[...] = _bf(w2_ref[...])

    @pl.when(used)
    def _():
        x_a, x_b = _unpack_rows(xs_ref[...])
        half = x_a.shape[1]
        a = _dot(x_a, w1b_ref[0:half, :]) + _dot(x_b, w1b_ref[half:, :])
        g = _dot(x_a, w3b_ref[0:half, :]) + _dot(x_b, w3b_ref[half:, :])
        hm = _bf((a * jax.nn.sigmoid(a)) * g)
        _store_rows_as_tiles(ys_ref, _dot(hm, w2b_ref[...]))


def _expert_ffn(layer, block_e, row_block, n_used, xs, w1, w3, w2):
    n_rows = xs.shape[0]
    n_blocks = block_e.shape[0]
    d, f = w1.shape[-2], w1.shape[-1]
    return pl.pallas_call(
        _ffn_kernel,
        grid_spec=pltpu.PrefetchScalarGridSpec(
            num_scalar_prefetch=3,
            grid=(n_blocks,),
            in_specs=[
                pl.BlockSpec((FFN_ROWS, d // 2), lambda i, be, rb, nb: (rb[i], 0)),
                pl.BlockSpec((None, None, d, f), lambda i, be, rb, nb: (layer, be[i], 0, 0)),
                pl.BlockSpec((None, None, d, f), lambda i, be, rb, nb: (layer, be[i], 0, 0)),
                pl.BlockSpec((None, None, f, d), lambda i, be, rb, nb: (layer, be[i], 0, 0)),
            ],
            out_specs=pl.BlockSpec((FFN_ROWS * SUBLANES, LANES), lambda i, be, rb, nb: (rb[i], 0)),
            scratch_shapes=[pltpu.VMEM((d, f), jnp.bfloat16), pltpu.VMEM((d, f), jnp.bfloat16),
                            pltpu.VMEM((f, d), jnp.bfloat16)],
        ),
        out_shape=jax.ShapeDtypeStruct((n_rows * SUBLANES, LANES), jnp.float32),
        compiler_params=pltpu.CompilerParams(
            dimension_semantics=("arbitrary",), vmem_limit_bytes=VMEM_LIMIT),
        name="expert_ffn",
    )(block_e, row_block, n_used, xs, w1, w3, w2)


def _combine_kernel(d0_ref, d1_ref, d0n_ref, d1n_ref, x_ref, mod_ref, w_ref, gf_ref, ys_ref, o_ref, ybuf, sems,
                    *, final_norm):
    tm = x_ref.shape[0]
    step = pl.program_id(0) * pl.num_programs(1) + pl.program_id(1)
    last = step == pl.num_programs(0) * pl.num_programs(1) - 1
    slot = step % 2

    def drain(sem):
        def body(g, c):
            for _ in range(TOP_K * SUBLANES):
                _tile_copy(ys_ref, 0, ybuf.at[0, 0], 0, sem).wait()
            return c
        lax.fori_loop(0, tm // SUBLANES, body, 0)

    @pl.when(step == 0)
    def _():
        def issue(g, c):
            for jj in range(SUBLANES):
                i = pl.multiple_of(g * SUBLANES, SUBLANES) + jj
                _tile_copy(ys_ref, d0_ref[i], ybuf.at[0, 0], i * SUBLANES, sems.at[0]).start()
                _tile_copy(ys_ref, d1_ref[i], ybuf.at[0, 1], i * SUBLANES, sems.at[0]).start()
            return c
        lax.fori_loop(0, tm // SUBLANES, issue, 0)

    nxt, sem_nxt = ybuf.at[1 - slot], sems.at[1 - slot]
    for i in range(tm):
        _tile_copy(ys_ref, d0n_ref[i], nxt.at[0], i * SUBLANES, sem_nxt).start(priority=0)
        _tile_copy(ys_ref, d1n_ref[i], nxt.at[1], i * SUBLANES, sem_nxt).start(priority=1)

    drain(sems.at[slot])
    w = w_ref[...]
    gt2 = mod_ref[...][5:6]
    out = x_ref[...] + gt2 * (w[:, 0:1] * _load_rows_from_tiles(ybuf, (slot, 0), tm)
                              + w[:, 1:2] * _load_rows_from_tiles(ybuf, (slot, 1), tm))
    if final_norm:
        out = _rms(out) * gf_ref[...]
    o_ref[...] = out

    @pl.when(last)
    def _():
        drain(sems.at[1 - slot])


def _combine(d0, d1, x, mod, wcol, g_final, ys, final_norm):
    bn, t, d = x.shape
    tm = min(MOVE_ROWS, t)
    nt = t // tm
    n_tiles = bn * nt

    def this_tile(b, i):
        return (b * nt + i,)

    def next_tile(b, i):
        return (jnp.minimum(b * nt + i + 1, n_tiles - 1),)

    return pl.pallas_call(
        functools.partial(_combine_kernel, final_norm=final_norm),
        grid=(bn, nt),
        in_specs=[
            pl.BlockSpec((tm,), this_tile, memory_space=pltpu.SMEM),
            pl.BlockSpec((tm,), this_tile, memory_space=pltpu.SMEM),
            pl.BlockSpec((tm,), next_tile, memory_space=pltpu.SMEM),
            pl.BlockSpec((tm,), next_tile, memory_space=pltpu.SMEM),
            pl.BlockSpec((None, tm, d), lambda b, i: (b, i, 0)),
            pl.BlockSpec((None, 6, d), lambda b, i: (b, 0, 0)),
            pl.BlockSpec((tm, 8), lambda b, i: (b * nt + i, 0)),
            pl.BlockSpec((1, d), lambda b, i: (0, 0)),
            pl.BlockSpec(memory_space=pl.ANY),
        ],
        out_specs=pl.BlockSpec((None, tm, d), lambda b, i: (b, i, 0)),
        out_shape=jax.ShapeDtypeStruct(x.shape, x.dtype),
        scratch_shapes=[pltpu.VMEM((2, TOP_K, tm * SUBLANES, LANES), jnp.float32), pltpu.SemaphoreType.DMA((2,))],
        compiler_params=pltpu.CompilerParams(
            dimension_semantics=("arbitrary", "arbitrary"), vmem_limit_bytes=VMEM_LIMIT),
        name="combine",
    )(d0, d1, d0, d1, x, mod, wcol, g_final.reshape(1, d), ys)


def _block_map(counts, cap, n_blocks):
    blocks_per_expert = (counts + FFN_ROWS - 1) // FFN_ROWS
    ends = jnp.cumsum(blocks_per_expert)
    n_used = ends[-1:]
    step = jnp.minimum(jnp.arange(n_blocks, dtype=jnp.int32), n_used - 1)
    block_e = jnp.minimum(jnp.sum((ends[None, :] <= step[:, None]).astype(jnp.int32), axis=1), N_EXPERTS - 1)
    first = (ends - blocks_per_expert)[block_e]
    row_block = block_e * (cap // FFN_ROWS) + (step - first)
    return block_e.astype(jnp.int32), row_block.astype(jnp.int32), n_used.astype(jnp.int32)


def kernel(x, c, w_ada, b_ada, g_norm1, g_norm2, w_in, w_out, gmlp_ln_g, gmlp_ln_b, gmlp_ws, gmlp_bs,
           gla_w_gate, gla_b_gate, gla_norm_g, conv_w, conv_b, router_w, router_b, exp_w1, exp_w3, exp_w2,
           g_final):
    depth = w_ada.shape[0]
    bn, t, d = x.shape
    n = bn * t
    n_blocks = -(-(n * TOP_K) // FFN_ROWS) + N_EXPERTS
    mod_all = _modulation(c, w_ada, b_ada).reshape(depth, bn, 6, d)
    prev = None
    for l in range(depth):
        mod = mod_all[l]
        x, dest, wcol, cnt, xs = _mixer(
            x, mod, g_norm1[l], g_norm2[l], w_in[l], w_out[l], gmlp_ln_g[l], gmlp_ln_b[l], gmlp_ws[l],
            gmlp_bs[l], gla_w_gate[l], gla_b_gate[l], gla_norm_g[l], conv_w[l], conv_b[l], router_w, router_b,
            prev=prev)
        block_e, row_block, n_used = _block_map(cnt[:, 0], n, n_blocks)
        ys = _expert_ffn(l, block_e, row_block, n_used, xs, exp_w1, exp_w3, exp_w2)
        prev = (dest[0] * SUBLANES, dest[1] * SUBLANES, wcol, mod, ys)
    d0, d1, wcol, mod, ys = prev
    return _combine(d0, d1, x, mod, wcol, g_final, ys, final_norm=True)
```

```python
import functools

import jax
import jax.numpy as jnp
from jax import lax
from jax.experimental import pallas as pl
from jax.experimental.pallas import tpu as pltpu

CHUNK = 64
GMLP_BLOCK = 128
A_DIM = 256
A_HEADS = 4
B_HEADS = 4
B_DK = 64
B_DV = 128
B_KDIM = B_HEADS * B_DK
B_VDIM = B_HEADS * B_DV
GLA_RANK = 16
GLA_TAU = 16.0
C_DIM = 256
N_EXPERTS = 32
N_GROUPS = 4
EXPERTS_PER_GROUP = N_EXPERTS // N_GROUPS
TOP_K = 2
EPS = 1e-6

LANES = 128
SUBLANES = 8
OFF_AU, OFF_AV, OFF_Q, OFF_K, OFF_V = 0, 256, 512, 768, 1024
OFF_OG, OFF_CB, OFF_CC, OFF_CX, OFF_GLR = 1536, 2048, 2304, 2560, 2816
P_PAD = OFF_GLR + LANES

MIX_ROWS = 512
COPY_POINTS_PER_BLOCK = 18
MOVE_ROWS = 512
FFN_ROWS = 512
VMEM_LIMIT = 56 * 1024 * 1024

_NT = (((1,), (1,)), ((), ()))
_TN = (((0,), (0,)), ((), ()))


def _dot(a, b, dims=None):
    if dims is None:
        return jnp.dot(a, b, preferred_element_type=jnp.float32)
    return lax.dot_general(a, b, dims, preferred_element_type=jnp.float32)


def _bf(x):
    return x.astype(jnp.bfloat16)


def _split_bf16(x):
    hi = _bf(x)
    lo = _bf(x - hi.astype(jnp.float32))
    return hi, lo


def _rms(x):
    return x * lax.rsqrt(jnp.mean(x * x, axis=-1, keepdims=True) + EPS)


def _pack_rows(v):
    half = v.shape[1] // 2
    hi = pltpu.bitcast(_bf(v[:, :half]).astype(jnp.float32), jnp.uint32)
    lo = pltpu.bitcast(_bf(v[:, half:]).astype(jnp.float32), jnp.uint32)
    return hi | (lo >> 16)


def _unpack_rows(w):
    hi = pltpu.bitcast(w & jnp.uint32(0xFFFF0000), jnp.float32)
    lo = pltpu.bitcast(w << 16, jnp.float32)
    return _bf(hi), _bf(lo)


def _row_copy(src_ref, src_row, dst_ref, dst_row, sem):
    return pltpu.make_async_copy(src_ref.at[pl.ds(src_row, 1)], dst_ref.at[pl.ds(dst_row, 1)], sem)


ROW_GROUP = 4

def _aligned(row):
    return row if isinstance(row, int) else pl.multiple_of(row, ROW_GROUP)


def _tile_copy(src_ref, src_row, dst_ref, dst_row, sem):
    return pltpu.make_async_copy(src_ref.at[pl.ds(_aligned(src_row), ROW_GROUP)],
                                 dst_ref.at[pl.ds(_aligned(dst_row), ROW_GROUP)], sem)


def _store_rows_as_tiles(ref, value):
    rows = value.shape[0]
    assert value.shape[1] == ROW_GROUP * LANES
    for c in range(ROW_GROUP):
        ref[pl.ds(c, rows, stride=ROW_GROUP), :] = value[:, c * LANES:(c + 1) * LANES]


def _load_rows_from_tiles(ref, idx, rows):
    packed = jnp.concatenate(
        [ref[idx + (pl.ds(c, rows, stride=ROW_GROUP), slice(None))] for c in range(ROW_GROUP)], axis=1)
    hi, lo = _unpack_rows(packed)
    return jnp.concatenate([hi.astype(jnp.float32), lo.astype(jnp.float32)], axis=1)


def _mod_kernel(c_ref, w_ref, b_ref, o_ref):
    c = c_ref[...]
    s = c * jax.nn.sigmoid(c)
    s_hi, s_lo = _split_bf16(s)
    w_hi, w_lo = _split_bf16(w_ref[...])
    acc = _dot(s_hi, w_hi) + _dot(s_hi, w_lo) + _dot(s_lo, w_hi)
    o_ref[...] = acc + b_ref[...]


def _modulation(c, w_ada, b_ada):
    depth, d, six_d = w_ada.shape
    bn = c.shape[0]
    cb = 1024
    return pl.pallas_call(
        _mod_kernel,
        grid=(depth, six_d // cb),
        in_specs=[
            pl.BlockSpec((bn, d), lambda l, j: (0, 0)),
            pl.BlockSpec((None, d, cb), lambda l, j: (l, 0, j)),
            pl.BlockSpec((None, 1, cb), lambda l, j: (l, 0, j)),
        ],
        out_specs=pl.BlockSpec((None, bn, cb), lambda l, j: (l, 0, j)),
        out_shape=jax.ShapeDtypeStruct((depth, bn, six_d), jnp.float32),
        compiler_params=pltpu.CompilerParams(
            dimension_semantics=("arbitrary", "arbitrary"), vmem_limit_bytes=VMEM_LIMIT),
        name="adaln_mod",
    )(c, w_ada, b_ada.reshape(depth, 1, six_d))


def _route(h2, rwt_ref, rb_ref, carry, cap):
    tr = h2.shape[0]
    h_hi, h_lo = _split_bf16(h2)
    w_hi, w_lo = _split_bf16(rwt_ref[...])
    logits = (_dot(w_hi, h_hi, _NT) + _dot(w_hi, h_lo, _NT) + _dot(w_lo, h_hi, _NT)) + rb_ref[...]
    ex = jnp.exp(logits - jnp.max(logits, axis=0, keepdims=True))
    probs = ex / jnp.sum(ex, axis=0, keepdims=True)

    idx8 = lax.broadcasted_iota(jnp.int32, (EXPERTS_PER_GROUP, tr), 0)
    best = None
    for g in range(N_GROUPS):
        pg = probs[g * EXPERTS_PER_GROUP:(g + 1) * EXPERTS_PER_GROUP]
        m1 = jnp.max(pg, axis=0, keepdims=True)
        i1 = jnp.min(jnp.where(pg == m1, idx8, EXPERTS_PER_GROUP), axis=0, keepdims=True)
        pg2 = jnp.where(idx8 == i1, -1.0, pg)
        m2 = jnp.max(pg2, axis=0, keepdims=True)
        i2 = jnp.min(jnp.where(pg2 == m2, idx8, EXPERTS_PER_GROUP), axis=0, keepdims=True)
        cand = (m1 + m2, m1, m2, i1 + g * EXPERTS_PER_GROUP, i2 + g * EXPERTS_PER_GROUP)
        if best is None:
            best = cand
        else:
            better = cand[0] > best[0]
            best = tuple(jnp.where(better, a, b) for a, b in zip(cand, best))
    _, p1, p2, e0, e1 = best
    denom = p1 + p2
    w0, w1 = p1 / denom, p2 / denom

    eidx = lax.broadcasted_iota(jnp.int32, (N_EXPERTS, tr), 0)
    hit0, hit1 = eidx == e0, eidx == e1
    onehot = jnp.where(jnp.logical_or(hit0, hit1), 1.0, 0.0)
    before = (lax.broadcasted_iota(jnp.int32, (tr, tr), 0) < lax.broadcasted_iota(jnp.int32, (tr, tr), 1))
    rank = _dot(_bf(onehot), _bf(jnp.where(before, 1.0, 0.0))) + carry
    r0 = jnp.sum(jnp.where(hit0, rank, 0.0), axis=0, keepdims=True).astype(jnp.int32)
    r1 = jnp.sum(jnp.where(hit1, rank, 0.0), axis=0, keepdims=True).astype(jnp.int32)
    carry = carry + jnp.sum(onehot, axis=1, keepdims=True)

    eye = lax.broadcasted_iota(jnp.int32, (tr, tr), 0) == lax.broadcasted_iota(jnp.int32, (tr, tr), 1)
    w0c = jnp.sum(jnp.where(eye, jnp.broadcast_to(w0, (tr, tr)), 0.0), axis=1, keepdims=True)
    w1c = jnp.sum(jnp.where(eye, jnp.broadcast_to(w1, (tr, tr)), 0.0), axis=1, keepdims=True)
    lane8 = lax.broadcasted_iota(jnp.int32, (tr, 8), 1)
    wcol = jnp.where(lane8 == 0, w0c, jnp.where(lane8 == 1, w1c, 0.0))
    return e0 * cap + r0, e1 * cap + r1, wcol, carry


def _zero_segment_padding(cnt_ref, cap, zero_ref, xs_ref, sem):
    copies = []
    for ex in range(N_EXPERTS):
        cnt = cnt_ref[ex, 0]
        seg = ex * cap
        pos = seg + cnt
        end8 = seg + ((cnt + SUBLANES - 1) // SUBLANES) * SUBLANES
        for j in range(SUBLANES - 1):
            copies.append((pos + j < end8, _row_copy(zero_ref, 0, xs_ref, pos + j, sem)))
        seg_end = seg + ((cnt + FFN_ROWS - 1) // FFN_ROWS) * FFN_ROWS
        groups = (seg_end - end8) // SUBLANES
        at = end8
        bit = FFN_ROWS // (2 * SUBLANES)
        while bit >= 1:
            rows = bit * SUBLANES
            pred = (groups & bit) != 0
            copies.append((pred, pltpu.make_async_copy(
                zero_ref.at[pl.ds(0, rows)], xs_ref.at[pl.ds(pl.multiple_of(at, SUBLANES), rows)], sem)))
            at = at + jnp.where(pred, rows, 0)
            bit //= 2
    for pred, cp in copies:
        pl.when(pred)(cp.start)
    for pred, cp in copies:
        pl.when(pred)(cp.wait)


N_MIXER_INPUTS = 17
N_MIXER_OUTPUTS = 5


def _mixer_kernel(*refs, cap, fuse_prev):
    ins, rest = refs[:N_MIXER_INPUTS], refs[N_MIXER_INPUTS:]
    if fuse_prev:
        dp0_ref, dp1_ref, dp0n_ref, dp1n_ref, wp_ref, modp_ref, ysp_ref = rest[:7]
        rest = rest[7:]
    outs, scratch = rest[:N_MIXER_OUTPUTS], rest[N_MIXER_OUTPUTS:]
    (x_ref, mod_ref, g1_ref, win_ref, wout_ref, lng_ref, lnb_ref, ws_ref, bsm_ref,
     wg_ref, bg_ref, glag_ref, cw_ref, cb_ref, g2_ref, rwt_ref, rb_ref) = ins
    o_ref, dest_ref, w_ref, cnt_ref, xs_ref = outs
    (h_ref, pa_ref, pb_ref, y0_ref, y1_ref, y2_ref, y3_ref, s_ref, zc_ref, wsm_ref, h2_ref,
     carry_ref, dvm_ref, d0s_ref, d1s_ref, cvm_ref, csm_ref, zero_ref, sem, sem_s, sem_z,
     ybuf, xin_ref, sem_g) = scratch

    bi, ti = pl.program_id(0), pl.program_id(1)
    tm = x_ref.shape[0]
    n_blk = tm // GMLP_BLOCK
    step = bi * pl.num_programs(1) + ti
    last_step = step == pl.num_programs(0) * pl.num_programs(1) - 1
    slot = step % 2
    spare_row0 = N_EXPERTS * cap

    def drain_rows(src_ref, dst_ref, s, copy=_row_copy):
        def body(g, c):
            for _ in range(TOP_K * SUBLANES):
                copy(src_ref, 0, dst_ref, 0, s).wait()
            return c
        lax.fori_loop(0, tm // SUBLANES, body, 0)

    @pl.when(step > 0)
    def _():
        pltpu.make_async_copy(dvm_ref.at[0], d0s_ref, sem_s).wait()
        pltpu.make_async_copy(dvm_ref.at[1], d1s_ref, sem_s).wait()

    if fuse_prev:
        @pl.when(step == 0)
        def _():
            def issue(g, c):
                for jj in range(SUBLANES):
                    i = pl.multiple_of(g * SUBLANES, SUBLANES) + jj
                    _tile_copy(ysp_ref, dp0_ref[i], ybuf.at[0, 0], i * ROW_GROUP, sem_g.at[0]).start()
                    _tile_copy(ysp_ref, dp1_ref[i], ybuf.at[0, 1], i * ROW_GROUP, sem_g.at[0]).start()
                return c
            lax.fori_loop(0, tm // SUBLANES, issue, 0)

        drain_rows(ysp_ref, ybuf.at[0, 0], sem_g.at[slot], _tile_copy)
        wp = wp_ref[...]
        xin_ref[...] = x_ref[...] + modp_ref[...][5:6] * (
            wp[:, 0:1] * _load_rows_from_tiles(ybuf, (slot, 0), tm)
            + wp[:, 1:2] * _load_rows_from_tiles(ybuf, (slot, 1), tm))
        x_in = xin_ref
    else:
        x_in = x_ref

    @pl.when(step == 0)
    def _():
        carry_ref[...] = jnp.zeros_like(carry_ref)
        h2_ref[...] = jnp.zeros_like(h2_ref)

        def fill(i, c):
            d0s_ref[i] = spare_row0 + i
            d1s_ref[i] = spare_row0 + tm + i
            return c

        lax.fori_loop(0, tm, fill, 0)
        tt = lax.broadcasted_iota(jnp.int32, (GMLP_BLOCK, GMLP_BLOCK), 0) // CHUNK
        ss = lax.broadcasted_iota(jnp.int32, (GMLP_BLOCK, GMLP_BLOCK), 1) // CHUNK
        for h in range(A_HEADS):
            wsm_ref[h] = _bf(jnp.where(tt >= ss, ws_ref[h], 0.0))

    @pl.when(ti == 0)
    def _():
        s_ref[...] = jnp.zeros_like(s_ref)
        zc_ref[...] = jnp.zeros_like(zc_ref)

    m = mod_ref[...]
    sh1, sc1, gt1, sh2, sc2 = m[0:1], m[1:2], m[2:3], m[3:4], m[4:5]
    h_ref[...] = _bf((_rms(x_in[...]) * g1_ref[...]) * (1.0 + sc1) + sh1)

    lane256 = lax.broadcasted_iota(jnp.int32, (CHUNK, B_KDIM), 1)
    head_of_lane = lane256 // B_DK
    r64 = lax.broadcasted_iota(jnp.int32, (CHUNK, CHUNK), 0)
    c64 = lax.broadcasted_iota(jnp.int32, (CHUNK, CHUNK), 1)
    tri = _bf(jnp.where(r64 >= c64, 1.0, 0.0))
    causal4 = jnp.concatenate([r64 >= c64] * B_HEADS, axis=0)
    eye256 = (lax.broadcasted_iota(jnp.int32, (B_KDIM, B_KDIM), 0)
              == lax.broadcasted_iota(jnp.int32, (B_KDIM, B_KDIM), 1))
    a_head_of_lane = lax.broadcasted_iota(jnp.int32, (GMLP_BLOCK, A_DIM), 1) // (A_DIM // A_HEADS)
    row128 = lax.broadcasted_iota(jnp.int32, (GMLP_BLOCK, C_DIM), 0)
    sqrt_half = 0.7071067811865476

    def gelu(v):
        return 0.5 * v * (1.0 + lax.erf(v * sqrt_half))

    h2_prev = h2_ref.at[1 - slot]
    row_copies = []
    for i in range(tm):
        row_copies.append(lambda i=i: _row_copy(h2_prev, i, xs_ref, d0s_ref[i], sem).start(priority=0))
        row_copies.append(lambda i=i: _row_copy(h2_prev, i, xs_ref, d1s_ref[i], sem).start(priority=1))
        if fuse_prev:
            row_copies.append(lambda i=i: _tile_copy(
                ysp_ref, dp0n_ref[i], ybuf.at[1 - slot, 0], i * ROW_GROUP, sem_g.at[1 - slot]).start(priority=0))
            row_copies.append(lambda i=i: _tile_copy(
                ysp_ref, dp1n_ref[i], ybuf.at[1 - slot, 1], i * ROW_GROUP, sem_g.at[1 - slot]).start(priority=1))
    row_copies.reverse()
    copies_per_point = -(-len(row_copies) // (n_blk * COPY_POINTS_PER_BLOCK))

    def start_row_copies():
        for _ in range(min(copies_per_point, len(row_copies))):
            row_copies.pop()()

    p_halves = (pa_ref, pb_ref)
    y_blocks = (y0_ref, y1_ref, y2_ref, y3_ref)
    assert n_blk == len(y_blocks)

    def project(half):
        rows = slice(half * (tm // 2), (half + 1) * (tm // 2))
        p_halves[half][...] = _dot(h_ref[rows, :], win_ref[...])

    def block(j, state, zprev):
        p_ref, y_ref = p_halves[j // (n_blk // 2)], y_blocks[j]
        r0 = (j % (n_blk // 2)) * GMLP_BLOCK
        rows = slice(r0, r0 + GMLP_BLOCK)

        u = gelu(p_ref[rows, OFF_AU:OFF_AU + A_DIM])
        vv = gelu(p_ref[rows, OFF_AV:OFF_AV + A_DIM])
        mu = jnp.mean(vv, axis=-1, keepdims=True)
        var = jnp.mean((vv - mu) ** 2, axis=-1, keepdims=True)
        vn = _bf((vv - mu) * lax.rsqrt(var + EPS) * lng_ref[...] + lnb_ref[...])
        sv = jnp.zeros((GMLP_BLOCK, A_DIM), jnp.float32)
        for hh in range(A_HEADS):
            sv = jnp.where(a_head_of_lane == hh, _dot(wsm_ref[hh], vn), sv)
        y_ref[:, 0:A_DIM] = _bf(u * (sv + bsm_ref[...]))
        start_row_copies()

        z = _dot(_bf(p_ref[rows, OFF_GLR:OFF_GLR + LANES]), wg_ref[...]) + bg_ref[...]
        glog = (jnp.minimum(z, 0.0) - jnp.log1p(jnp.exp(-jnp.abs(z)))) / GLA_TAU
        def chunk_head(c):
            rc = slice(r0 + c * CHUNK, r0 + (c + 1) * CHUNK)
            g = glog[c * CHUNK:(c + 1) * CHUNK]
            g_hi = _bf(g)
            g_r1 = g - g_hi.astype(jnp.float32)
            g_mid = _bf(g_r1)
            g_lo = _bf(g_r1 - g_mid.astype(jnp.float32))
            b = _dot(tri, g_hi) + _dot(tri, g_mid) + _dot(tri, g_lo)
            b_mid = b[CHUNK // 2:CHUNK // 2 + 1]
            b_last = b[CHUNK - 1:CHUNK]
            q = p_ref[rc, OFF_Q:OFF_Q + B_KDIM] * (B_DK ** -0.5)
            k = p_ref[rc, OFF_K:OFF_K + B_KDIM]
            vb = _bf(p_ref[rc, OFF_V:OFF_V + B_VDIM])
            qs = q * jnp.exp(b - b_mid)
            ks = _bf(k * jnp.exp(b_mid - b))
            kd = _bf(k * jnp.exp(b_last - b))
            qb = q * jnp.exp(b)
            qs_st = _bf(jnp.concatenate(
                [jnp.where(head_of_lane == hh, qs, 0.0) for hh in range(B_HEADS)], axis=0))
            qb_st = _bf(jnp.concatenate(
                [jnp.where(head_of_lane == hh, qb, 0.0) for hh in range(B_HEADS)], axis=0))
            scores = _bf(jnp.where(causal4, _dot(qs_st, ks, _NT), 0.0))
            start_row_copies()
            kv_all = _dot(kd, vb, _TN)
            start_row_copies()
            o_intra, kvs = [], []
            for hh in range(B_HEADS):
                rs = slice(hh * CHUNK, (hh + 1) * CHUNK)
                cs = slice(hh * B_DV, (hh + 1) * B_DV)
                o_intra.append(_dot(scores[rs], vb[:, cs]))
                kvs.append(kv_all[rs, cs])
                start_row_copies()
            decay_row = jnp.broadcast_to(jnp.exp(b_last), (B_KDIM, B_KDIM))
            decay_col = jnp.sum(jnp.where(eye256, decay_row, 0.0), axis=1, keepdims=True)
            og = p_ref[rc, OFF_OG:OFF_OG + B_VDIM]
            return qb_st, o_intra, jnp.concatenate(kvs, axis=0), decay_col, og * jax.nn.sigmoid(og)

        heads = [chunk_head(c) for c in range(GMLP_BLOCK // CHUNK)]
        for c, (qb_st, o_intra, kv, decay_col, out_gate) in enumerate(heads):
            o_inter = _dot(qb_st, _bf(state))
            state = decay_col * state + kv
            outs = [_rms(o_intra[hh] + o_inter[hh * CHUNK:(hh + 1) * CHUNK]) for hh in range(B_HEADS)]
            on = jnp.concatenate(outs, axis=1) * glag_ref[...]
            y_ref[c * CHUNK:(c + 1) * CHUNK, A_DIM:A_DIM + B_VDIM] = _bf(on * out_gate)
            start_row_copies()

        zz = p_ref[rows, OFF_CC:OFF_CC + C_DIM] * p_ref[rows, OFF_CX:OFF_CX + C_DIM]
        z1 = jnp.where(row128 == 0, zprev[7:8], pltpu.roll(zz, 1, 0))
        z2 = jnp.where(row128 == 0, zprev[6:7], jnp.where(row128 == 1, zprev[7:8], pltpu.roll(zz, 2, 0)))
        cw = cw_ref[...]
        yc = cb_ref[...] + cw[0:1] * z2
        yc = yc + cw[1:2] * z1
        yc = yc + cw[2:3] * zz
        y_ref[:, A_DIM + B_VDIM:A_DIM + B_VDIM + C_DIM] = _bf(p_ref[rows, OFF_CB:OFF_CB + C_DIM] * yc)
        start_row_copies()
        return state, zz[GMLP_BLOCK - SUBLANES:GMLP_BLOCK]

    state, zprev, carry = s_ref[...], zc_ref[...], carry_ref[...]
    project(0)
    for j in range(n_blk):
        if j == 0:
            project(1)
        state, zprev = block(j, state, zprev)
        rows = slice(j * GMLP_BLOCK, (j + 1) * GMLP_BLOCK)
        xo = x_in[rows, :] + gt1 * _dot(y_blocks[j][...], wout_ref[...])
        o_ref[rows, :] = xo
        start_row_copies()
        h2 = (_rms(xo) * g2_ref[...]) * (1.0 + sc2) + sh2
        h2_ref[slot, rows, :] = _pack_rows(h2)
        d0, d1, wcol, carry = _route(h2, rwt_ref, rb_ref, carry, cap)
        dvm_ref[0:1, rows] = d0
        dvm_ref[1:2, rows] = d1
        w_ref[rows, :] = wcol
        start_row_copies()
    while row_copies:
        row_copies.pop()()
    s_ref[...] = state
    zc_ref[...] = zprev
    carry_ref[...] = carry
    cnt = jnp.broadcast_to(carry, cnt_ref.shape).astype(jnp.int32)
    cnt_ref[...] = cnt
    dest_ref[...] = dvm_ref[0:TOP_K, :]

    drain_rows(h2_prev, xs_ref, sem)

    to_smem = [pltpu.make_async_copy(dvm_ref.at[0], d0s_ref, sem_s),
               pltpu.make_async_copy(dvm_ref.at[1], d1s_ref, sem_s)]
    for cp in to_smem:
        cp.start()

    @pl.when(last_step)
    def _():
        for cp in to_smem:
            cp.wait()
        h2_last = h2_ref.at[slot]

        def issue(g, c):
            for jj in range(SUBLANES):
                i = pl.multiple_of(g * SUBLANES, SUBLANES) + jj
                _row_copy(h2_last, i, xs_ref, d0s_ref[i], sem).start()
                _row_copy(h2_last, i, xs_ref, d1s_ref[i], sem).start()
            return c

        lax.fori_loop(0, tm // SUBLANES, issue, 0)
        drain_rows(h2_last, xs_ref, sem)
        if fuse_prev:
            drain_rows(ysp_ref, ybuf.at[0, 0], sem_g.at[1 - slot], _tile_copy)
        cvm_ref[...] = cnt
        zero_ref[...] = jnp.zeros_like(zero_ref)
        cp = pltpu.make_async_copy(cvm_ref, csm_ref, sem_s)
        cp.start()
        cp.wait()
        _zero_segment_padding(csm_ref, cap, zero_ref, xs_ref, sem_z)


def _mixer(x, mod, g1, g2, w_in, w_out, ln_g, ln_b, w_s, b_s, w_gate, b_gate, gla_g, conv_w, conv_b,
           router_w, router_b, prev=None):
    bn, t, d = x.shape
    n = bn * t
    tm = min(MIX_ROWS, t)
    nt = t // tm
    cap = n
    glr0 = 2 * A_DIM + 2 * B_KDIM + B_VDIM
    w_in_p = _bf(jnp.concatenate(
        [w_in[:, :glr0], w_in[:, glr0 + GLA_RANK:], w_in[:, glr0:glr0 + GLA_RANK],
         jnp.zeros((d, LANES - GLA_RANK), w_in.dtype)], axis=1))
    w_gate_p = _bf(jnp.concatenate(
        [w_gate, jnp.zeros((LANES - GLA_RANK, B_KDIM), w_gate.dtype)], axis=0))
    bsm = jnp.repeat(b_s.T, A_DIM // A_HEADS, axis=1)

    def whole(shape):
        return pl.BlockSpec(shape, lambda b, i: (0,) * len(shape))

    prev_specs, prev_args = [], ()
    if prev is not None:
        n_tiles = bn * nt

        def this_tile(b, i):
            return (b * nt + i,)

        def next_tile(b, i):
            return (jnp.minimum(b * nt + i + 1, n_tiles - 1),)

        d0p, d1p, wcolp, modp, ysp = prev
        prev_specs = [
            pl.BlockSpec((tm,), this_tile, memory_space=pltpu.SMEM),
            pl.BlockSpec((tm,), this_tile, memory_space=pltpu.SMEM),
            pl.BlockSpec((tm,), next_tile, memory_space=pltpu.SMEM),
            pl.BlockSpec((tm,), next_tile, memory_space=pltpu.SMEM),
            pl.BlockSpec((tm, 8), lambda b, i: (b * nt + i, 0)),
            pl.BlockSpec((None, 6, d), lambda b, i: (b, 0, 0)),
            pl.BlockSpec(memory_space=pl.ANY),
        ]
        prev_args = (d0p, d1p, d0p, d1p, wcolp, modp, ysp)

    return pl.pallas_call(
        functools.partial(_mixer_kernel, cap=cap, fuse_prev=prev is not None),
        grid=(bn, nt),
        in_specs=[
            pl.BlockSpec((None, tm, d), lambda b, i: (b, i, 0)),
            pl.BlockSpec((None, 6, d), lambda b, i: (b, 0, 0)),
            whole((1, d)),
            whole((d, P_PAD)),
            whole((d, d)),
            whole((1, A_DIM)),
            whole((1, A_DIM)),
            whole((A_HEADS, GMLP_BLOCK, GMLP_BLOCK)),
            whole((GMLP_BLOCK, A_DIM)),
            whole((LANES, B_KDIM)),
            whole((1, B_KDIM)),
            whole((1, B_VDIM)),
            whole((3, C_DIM)),
            whole((1, C_DIM)),
            whole((1, d)),
            whole((N_EXPERTS, d)),
            whole((N_EXPERTS, 1)),
        ] + prev_specs,
        out_specs=[
            pl.BlockSpec((None, tm, d), lambda b, i: (b, i, 0)),
            pl.BlockSpec((TOP_K, tm), lambda b, i: (0, b * nt + i)),
            pl.BlockSpec((tm, 8), lambda b, i: (b * nt + i, 0)),
            whole((N_EXPERTS, LANES)),
            pl.BlockSpec(memory_space=pl.ANY),
        ],
        out_shape=[
            jax.ShapeDtypeStruct(x.shape, x.dtype),
            jax.ShapeDtypeStruct((TOP_K, n), jnp.int32),
            jax.ShapeDtypeStruct((n, 8), jnp.float32),
            jax.ShapeDtypeStruct((N_EXPERTS, LANES), jnp.int32),
            jax.ShapeDtypeStruct((N_EXPERTS * cap + TOP_K * tm, d // 2), jnp.uint32),
        ],
        scratch_shapes=[
            pltpu.VMEM((tm, d), jnp.bfloat16),
            pltpu.VMEM((tm // 2, P_PAD), jnp.float32),
            pltpu.VMEM((tm // 2, P_PAD), jnp.float32),
            pltpu.VMEM((GMLP_BLOCK, d), jnp.bfloat16),
            pltpu.VMEM((GMLP_BLOCK, d), jnp.bfloat16),
            pltpu.VMEM((GMLP_BLOCK, d), jnp.bfloat16),
            pltpu.VMEM((GMLP_BLOCK, d), jnp.bfloat16),
            pltpu.VMEM((B_KDIM, B_DV), jnp.float32),
            pltpu.VMEM((SUBLANES, C_DIM), jnp.float32),
            pltpu.VMEM((A_HEADS, GMLP_BLOCK, GMLP_BLOCK), jnp.bfloat16),
            pltpu.VMEM((2, tm, d // 2), jnp.uint32),
            pltpu.VMEM((N_EXPERTS, 1), jnp.float32),
            pltpu.VMEM((SUBLANES, tm), jnp.int32),
            pltpu.SMEM((tm,), jnp.int32),
            pltpu.SMEM((tm,), jnp.int32),
            pltpu.VMEM((N_EXPERTS, LANES), jnp.int32),
            pltpu.SMEM((N_EXPERTS, LANES), jnp.int32),
            pltpu.VMEM((FFN_ROWS // 2, d // 2), jnp.uint32),
            pltpu.SemaphoreType.DMA,
            pltpu.SemaphoreType.DMA,
            pltpu.SemaphoreType.DMA,
            pltpu.VMEM((2, TOP_K, tm * ROW_GROUP, LANES), jnp.uint32),
            pltpu.VMEM((tm, d), jnp.float32),
            pltpu.SemaphoreType.DMA((2,)),
        ],
        compiler_params=pltpu.CompilerParams(
            dimension_semantics=("arbitrary", "arbitrary"), vmem_limit_bytes=VMEM_LIMIT),
        name="mixer",
    )(x, mod, g1.reshape(1, d), w_in_p, _bf(w_out), ln_g.reshape(1, -1), ln_b.reshape(1, -1), w_s, bsm,
      w_gate_p, b_gate.reshape(1, -1), gla_g.reshape(1, -1), conv_w, conv_b.reshape(1, -1),
      g2.reshape(1, d), router_w.T, router_b.reshape(N_EXPERTS, 1), *prev_args)


def _ffn_kernel(be_ref, rb_ref, nb_ref, xs_ref, w1_ref, w3_ref, w2_ref, ys_ref, w1b_ref, w3b_ref, w2b_ref):
    del rb_ref
    i = pl.program_id(0)
    used = i < nb_ref[0]
    new_expert = jnp.logical_or(i == 0, be_ref[i] != be_ref[jnp.maximum(i - 1, 0)])

    @pl.when(jnp.logical_and(used, new_expert))
    def _():
        w1b_ref[...] = _bf(w1_ref[...])
        w3b_ref[...] = _bf(w3_ref[...])
        w2b_ref[...] = _bf(w2_ref[...])

    @pl.when(used)
    def _():
        x_a, x_b = _unpack_rows(xs_ref[...])
        half = x_a.shape[1]
        a = _dot(x_a, w1b_ref[0:half, :]) + _dot(x_b, w1b_ref[half:, :])
        g = _dot(x_a, w3b_ref[0:half, :]) + _dot(x_b, w3b_ref[half:, :])
        hm = _bf((a * jax.nn.sigmoid(a)) * g)
        _store_rows_as_tiles(ys_ref, _pack_rows(_dot(hm, w2b_ref[...])))


def _expert_ffn(layer, block_e, row_block, n_used, xs, w1, w3, w2):
    n_rows = xs.shape[0]
    n_blocks = block_e.shape[0]
    d, f = w1.shape[-2], w1.shape[-1]
    return pl.pallas_call(
        _ffn_kernel,
        grid_spec=pltpu.PrefetchScalarGridSpec(
            num_scalar_prefetch=3,
            grid=(n_blocks,),
            in_specs=[
                pl.BlockSpec((FFN_ROWS, d // 2), lambda i, be, rb, nb: (rb[i], 0)),
                pl.BlockSpec((None, None, d, f), lambda i, be, rb, nb: (layer, be[i], 0, 0)),
                pl.BlockSpec((None, None, d, f), lambda i, be, rb, nb: (layer, be[i], 0, 0)),
                pl.BlockSpec((None, None, f, d), lambda i, be, rb, nb: (layer, be[i], 0, 0)),
            ],
            out_specs=pl.BlockSpec((FFN_ROWS * ROW_GROUP, LANES), lambda i, be, rb, nb: (rb[i], 0)),
            scratch_shapes=[pltpu.VMEM((d, f), jnp.bfloat16), pltpu.VMEM((d, f), jnp.bfloat16),
                            pltpu.VMEM((f, d), jnp.bfloat16)],
        ),
        out_shape=jax.ShapeDtypeStruct((n_rows * ROW_GROUP, LANES), jnp.uint32),
        compiler_params=pltpu.CompilerParams(
            dimension_semantics=("arbitrary",), vmem_limit_bytes=VMEM_LIMIT),
        name="expert_ffn",
    )(block_e, row_block, n_used, xs, w1, w3, w2)


def _combine_kernel(d0_ref, d1_ref, d0n_ref, d1n_ref, x_ref, mod_ref, w_ref, gf_ref, ys_ref, o_ref, ybuf, sems,
                    *, final_norm):
    tm = x_ref.shape[0]
    step = pl.program_id(0) * pl.num_programs(1) + pl.program_id(1)
    last = step == pl.num_programs(0) * pl.num_programs(1) - 1
    slot = step % 2

    def drain(sem):
        def body(g, c):
            for _ in range(TOP_K * SUBLANES):
                _tile_copy(ys_ref, 0, ybuf.at[0, 0], 0, sem).wait()
            return c
        lax.fori_loop(0, tm // SUBLANES, body, 0)

    @pl.when(step == 0)
    def _():
        def issue(g, c):
            for jj in range(SUBLANES):
                i = pl.multiple_of(g * SUBLANES, SUBLANES) + jj
                _tile_copy(ys_ref, d0_ref[i], ybuf.at[0, 0], i * ROW_GROUP, sems.at[0]).start()
                _tile_copy(ys_ref, d1_ref[i], ybuf.at[0, 1], i * ROW_GROUP, sems.at[0]).start()
            return c
        lax.fori_loop(0, tm // SUBLANES, issue, 0)

    nxt, sem_nxt = ybuf.at[1 - slot], sems.at[1 - slot]
    for i in range(tm):
        _tile_copy(ys_ref, d0n_ref[i], nxt.at[0], i * ROW_GROUP, sem_nxt).start(priority=0)
        _tile_copy(ys_ref, d1n_ref[i], nxt.at[1], i * ROW_GROUP, sem_nxt).start(priority=1)

    drain(sems.at[slot])
    w = w_ref[...]
    gt2 = mod_ref[...][5:6]
    out = x_ref[...] + gt2 * (w[:, 0:1] * _load_rows_from_tiles(ybuf, (slot, 0), tm)
                              + w[:, 1:2] * _load_rows_from_tiles(ybuf, (slot, 1), tm))
    if final_norm:
        out = _rms(out) * gf_ref[...]
    o_ref[...] = out

    @pl.when(last)
    def _():
        drain(sems.at[1 - slot])


def _combine(d0, d1, x, mod, wcol, g_final, ys, final_norm):
    bn, t, d = x.shape
    tm = min(MOVE_ROWS, t)
    nt = t // tm
    n_tiles = bn * nt

    def this_tile(b, i):
        return (b * nt + i,)

    def next_tile(b, i):
        return (jnp.minimum(b * nt + i + 1, n_tiles - 1),)

    return pl.pallas_call(
        functools.partial(_combine_kernel, final_norm=final_norm),
        grid=(bn, nt),
        in_specs=[
            pl.BlockSpec((tm,), this_tile, memory_space=pltpu.SMEM),
            pl.BlockSpec((tm,), this_tile, memory_space=pltpu.SMEM),
            pl.BlockSpec((tm,), next_tile, memory_space=pltpu.SMEM),
            pl.BlockSpec((tm,), next_tile, memory_space=pltpu.SMEM),
            pl.BlockSpec((None, tm, d), lambda b, i: (b, i, 0)),
            pl.BlockSpec((None, 6, d), lambda b, i: (b, 0, 0)),
            pl.BlockSpec((tm, 8), lambda b, i: (b * nt + i, 0)),
            pl.BlockSpec((1, d), lambda b, i: (0, 0)),
            pl.BlockSpec(memory_space=pl.ANY),
        ],
        out_specs=pl.BlockSpec((None, tm, d), lambda b, i: (b, i, 0)),
        out_shape=jax.ShapeDtypeStruct(x.shape, x.dtype),
        scratch_shapes=[pltpu.VMEM((2, TOP_K, tm * ROW_GROUP, LANES), jnp.uint32), pltpu.SemaphoreType.DMA((2,))],
        compiler_params=pltpu.CompilerParams(
            dimension_semantics=("arbitrary", "arbitrary"), vmem_limit_bytes=VMEM_LIMIT),
        name="combine",
    )(d0, d1, d0, d1, x, mod, wcol, g_final.reshape(1, d), ys)


def _block_map(counts, cap, n_blocks):
    blocks_per_expert = (counts + FFN_ROWS - 1) // FFN_ROWS
    ends = jnp.cumsum(blocks_per_expert)
    n_used = ends[-1:]
    step = jnp.minimum(jnp.arange(n_blocks, dtype=jnp.int32), n_used - 1)
    block_e = jnp.minimum(jnp.sum((ends[None, :] <= step[:, None]).astype(jnp.int32), axis=1), N_EXPERTS - 1)
    first = (ends - blocks_per_expert)[block_e]
    row_block = block_e * (cap // FFN_ROWS) + (step - first)
    return block_e.astype(jnp.int32), row_block.astype(jnp.int32), n_used.astype(jnp.int32)


def kernel(x, c, w_ada, b_ada, g_norm1, g_norm2, w_in, w_out, gmlp_ln_g, gmlp_ln_b, gmlp_ws, gmlp_bs,
           gla_w_gate, gla_b_gate, gla_norm_g, conv_w, conv_b, router_w, router_b, exp_w1, exp_w3, exp_w2,
           g_final):
    depth = w_ada.shape[0]
    bn, t, d = x.shape
    n = bn * t
    n_blocks = -(-(n * TOP_K) // FFN_ROWS) + N_EXPERTS
    mod_all = _modulation(c, w_ada, b_ada).reshape(depth, bn, 6, d)
    prev = None
    for l in range(depth):
        mod = mod_all[l]
        x, dest, wcol, cnt, xs = _mixer(
            x, mod, g_norm1[l], g_norm2[l], w_in[l], w_out[l], gmlp_ln_g[l], gmlp_ln_b[l], gmlp_ws[l],
            gmlp_bs[l], gla_w_gate[l], gla_b_gate[l], gla_norm_g[l], conv_w[l], conv_b[l], router_w, router_b,
            prev=prev)
        block_e, row_block, n_used = _block_map(cnt[:, 0], n, n_blocks)
        ys = _expert_ffn(l, block_e, row_block, n_used, xs, exp_w1, exp_w3, exp_w2)
        prev = (dest[0] * ROW_GROUP, dest[1] * ROW_GROUP, wcol, mod, ys)
    d0, d1, wcol, mod, ys = prev
    return _combine(d0, d1, x, mod, wcol, g_final, ys, final_norm=True)
```

```python
import functools

import jax
import jax.numpy as jnp
from jax import lax
from jax.experimental import pallas as pl
from jax.experimental.pallas import tpu as pltpu

CHUNK = 64
GMLP_BLOCK = 128
A_DIM = 256
A_HEADS = 4
B_HEADS = 4
B_DK = 64
B_DV = 128
B_KDIM = B_HEADS * B_DK
B_VDIM = B_HEADS * B_DV
GLA_RANK = 16
GLA_TAU = 16.0
C_DIM = 256
N_EXPERTS = 32
N_GROUPS = 4
EXPERTS_PER_GROUP = N_EXPERTS // N_GROUPS
TOP_K = 2
EPS = 1e-6

LANES = 128
SUBLANES = 8
OFF_AU, OFF_AV, OFF_Q, OFF_K, OFF_V = 0, 256, 512, 768, 1024
OFF_OG, OFF_CB, OFF_CC, OFF_CX, OFF_GLR = 1536, 2048, 2304, 2560, 2816
P_PAD = OFF_GLR + LANES

MIX_ROWS = 512
COPY_POINTS_PER_BLOCK = 18
MOVE_ROWS = 512
FFN_ROWS = 512
VMEM_LIMIT = 56 * 1024 * 1024

_NT = (((1,), (1,)), ((), ()))
_TN = (((0,), (0,)), ((), ()))


def _dot(a, b, dims=None):
    if dims is None:
        return jnp.dot(a, b, preferred_element_type=jnp.float32)
    return lax.dot_general(a, b, dims, preferred_element_type=jnp.float32)


def _bf(x):
    return x.astype(jnp.bfloat16)


def _split_bf16(x):
    hi = _bf(x)
    lo = _bf(x - hi.astype(jnp.float32))
    return hi, lo


def _rms(x):
    return x * lax.rsqrt(jnp.mean(x * x, axis=-1, keepdims=True) + EPS)


def _pack_rows(v):
    half = v.shape[1] // 2
    hi = pltpu.bitcast(_bf(v[:, :half]).astype(jnp.float32), jnp.uint32)
    lo = pltpu.bitcast(_bf(v[:, half:]).astype(jnp.float32), jnp.uint32)
    return hi | (lo >> 16)


def _unpack_rows(w):
    hi = pltpu.bitcast(w & jnp.uint32(0xFFFF0000), jnp.float32)
    lo = pltpu.bitcast(w << 16, jnp.float32)
    return _bf(hi), _bf(lo)


def _row_copy(src_ref, src_row, dst_ref, dst_row, sem):
    return pltpu.make_async_copy(src_ref.at[pl.ds(src_row, 1)], dst_ref.at[pl.ds(dst_row, 1)], sem)


ROW_GROUP = 4

def _aligned(row):
    return row if isinstance(row, int) else pl.multiple_of(row, ROW_GROUP)


def _tile_copy(src_ref, src_row, dst_ref, dst_row, sem):
    return pltpu.make_async_copy(src_ref.at[pl.ds(_aligned(src_row), ROW_GROUP)],
                                 dst_ref.at[pl.ds(_aligned(dst_row), ROW_GROUP)], sem)


def _store_rows_as_tiles(ref, value):
    rows = value.shape[0]
    assert value.shape[1] == ROW_GROUP * LANES
    for c in range(ROW_GROUP):
        ref[pl.ds(c, rows, stride=ROW_GROUP), :] = value[:, c * LANES:(c + 1) * LANES]


def _load_packed_rows(ref, idx, rows):
    return jnp.concatenate(
        [ref[idx + (pl.ds(c, rows, stride=ROW_GROUP), slice(None))] for c in range(ROW_GROUP)], axis=1)


def _load_rows_from_tiles(ref, idx, rows):
    hi, lo = _unpack_rows(_load_packed_rows(ref, idx, rows))
    return jnp.concatenate([hi.astype(jnp.float32), lo.astype(jnp.float32)], axis=1)


def _mod_kernel(c_ref, w_ref, b_ref, o_ref):
    c = c_ref[...]
    s = c * jax.nn.sigmoid(c)
    s_hi, s_lo = _split_bf16(s)
    w_hi, w_lo = _split_bf16(w_ref[...])
    acc = _dot(s_hi, w_hi) + _dot(s_hi, w_lo) + _dot(s_lo, w_hi)
    o_ref[...] = acc + b_ref[...]


def _modulation(c, w_ada, b_ada):
    depth, d, six_d = w_ada.shape
    bn = c.shape[0]
    cb = 1024
    return pl.pallas_call(
        _mod_kernel,
        grid=(depth, six_d // cb),
        in_specs=[
            pl.BlockSpec((bn, d), lambda l, j: (0, 0)),
            pl.BlockSpec((None, d, cb), lambda l, j: (l, 0, j)),
            pl.BlockSpec((None, 1, cb), lambda l, j: (l, 0, j)),
        ],
        out_specs=pl.BlockSpec((None, bn, cb), lambda l, j: (l, 0, j)),
        out_shape=jax.ShapeDtypeStruct((depth, bn, six_d), jnp.float32),
        compiler_params=pltpu.CompilerParams(
            dimension_semantics=("arbitrary", "arbitrary"), vmem_limit_bytes=VMEM_LIMIT),
        name="adaln_mod",
    )(c, w_ada, b_ada.reshape(depth, 1, six_d))


def _route(h2, rwt_ref, rb_ref, carry, cap):
    tr = h2.shape[0]
    h_hi, h_lo = _split_bf16(h2)
    w_hi, w_lo = _split_bf16(rwt_ref[...])
    logits = (_dot(w_hi, h_hi, _NT) + _dot(w_hi, h_lo, _NT) + _dot(w_lo, h_hi, _NT)) + rb_ref[...]
    ex = jnp.exp(logits - jnp.max(logits, axis=0, keepdims=True))
    probs = ex / jnp.sum(ex, axis=0, keepdims=True)

    idx8 = lax.broadcasted_iota(jnp.int32, (EXPERTS_PER_GROUP, tr), 0)
    best = None
    for g in range(N_GROUPS):
        pg = probs[g * EXPERTS_PER_GROUP:(g + 1) * EXPERTS_PER_GROUP]
        m1 = jnp.max(pg, axis=0, keepdims=True)
        i1 = jnp.min(jnp.where(pg == m1, idx8, EXPERTS_PER_GROUP), axis=0, keepdims=True)
        pg2 = jnp.where(idx8 == i1, -1.0, pg)
        m2 = jnp.max(pg2, axis=0, keepdims=True)
        i2 = jnp.min(jnp.where(pg2 == m2, idx8, EXPERTS_PER_GROUP), axis=0, keepdims=True)
        cand = (m1 + m2, m1, m2, i1 + g * EXPERTS_PER_GROUP, i2 + g * EXPERTS_PER_GROUP)
        if best is None:
            best = cand
        else:
            better = cand[0] > best[0]
            best = tuple(jnp.where(better, a, b) for a, b in zip(cand, best))
    _, p1, p2, e0, e1 = best
    denom = p1 + p2
    w0, w1 = p1 / denom, p2 / denom

    eidx = lax.broadcasted_iota(jnp.int32, (N_EXPERTS, tr), 0)
    hit0, hit1 = eidx == e0, eidx == e1
    onehot = jnp.where(jnp.logical_or(hit0, hit1), 1.0, 0.0)
    before = (lax.broadcasted_iota(jnp.int32, (tr, tr), 0) < lax.broadcasted_iota(jnp.int32, (tr, tr), 1))
    rank = _dot(_bf(onehot), _bf(jnp.where(before, 1.0, 0.0))) + carry
    r0 = jnp.sum(jnp.where(hit0, rank, 0.0), axis=0, keepdims=True).astype(jnp.int32)
    r1 = jnp.sum(jnp.where(hit1, rank, 0.0), axis=0, keepdims=True).astype(jnp.int32)
    carry = carry + jnp.sum(onehot, axis=1, keepdims=True)

    eye = lax.broadcasted_iota(jnp.int32, (tr, tr), 0) == lax.broadcasted_iota(jnp.int32, (tr, tr), 1)
    w0c = jnp.sum(jnp.where(eye, jnp.broadcast_to(w0, (tr, tr)), 0.0), axis=1, keepdims=True)
    w1c = jnp.sum(jnp.where(eye, jnp.broadcast_to(w1, (tr, tr)), 0.0), axis=1, keepdims=True)
    lane8 = lax.broadcasted_iota(jnp.int32, (tr, 8), 1)
    wcol = jnp.where(lane8 == 0, w0c, jnp.where(lane8 == 1, w1c, 0.0))
    return (e0 * cap + r0) * ROW_GROUP, (e1 * cap + r1) * ROW_GROUP, wcol, carry


def _zero_segment_padding(cnt_ref, cap, zero_ref, xs_ref, sem):
    copies = []
    for ex in range(N_EXPERTS):
        cnt = cnt_ref[ex, 0]
        seg = ex * cap
        pos = seg + cnt
        end8 = seg + ((cnt + SUBLANES - 1) // SUBLANES) * SUBLANES
        for j in range(SUBLANES - 1):
            copies.append((pos + j < end8, _tile_copy(zero_ref, 0, xs_ref, (pos + j) * ROW_GROUP, sem)))
        seg_end = seg + ((cnt + FFN_ROWS - 1) // FFN_ROWS) * FFN_ROWS
        groups = (seg_end - end8) // SUBLANES
        at = end8
        bit = FFN_ROWS // (2 * SUBLANES)
        while bit >= 1:
            rows = bit * SUBLANES
            pred = (groups & bit) != 0
            copies.append((pred, pltpu.make_async_copy(
                zero_ref.at[pl.ds(0, rows * ROW_GROUP)],
                xs_ref.at[pl.ds(pl.multiple_of(at * ROW_GROUP, SUBLANES), rows * ROW_GROUP)], sem)))
            at = at + jnp.where(pred, rows, 0)
            bit //= 2
    for pred, cp in copies:
        pl.when(pred)(cp.start)
    for pred, cp in copies:
        pl.when(pred)(cp.wait)


N_MIXER_INPUTS = 17
N_MIXER_OUTPUTS = 5


def _mixer_kernel(*refs, cap, fuse_prev):
    ins, rest = refs[:N_MIXER_INPUTS], refs[N_MIXER_INPUTS:]
    if fuse_prev:
        dp0_ref, dp1_ref, dp0n_ref, dp1n_ref, wp_ref, modp_ref, ysp_ref = rest[:7]
        rest = rest[7:]
    outs, scratch = rest[:N_MIXER_OUTPUTS], rest[N_MIXER_OUTPUTS:]
    (x_ref, mod_ref, g1_ref, win_ref, wout_ref, lng_ref, lnb_ref, ws_ref, bsm_ref,
     wg_ref, bg_ref, glag_ref, cw_ref, cb_ref, g2_ref, rwt_ref, rb_ref) = ins
    o_ref, dest_ref, w_ref, cnt_ref, xs_ref = outs
    (h_ref, pa_ref, pb_ref, y0_ref, y1_ref, y2_ref, y3_ref, s_ref, zc_ref, wsm_ref, h2_ref,
     carry_ref, dvm_ref, d0s_ref, d1s_ref, cvm_ref, csm_ref, zero_ref, sem, sem_s, sem_z,
     ybuf, xin_ref, sem_g) = scratch

    bi, ti = pl.program_id(0), pl.program_id(1)
    tm = x_ref.shape[0]
    n_blk = tm // GMLP_BLOCK
    step = bi * pl.num_programs(1) + ti
    last_step = step == pl.num_programs(0) * pl.num_programs(1) - 1
    slot = step % 2
    spare_row0 = N_EXPERTS * cap

    def drain_rows(src_ref, dst_ref, s, copy=_row_copy):
        def body(g, c):
            for _ in range(TOP_K * SUBLANES):
                copy(src_ref, 0, dst_ref, 0, s).wait()
            return c
        lax.fori_loop(0, tm // SUBLANES, body, 0)

    @pl.when(step > 0)
    def _():
        pltpu.make_async_copy(dvm_ref.at[0], d0s_ref, sem_s).wait()
        pltpu.make_async_copy(dvm_ref.at[1], d1s_ref, sem_s).wait()

    if fuse_prev:
        @pl.when(step == 0)
        def _():
            def issue(g, c):
                for jj in range(SUBLANES):
                    i = pl.multiple_of(g * SUBLANES, SUBLANES) + jj
                    _tile_copy(ysp_ref, dp0_ref[i], ybuf.at[0, 0], i * ROW_GROUP, sem_g.at[0]).start()
                    _tile_copy(ysp_ref, dp1_ref[i], ybuf.at[0, 1], i * ROW_GROUP, sem_g.at[0]).start()
                return c
            lax.fori_loop(0, tm // SUBLANES, issue, 0)

        drain_rows(ysp_ref, ybuf.at[0, 0], sem_g.at[slot], _tile_copy)
        wp = wp_ref[...]
        xin_ref[...] = x_ref[...] + modp_ref[...][5:6] * (
            wp[:, 0:1] * _load_rows_from_tiles(ybuf, (slot, 0), tm)
            + wp[:, 1:2] * _load_rows_from_tiles(ybuf, (slot, 1), tm))
        x_in = xin_ref
    else:
        x_in = x_ref

    @pl.when(step == 0)
    def _():
        carry_ref[...] = jnp.zeros_like(carry_ref)
        h2_ref[...] = jnp.zeros_like(h2_ref)

        def fill(i, c):
            d0s_ref[i] = (spare_row0 + i) * ROW_GROUP
            d1s_ref[i] = (spare_row0 + tm + i) * ROW_GROUP
            return c

        lax.fori_loop(0, tm, fill, 0)
        tt = lax.broadcasted_iota(jnp.int32, (GMLP_BLOCK, GMLP_BLOCK), 0) // CHUNK
        ss = lax.broadcasted_iota(jnp.int32, (GMLP_BLOCK, GMLP_BLOCK), 1) // CHUNK
        for h in range(A_HEADS):
            wsm_ref[h] = _bf(jnp.where(tt >= ss, ws_ref[h], 0.0))

    @pl.when(ti == 0)
    def _():
        s_ref[...] = jnp.zeros_like(s_ref)
        zc_ref[...] = jnp.zeros_like(zc_ref)

    m = mod_ref[...]
    sh1, sc1, gt1, sh2, sc2 = m[0:1], m[1:2], m[2:3], m[3:4], m[4:5]
    h_ref[...] = _bf((_rms(x_in[...]) * g1_ref[...]) * (1.0 + sc1) + sh1)

    lane256 = lax.broadcasted_iota(jnp.int32, (CHUNK, B_KDIM), 1)
    head_of_lane = lane256 // B_DK
    r64 = lax.broadcasted_iota(jnp.int32, (CHUNK, CHUNK), 0)
    c64 = lax.broadcasted_iota(jnp.int32, (CHUNK, CHUNK), 1)
    tri = _bf(jnp.where(r64 >= c64, 1.0, 0.0))
    causal4 = jnp.concatenate([r64 >= c64] * B_HEADS, axis=0)
    eye256 = (lax.broadcasted_iota(jnp.int32, (B_KDIM, B_KDIM), 0)
              == lax.broadcasted_iota(jnp.int32, (B_KDIM, B_KDIM), 1))
    a_head_of_lane = lax.broadcasted_iota(jnp.int32, (GMLP_BLOCK, A_DIM), 1) // (A_DIM // A_HEADS)
    row128 = lax.broadcasted_iota(jnp.int32, (GMLP_BLOCK, C_DIM), 0)
    sqrt_half = 0.7071067811865476

    def gelu(v):
        return 0.5 * v * (1.0 + lax.erf(v * sqrt_half))

    h2_prev = h2_ref.at[1 - slot]
    row_copies = []
    for i in range(tm):
        row_copies.append(
            lambda i=i: _tile_copy(h2_prev, i * ROW_GROUP, xs_ref, d0s_ref[i], sem).start(priority=0))
        row_copies.append(
            lambda i=i: _tile_copy(h2_prev, i * ROW_GROUP, xs_ref, d1s_ref[i], sem).start(priority=1))
        if fuse_prev:
            row_copies.append(lambda i=i: _tile_copy(
                ysp_ref, dp0n_ref[i], ybuf.at[1 - slot, 0], i * ROW_GROUP, sem_g.at[1 - slot]).start(priority=0))
            row_copies.append(lambda i=i: _tile_copy(
                ysp_ref, dp1n_ref[i], ybuf.at[1 - slot, 1], i * ROW_GROUP, sem_g.at[1 - slot]).start(priority=1))
    row_copies.reverse()
    copies_per_point = -(-len(row_copies) // (n_blk * COPY_POINTS_PER_BLOCK))

    def start_row_copies():
        for _ in range(min(copies_per_point, len(row_copies))):
            row_copies.pop()()

    p_halves = (pa_ref, pb_ref)
    y_blocks = (y0_ref, y1_ref, y2_ref, y3_ref)
    assert n_blk == len(y_blocks)

    def project(half):
        rows = slice(half * (tm // 2), (half + 1) * (tm // 2))
        p_halves[half][...] = _dot(h_ref[rows, :], win_ref[...])

    def block(j, state, zprev):
        p_ref, y_ref = p_halves[j // (n_blk // 2)], y_blocks[j]
        r0 = (j % (n_blk // 2)) * GMLP_BLOCK
        rows = slice(r0, r0 + GMLP_BLOCK)

        u = gelu(p_ref[rows, OFF_AU:OFF_AU + A_DIM])
        vv = gelu(p_ref[rows, OFF_AV:OFF_AV + A_DIM])
        mu = jnp.mean(vv, axis=-1, keepdims=True)
        var = jnp.mean((vv - mu) ** 2, axis=-1, keepdims=True)
        vn = _bf((vv - mu) * lax.rsqrt(var + EPS) * lng_ref[...] + lnb_ref[...])
        sv = jnp.zeros((GMLP_BLOCK, A_DIM), jnp.float32)
        for hh in range(A_HEADS):
            sv = jnp.where(a_head_of_lane == hh, _dot(wsm_ref[hh], vn), sv)
        y_ref[:, 0:A_DIM] = _bf(u * (sv + bsm_ref[...]))
        start_row_copies()

        z = _dot(_bf(p_ref[rows, OFF_GLR:OFF_GLR + LANES]), wg_ref[...]) + bg_ref[...]
        glog = (jnp.minimum(z, 0.0) - jnp.log1p(jnp.exp(-jnp.abs(z)))) / GLA_TAU
        def chunk_head(c):
            rc = slice(r0 + c * CHUNK, r0 + (c + 1) * CHUNK)
            g = glog[c * CHUNK:(c + 1) * CHUNK]
            g_hi = _bf(g)
            g_r1 = g - g_hi.astype(jnp.float32)
            g_mid = _bf(g_r1)
            g_lo = _bf(g_r1 - g_mid.astype(jnp.float32))
            b = _dot(tri, g_hi) + _dot(tri, g_mid) + _dot(tri, g_lo)
            b_mid = b[CHUNK // 2:CHUNK // 2 + 1]
            b_last = b[CHUNK - 1:CHUNK]
            q = p_ref[rc, OFF_Q:OFF_Q + B_KDIM] * (B_DK ** -0.5)
            k = p_ref[rc, OFF_K:OFF_K + B_KDIM]
            vb = _bf(p_ref[rc, OFF_V:OFF_V + B_VDIM])
            qs = q * jnp.exp(b - b_mid)
            ks = _bf(k * jnp.exp(b_mid - b))
            kd = _bf(k * jnp.exp(b_last - b))
            qb = q * jnp.exp(b)
            qs_st = _bf(jnp.concatenate(
                [jnp.where(head_of_lane == hh, qs, 0.0) for hh in range(B_HEADS)], axis=0))
            qb_st = _bf(jnp.concatenate(
                [jnp.where(head_of_lane == hh, qb, 0.0) for hh in range(B_HEADS)], axis=0))
            scores = _bf(jnp.where(causal4, _dot(qs_st, ks, _NT), 0.0))
            start_row_copies()
            kv_all = _dot(kd, vb, _TN)
            start_row_copies()
            o_intra, kvs = [], []
            for hh in range(B_HEADS):
                rs = slice(hh * CHUNK, (hh + 1) * CHUNK)
                cs = slice(hh * B_DV, (hh + 1) * B_DV)
                o_intra.append(_dot(scores[rs], vb[:, cs]))
                kvs.append(kv_all[rs, cs])
                start_row_copies()
            decay_row = jnp.broadcast_to(jnp.exp(b_last), (B_KDIM, B_KDIM))
            decay_col = jnp.sum(jnp.where(eye256, decay_row, 0.0), axis=1, keepdims=True)
            og = p_ref[rc, OFF_OG:OFF_OG + B_VDIM]
            return qb_st, o_intra, jnp.concatenate(kvs, axis=0), decay_col, og * jax.nn.sigmoid(og)

        heads = [chunk_head(c) for c in range(GMLP_BLOCK // CHUNK)]
        for c, (qb_st, o_intra, kv, decay_col, out_gate) in enumerate(heads):
            o_inter = _dot(qb_st, _bf(state))
            state = decay_col * state + kv
            outs = [_rms(o_intra[hh] + o_inter[hh * CHUNK:(hh + 1) * CHUNK]) for hh in range(B_HEADS)]
            on = jnp.concatenate(outs, axis=1) * glag_ref[...]
            y_ref[c * CHUNK:(c + 1) * CHUNK, A_DIM:A_DIM + B_VDIM] = _bf(on * out_gate)
            start_row_copies()

        zz = p_ref[rows, OFF_CC:OFF_CC + C_DIM] * p_ref[rows, OFF_CX:OFF_CX + C_DIM]
        z1 = jnp.where(row128 == 0, zprev[7:8], pltpu.roll(zz, 1, 0))
        z2 = jnp.where(row128 == 0, zprev[6:7], jnp.where(row128 == 1, zprev[7:8], pltpu.roll(zz, 2, 0)))
        cw = cw_ref[...]
        yc = cb_ref[...] + cw[0:1] * z2
        yc = yc + cw[1:2] * z1
        yc = yc + cw[2:3] * zz
        y_ref[:, A_DIM + B_VDIM:A_DIM + B_VDIM + C_DIM] = _bf(p_ref[rows, OFF_CB:OFF_CB + C_DIM] * yc)
        start_row_copies()
        return state, zz[GMLP_BLOCK - SUBLANES:GMLP_BLOCK]

    state, zprev, carry = s_ref[...], zc_ref[...], carry_ref[...]
    project(0)
    for j in range(n_blk):
        if j == 0:
            project(1)
        state, zprev = block(j, state, zprev)
        rows = slice(j * GMLP_BLOCK, (j + 1) * GMLP_BLOCK)
        xo = x_in[rows, :] + gt1 * _dot(y_blocks[j][...], wout_ref[...])
        o_ref[rows, :] = xo
        start_row_copies()
        h2 = (_rms(xo) * g2_ref[...]) * (1.0 + sc2) + sh2
        _store_rows_as_tiles(
            h2_ref.at[slot, pl.ds(j * GMLP_BLOCK * ROW_GROUP, GMLP_BLOCK * ROW_GROUP)], _pack_rows(h2))
        d0, d1, wcol, carry = _route(h2, rwt_ref, rb_ref, carry, cap)
        dvm_ref[0:1, rows] = d0
        dvm_ref[1:2, rows] = d1
        w_ref[rows, :] = wcol
        start_row_copies()
    while row_copies:
        row_copies.pop()()
    s_ref[...] = state
    zc_ref[...] = zprev
    carry_ref[...] = carry
    cnt = jnp.broadcast_to(carry, cnt_ref.shape).astype(jnp.int32)
    cnt_ref[...] = cnt
    dest_ref[...] = dvm_ref[0:TOP_K, :]

    drain_rows(h2_prev, xs_ref, sem, _tile_copy)

    to_smem = [pltpu.make_async_copy(dvm_ref.at[0], d0s_ref, sem_s),
               pltpu.make_async_copy(dvm_ref.at[1], d1s_ref, sem_s)]
    for cp in to_smem:
        cp.start()

    @pl.when(last_step)
    def _():
        for cp in to_smem:
            cp.wait()
        h2_last = h2_ref.at[slot]

        def issue(g, c):
            for jj in range(SUBLANES):
                i = pl.multiple_of(g * SUBLANES, SUBLANES) + jj
                _tile_copy(h2_last, i * ROW_GROUP, xs_ref, d0s_ref[i], sem).start()
                _tile_copy(h2_last, i * ROW_GROUP, xs_ref, d1s_ref[i], sem).start()
            return c

        lax.fori_loop(0, tm // SUBLANES, issue, 0)
        drain_rows(h2_last, xs_ref, sem, _tile_copy)
        if fuse_prev:
            drain_rows(ysp_ref, ybuf.at[0, 0], sem_g.at[1 - slot], _tile_copy)
        cvm_ref[...] = cnt
        zero_ref[...] = jnp.zeros_like(zero_ref)
        cp = pltpu.make_async_copy(cvm_ref, csm_ref, sem_s)
        cp.start()
        cp.wait()
        _zero_segment_padding(csm_ref, cap, zero_ref, xs_ref, sem_z)


def _mixer(x, mod, g1, g2, w_in, w_out, ln_g, ln_b, w_s, b_s, w_gate, b_gate, gla_g, conv_w, conv_b,
           router_w, router_b, prev=None):
    bn, t, d = x.shape
    n = bn * t
    tm = min(MIX_ROWS, t)
    nt = t // tm
    cap = n
    glr0 = 2 * A_DIM + 2 * B_KDIM + B_VDIM
    w_in_p = _bf(jnp.concatenate(
        [w_in[:, :glr0], w_in[:, glr0 + GLA_RANK:], w_in[:, glr0:glr0 + GLA_RANK],
         jnp.zeros((d, LANES - GLA_RANK), w_in.dtype)], axis=1))
    w_gate_p = _bf(jnp.concatenate(
        [w_gate, jnp.zeros((LANES - GLA_RANK, B_KDIM), w_gate.dtype)], axis=0))
    bsm = jnp.repeat(b_s.T, A_DIM // A_HEADS, axis=1)

    def whole(shape):
        return pl.BlockSpec(shape, lambda b, i: (0,) * len(shape))

    prev_specs, prev_args = [], ()
    if prev is not None:
        n_tiles = bn * nt

        def this_tile(b, i):
            return (b * nt + i,)

        def next_tile(b, i):
            return (jnp.minimum(b * nt + i + 1, n_tiles - 1),)

        d0p, d1p, wcolp, modp, ysp = prev
        prev_specs = [
            pl.BlockSpec((tm,), this_tile, memory_space=pltpu.SMEM),
            pl.BlockSpec((tm,), this_tile, memory_space=pltpu.SMEM),
            pl.BlockSpec((tm,), next_tile, memory_space=pltpu.SMEM),
            pl.BlockSpec((tm,), next_tile, memory_space=pltpu.SMEM),
            pl.BlockSpec((tm, 8), lambda b, i: (b * nt + i, 0)),
            pl.BlockSpec((None, 6, d), lambda b, i: (b, 0, 0)),
            pl.BlockSpec(memory_space=pl.ANY),
        ]
        prev_args = (d0p, d1p, d0p, d1p, wcolp, modp, ysp)

    return pl.pallas_call(
        functools.partial(_mixer_kernel, cap=cap, fuse_prev=prev is not None),
        grid=(bn, nt),
        in_specs=[
            pl.BlockSpec((None, tm, d), lambda b, i: (b, i, 0)),
            pl.BlockSpec((None, 6, d), lambda b, i: (b, 0, 0)),
            whole((1, d)),
            whole((d, P_PAD)),
            whole((d, d)),
            whole((1, A_DIM)),
            whole((1, A_DIM)),
            whole((A_HEADS, GMLP_BLOCK, GMLP_BLOCK)),
            whole((GMLP_BLOCK, A_DIM)),
            whole((LANES, B_KDIM)),
            whole((1, B_KDIM)),
            whole((1, B_VDIM)),
            whole((3, C_DIM)),
            whole((1, C_DIM)),
            whole((1, d)),
            whole((N_EXPERTS, d)),
            whole((N_EXPERTS, 1)),
        ] + prev_specs,
        out_specs=[
            pl.BlockSpec((None, tm, d), lambda b, i: (b, i, 0)),
            pl.BlockSpec((TOP_K, tm), lambda b, i: (0, b * nt + i)),
            pl.BlockSpec((tm, 8), lambda b, i: (b * nt + i, 0)),
            whole((N_EXPERTS, LANES)),
            pl.BlockSpec(memory_space=pl.ANY),
        ],
        out_shape=[
            jax.ShapeDtypeStruct(x.shape, x.dtype),
            jax.ShapeDtypeStruct((TOP_K, n), jnp.int32),
            jax.ShapeDtypeStruct((n, 8), jnp.float32),
            jax.ShapeDtypeStruct((N_EXPERTS, LANES), jnp.int32),
            jax.ShapeDtypeStruct(((N_EXPERTS * cap + TOP_K * tm) * ROW_GROUP, LANES), jnp.uint32),
        ],
        scratch_shapes=[
            pltpu.VMEM((tm, d), jnp.bfloat16),
            pltpu.VMEM((tm // 2, P_PAD), jnp.float32),
            pltpu.VMEM((tm // 2, P_PAD), jnp.float32),
            pltpu.VMEM((GMLP_BLOCK, d), jnp.bfloat16),
            pltpu.VMEM((GMLP_BLOCK, d), jnp.bfloat16),
            pltpu.VMEM((GMLP_BLOCK, d), jnp.bfloat16),
            pltpu.VMEM((GMLP_BLOCK, d), jnp.bfloat16),
            pltpu.VMEM((B_KDIM, B_DV), jnp.float32),
            pltpu.VMEM((SUBLANES, C_DIM), jnp.float32),
            pltpu.VMEM((A_HEADS, GMLP_BLOCK, GMLP_BLOCK), jnp.bfloat16),
            pltpu.VMEM((2, tm * ROW_GROUP, LANES), jnp.uint32),
            pltpu.VMEM((N_EXPERTS, 1), jnp.float32),
            pltpu.VMEM((SUBLANES, tm), jnp.int32),
            pltpu.SMEM((tm,), jnp.int32),
            pltpu.SMEM((tm,), jnp.int32),
            pltpu.VMEM((N_EXPERTS, LANES), jnp.int32),
            pltpu.SMEM((N_EXPERTS, LANES), jnp.int32),
            pltpu.VMEM((FFN_ROWS // 2 * ROW_GROUP, LANES), jnp.uint32),
            pltpu.SemaphoreType.DMA,
            pltpu.SemaphoreType.DMA,
            pltpu.SemaphoreType.DMA,
            pltpu.VMEM((2, TOP_K, tm * ROW_GROUP, LANES), jnp.uint32),
            pltpu.VMEM((tm, d), jnp.float32),
            pltpu.SemaphoreType.DMA((2,)),
        ],
        compiler_params=pltpu.CompilerParams(
            dimension_semantics=("arbitrary", "arbitrary"), vmem_limit_bytes=VMEM_LIMIT),
        name="mixer",
    )(x, mod, g1.reshape(1, d), w_in_p, _bf(w_out), ln_g.reshape(1, -1), ln_b.reshape(1, -1), w_s, bsm,
      w_gate_p, b_gate.reshape(1, -1), gla_g.reshape(1, -1), conv_w, conv_b.reshape(1, -1),
      g2.reshape(1, d), router_w.T, router_b.reshape(N_EXPERTS, 1), *prev_args)


def _ffn_kernel(be_ref, rb_ref, nb_ref, xs_ref, w1_ref, w3_ref, w2_ref, ys_ref, w1b_ref, w3b_ref, w2b_ref):
    del rb_ref
    i = pl.program_id(0)
    used = i < nb_ref[0]
    new_expert = jnp.logical_or(i == 0, be_ref[i] != be_ref[jnp.maximum(i - 1, 0)])

    @pl.when(jnp.logical_and(used, new_expert))
    def _():
        w1b_ref[...] = _bf(w1_ref[...])
        w3b_ref[...] = _bf(w3_ref[...])
        w2b_ref[...] = _bf(w2_ref[...])

    @pl.when(used)
    def _():
        x_a, x_b = _unpack_rows(_load_packed_rows(xs_ref, (), FFN_ROWS))
        half = x_a.shape[1]
        a = _dot(x_a, w1b_ref[0:half, :]) + _dot(x_b, w1b_ref[half:, :])
        g = _dot(x_a, w3b_ref[0:half, :]) + _dot(x_b, w3b_ref[half:, :])
        hm = _bf((a * jax.nn.sigmoid(a)) * g)
        _store_rows_as_tiles(ys_ref, _pack_rows(_dot(hm, w2b_ref[...])))


def _expert_ffn(layer, block_e, row_block, n_used, xs, w1, w3, w2):
    n_rows = xs.shape[0] // ROW_GROUP
    n_blocks = block_e.shape[0]
    d, f = w1.shape[-2], w1.shape[-1]
    return pl.pallas_call(
        _ffn_kernel,
        grid_spec=pltpu.PrefetchScalarGridSpec(
            num_scalar_prefetch=3,
            grid=(n_blocks,),
            in_specs=[
                pl.BlockSpec((FFN_ROWS * ROW_GROUP, LANES), lambda i, be, rb, nb: (rb[i], 0)),
                pl.BlockSpec((None, None, d, f), lambda i, be, rb, nb: (layer, be[i], 0, 0)),
                pl.BlockSpec((None, None, d, f), lambda i, be, rb, nb: (layer, be[i], 0, 0)),
                pl.BlockSpec((None, None, f, d), lambda i, be, rb, nb: (layer, be[i], 0, 0)),
            ],
            out_specs=pl.BlockSpec((FFN_ROWS * ROW_GROUP, LANES), lambda i, be, rb, nb: (rb[i], 0)),
            scratch_shapes=[pltpu.VMEM((d, f), jnp.bfloat16), pltpu.VMEM((d, f), jnp.bfloat16),
                            pltpu.VMEM((f, d), jnp.bfloat16)],
        ),
        out_shape=jax.ShapeDtypeStruct((n_rows * ROW_GROUP, LANES), jnp.uint32),
        compiler_params=pltpu.CompilerParams(
            dimension_semantics=("arbitrary",), vmem_limit_bytes=VMEM_LIMIT),
        name="expert_ffn",
    )(block_e, row_block, n_used, xs, w1, w3, w2)


def _combine_kernel(d0_ref, d1_ref, d0n_ref, d1n_ref, x_ref, mod_ref, w_ref, gf_ref, ys_ref, o_ref, ybuf, sems,
                    *, final_norm):
    tm = x_ref.shape[0]
    step = pl.program_id(0) * pl.num_programs(1) + pl.program_id(1)
    last = step == pl.num_programs(0) * pl.num_programs(1) - 1
    slot = step % 2

    def drain(sem):
        def body(g, c):
            for _ in range(TOP_K * SUBLANES):
                _tile_copy(ys_ref, 0, ybuf.at[0, 0], 0, sem).wait()
            return c
        lax.fori_loop(0, tm // SUBLANES, body, 0)

    @pl.when(step == 0)
    def _():
        def issue(g, c):
            for jj in range(SUBLANES):
                i = pl.multiple_of(g * SUBLANES, SUBLANES) + jj
                _tile_copy(ys_ref, d0_ref[i], ybuf.at[0, 0], i * ROW_GROUP, sems.at[0]).start()
                _tile_copy(ys_ref, d1_ref[i], ybuf.at[0, 1], i * ROW_GROUP, sems.at[0]).start()
            return c
        lax.fori_loop(0, tm // SUBLANES, issue, 0)

    nxt, sem_nxt = ybuf.at[1 - slot], sems.at[1 - slot]
    for i in range(tm):
        _tile_copy(ys_ref, d0n_ref[i], nxt.at[0], i * ROW_GROUP, sem_nxt).start(priority=0)
        _tile_copy(ys_ref, d1n_ref[i], nxt.at[1], i * ROW_GROUP, sem_nxt).start(priority=1)

    drain(sems.at[slot])
    w = w_ref[...]
    gt2 = mod_ref[...][5:6]
    out = x_ref[...] + gt2 * (w[:, 0:1] * _load_rows_from_tiles(ybuf, (slot, 0), tm)
                              + w[:, 1:2] * _load_rows_from_tiles(ybuf, (slot, 1), tm))
    if final_norm:
        out = _rms(out) * gf_ref[...]
    o_ref[...] = out

    @pl.when(last)
    def _():
        drain(sems.at[1 - slot])


def _combine(d0, d1, x, mod, wcol, g_final, ys, final_norm):
    bn, t, d = x.shape
    tm = min(MOVE_ROWS, t)
    nt = t // tm
    n_tiles = bn * nt

    def this_tile(b, i):
        return (b * nt + i,)

    def next_tile(b, i):
        return (jnp.minimum(b * nt + i + 1, n_tiles - 1),)

    return pl.pallas_call(
        functools.partial(_combine_kernel, final_norm=final_norm),
        grid=(bn, nt),
        in_specs=[
            pl.BlockSpec((tm,), this_tile, memory_space=pltpu.SMEM),
            pl.BlockSpec((tm,), this_tile, memory_space=pltpu.SMEM),
            pl.BlockSpec((tm,), next_tile, memory_space=pltpu.SMEM),
            pl.BlockSpec((tm,), next_tile, memory_space=pltpu.SMEM),
            pl.BlockSpec((None, tm, d), lambda b, i: (b, i, 0)),
            pl.BlockSpec((None, 6, d), lambda b, i: (b, 0, 0)),
            pl.BlockSpec((tm, 8), lambda b, i: (b * nt + i, 0)),
            pl.BlockSpec((1, d), lambda b, i: (0, 0)),
            pl.BlockSpec(memory_space=pl.ANY),
        ],
        out_specs=pl.BlockSpec((None, tm, d), lambda b, i: (b, i, 0)),
        out_shape=jax.ShapeDtypeStruct(x.shape, x.dtype),
        scratch_shapes=[pltpu.VMEM((2, TOP_K, tm * ROW_GROUP, LANES), jnp.uint32), pltpu.SemaphoreType.DMA((2,))],
        compiler_params=pltpu.CompilerParams(
            dimension_semantics=("arbitrary", "arbitrary"), vmem_limit_bytes=VMEM_LIMIT),
        name="combine",
    )(d0, d1, d0, d1, x, mod, wcol, g_final.reshape(1, d), ys)


def _block_map(counts, cap, n_blocks):
    blocks_per_expert = (counts + FFN_ROWS - 1) // FFN_ROWS
    ends = jnp.cumsum(blocks_per_expert)
    n_used = ends[-1:]
    step = jnp.minimum(jnp.arange(n_blocks, dtype=jnp.int32), n_used - 1)
    block_e = jnp.minimum(jnp.sum((ends[None, :] <= step[:, None]).astype(jnp.int32), axis=1), N_EXPERTS - 1)
    first = (ends - blocks_per_expert)[block_e]
    row_block = block_e * (cap // FFN_ROWS) + (step - first)
    return block_e.astype(jnp.int32), row_block.astype(jnp.int32), n_used.astype(jnp.int32)


def kernel(x, c, w_ada, b_ada, g_norm1, g_norm2, w_in, w_out, gmlp_ln_g, gmlp_ln_b, gmlp_ws, gmlp_bs,
           gla_w_gate, gla_b_gate, gla_norm_g, conv_w, conv_b, router_w, router_b, exp_w1, exp_w3, exp_w2,
           g_final):
    depth = w_ada.shape[0]
    bn, t, d = x.shape
    n = bn * t
    n_blocks = -(-(n * TOP_K) // FFN_ROWS) + N_EXPERTS
    mod_all = _modulation(c, w_ada, b_ada).reshape(depth, bn, 6, d)
    prev = None
    for l in range(depth):
        mod = mod_all[l]
        x, dest, wcol, cnt, xs = _mixer(
            x, mod, g_norm1[l], g_norm2[l], w_in[l], w_out[l], gmlp_ln_g[l], gmlp_ln_b[l], gmlp_ws[l],
            gmlp_bs[l], gla_w_gate[l], gla_b_gate[l], gla_norm_g[l], conv_w[l], conv_b[l], router_w, router_b,
            prev=prev)
        block_e, row_block, n_used = _block_map(cnt[:, 0], n, n_blocks)
        ys = _expert_ffn(l, block_e, row_block, n_used, xs, exp_w1, exp_w3, exp_w2)
        prev = (dest[0], dest[1], wcol, mod, ys)
    d0, d1, wcol, mod, ys = prev
    return _combine(d0, d1, x, mod, wcol, g_final, ys, final_norm=True)
```

```python
import functools

import jax
import jax.numpy as jnp
from jax import lax
from jax.experimental import pallas as pl
from jax.experimental.pallas import tpu as pltpu

CHUNK = 64
GMLP_BLOCK = 128
A_DIM = 256
A_HEADS = 4
B_HEADS = 4
B_DK = 64
B_DV = 128
B_KDIM = B_HEADS * B_DK
B_VDIM = B_HEADS * B_DV
GLA_RANK = 16
GLA_TAU = 16.0
C_DIM = 256
N_EXPERTS = 32
N_GROUPS = 4
EXPERTS_PER_GROUP = N_EXPERTS // N_GROUPS
TOP_K = 2
EPS = 1e-6

LANES = 128
SUBLANES = 8
OFF_AU, OFF_AV, OFF_Q, OFF_K, OFF_V = 0, 256, 512, 768, 1024
OFF_OG, OFF_CB, OFF_CC, OFF_CX, OFF_GLR = 1536, 2048, 2304, 2560, 2816
P_PAD = OFF_GLR + LANES

MIX_ROWS = 512
COPY_POINTS_PER_BLOCK = 18
MOVE_ROWS = 512
FFN_ROWS = 512
VMEM_LIMIT = 56 * 1024 * 1024

_NT = (((1,), (1,)), ((), ()))
_TN = (((0,), (0,)), ((), ()))


def _dot(a, b, dims=None):
    if dims is None:
        return jnp.dot(a, b, preferred_element_type=jnp.float32)
    return lax.dot_general(a, b, dims, preferred_element_type=jnp.float32)


def _bf(x):
    return x.astype(jnp.bfloat16)


def _split_bf16(x):
    hi = _bf(x)
    lo = _bf(x - hi.astype(jnp.float32))
    return hi, lo


def _rms(x):
    return x * lax.rsqrt(jnp.mean(x * x, axis=-1, keepdims=True) + EPS)


def _pack_rows(v):
    half = v.shape[1] // 2
    hi = pltpu.bitcast(_bf(v[:, :half]).astype(jnp.float32), jnp.uint32)
    lo = pltpu.bitcast(_bf(v[:, half:]).astype(jnp.float32), jnp.uint32)
    return hi | (lo >> 16)


def _unpack_rows(w):
    hi = pltpu.bitcast(w & jnp.uint32(0xFFFF0000), jnp.float32)
    lo = pltpu.bitcast(w << 16, jnp.float32)
    return _bf(hi), _bf(lo)


def _row_copy(src_ref, src_row, dst_ref, dst_row, sem):
    return pltpu.make_async_copy(src_ref.at[pl.ds(src_row, 1)], dst_ref.at[pl.ds(dst_row, 1)], sem)


ROW_GROUP = 4

def _aligned(row):
    return row if isinstance(row, int) else pl.multiple_of(row, ROW_GROUP)


def _tile_copy(src_ref, src_row, dst_ref, dst_row, sem):
    return pltpu.make_async_copy(src_ref.at[pl.ds(_aligned(src_row), ROW_GROUP)],
                                 dst_ref.at[pl.ds(_aligned(dst_row), ROW_GROUP)], sem)


def _store_rows_as_tiles(ref, value):
    rows = value.shape[0]
    assert value.shape[1] == ROW_GROUP * LANES
    for c in range(ROW_GROUP):
        ref[pl.ds(c, rows, stride=ROW_GROUP), :] = value[:, c * LANES:(c + 1) * LANES]


def _load_packed_rows(ref, idx, rows):
    return jnp.concatenate(
        [ref[idx + (pl.ds(c, rows, stride=ROW_GROUP), slice(None))] for c in range(ROW_GROUP)], axis=1)


def _load_rows_from_tiles(ref, idx, rows):
    hi, lo = _unpack_rows(_load_packed_rows(ref, idx, rows))
    return jnp.concatenate([hi.astype(jnp.float32), lo.astype(jnp.float32)], axis=1)


def _mod_kernel(c_ref, w_ref, b_ref, o_ref):
    c = c_ref[...]
    s = c * jax.nn.sigmoid(c)
    s_hi, s_lo = _split_bf16(s)
    w_hi, w_lo = _split_bf16(w_ref[...])
    acc = _dot(s_hi, w_hi) + _dot(s_hi, w_lo) + _dot(s_lo, w_hi)
    o_ref[...] = acc + b_ref[...]


def _modulation(c, w_ada, b_ada):
    depth, d, six_d = w_ada.shape
    bn = c.shape[0]
    cb = 2048
    return pl.pallas_call(
        _mod_kernel,
        grid=(depth, six_d // cb),
        in_specs=[
            pl.BlockSpec((bn, d), lambda l, j: (0, 0)),
            pl.BlockSpec((None, d, cb), lambda l, j: (l, 0, j)),
            pl.BlockSpec((None, 1, cb), lambda l, j: (l, 0, j)),
        ],
        out_specs=pl.BlockSpec((None, bn, cb), lambda l, j: (l, 0, j)),
        out_shape=jax.ShapeDtypeStruct((depth, bn, six_d), jnp.float32),
        compiler_params=pltpu.CompilerParams(
            dimension_semantics=("arbitrary", "arbitrary"), vmem_limit_bytes=VMEM_LIMIT),
        name="adaln_mod",
    )(c, w_ada, b_ada.reshape(depth, 1, six_d))


def _route(h2, rwt_ref, rb_ref, carry, cap):
    tr = h2.shape[0]
    h_hi, h_lo = _split_bf16(h2)
    w_hi, w_lo = _split_bf16(rwt_ref[...])
    logits = (_dot(w_hi, h_hi, _NT) + _dot(w_hi, h_lo, _NT) + _dot(w_lo, h_hi, _NT)) + rb_ref[...]
    ex = jnp.exp(logits - jnp.max(logits, axis=0, keepdims=True))
    probs = ex / jnp.sum(ex, axis=0, keepdims=True)

    idx8 = lax.broadcasted_iota(jnp.int32, (EXPERTS_PER_GROUP, tr), 0)
    best = None
    for g in range(N_GROUPS):
        pg = probs[g * EXPERTS_PER_GROUP:(g + 1) * EXPERTS_PER_GROUP]
        m1 = jnp.max(pg, axis=0, keepdims=True)
        i1 = jnp.min(jnp.where(pg == m1, idx8, EXPERTS_PER_GROUP), axis=0, keepdims=True)
        pg2 = jnp.where(idx8 == i1, -1.0, pg)
        m2 = jnp.max(pg2, axis=0, keepdims=True)
        i2 = jnp.min(jnp.where(pg2 == m2, idx8, EXPERTS_PER_GROUP), axis=0, keepdims=True)
        cand = (m1 + m2, m1, m2, i1 + g * EXPERTS_PER_GROUP, i2 + g * EXPERTS_PER_GROUP)
        if best is None:
            best = cand
        else:
            better = cand[0] > best[0]
            best = tuple(jnp.where(better, a, b) for a, b in zip(cand, best))
    _, p1, p2, e0, e1 = best
    denom = p1 + p2
    w0, w1 = p1 / denom, p2 / denom

    eidx = lax.broadcasted_iota(jnp.int32, (N_EXPERTS, tr), 0)
    hit0, hit1 = eidx == e0, eidx == e1
    onehot = jnp.where(jnp.logical_or(hit0, hit1), 1.0, 0.0)
    before = (lax.broadcasted_iota(jnp.int32, (tr, tr), 0) < lax.broadcasted_iota(jnp.int32, (tr, tr), 1))
    rank = _dot(_bf(onehot), _bf(jnp.where(before, 1.0, 0.0))) + carry
    r0 = jnp.sum(jnp.where(hit0, rank, 0.0), axis=0, keepdims=True).astype(jnp.int32)
    r1 = jnp.sum(jnp.where(hit1, rank, 0.0), axis=0, keepdims=True).astype(jnp.int32)
    carry = carry + jnp.sum(onehot, axis=1, keepdims=True)

    eye = lax.broadcasted_iota(jnp.int32, (tr, tr), 0) == lax.broadcasted_iota(jnp.int32, (tr, tr), 1)
    w0c = jnp.sum(jnp.where(eye, jnp.broadcast_to(w0, (tr, tr)), 0.0), axis=1, keepdims=True)
    w1c = jnp.sum(jnp.where(eye, jnp.broadcast_to(w1, (tr, tr)), 0.0), axis=1, keepdims=True)
    lane8 = lax.broadcasted_iota(jnp.int32, (tr, 8), 1)
    wcol = jnp.where(lane8 == 0, w0c, jnp.where(lane8 == 1, w1c, 0.0))
    return (e0 * cap + r0) * ROW_GROUP, (e1 * cap + r1) * ROW_GROUP, wcol, carry


def _zero_segment_padding(cnt_ref, cap, zero_ref, xs_ref, sem):
    copies = []
    for ex in range(N_EXPERTS):
        cnt = cnt_ref[ex, 0]
        seg = ex * cap
        pos = seg + cnt
        end8 = seg + ((cnt + SUBLANES - 1) // SUBLANES) * SUBLANES
        for j in range(SUBLANES - 1):
            copies.append((pos + j < end8, _tile_copy(zero_ref, 0, xs_ref, (pos + j) * ROW_GROUP, sem)))
        seg_end = seg + ((cnt + FFN_ROWS - 1) // FFN_ROWS) * FFN_ROWS
        groups = (seg_end - end8) // SUBLANES
        at = end8
        bit = FFN_ROWS // (2 * SUBLANES)
        while bit >= 1:
            rows = bit * SUBLANES
            pred = (groups & bit) != 0
            copies.append((pred, pltpu.make_async_copy(
                zero_ref.at[pl.ds(0, rows * ROW_GROUP)],
                xs_ref.at[pl.ds(pl.multiple_of(at * ROW_GROUP, SUBLANES), rows * ROW_GROUP)], sem)))
            at = at + jnp.where(pred, rows, 0)
            bit //= 2
    for pred, cp in copies:
        pl.when(pred)(cp.start)
    for pred, cp in copies:
        pl.when(pred)(cp.wait)


N_MIXER_INPUTS = 17
N_MIXER_OUTPUTS = 5


def _mixer_kernel(*refs, cap, fuse_prev):
    ins, rest = refs[:N_MIXER_INPUTS], refs[N_MIXER_INPUTS:]
    if fuse_prev:
        dp0_ref, dp1_ref, dp0n_ref, dp1n_ref, wp_ref, modp_ref, ysp_ref = rest[:7]
        rest = rest[7:]
    outs, scratch = rest[:N_MIXER_OUTPUTS], rest[N_MIXER_OUTPUTS:]
    (x_ref, mod_ref, g1_ref, win_ref, wout_ref, lng_ref, lnb_ref, ws_ref, bsm_ref,
     wg_ref, bg_ref, glag_ref, cw_ref, cb_ref, g2_ref, rwt_ref, rb_ref) = ins
    o_ref, dest_ref, w_ref, cnt_ref, xs_ref = outs
    (h_ref, pa_ref, pb_ref, y0_ref, y1_ref, y2_ref, y3_ref, s_ref, zc_ref, wsm_ref, h2_ref,
     carry_ref, dvm_ref, d0s_ref, d1s_ref, cvm_ref, csm_ref, zero_ref, sem, sem_s, sem_z,
     ybuf, xin_ref, sem_g) = scratch

    bi, ti = pl.program_id(0), pl.program_id(1)
    tm = x_ref.shape[0]
    n_blk = tm // GMLP_BLOCK
    step = bi * pl.num_programs(1) + ti
    last_step = step == pl.num_programs(0) * pl.num_programs(1) - 1
    slot = step % 2
    spare_row0 = N_EXPERTS * cap

    def drain_rows(src_ref, dst_ref, s, copy=_row_copy):
        def body(g, c):
            for _ in range(TOP_K * SUBLANES):
                copy(src_ref, 0, dst_ref, 0, s).wait()
            return c
        lax.fori_loop(0, tm // SUBLANES, body, 0)

    @pl.when(step > 0)
    def _():
        pltpu.make_async_copy(dvm_ref.at[0], d0s_ref, sem_s).wait()
        pltpu.make_async_copy(dvm_ref.at[1], d1s_ref, sem_s).wait()

    if fuse_prev:
        @pl.when(step == 0)
        def _():
            def issue(g, c):
                for jj in range(SUBLANES):
                    i = pl.multiple_of(g * SUBLANES, SUBLANES) + jj
                    _tile_copy(ysp_ref, dp0_ref[i], ybuf.at[0, 0], i * ROW_GROUP, sem_g.at[0]).start()
                    _tile_copy(ysp_ref, dp1_ref[i], ybuf.at[0, 1], i * ROW_GROUP, sem_g.at[0]).start()
                return c
            lax.fori_loop(0, tm // SUBLANES, issue, 0)

        drain_rows(ysp_ref, ybuf.at[0, 0], sem_g.at[slot], _tile_copy)
        wp = wp_ref[...]
        xin_ref[...] = x_ref[...] + modp_ref[...][5:6] * (
            wp[:, 0:1] * _load_rows_from_tiles(ybuf, (slot, 0), tm)
            + wp[:, 1:2] * _load_rows_from_tiles(ybuf, (slot, 1), tm))
        x_in = xin_ref
    else:
        x_in = x_ref

    @pl.when(step == 0)
    def _():
        carry_ref[...] = jnp.zeros_like(carry_ref)
        h2_ref[...] = jnp.zeros_like(h2_ref)

        def fill(i, c):
            d0s_ref[i] = (spare_row0 + i) * ROW_GROUP
            d1s_ref[i] = (spare_row0 + tm + i) * ROW_GROUP
            return c

        lax.fori_loop(0, tm, fill, 0)
        tt = lax.broadcasted_iota(jnp.int32, (GMLP_BLOCK, GMLP_BLOCK), 0) // CHUNK
        ss = lax.broadcasted_iota(jnp.int32, (GMLP_BLOCK, GMLP_BLOCK), 1) // CHUNK
        for h in range(A_HEADS):
            wsm_ref[h] = _bf(jnp.where(tt >= ss, ws_ref[h], 0.0))

    @pl.when(ti == 0)
    def _():
        s_ref[...] = jnp.zeros_like(s_ref)
        zc_ref[...] = jnp.zeros_like(zc_ref)

    m = mod_ref[...]
    sh1, sc1, gt1, sh2, sc2 = m[0:1], m[1:2], m[2:3], m[3:4], m[4:5]
    h_ref[...] = _bf((_rms(x_in[...]) * g1_ref[...]) * (1.0 + sc1) + sh1)

    lane256 = lax.broadcasted_iota(jnp.int32, (CHUNK, B_KDIM), 1)
    head_of_lane = lane256 // B_DK
    r64 = lax.broadcasted_iota(jnp.int32, (CHUNK, CHUNK), 0)
    c64 = lax.broadcasted_iota(jnp.int32, (CHUNK, CHUNK), 1)
    tri = _bf(jnp.where(r64 >= c64, 1.0, 0.0))
    causal4 = jnp.concatenate([r64 >= c64] * B_HEADS, axis=0)
    eye256 = (lax.broadcasted_iota(jnp.int32, (B_KDIM, B_KDIM), 0)
              == lax.broadcasted_iota(jnp.int32, (B_KDIM, B_KDIM), 1))
    a_head_of_lane = lax.broadcasted_iota(jnp.int32, (GMLP_BLOCK, A_DIM), 1) // (A_DIM // A_HEADS)
    row128 = lax.broadcasted_iota(jnp.int32, (GMLP_BLOCK, C_DIM), 0)
    sqrt_half = 0.7071067811865476

    def gelu(v):
        return 0.5 * v * (1.0 + lax.erf(v * sqrt_half))

    h2_prev = h2_ref.at[1 - slot]
    row_copies = []
    for i in range(tm):
        row_copies.append(
            lambda i=i: _tile_copy(h2_prev, i * ROW_GROUP, xs_ref, d0s_ref[i], sem).start(priority=0))
        row_copies.append(
            lambda i=i: _tile_copy(h2_prev, i * ROW_GROUP, xs_ref, d1s_ref[i], sem).start(priority=1))
        if fuse_prev:
            row_copies.append(lambda i=i: _tile_copy(
                ysp_ref, dp0n_ref[i], ybuf.at[1 - slot, 0], i * ROW_GROUP, sem_g.at[1 - slot]).start(priority=0))
            row_copies.append(lambda i=i: _tile_copy(
                ysp_ref, dp1n_ref[i], ybuf.at[1 - slot, 1], i * ROW_GROUP, sem_g.at[1 - slot]).start(priority=1))
    row_copies.reverse()
    copies_per_point = -(-len(row_copies) // (n_blk * COPY_POINTS_PER_BLOCK))

    def start_row_copies():
        for _ in range(min(copies_per_point, len(row_copies))):
            row_copies.pop()()

    p_halves = (pa_ref, pb_ref)
    y_blocks = (y0_ref, y1_ref, y2_ref, y3_ref)
    assert n_blk == len(y_blocks)

    def project(half):
        rows = slice(half * (tm // 2), (half + 1) * (tm // 2))
        p_halves[half][...] = _dot(h_ref[rows, :], win_ref[...])

    def block(j, state, zprev):
        p_ref, y_ref = p_halves[j // (n_blk // 2)], y_blocks[j]
        r0 = (j % (n_blk // 2)) * GMLP_BLOCK
        rows = slice(r0, r0 + GMLP_BLOCK)

        u = gelu(p_ref[rows, OFF_AU:OFF_AU + A_DIM])
        vv = gelu(p_ref[rows, OFF_AV:OFF_AV + A_DIM])
        mu = jnp.mean(vv, axis=-1, keepdims=True)
        var = jnp.mean((vv - mu) ** 2, axis=-1, keepdims=True)
        vn = _bf((vv - mu) * lax.rsqrt(var + EPS) * lng_ref[...] + lnb_ref[...])
        sv = jnp.zeros((GMLP_BLOCK, A_DIM), jnp.float32)
        for hh in range(A_HEADS):
            sv = jnp.where(a_head_of_lane == hh, _dot(wsm_ref[hh], vn), sv)
        y_ref[:, 0:A_DIM] = _bf(u * (sv + bsm_ref[...]))
        start_row_copies()

        z = _dot(_bf(p_ref[rows, OFF_GLR:OFF_GLR + LANES]), wg_ref[...]) + bg_ref[...]
        glog = (jnp.minimum(z, 0.0) - jnp.log1p(jnp.exp(-jnp.abs(z)))) / GLA_TAU
        def chunk_head(c):
            rc = slice(r0 + c * CHUNK, r0 + (c + 1) * CHUNK)
            g = glog[c * CHUNK:(c + 1) * CHUNK]
            g_hi = _bf(g)
            g_r1 = g - g_hi.astype(jnp.float32)
            g_mid = _bf(g_r1)
            g_lo = _bf(g_r1 - g_mid.astype(jnp.float32))
            b = _dot(tri, g_hi) + _dot(tri, g_mid) + _dot(tri, g_lo)
            b_mid = b[CHUNK // 2:CHUNK // 2 + 1]
            b_last = b[CHUNK - 1:CHUNK]
            q = p_ref[rc, OFF_Q:OFF_Q + B_KDIM] * (B_DK ** -0.5)
            k = p_ref[rc, OFF_K:OFF_K + B_KDIM]
            vb = _bf(p_ref[rc, OFF_V:OFF_V + B_VDIM])
            qs = q * jnp.exp(b - b_mid)
            ks = _bf(k * jnp.exp(b_mid - b))
            kd = _bf(k * jnp.exp(b_last - b))
            qb = q * jnp.exp(b)
            qs_st = _bf(jnp.concatenate(
                [jnp.where(head_of_lane == hh, qs, 0.0) for hh in range(B_HEADS)], axis=0))
            qb_st = _bf(jnp.concatenate(
                [jnp.where(head_of_lane == hh, qb, 0.0) for hh in range(B_HEADS)], axis=0))
            scores = _bf(jnp.where(causal4, _dot(qs_st, ks, _NT), 0.0))
            start_row_copies()
            kv_all = _dot(kd, vb, _TN)
            start_row_copies()
            o_intra, kvs = [], []
            for hh in range(B_HEADS):
                rs = slice(hh * CHUNK, (hh + 1) * CHUNK)
                cs = slice(hh * B_DV, (hh + 1) * B_DV)
                o_intra.append(_dot(scores[rs], vb[:, cs]))
                kvs.append(kv_all[rs, cs])
                start_row_copies()
            decay_row = jnp.broadcast_to(jnp.exp(b_last), (B_KDIM, B_KDIM))
            decay_col = jnp.sum(jnp.where(eye256, decay_row, 0.0), axis=1, keepdims=True)
            og = p_ref[rc, OFF_OG:OFF_OG + B_VDIM]
            return qb_st, o_intra, jnp.concatenate(kvs, axis=0), decay_col, og * jax.nn.sigmoid(og)

        heads = [chunk_head(c) for c in range(GMLP_BLOCK // CHUNK)]
        for c, (qb_st, o_intra, kv, decay_col, out_gate) in enumerate(heads):
            o_inter = _dot(qb_st, _bf(state))
            state = decay_col * state + kv
            outs = [_rms(o_intra[hh] + o_inter[hh * CHUNK:(hh + 1) * CHUNK]) for hh in range(B_HEADS)]
            on = jnp.concatenate(outs, axis=1) * glag_ref[...]
            y_ref[c * CHUNK:(c + 1) * CHUNK, A_DIM:A_DIM + B_VDIM] = _bf(on * out_gate)
            start_row_copies()

        zz = p_ref[rows, OFF_CC:OFF_CC + C_DIM] * p_ref[rows, OFF_CX:OFF_CX + C_DIM]
        z1 = jnp.where(row128 == 0, zprev[7:8], pltpu.roll(zz, 1, 0))
        z2 = jnp.where(row128 == 0, zprev[6:7], jnp.where(row128 == 1, zprev[7:8], pltpu.roll(zz, 2, 0)))
        cw = cw_ref[...]
        yc = cb_ref[...] + cw[0:1] * z2
        yc = yc + cw[1:2] * z1
        yc = yc + cw[2:3] * zz
        y_ref[:, A_DIM + B_VDIM:A_DIM + B_VDIM + C_DIM] = _bf(p_ref[rows, OFF_CB:OFF_CB + C_DIM] * yc)
        start_row_copies()
        return state, zz[GMLP_BLOCK - SUBLANES:GMLP_BLOCK]

    state, zprev, carry = s_ref[...], zc_ref[...], carry_ref[...]
    project(0)
    for j in range(n_blk):
        if j == 0:
            project(1)
        state, zprev = block(j, state, zprev)
        rows = slice(j * GMLP_BLOCK, (j + 1) * GMLP_BLOCK)
        xo = x_in[rows, :] + gt1 * _dot(y_blocks[j][...], wout_ref[...])
        o_ref[rows, :] = xo
        start_row_copies()
        h2 = (_rms(xo) * g2_ref[...]) * (1.0 + sc2) + sh2
        _store_rows_as_tiles(
            h2_ref.at[slot, pl.ds(j * GMLP_BLOCK * ROW_GROUP, GMLP_BLOCK * ROW_GROUP)], _pack_rows(h2))
        d0, d1, wcol, carry = _route(h2, rwt_ref, rb_ref, carry, cap)
        dvm_ref[0:1, rows] = d0
        dvm_ref[1:2, rows] = d1
        w_ref[rows, :] = wcol
        start_row_copies()
    while row_copies:
        row_copies.pop()()
    s_ref[...] = state
    zc_ref[...] = zprev
    carry_ref[...] = carry
    cnt = jnp.broadcast_to(carry, cnt_ref.shape).astype(jnp.int32)
    cnt_ref[...] = cnt
    dest_ref[...] = dvm_ref[0:TOP_K, :]

    drain_rows(h2_prev, xs_ref, sem, _tile_copy)

    to_smem = [pltpu.make_async_copy(dvm_ref.at[0], d0s_ref, sem_s),
               pltpu.make_async_copy(dvm_ref.at[1], d1s_ref, sem_s)]
    for cp in to_smem:
        cp.start()

    @pl.when(last_step)
    def _():
        for cp in to_smem:
            cp.wait()
        h2_last = h2_ref.at[slot]

        def issue(g, c):
            for jj in range(SUBLANES):
                i = pl.multiple_of(g * SUBLANES, SUBLANES) + jj
                _tile_copy(h2_last, i * ROW_GROUP, xs_ref, d0s_ref[i], sem).start()
                _tile_copy(h2_last, i * ROW_GROUP, xs_ref, d1s_ref[i], sem).start()
            return c

        lax.fori_loop(0, tm // SUBLANES, issue, 0)
        drain_rows(h2_last, xs_ref, sem, _tile_copy)
        if fuse_prev:
            drain_rows(ysp_ref, ybuf.at[0, 0], sem_g.at[1 - slot], _tile_copy)
        cvm_ref[...] = cnt
        zero_ref[...] = jnp.zeros_like(zero_ref)
        cp = pltpu.make_async_copy(cvm_ref, csm_ref, sem_s)
        cp.start()
        cp.wait()
        _zero_segment_padding(csm_ref, cap, zero_ref, xs_ref, sem_z)


def _mixer(x, mod, g1, g2, w_in, w_out, ln_g, ln_b, w_s, b_s, w_gate, b_gate, gla_g, conv_w, conv_b,
           router_w, router_b, prev=None):
    bn, t, d = x.shape
    n = bn * t
    tm = min(MIX_ROWS, t)
    nt = t // tm
    cap = n
    glr0 = 2 * A_DIM + 2 * B_KDIM + B_VDIM
    w_in_p = jnp.concatenate(
        [_bf(w_in[:, :glr0]), _bf(w_in[:, glr0 + GLA_RANK:]), _bf(w_in[:, glr0:glr0 + GLA_RANK]),
         jnp.zeros((d, LANES - GLA_RANK), jnp.bfloat16)], axis=1)
    w_gate_p = jnp.concatenate(
        [_bf(w_gate), jnp.zeros((LANES - GLA_RANK, B_KDIM), jnp.bfloat16)], axis=0)
    bsm = jnp.repeat(b_s.T, A_DIM // A_HEADS, axis=1)

    def whole(shape):
        return pl.BlockSpec(shape, lambda b, i: (0,) * len(shape))

    prev_specs, prev_args = [], ()
    if prev is not None:
        n_tiles = bn * nt

        def this_tile(b, i):
            return (b * nt + i,)

        def next_tile(b, i):
            return (jnp.minimum(b * nt + i + 1, n_tiles - 1),)

        d0p, d1p, wcolp, modp, ysp = prev
        prev_specs = [
            pl.BlockSpec((tm,), this_tile, memory_space=pltpu.SMEM),
            pl.BlockSpec((tm,), this_tile, memory_space=pltpu.SMEM),
            pl.BlockSpec((tm,), next_tile, memory_space=pltpu.SMEM),
            pl.BlockSpec((tm,), next_tile, memory_space=pltpu.SMEM),
            pl.BlockSpec((tm, 8), lambda b, i: (b * nt + i, 0)),
            pl.BlockSpec((None, 6, d), lambda b, i: (b, 0, 0)),
            pl.BlockSpec(memory_space=pl.ANY),
        ]
        prev_args = (d0p, d1p, d0p, d1p, wcolp, modp, ysp)

    return pl.pallas_call(
        functools.partial(_mixer_kernel, cap=cap, fuse_prev=prev is not None),
        grid=(bn, nt),
        in_specs=[
            pl.BlockSpec((None, tm, d), lambda b, i: (b, i, 0)),
            pl.BlockSpec((None, 6, d), lambda b, i: (b, 0, 0)),
            whole((1, d)),
            whole((d, P_PAD)),
            whole((d, d)),
            whole((1, A_DIM)),
            whole((1, A_DIM)),
            whole((A_HEADS, GMLP_BLOCK, GMLP_BLOCK)),
            whole((GMLP_BLOCK, A_DIM)),
            whole((LANES, B_KDIM)),
            whole((1, B_KDIM)),
            whole((1, B_VDIM)),
            whole((3, C_DIM)),
            whole((1, C_DIM)),
            whole((1, d)),
            whole((N_EXPERTS, d)),
            whole((N_EXPERTS, 1)),
        ] + prev_specs,
        out_specs=[
            pl.BlockSpec((None, tm, d), lambda b, i: (b, i, 0)),
            pl.BlockSpec((TOP_K, tm), lambda b, i: (0, b * nt + i)),
            pl.BlockSpec((tm, 8), lambda b, i: (b * nt + i, 0)),
            whole((N_EXPERTS, LANES)),
            pl.BlockSpec(memory_space=pl.ANY),
        ],
        out_shape=[
            jax.ShapeDtypeStruct(x.shape, x.dtype),
            jax.ShapeDtypeStruct((TOP_K, n), jnp.int32),
            jax.ShapeDtypeStruct((n, 8), jnp.float32),
            jax.ShapeDtypeStruct((N_EXPERTS, LANES), jnp.int32),
            jax.ShapeDtypeStruct(((N_EXPERTS * cap + TOP_K * tm) * ROW_GROUP, LANES), jnp.uint32),
        ],
        scratch_shapes=[
            pltpu.VMEM((tm, d), jnp.bfloat16),
            pltpu.VMEM((tm // 2, P_PAD), jnp.float32),
            pltpu.VMEM((tm // 2, P_PAD), jnp.float32),
            pltpu.VMEM((GMLP_BLOCK, d), jnp.bfloat16),
            pltpu.VMEM((GMLP_BLOCK, d), jnp.bfloat16),
            pltpu.VMEM((GMLP_BLOCK, d), jnp.bfloat16),
            pltpu.VMEM((GMLP_BLOCK, d), jnp.bfloat16),
            pltpu.VMEM((B_KDIM, B_DV), jnp.float32),
            pltpu.VMEM((SUBLANES, C_DIM), jnp.float32),
            pltpu.VMEM((A_HEADS, GMLP_BLOCK, GMLP_BLOCK), jnp.bfloat16),
            pltpu.VMEM((2, tm * ROW_GROUP, LANES), jnp.uint32),
            pltpu.VMEM((N_EXPERTS, 1), jnp.float32),
            pltpu.VMEM((SUBLANES, tm), jnp.int32),
            pltpu.SMEM((tm,), jnp.int32),
            pltpu.SMEM((tm,), jnp.int32),
            pltpu.VMEM((N_EXPERTS, LANES), jnp.int32),
            pltpu.SMEM((N_EXPERTS, LANES), jnp.int32),
            pltpu.VMEM((FFN_ROWS // 2 * ROW_GROUP, LANES), jnp.uint32),
            pltpu.SemaphoreType.DMA,
            pltpu.SemaphoreType.DMA,
            pltpu.SemaphoreType.DMA,
            pltpu.VMEM((2, TOP_K, tm * ROW_GROUP, LANES), jnp.uint32),
            pltpu.VMEM((tm, d), jnp.float32),
            pltpu.SemaphoreType.DMA((2,)),
        ],
        compiler_params=pltpu.CompilerParams(
            dimension_semantics=("arbitrary", "arbitrary"), vmem_limit_bytes=VMEM_LIMIT),
        name="mixer",
    )(x, mod, g1.reshape(1, d), w_in_p, _bf(w_out), ln_g.reshape(1, -1), ln_b.reshape(1, -1), w_s, bsm,
      w_gate_p, b_gate.reshape(1, -1), gla_g.reshape(1, -1), conv_w, conv_b.reshape(1, -1),
      g2.reshape(1, d), router_w.T, router_b.reshape(N_EXPERTS, 1), *prev_args)


def _ffn_kernel(be_ref, rb_ref, nb_ref, xs_ref, w1_ref, w3_ref, w2_ref, ys_ref, w1b_ref, w3b_ref, w2b_ref):
    del rb_ref
    i = pl.program_id(0)
    used = i < nb_ref[0]
    new_expert = jnp.logical_or(i == 0, be_ref[i] != be_ref[jnp.maximum(i - 1, 0)])

    @pl.when(jnp.logical_and(used, new_expert))
    def _():
        w1b_ref[...] = _bf(w1_ref[...])
        w3b_ref[...] = _bf(w3_ref[...])
        w2b_ref[...] = _bf(w2_ref[...])

    @pl.when(used)
    def _():
        x_a, x_b = _unpack_rows(_load_packed_rows(xs_ref, (), FFN_ROWS))
        half = x_a.shape[1]
        a = _dot(x_a, w1b_ref[0:half, :]) + _dot(x_b, w1b_ref[half:, :])
        g = _dot(x_a, w3b_ref[0:half, :]) + _dot(x_b, w3b_ref[half:, :])
        hm = _bf((a * jax.nn.sigmoid(a)) * g)
        _store_rows_as_tiles(ys_ref, _pack_rows(_dot(hm, w2b_ref[...])))


def _expert_ffn(layer, block_e, row_block, n_used, xs, w1, w3, w2):
    n_rows = xs.shape[0] // ROW_GROUP
    n_blocks = block_e.shape[0]
    d, f = w1.shape[-2], w1.shape[-1]
    return pl.pallas_call(
        _ffn_kernel,
        grid_spec=pltpu.PrefetchScalarGridSpec(
            num_scalar_prefetch=3,
            grid=(n_blocks,),
            in_specs=[
                pl.BlockSpec((FFN_ROWS * ROW_GROUP, LANES), lambda i, be, rb, nb: (rb[i], 0)),
                pl.BlockSpec((None, None, d, f), lambda i, be, rb, nb: (layer, be[i], 0, 0)),
                pl.BlockSpec((None, None, d, f), lambda i, be, rb, nb: (layer, be[i], 0, 0)),
                pl.BlockSpec((None, None, f, d), lambda i, be, rb, nb: (layer, be[i], 0, 0)),
            ],
            out_specs=pl.BlockSpec((FFN_ROWS * ROW_GROUP, LANES), lambda i, be, rb, nb: (rb[i], 0)),
            scratch_shapes=[pltpu.VMEM((d, f), jnp.bfloat16), pltpu.VMEM((d, f), jnp.bfloat16),
                            pltpu.VMEM((f, d), jnp.bfloat16)],
        ),
        out_shape=jax.ShapeDtypeStruct((n_rows * ROW_GROUP, LANES), jnp.uint32),
        compiler_params=pltpu.CompilerParams(
            dimension_semantics=("arbitrary",), vmem_limit_bytes=VMEM_LIMIT),
        name="expert_ffn",
    )(block_e, row_block, n_used, xs, w1, w3, w2)


def _combine_kernel(d0_ref, d1_ref, d0n_ref, d1n_ref, x_ref, mod_ref, w_ref, gf_ref, ys_ref, o_ref, ybuf, sems,
                    *, final_norm):
    tm = x_ref.shape[0]
    step = pl.program_id(0) * pl.num_programs(1) + pl.program_id(1)
    last = step == pl.num_programs(0) * pl.num_programs(1) - 1
    slot = step % 2

    def drain(sem):
        def body(g, c):
            for _ in range(TOP_K * SUBLANES):
                _tile_copy(ys_ref, 0, ybuf.at[0, 0], 0, sem).wait()
            return c
        lax.fori_loop(0, tm // SUBLANES, body, 0)

    @pl.when(step == 0)
    def _():
        def issue(g, c):
            for jj in range(SUBLANES):
                i = pl.multiple_of(g * SUBLANES, SUBLANES) + jj
                _tile_copy(ys_ref, d0_ref[i], ybuf.at[0, 0], i * ROW_GROUP, sems.at[0]).start()
                _tile_copy(ys_ref, d1_ref[i], ybuf.at[0, 1], i * ROW_GROUP, sems.at[0]).start()
            return c
        lax.fori_loop(0, tm // SUBLANES, issue, 0)

    nxt, sem_nxt = ybuf.at[1 - slot], sems.at[1 - slot]
    for i in range(tm):
        _tile_copy(ys_ref, d0n_ref[i], nxt.at[0], i * ROW_GROUP, sem_nxt).start(priority=0)
        _tile_copy(ys_ref, d1n_ref[i], nxt.at[1], i * ROW_GROUP, sem_nxt).start(priority=1)

    drain(sems.at[slot])
    w = w_ref[...]
    gt2 = mod_ref[...][5:6]
    out = x_ref[...] + gt2 * (w[:, 0:1] * _load_rows_from_tiles(ybuf, (slot, 0), tm)
                              + w[:, 1:2] * _load_rows_from_tiles(ybuf, (slot, 1), tm))
    if final_norm:
        out = _rms(out) * gf_ref[...]
    o_ref[...] = out

    @pl.when(last)
    def _():
        drain(sems.at[1 - slot])


def _combine(d0, d1, x, mod, wcol, g_final, ys, final_norm):
    bn, t, d = x.shape
    tm = min(MOVE_ROWS, t)
    nt = t // tm
    n_tiles = bn * nt

    def this_tile(b, i):
        return (b * nt + i,)

    def next_tile(b, i):
        return (jnp.minimum(b * nt + i + 1, n_tiles - 1),)

    return pl.pallas_call(
        functools.partial(_combine_kernel, final_norm=final_norm),
        grid=(bn, nt),
        in_specs=[
            pl.BlockSpec((tm,), this_tile, memory_space=pltpu.SMEM),
            pl.BlockSpec((tm,), this_tile, memory_space=pltpu.SMEM),
            pl.BlockSpec((tm,), next_tile, memory_space=pltpu.SMEM),
            pl.BlockSpec((tm,), next_tile, memory_space=pltpu.SMEM),
            pl.BlockSpec((None, tm, d), lambda b, i: (b, i, 0)),
            pl.BlockSpec((None, 6, d), lambda b, i: (b, 0, 0)),
            pl.BlockSpec((tm, 8), lambda b, i: (b * nt + i, 0)),
            pl.BlockSpec((1, d), lambda b, i: (0, 0)),
            pl.BlockSpec(memory_space=pl.ANY),
        ],
        out_specs=pl.BlockSpec((None, tm, d), lambda b, i: (b, i, 0)),
        out_shape=jax.ShapeDtypeStruct(x.shape, x.dtype),
        scratch_shapes=[pltpu.VMEM((2, TOP_K, tm * ROW_GROUP, LANES), jnp.uint32), pltpu.SemaphoreType.DMA((2,))],
        compiler_params=pltpu.CompilerParams(
            dimension_semantics=("arbitrary", "arbitrary"), vmem_limit_bytes=VMEM_LIMIT),
        name="combine",
    )(d0, d1, d0, d1, x, mod, wcol, g_final.reshape(1, d), ys)


def _block_map(counts, cap, n_blocks):
    blocks_per_expert = (counts + FFN_ROWS - 1) // FFN_ROWS
    ends = jnp.cumsum(blocks_per_expert)
    n_used = ends[-1:]
    step = jnp.minimum(jnp.arange(n_blocks, dtype=jnp.int32), n_used - 1)
    block_e = jnp.minimum(jnp.sum((ends[None, :] <= step[:, None]).astype(jnp.int32), axis=1), N_EXPERTS - 1)
    first = (ends - blocks_per_expert)[block_e]
    row_block = block_e * (cap // FFN_ROWS) + (step - first)
    return block_e.astype(jnp.int32), row_block.astype(jnp.int32), n_used.astype(jnp.int32)


def kernel(x, c, w_ada, b_ada, g_norm1, g_norm2, w_in, w_out, gmlp_ln_g, gmlp_ln_b, gmlp_ws, gmlp_bs,
           gla_w_gate, gla_b_gate, gla_norm_g, conv_w, conv_b, router_w, router_b, exp_w1, exp_w3, exp_w2,
           g_final):
    depth = w_ada.shape[0]
    bn, t, d = x.shape
    n = bn * t
    n_blocks = -(-(n * TOP_K) // FFN_ROWS) + N_EXPERTS
    mod_all = _modulation(c, w_ada, b_ada).reshape(depth, bn, 6, d)
    prev = None
    for l in range(depth):
        mod = mod_all[l]
        x, dest, wcol, cnt, xs = _mixer(
            x, mod, g_norm1[l], g_norm2[l], w_in[l], w_out[l], gmlp_ln_g[l], gmlp_ln_b[l], gmlp_ws[l],
            gmlp_bs[l], gla_w_gate[l], gla_b_gate[l], gla_norm_g[l], conv_w[l], conv_b[l], router_w, router_b,
            prev=prev)
        block_e, row_block, n_used = _block_map(cnt[:, 0], n, n_blocks)
        ys = _expert_ffn(l, block_e, row_block, n_used, xs, exp_w1, exp_w3, exp_w2)
        prev = (dest[0], dest[1], wcol, mod, ys)
    d0, d1, wcol, mod, ys = prev
    return _combine(d0, d1, x, mod, wcol, g_final, ys, final_norm=True)
```

```python
import functools

import jax
import jax.numpy as jnp
from jax import lax
from jax.experimental import pallas as pl
from jax.experimental.pallas import tpu as pltpu

CHUNK = 64
GMLP_BLOCK = 128
A_DIM = 256
A_HEADS = 4
B_HEADS = 4
B_DK = 64
B_DV = 128
B_KDIM = B_HEADS * B_DK
B_VDIM = B_HEADS * B_DV
GLA_RANK = 16
GLA_TAU = 16.0
C_DIM = 256
N_EXPERTS = 32
N_GROUPS = 4
EXPERTS_PER_GROUP = N_EXPERTS // N_GROUPS
TOP_K = 2
EPS = 1e-6

LANES = 128
SUBLANES = 8
OFF_AU, OFF_AV, OFF_Q, OFF_K, OFF_V = 0, 256, 512, 768, 1024
OFF_OG, OFF_CB, OFF_CC, OFF_CX, OFF_GLR = 1536, 2048, 2304, 2560, 2816
P_PAD = OFF_GLR + LANES

MIX_ROWS = 512
COPY_POINTS_PER_BLOCK = 18
MOVE_ROWS = 512
FFN_ROWS = 512
VMEM_LIMIT = 56 * 1024 * 1024

_NT = (((1,), (1,)), ((), ()))
_TN = (((0,), (0,)), ((), ()))


def _dot(a, b, dims=None):
    if dims is None:
        return jnp.dot(a, b, preferred_element_type=jnp.float32)
    return lax.dot_general(a, b, dims, preferred_element_type=jnp.float32)


def _bf(x):
    return x.astype(jnp.bfloat16)


def _split_bf16(x):
    hi = _bf(x)
    lo = _bf(x - hi.astype(jnp.float32))
    return hi, lo


def _rms(x):
    return x * lax.rsqrt(jnp.mean(x * x, axis=-1, keepdims=True) + EPS)


def _pack_rows(v):
    half = v.shape[1] // 2
    hi = pltpu.bitcast(_bf(v[:, :half]).astype(jnp.float32), jnp.uint32)
    lo = pltpu.bitcast(_bf(v[:, half:]).astype(jnp.float32), jnp.uint32)
    return hi | (lo >> 16)


def _unpack_rows(w):
    hi = pltpu.bitcast(w & jnp.uint32(0xFFFF0000), jnp.float32)
    lo = pltpu.bitcast(w << 16, jnp.float32)
    return _bf(hi), _bf(lo)


def _row_copy(src_ref, src_row, dst_ref, dst_row, sem):
    return pltpu.make_async_copy(src_ref.at[pl.ds(src_row, 1)], dst_ref.at[pl.ds(dst_row, 1)], sem)


ROW_GROUP = 4

def _aligned(row):
    return row if isinstance(row, int) else pl.multiple_of(row, ROW_GROUP)


def _tile_copy(src_ref, src_row, dst_ref, dst_row, sem):
    return pltpu.make_async_copy(src_ref.at[pl.ds(_aligned(src_row), ROW_GROUP)],
                                 dst_ref.at[pl.ds(_aligned(dst_row), ROW_GROUP)], sem)


def _store_rows_as_tiles(ref, value):
    rows = value.shape[0]
    assert value.shape[1] == ROW_GROUP * LANES
    for c in range(ROW_GROUP):
        ref[pl.ds(c, rows, stride=ROW_GROUP), :] = value[:, c * LANES:(c + 1) * LANES]


def _load_packed_rows(ref, idx, rows):
    return jnp.concatenate(
        [ref[idx + (pl.ds(c, rows, stride=ROW_GROUP), slice(None))] for c in range(ROW_GROUP)], axis=1)


def _load_rows_from_tiles(ref, idx, rows):
    hi, lo = _unpack_rows(_load_packed_rows(ref, idx, rows))
    return jnp.concatenate([hi.astype(jnp.float32), lo.astype(jnp.float32)], axis=1)


def _mod_kernel(c_ref, w_ref, b_ref, o_ref):
    c = c_ref[...]
    s = c * jax.nn.sigmoid(c)
    s_hi, s_lo = _split_bf16(s)
    w_hi, w_lo = _split_bf16(w_ref[...])
    acc = _dot(s_hi, w_hi) + _dot(s_hi, w_lo) + _dot(s_lo, w_hi)
    o_ref[...] = acc + b_ref[...]


def _modulation(c, w_ada, b_ada):
    depth, d, six_d = w_ada.shape
    bn = c.shape[0]
    cb = 2048
    return pl.pallas_call(
        _mod_kernel,
        grid=(depth, six_d // cb),
        in_specs=[
            pl.BlockSpec((bn, d), lambda l, j: (0, 0)),
            pl.BlockSpec((None, d, cb), lambda l, j: (l, 0, j)),
            pl.BlockSpec((None, 1, cb), lambda l, j: (l, 0, j)),
        ],
        out_specs=pl.BlockSpec((None, bn, cb), lambda l, j: (l, 0, j)),
        out_shape=jax.ShapeDtypeStruct((depth, bn, six_d), jnp.float32),
        compiler_params=pltpu.CompilerParams(
            dimension_semantics=("arbitrary", "arbitrary"), vmem_limit_bytes=VMEM_LIMIT),
        name="adaln_mod",
    )(c, w_ada, b_ada.reshape(depth, 1, six_d))


def _route(h2, rwt_ref, rb_ref, carry, cap):
    tr = h2.shape[0]
    h_hi, h_lo = _split_bf16(h2)
    w_hi, w_lo = _split_bf16(rwt_ref[...])
    logits = (_dot(w_hi, h_hi, _NT) + _dot(w_hi, h_lo, _NT) + _dot(w_lo, h_hi, _NT)) + rb_ref[...]
    ex = jnp.exp(logits - jnp.max(logits, axis=0, keepdims=True))
    probs = ex / jnp.sum(ex, axis=0, keepdims=True)

    idx8 = lax.broadcasted_iota(jnp.int32, (EXPERTS_PER_GROUP, tr), 0)
    best = None
    for g in range(N_GROUPS):
        pg = probs[g * EXPERTS_PER_GROUP:(g + 1) * EXPERTS_PER_GROUP]
        m1 = jnp.max(pg, axis=0, keepdims=True)
        i1 = jnp.min(jnp.where(pg == m1, idx8, EXPERTS_PER_GROUP), axis=0, keepdims=True)
        pg2 = jnp.where(idx8 == i1, -1.0, pg)
        m2 = jnp.max(pg2, axis=0, keepdims=True)
        i2 = jnp.min(jnp.where(pg2 == m2, idx8, EXPERTS_PER_GROUP), axis=0, keepdims=True)
        cand = (m1 + m2, m1, m2, i1 + g * EXPERTS_PER_GROUP, i2 + g * EXPERTS_PER_GROUP)
        if best is None:
            best = cand
        else:
            better = cand[0] > best[0]
            best = tuple(jnp.where(better, a, b) for a, b in zip(cand, best))
    _, p1, p2, e0, e1 = best
    denom = p1 + p2
    w0, w1 = p1 / denom, p2 / denom

    eidx = lax.broadcasted_iota(jnp.int32, (N_EXPERTS, tr), 0)
    hit0, hit1 = eidx == e0, eidx == e1
    onehot = jnp.where(jnp.logical_or(hit0, hit1), 1.0, 0.0)
    before = (lax.broadcasted_iota(jnp.int32, (tr, tr), 0) < lax.broadcasted_iota(jnp.int32, (tr, tr), 1))
    rank = _dot(_bf(onehot), _bf(jnp.where(before, 1.0, 0.0))) + carry
    r0 = jnp.sum(jnp.where(hit0, rank, 0.0), axis=0, keepdims=True).astype(jnp.int32)
    r1 = jnp.sum(jnp.where(hit1, rank, 0.0), axis=0, keepdims=True).astype(jnp.int32)
    carry = carry + jnp.sum(onehot, axis=1, keepdims=True)

    eye = lax.broadcasted_iota(jnp.int32, (tr, tr), 0) == lax.broadcasted_iota(jnp.int32, (tr, tr), 1)
    w0c = jnp.sum(jnp.where(eye, jnp.broadcast_to(w0, (tr, tr)), 0.0), axis=1, keepdims=True)
    w1c = jnp.sum(jnp.where(eye, jnp.broadcast_to(w1, (tr, tr)), 0.0), axis=1, keepdims=True)
    lane8 = lax.broadcasted_iota(jnp.int32, (tr, 8), 1)
    wcol = jnp.where(lane8 == 0, w0c, jnp.where(lane8 == 1, w1c, 0.0))
    return (e0 * cap + r0) * ROW_GROUP, (e1 * cap + r1) * ROW_GROUP, wcol, carry


def _zero_segment_padding(cnt_ref, cap, zero_ref, xs_ref, sem):
    copies = []
    for ex in range(N_EXPERTS):
        cnt = cnt_ref[ex, 0]
        seg = ex * cap
        pos = seg + cnt
        end8 = seg + ((cnt + SUBLANES - 1) // SUBLANES) * SUBLANES
        for j in range(SUBLANES - 1):
            copies.append((pos + j < end8, _tile_copy(zero_ref, 0, xs_ref, (pos + j) * ROW_GROUP, sem)))
        seg_end = seg + ((cnt + FFN_ROWS - 1) // FFN_ROWS) * FFN_ROWS
        groups = (seg_end - end8) // SUBLANES
        at = end8
        bit = FFN_ROWS // (2 * SUBLANES)
        while bit >= 1:
            rows = bit * SUBLANES
            pred = (groups & bit) != 0
            copies.append((pred, pltpu.make_async_copy(
                zero_ref.at[pl.ds(0, rows * ROW_GROUP)],
                xs_ref.at[pl.ds(pl.multiple_of(at * ROW_GROUP, SUBLANES), rows * ROW_GROUP)], sem)))
            at = at + jnp.where(pred, rows, 0)
            bit //= 2
    for pred, cp in copies:
        pl.when(pred)(cp.start)
    for pred, cp in copies:
        pl.when(pred)(cp.wait)


N_MIXER_INPUTS = 19
N_MIXER_OUTPUTS = 5


def _mixer_kernel(*refs, cap, fuse_prev):
    ins, rest = refs[:N_MIXER_INPUTS], refs[N_MIXER_INPUTS:]
    if fuse_prev:
        dp0_ref, dp1_ref, dp0n_ref, dp1n_ref, wp_ref, modp_ref, ysp_ref = rest[:7]
        rest = rest[7:]
    outs, scratch = rest[:N_MIXER_OUTPUTS], rest[N_MIXER_OUTPUTS:]
    (x_ref, mod_ref, g1_ref, wina_ref, winb_ref, winc_ref, wout_ref, lng_ref, lnb_ref, ws_ref, bsm_ref,
     wg_ref, bg_ref, glag_ref, cw_ref, cb_ref, g2_ref, rwt_ref, rb_ref) = ins
    o_ref, dest_ref, w_ref, cnt_ref, xs_ref = outs
    (h_ref, pa_ref, pb_ref, y0_ref, y1_ref, y2_ref, y3_ref, s_ref, zc_ref, wsm_ref, h2_ref,
     carry_ref, dvm_ref, d0s_ref, d1s_ref, cvm_ref, csm_ref, zero_ref, sem, sem_s, sem_z,
     ybuf, xin_ref, sem_g) = scratch

    bi, ti = pl.program_id(0), pl.program_id(1)
    tm = x_ref.shape[0]
    n_blk = tm // GMLP_BLOCK
    step = bi * pl.num_programs(1) + ti
    last_step = step == pl.num_programs(0) * pl.num_programs(1) - 1
    slot = step % 2
    spare_row0 = N_EXPERTS * cap

    def drain_rows(src_ref, dst_ref, s, copy=_row_copy):
        def body(g, c):
            for _ in range(TOP_K * SUBLANES):
                copy(src_ref, 0, dst_ref, 0, s).wait()
            return c
        lax.fori_loop(0, tm // SUBLANES, body, 0)

    @pl.when(step > 0)
    def _():
        pltpu.make_async_copy(dvm_ref.at[0], d0s_ref, sem_s).wait()
        pltpu.make_async_copy(dvm_ref.at[1], d1s_ref, sem_s).wait()

    if fuse_prev:
        @pl.when(step == 0)
        def _():
            def issue(g, c):
                for jj in range(SUBLANES):
                    i = pl.multiple_of(g * SUBLANES, SUBLANES) + jj
                    _tile_copy(ysp_ref, dp0_ref[i], ybuf.at[0, 0], i * ROW_GROUP, sem_g.at[0]).start()
                    _tile_copy(ysp_ref, dp1_ref[i], ybuf.at[0, 1], i * ROW_GROUP, sem_g.at[0]).start()
                return c
            lax.fori_loop(0, tm // SUBLANES, issue, 0)

        drain_rows(ysp_ref, ybuf.at[0, 0], sem_g.at[slot], _tile_copy)
        wp = wp_ref[...]
        xin_ref[...] = x_ref[...] + modp_ref[...][5:6] * (
            wp[:, 0:1] * _load_rows_from_tiles(ybuf, (slot, 0), tm)
            + wp[:, 1:2] * _load_rows_from_tiles(ybuf, (slot, 1), tm))
        x_in = xin_ref
    else:
        x_in = x_ref

    @pl.when(step == 0)
    def _():
        carry_ref[...] = jnp.zeros_like(carry_ref)
        h2_ref[...] = jnp.zeros_like(h2_ref)

        def fill(i, c):
            d0s_ref[i] = (spare_row0 + i) * ROW_GROUP
            d1s_ref[i] = (spare_row0 + tm + i) * ROW_GROUP
            return c

        lax.fori_loop(0, tm, fill, 0)
        tt = lax.broadcasted_iota(jnp.int32, (GMLP_BLOCK, GMLP_BLOCK), 0) // CHUNK
        ss = lax.broadcasted_iota(jnp.int32, (GMLP_BLOCK, GMLP_BLOCK), 1) // CHUNK
        for h in range(A_HEADS):
            wsm_ref[h] = _bf(jnp.where(tt >= ss, ws_ref[h], 0.0))

    @pl.when(ti == 0)
    def _():
        s_ref[...] = jnp.zeros_like(s_ref)
        zc_ref[...] = jnp.zeros_like(zc_ref)

    m = mod_ref[...]
    sh1, sc1, gt1, sh2, sc2 = m[0:1], m[1:2], m[2:3], m[3:4], m[4:5]
    h_ref[...] = _bf((_rms(x_in[...]) * g1_ref[...]) * (1.0 + sc1) + sh1)

    lane256 = lax.broadcasted_iota(jnp.int32, (CHUNK, B_KDIM), 1)
    head_of_lane = lane256 // B_DK
    r64 = lax.broadcasted_iota(jnp.int32, (CHUNK, CHUNK), 0)
    c64 = lax.broadcasted_iota(jnp.int32, (CHUNK, CHUNK), 1)
    tri = _bf(jnp.where(r64 >= c64, 1.0, 0.0))
    causal4 = jnp.concatenate([r64 >= c64] * B_HEADS, axis=0)
    eye256 = (lax.broadcasted_iota(jnp.int32, (B_KDIM, B_KDIM), 0)
              == lax.broadcasted_iota(jnp.int32, (B_KDIM, B_KDIM), 1))
    a_head_of_lane = lax.broadcasted_iota(jnp.int32, (GMLP_BLOCK, A_DIM), 1) // (A_DIM // A_HEADS)
    row128 = lax.broadcasted_iota(jnp.int32, (GMLP_BLOCK, C_DIM), 0)
    sqrt_half = 0.7071067811865476

    def gelu(v):
        return 0.5 * v * (1.0 + lax.erf(v * sqrt_half))

    h2_prev = h2_ref.at[1 - slot]
    row_copies = []
    for i in range(tm):
        row_copies.append(
            lambda i=i: _tile_copy(h2_prev, i * ROW_GROUP, xs_ref, d0s_ref[i], sem).start(priority=0))
        row_copies.append(
            lambda i=i: _tile_copy(h2_prev, i * ROW_GROUP, xs_ref, d1s_ref[i], sem).start(priority=1))
        if fuse_prev:
            row_copies.append(lambda i=i: _tile_copy(
                ysp_ref, dp0n_ref[i], ybuf.at[1 - slot, 0], i * ROW_GROUP, sem_g.at[1 - slot]).start(priority=0))
            row_copies.append(lambda i=i: _tile_copy(
                ysp_ref, dp1n_ref[i], ybuf.at[1 - slot, 1], i * ROW_GROUP, sem_g.at[1 - slot]).start(priority=1))
    row_copies.reverse()
    copies_per_point = -(-len(row_copies) // (n_blk * COPY_POINTS_PER_BLOCK))

    def start_row_copies():
        for _ in range(min(copies_per_point, len(row_copies))):
            row_copies.pop()()

    p_halves = (pa_ref, pb_ref)
    y_blocks = (y0_ref, y1_ref, y2_ref, y3_ref)
    assert n_blk == len(y_blocks)

    def project(half):
        rows = slice(half * (tm // 2), (half + 1) * (tm // 2))
        hb = h_ref[rows, :]
        p_halves[half][:, 0:OFF_OG] = _dot(hb, wina_ref[...])
        p_halves[half][:, OFF_OG:OFF_GLR] = _dot(hb, winb_ref[...])
        p_halves[half][:, OFF_GLR:P_PAD] = _dot(hb, winc_ref[...])

    def block(j, state, zprev):
        p_ref, y_ref = p_halves[j // (n_blk // 2)], y_blocks[j]
        r0 = (j % (n_blk // 2)) * GMLP_BLOCK
        rows = slice(r0, r0 + GMLP_BLOCK)

        u = gelu(p_ref[rows, OFF_AU:OFF_AU + A_DIM])
        vv = gelu(p_ref[rows, OFF_AV:OFF_AV + A_DIM])
        mu = jnp.mean(vv, axis=-1, keepdims=True)
        var = jnp.mean((vv - mu) ** 2, axis=-1, keepdims=True)
        vn = _bf((vv - mu) * lax.rsqrt(var + EPS) * lng_ref[...] + lnb_ref[...])
        sv = jnp.zeros((GMLP_BLOCK, A_DIM), jnp.float32)
        for hh in range(A_HEADS):
            sv = jnp.where(a_head_of_lane == hh, _dot(wsm_ref[hh], vn), sv)
        y_ref[:, 0:A_DIM] = _bf(u * (sv + bsm_ref[...]))
        start_row_copies()

        z = _dot(_bf(p_ref[rows, OFF_GLR:OFF_GLR + LANES]), wg_ref[...]) + bg_ref[...]
        glog = (jnp.minimum(z, 0.0) - jnp.log1p(jnp.exp(-jnp.abs(z)))) / GLA_TAU
        def chunk_head(c):
            rc = slice(r0 + c * CHUNK, r0 + (c + 1) * CHUNK)
            g = glog[c * CHUNK:(c + 1) * CHUNK]
            g_hi = _bf(g)
            g_r1 = g - g_hi.astype(jnp.float32)
            g_mid = _bf(g_r1)
            g_lo = _bf(g_r1 - g_mid.astype(jnp.float32))
            b = _dot(tri, g_hi) + _dot(tri, g_mid) + _dot(tri, g_lo)
            b_mid = b[CHUNK // 2:CHUNK // 2 + 1]
            b_last = b[CHUNK - 1:CHUNK]
            q = p_ref[rc, OFF_Q:OFF_Q + B_KDIM] * (B_DK ** -0.5)
            k = p_ref[rc, OFF_K:OFF_K + B_KDIM]
            vb = _bf(p_ref[rc, OFF_V:OFF_V + B_VDIM])
            qs = q * jnp.exp(b - b_mid)
            ks = _bf(k * jnp.exp(b_mid - b))
            kd = _bf(k * jnp.exp(b_last - b))
            qb = q * jnp.exp(b)
            qs_st = _bf(jnp.concatenate(
                [jnp.where(head_of_lane == hh, qs, 0.0) for hh in range(B_HEADS)], axis=0))
            qb_st = _bf(jnp.concatenate(
                [jnp.where(head_of_lane == hh, qb, 0.0) for hh in range(B_HEADS)], axis=0))
            scores = _bf(jnp.where(causal4, _dot(qs_st, ks, _NT), 0.0))
            start_row_copies()
            kv_all = _dot(kd, vb, _TN)
            start_row_copies()
            o_intra, kvs = [], []
            for hh in range(B_HEADS):
                rs = slice(hh * CHUNK, (hh + 1) * CHUNK)
                cs = slice(hh * B_DV, (hh + 1) * B_DV)
                o_intra.append(_dot(scores[rs], vb[:, cs]))
                kvs.append(kv_all[rs, cs])
                start_row_copies()
            decay_row = jnp.broadcast_to(jnp.exp(b_last), (B_KDIM, B_KDIM))
            decay_col = jnp.sum(jnp.where(eye256, decay_row, 0.0), axis=1, keepdims=True)
            og = p_ref[rc, OFF_OG:OFF_OG + B_VDIM]
            return qb_st, o_intra, jnp.concatenate(kvs, axis=0), decay_col, og * jax.nn.sigmoid(og)

        heads = [chunk_head(c) for c in range(GMLP_BLOCK // CHUNK)]
        for c, (qb_st, o_intra, kv, decay_col, out_gate) in enumerate(heads):
            o_inter = _dot(qb_st, _bf(state))
            state = decay_col * state + kv
            outs = [_rms(o_intra[hh] + o_inter[hh * CHUNK:(hh + 1) * CHUNK]) for hh in range(B_HEADS)]
            on = jnp.concatenate(outs, axis=1) * glag_ref[...]
            y_ref[c * CHUNK:(c + 1) * CHUNK, A_DIM:A_DIM + B_VDIM] = _bf(on * out_gate)
            start_row_copies()

        zz = p_ref[rows, OFF_CC:OFF_CC + C_DIM] * p_ref[rows, OFF_CX:OFF_CX + C_DIM]
        z1 = jnp.where(row128 == 0, zprev[7:8], pltpu.roll(zz, 1, 0))
        z2 = jnp.where(row128 == 0, zprev[6:7], jnp.where(row128 == 1, zprev[7:8], pltpu.roll(zz, 2, 0)))
        cw = cw_ref[...]
        yc = cb_ref[...] + cw[0:1] * z2
        yc = yc + cw[1:2] * z1
        yc = yc + cw[2:3] * zz
        y_ref[:, A_DIM + B_VDIM:A_DIM + B_VDIM + C_DIM] = _bf(p_ref[rows, OFF_CB:OFF_CB + C_DIM] * yc)
        start_row_copies()
        return state, zz[GMLP_BLOCK - SUBLANES:GMLP_BLOCK]

    state, zprev, carry = s_ref[...], zc_ref[...], carry_ref[...]
    project(0)
    for j in range(n_blk):
        if j == 0:
            project(1)
        state, zprev = block(j, state, zprev)
        rows = slice(j * GMLP_BLOCK, (j + 1) * GMLP_BLOCK)
        xo = x_in[rows, :] + gt1 * _dot(y_blocks[j][...], wout_ref[...])
        o_ref[rows, :] = xo
        start_row_copies()
        h2 = (_rms(xo) * g2_ref[...]) * (1.0 + sc2) + sh2
        _store_rows_as_tiles(
            h2_ref.at[slot, pl.ds(j * GMLP_BLOCK * ROW_GROUP, GMLP_BLOCK * ROW_GROUP)], _pack_rows(h2))
        d0, d1, wcol, carry = _route(h2, rwt_ref, rb_ref, carry, cap)
        dvm_ref[0:1, rows] = d0
        dvm_ref[1:2, rows] = d1
        w_ref[rows, :] = wcol
        start_row_copies()
    while row_copies:
        row_copies.pop()()
    s_ref[...] = state
    zc_ref[...] = zprev
    carry_ref[...] = carry
    cnt = jnp.broadcast_to(carry, cnt_ref.shape).astype(jnp.int32)
    cnt_ref[...] = cnt
    dest_ref[...] = dvm_ref[0:TOP_K, :]

    drain_rows(h2_prev, xs_ref, sem, _tile_copy)

    to_smem = [pltpu.make_async_copy(dvm_ref.at[0], d0s_ref, sem_s),
               pltpu.make_async_copy(dvm_ref.at[1], d1s_ref, sem_s)]
    for cp in to_smem:
        cp.start()

    @pl.when(last_step)
    def _():
        for cp in to_smem:
            cp.wait()
        h2_last = h2_ref.at[slot]

        def issue(g, c):
            for jj in range(SUBLANES):
                i = pl.multiple_of(g * SUBLANES, SUBLANES) + jj
                _tile_copy(h2_last, i * ROW_GROUP, xs_ref, d0s_ref[i], sem).start()
                _tile_copy(h2_last, i * ROW_GROUP, xs_ref, d1s_ref[i], sem).start()
            return c

        lax.fori_loop(0, tm // SUBLANES, issue, 0)
        drain_rows(h2_last, xs_ref, sem, _tile_copy)
        if fuse_prev:
            drain_rows(ysp_ref, ybuf.at[0, 0], sem_g.at[1 - slot], _tile_copy)
        cvm_ref[...] = cnt
        zero_ref[...] = jnp.zeros_like(zero_ref)
        cp = pltpu.make_async_copy(cvm_ref, csm_ref, sem_s)
        cp.start()
        cp.wait()
        _zero_segment_padding(csm_ref, cap, zero_ref, xs_ref, sem_z)


def _mixer(x, mod, g1, g2, w_in, w_out, ln_g, ln_b, w_s, b_s, w_gate, b_gate, gla_g, conv_w, conv_b,
           router_w, router_b, prev=None):
    bn, t, d = x.shape
    n = bn * t
    tm = min(MIX_ROWS, t)
    nt = t // tm
    cap = n
    glr0 = 2 * A_DIM + 2 * B_KDIM + B_VDIM
    w_in_a = _bf(w_in[:, :glr0])
    w_in_b = _bf(w_in[:, glr0 + GLA_RANK:])
    w_in_c = jnp.concatenate(
        [_bf(w_in[:, glr0:glr0 + GLA_RANK]), jnp.zeros((d, LANES - GLA_RANK), jnp.bfloat16)], axis=1)
    w_gate_p = jnp.concatenate(
        [_bf(w_gate), jnp.zeros((LANES - GLA_RANK, B_KDIM), jnp.bfloat16)], axis=0)
    bsm = jnp.repeat(b_s.T, A_DIM // A_HEADS, axis=1)

    def whole(shape):
        return pl.BlockSpec(shape, lambda b, i: (0,) * len(shape))

    prev_specs, prev_args = [], ()
    if prev is not None:
        n_tiles = bn * nt

        def this_tile(b, i):
            return (b * nt + i,)

        def next_tile(b, i):
            return (jnp.minimum(b * nt + i + 1, n_tiles - 1),)

        d0p, d1p, wcolp, modp, ysp = prev
        prev_specs = [
            pl.BlockSpec((tm,), this_tile, memory_space=pltpu.SMEM),
            pl.BlockSpec((tm,), this_tile, memory_space=pltpu.SMEM),
            pl.BlockSpec((tm,), next_tile, memory_space=pltpu.SMEM),
            pl.BlockSpec((tm,), next_tile, memory_space=pltpu.SMEM),
            pl.BlockSpec((tm, 8), lambda b, i: (b * nt + i, 0)),
            pl.BlockSpec((None, 6, d), lambda b, i: (b, 0, 0)),
            pl.BlockSpec(memory_space=pl.ANY),
        ]
        prev_args = (d0p, d1p, d0p, d1p, wcolp, modp, ysp)

    return pl.pallas_call(
        functools.partial(_mixer_kernel, cap=cap, fuse_prev=prev is not None),
        grid=(bn, nt),
        in_specs=[
            pl.BlockSpec((None, tm, d), lambda b, i: (b, i, 0)),
            pl.BlockSpec((None, 6, d), lambda b, i: (b, 0, 0)),
            whole((1, d)),
            whole((d, OFF_OG)),
            whole((d, OFF_GLR - OFF_OG)),
            whole((d, LANES)),
            whole((d, d)),
            whole((1, A_DIM)),
            whole((1, A_DIM)),
            whole((A_HEADS, GMLP_BLOCK, GMLP_BLOCK)),
            whole((GMLP_BLOCK, A_DIM)),
            whole((LANES, B_KDIM)),
            whole((1, B_KDIM)),
            whole((1, B_VDIM)),
            whole((3, C_DIM)),
            whole((1, C_DIM)),
            whole((1, d)),
            whole((N_EXPERTS, d)),
            whole((N_EXPERTS, 1)),
        ] + prev_specs,
        out_specs=[
            pl.BlockSpec((None, tm, d), lambda b, i: (b, i, 0)),
            pl.BlockSpec((TOP_K, tm), lambda b, i: (0, b * nt + i)),
            pl.BlockSpec((tm, 8), lambda b, i: (b * nt + i, 0)),
            whole((N_EXPERTS, LANES)),
            pl.BlockSpec(memory_space=pl.ANY),
        ],
        out_shape=[
            jax.ShapeDtypeStruct(x.shape, x.dtype),
            jax.ShapeDtypeStruct((TOP_K, n), jnp.int32),
            jax.ShapeDtypeStruct((n, 8), jnp.float32),
            jax.ShapeDtypeStruct((N_EXPERTS, LANES), jnp.int32),
            jax.ShapeDtypeStruct(((N_EXPERTS * cap + TOP_K * tm) * ROW_GROUP, LANES), jnp.uint32),
        ],
        scratch_shapes=[
            pltpu.VMEM((tm, d), jnp.bfloat16),
            pltpu.VMEM((tm // 2, P_PAD), jnp.float32),
            pltpu.VMEM((tm // 2, P_PAD), jnp.float32),
            pltpu.VMEM((GMLP_BLOCK, d), jnp.bfloat16),
            pltpu.VMEM((GMLP_BLOCK, d), jnp.bfloat16),
            pltpu.VMEM((GMLP_BLOCK, d), jnp.bfloat16),
            pltpu.VMEM((GMLP_BLOCK, d), jnp.bfloat16),
            pltpu.VMEM((B_KDIM, B_DV), jnp.float32),
            pltpu.VMEM((SUBLANES, C_DIM), jnp.float32),
            pltpu.VMEM((A_HEADS, GMLP_BLOCK, GMLP_BLOCK), jnp.bfloat16),
            pltpu.VMEM((2, tm * ROW_GROUP, LANES), jnp.uint32),
            pltpu.VMEM((N_EXPERTS, 1), jnp.float32),
            pltpu.VMEM((SUBLANES, tm), jnp.int32),
            pltpu.SMEM((tm,), jnp.int32),
            pltpu.SMEM((tm,), jnp.int32),
            pltpu.VMEM((N_EXPERTS, LANES), jnp.int32),
            pltpu.SMEM((N_EXPERTS, LANES), jnp.int32),
            pltpu.VMEM((FFN_ROWS // 2 * ROW_GROUP, LANES), jnp.uint32),
            pltpu.SemaphoreType.DMA,
            pltpu.SemaphoreType.DMA,
            pltpu.SemaphoreType.DMA,
            pltpu.VMEM((2, TOP_K, tm * ROW_GROUP, LANES), jnp.uint32),
            pltpu.VMEM((tm, d), jnp.float32),
            pltpu.SemaphoreType.DMA((2,)),
        ],
        compiler_params=pltpu.CompilerParams(
            dimension_semantics=("arbitrary", "arbitrary"), vmem_limit_bytes=VMEM_LIMIT),
        name="mixer",
    )(x, mod, g1.reshape(1, d), w_in_a, w_in_b, w_in_c, _bf(w_out), ln_g.reshape(1, -1), ln_b.reshape(1, -1), w_s, bsm,
      w_gate_p, b_gate.reshape(1, -1), gla_g.reshape(1, -1), conv_w, conv_b.reshape(1, -1),
      g2.reshape(1, d), router_w.T, router_b.reshape(N_EXPERTS, 1), *prev_args)


def _ffn_kernel(be_ref, rb_ref, nb_ref, xs_ref, w1_ref, w3_ref, w2_ref, ys_ref, w1b_ref, w3b_ref, w2b_ref):
    del rb_ref
    i = pl.program_id(0)
    used = i < nb_ref[0]
    new_expert = jnp.logical_or(i == 0, be_ref[i] != be_ref[jnp.maximum(i - 1, 0)])

    @pl.when(jnp.logical_and(used, new_expert))
    def _():
        w1b_ref[...] = _bf(w1_ref[...])
        w3b_ref[...] = _bf(w3_ref[...])
        w2b_ref[...] = _bf(w2_ref[...])

    @pl.when(used)
    def _():
        x_a, x_b = _unpack_rows(_load_packed_rows(xs_ref, (), FFN_ROWS))
        half = x_a.shape[1]
        a = _dot(x_a, w1b_ref[0:half, :]) + _dot(x_b, w1b_ref[half:, :])
        g = _dot(x_a, w3b_ref[0:half, :]) + _dot(x_b, w3b_ref[half:, :])
        hm = _bf((a * jax.nn.sigmoid(a)) * g)
        _store_rows_as_tiles(ys_ref, _pack_rows(_dot(hm, w2b_ref[...])))


def _expert_ffn(layer, block_e, row_block, n_used, xs, w1, w3, w2):
    n_rows = xs.shape[0] // ROW_GROUP
    n_blocks = block_e.shape[0]
    d, f = w1.shape[-2], w1.shape[-1]
    return pl.pallas_call(
        _ffn_kernel,
        grid_spec=pltpu.PrefetchScalarGridSpec(
            num_scalar_prefetch=3,
            grid=(n_blocks,),
            in_specs=[
                pl.BlockSpec((FFN_ROWS * ROW_GROUP, LANES), lambda i, be, rb, nb: (rb[i], 0)),
                pl.BlockSpec((None, None, d, f), lambda i, be, rb, nb: (layer, be[i], 0, 0)),
                pl.BlockSpec((None, None, d, f), lambda i, be, rb, nb: (layer, be[i], 0, 0)),
                pl.BlockSpec((None, None, f, d), lambda i, be, rb, nb: (layer, be[i], 0, 0)),
            ],
            out_specs=pl.BlockSpec((FFN_ROWS * ROW_GROUP, LANES), lambda i, be, rb, nb: (rb[i], 0)),
            scratch_shapes=[pltpu.VMEM((d, f), jnp.bfloat16), pltpu.VMEM((d, f), jnp.bfloat16),
                            pltpu.VMEM((f, d), jnp.bfloat16)],
        ),
        out_shape=jax.ShapeDtypeStruct((n_rows * ROW_GROUP, LANES), jnp.uint32),
        compiler_params=pltpu.CompilerParams(
            dimension_semantics=("arbitrary",), vmem_limit_bytes=VMEM_LIMIT),
        name="expert_ffn",
    )(block_e, row_block, n_used, xs, w1, w3, w2)


def _combine_kernel(d0_ref, d1_ref, d0n_ref, d1n_ref, x_ref, mod_ref, w_ref, gf_ref, ys_ref, o_ref, ybuf, sems,
                    *, final_norm):
    tm = x_ref.shape[0]
    step = pl.program_id(0) * pl.num_programs(1) + pl.program_id(1)
    last = step == pl.num_programs(0) * pl.num_programs(1) - 1
    slot = step % 2

    def drain(sem):
        def body(g, c):
            for _ in range(TOP_K * SUBLANES):
                _tile_copy(ys_ref, 0, ybuf.at[0, 0], 0, sem).wait()
            return c
        lax.fori_loop(0, tm // SUBLANES, body, 0)

    @pl.when(step == 0)
    def _():
        def issue(g, c):
            for jj in range(SUBLANES):
                i = pl.multiple_of(g * SUBLANES, SUBLANES) + jj
                _tile_copy(ys_ref, d0_ref[i], ybuf.at[0, 0], i * ROW_GROUP, sems.at[0]).start()
                _tile_copy(ys_ref, d1_ref[i], ybuf.at[0, 1], i * ROW_GROUP, sems.at[0]).start()
            return c
        lax.fori_loop(0, tm // SUBLANES, issue, 0)

    nxt, sem_nxt = ybuf.at[1 - slot], sems.at[1 - slot]
    for i in range(tm):
        _tile_copy(ys_ref, d0n_ref[i], nxt.at[0], i * ROW_GROUP, sem_nxt).start(priority=0)
        _tile_copy(ys_ref, d1n_ref[i], nxt.at[1], i * ROW_GROUP, sem_nxt).start(priority=1)

    drain(sems.at[slot])
    w = w_ref[...]
    gt2 = mod_ref[...][5:6]
    out = x_ref[...] + gt2 * (w[:, 0:1] * _load_rows_from_tiles(ybuf, (slot, 0), tm)
                              + w[:, 1:2] * _load_rows_from_tiles(ybuf, (slot, 1), tm))
    if final_norm:
        out = _rms(out) * gf_ref[...]
    o_ref[...] = out

    @pl.when(last)
    def _():
        drain(sems.at[1 - slot])


def _combine(d0, d1, x, mod, wcol, g_final, ys, final_norm):
    bn, t, d = x.shape
    tm = min(MOVE_ROWS, t)
    nt = t // tm
    n_tiles = bn * nt

    def this_tile(b, i):
        return (b * nt + i,)

    def next_tile(b, i):
        return (jnp.minimum(b * nt + i + 1, n_tiles - 1),)

    return pl.pallas_call(
        functools.partial(_combine_kernel, final_norm=final_norm),
        grid=(bn, nt),
        in_specs=[
            pl.BlockSpec((tm,), this_tile, memory_space=pltpu.SMEM),
            pl.BlockSpec((tm,), this_tile, memory_space=pltpu.SMEM),
            pl.BlockSpec((tm,), next_tile, memory_space=pltpu.SMEM),
            pl.BlockSpec((tm,), next_tile, memory_space=pltpu.SMEM),
            pl.BlockSpec((None, tm, d), lambda b, i: (b, i, 0)),
            pl.BlockSpec((None, 6, d), lambda b, i: (b, 0, 0)),
            pl.BlockSpec((tm, 8), lambda b, i: (b * nt + i, 0)),
            pl.BlockSpec((1, d), lambda b, i: (0, 0)),
            pl.BlockSpec(memory_space=pl.ANY),
        ],
        out_specs=pl.BlockSpec((None, tm, d), lambda b, i: (b, i, 0)),
        out_shape=jax.ShapeDtypeStruct(x.shape, x.dtype),
        scratch_shapes=[pltpu.VMEM((2, TOP_K, tm * ROW_GROUP, LANES), jnp.uint32), pltpu.SemaphoreType.DMA((2,))],
        compiler_params=pltpu.CompilerParams(
            dimension_semantics=("arbitrary", "arbitrary"), vmem_limit_bytes=VMEM_LIMIT),
        name="combine",
    )(d0, d1, d0, d1, x, mod, wcol, g_final.reshape(1, d), ys)


def _block_map(counts, cap, n_blocks):
    blocks_per_expert = (counts + FFN_ROWS - 1) // FFN_ROWS
    ends = jnp.cumsum(blocks_per_expert)
    n_used = ends[-1:]
    step = jnp.minimum(jnp.arange(n_blocks, dtype=jnp.int32), n_used - 1)
    block_e = jnp.minimum(jnp.sum((ends[None, :] <= step[:, None]).astype(jnp.int32), axis=1), N_EXPERTS - 1)
    first = (ends - blocks_per_expert)[block_e]
    row_block = block_e * (cap // FFN_ROWS) + (step - first)
    return block_e.astype(jnp.int32), row_block.astype(jnp.int32), n_used.astype(jnp.int32)


def kernel(x, c, w_ada, b_ada, g_norm1, g_norm2, w_in, w_out, gmlp_ln_g, gmlp_ln_b, gmlp_ws, gmlp_bs,
           gla_w_gate, gla_b_gate, gla_norm_g, conv_w, conv_b, router_w, router_b, exp_w1, exp_w3, exp_w2,
           g_final):
    depth = w_ada.shape[0]
    bn, t, d = x.shape
    n = bn * t
    n_blocks = -(-(n * TOP_K) // FFN_ROWS) + N_EXPERTS
    mod_all = _modulation(c, w_ada, b_ada).reshape(depth, bn, 6, d)
    prev = None
    for l in range(depth):
        mod = mod_all[l]
        x, dest, wcol, cnt, xs = _mixer(
            x, mod, g_norm1[l], g_norm2[l], w_in[l], w_out[l], gmlp_ln_g[l], gmlp_ln_b[l], gmlp_ws[l],
            gmlp_bs[l], gla_w_gate[l], gla_b_gate[l], gla_norm_g[l], conv_w[l], conv_b[l], router_w, router_b,
            prev=prev)
        block_e, row_block, n_used = _block_map(cnt[:, 0], n, n_blocks)
        ys = _expert_ffn(l, block_e, row_block, n_used, xs, exp_w1, exp_w3, exp_w2)
        prev = (dest[0], dest[1], wcol, mod, ys)
    d0, d1, wcol, mod, ys = prev
    return _combine(d0, d1, x, mod, wcol, g_final, ys, final_norm=True)
```

```python
import functools

import jax
import jax.numpy as jnp
from jax import lax
from jax.experimental import pallas as pl
from jax.experimental.pallas import tpu as pltpu

CHUNK = 64
GMLP_BLOCK = 128
A_DIM = 256
A_HEADS = 4
B_HEADS = 4
B_DK = 64
B_DV = 128
B_KDIM = B_HEADS * B_DK
B_VDIM = B_HEADS * B_DV
GLA_RANK = 16
GLA_TAU = 16.0
C_DIM = 256
N_EXPERTS = 32
N_GROUPS = 4
EXPERTS_PER_GROUP = N_EXPERTS // N_GROUPS
TOP_K = 2
EPS = 1e-6

LANES = 128
SUBLANES = 8
OFF_AU, OFF_AV, OFF_Q, OFF_K, OFF_V = 0, 256, 512, 768, 1024
OFF_OG, OFF_CB, OFF_CC, OFF_CX, OFF_GLR = 1536, 2048, 2304, 2560, 2816
P_PAD = OFF_GLR + LANES

MIX_ROWS = 512
COPY_POINTS_PER_BLOCK = 18
MOVE_ROWS = 512
FFN_ROWS = 512
VMEM_LIMIT = 56 * 1024 * 1024

_NT = (((1,), (1,)), ((), ()))
_TN = (((0,), (0,)), ((), ()))


def _dot(a, b, dims=None):
    if dims is None:
        return jnp.dot(a, b, preferred_element_type=jnp.float32)
    return lax.dot_general(a, b, dims, preferred_element_type=jnp.float32)


def _bf(x):
    return x.astype(jnp.bfloat16)


def _split_bf16(x):
    hi = _bf(x)
    lo = _bf(x - hi.astype(jnp.float32))
    return hi, lo


def _rms(x):
    return x * lax.rsqrt(jnp.mean(x * x, axis=-1, keepdims=True) + EPS)


def _pack_rows(v):
    half = v.shape[1] // 2
    hi = pltpu.bitcast(_bf(v[:, :half]).astype(jnp.float32), jnp.uint32)
    lo = pltpu.bitcast(_bf(v[:, half:]).astype(jnp.float32), jnp.uint32)
    return hi | (lo >> 16)


def _unpack_rows(w):
    hi = pltpu.bitcast(w & jnp.uint32(0xFFFF0000), jnp.float32)
    lo = pltpu.bitcast(w << 16, jnp.float32)
    return _bf(hi), _bf(lo)


def _row_copy(src_ref, src_row, dst_ref, dst_row, sem):
    return pltpu.make_async_copy(src_ref.at[pl.ds(src_row, 1)], dst_ref.at[pl.ds(dst_row, 1)], sem)


ROW_GROUP = 4

def _aligned(row):
    return row if isinstance(row, int) else pl.multiple_of(row, ROW_GROUP)


def _tile_copy(src_ref, src_row, dst_ref, dst_row, sem):
    return pltpu.make_async_copy(src_ref.at[pl.ds(_aligned(src_row), ROW_GROUP)],
                                 dst_ref.at[pl.ds(_aligned(dst_row), ROW_GROUP)], sem)


def _store_rows_as_tiles(ref, value):
    rows = value.shape[0]
    assert value.shape[1] == ROW_GROUP * LANES
    for c in range(ROW_GROUP):
        ref[pl.ds(c, rows, stride=ROW_GROUP), :] = value[:, c * LANES:(c + 1) * LANES]


def _load_packed_rows(ref, idx, rows):
    return jnp.concatenate(
        [ref[idx + (pl.ds(c, rows, stride=ROW_GROUP), slice(None))] for c in range(ROW_GROUP)], axis=1)


def _load_rows_from_tiles(ref, idx, rows):
    hi, lo = _unpack_rows(_load_packed_rows(ref, idx, rows))
    return jnp.concatenate([hi.astype(jnp.float32), lo.astype(jnp.float32)], axis=1)


def _mod_kernel(c_ref, w_ref, b_ref, o_ref):
    c = c_ref[...]
    s = c * jax.nn.sigmoid(c)
    s_hi, s_lo = _split_bf16(s)
    w_hi, w_lo = _split_bf16(w_ref[...])
    acc = _dot(s_hi, w_hi) + _dot(s_hi, w_lo) + _dot(s_lo, w_hi)
    o_ref[...] = acc + b_ref[...]


def _modulation(c, w_ada, b_ada):
    depth, d, six_d = w_ada.shape
    bn = c.shape[0]
    cb = 2048
    return pl.pallas_call(
        _mod_kernel,
        grid=(depth, six_d // cb),
        in_specs=[
            pl.BlockSpec((bn, d), lambda l, j: (0, 0)),
            pl.BlockSpec((None, d, cb), lambda l, j: (l, 0, j)),
            pl.BlockSpec((None, 1, cb), lambda l, j: (l, 0, j)),
        ],
        out_specs=pl.BlockSpec((None, bn, cb), lambda l, j: (l, 0, j)),
        out_shape=jax.ShapeDtypeStruct((depth, bn, six_d), jnp.float32),
        compiler_params=pltpu.CompilerParams(
            dimension_semantics=("arbitrary", "arbitrary"), vmem_limit_bytes=VMEM_LIMIT),
        name="adaln_mod",
    )(c, w_ada, b_ada.reshape(depth, 1, six_d))


def _route(h2, rwt_ref, rb_ref, carry, cap):
    tr = h2.shape[0]
    h_hi, h_lo = _split_bf16(h2)
    w_hi, w_lo = _split_bf16(rwt_ref[...])
    logits = (_dot(w_hi, h_hi, _NT) + _dot(w_hi, h_lo, _NT) + _dot(w_lo, h_hi, _NT)) + rb_ref[...]
    ex = jnp.exp(logits - jnp.max(logits, axis=0, keepdims=True))
    probs = ex / jnp.sum(ex, axis=0, keepdims=True)

    idx8 = lax.broadcasted_iota(jnp.int32, (EXPERTS_PER_GROUP, tr), 0)
    best = None
    for g in range(N_GROUPS):
        pg = probs[g * EXPERTS_PER_GROUP:(g + 1) * EXPERTS_PER_GROUP]
        m1 = jnp.max(pg, axis=0, keepdims=True)
        i1 = jnp.min(jnp.where(pg == m1, idx8, EXPERTS_PER_GROUP), axis=0, keepdims=True)
        pg2 = jnp.where(idx8 == i1, -1.0, pg)
        m2 = jnp.max(pg2, axis=0, keepdims=True)
        i2 = jnp.min(jnp.where(pg2 == m2, idx8, EXPERTS_PER_GROUP), axis=0, keepdims=True)
        cand = (m1 + m2, m1, m2, i1 + g * EXPERTS_PER_GROUP, i2 + g * EXPERTS_PER_GROUP)
        if best is None:
            best = cand
        else:
            better = cand[0] > best[0]
            best = tuple(jnp.where(better, a, b) for a, b in zip(cand, best))
    _, p1, p2, e0, e1 = best
    denom = p1 + p2
    w0, w1 = p1 / denom, p2 / denom

    eidx = lax.broadcasted_iota(jnp.int32, (N_EXPERTS, tr), 0)
    hit0, hit1 = eidx == e0, eidx == e1
    onehot = jnp.where(jnp.logical_or(hit0, hit1), 1.0, 0.0)
    before = (lax.broadcasted_iota(jnp.int32, (tr, tr), 0) < lax.broadcasted_iota(jnp.int32, (tr, tr), 1))
    rank = _dot(_bf(onehot), _bf(jnp.where(before, 1.0, 0.0))) + carry
    r0 = jnp.sum(jnp.where(hit0, rank, 0.0), axis=0, keepdims=True).astype(jnp.int32)
    r1 = jnp.sum(jnp.where(hit1, rank, 0.0), axis=0, keepdims=True).astype(jnp.int32)
    carry = carry + jnp.sum(onehot, axis=1, keepdims=True)

    eye = lax.broadcasted_iota(jnp.int32, (tr, tr), 0) == lax.broadcasted_iota(jnp.int32, (tr, tr), 1)
    w0c = jnp.sum(jnp.where(eye, jnp.broadcast_to(w0, (tr, tr)), 0.0), axis=1, keepdims=True)
    w1c = jnp.sum(jnp.where(eye, jnp.broadcast_to(w1, (tr, tr)), 0.0), axis=1, keepdims=True)
    lane8 = lax.broadcasted_iota(jnp.int32, (tr, 8), 1)
    wcol = jnp.where(lane8 == 0, w0c, jnp.where(lane8 == 1, w1c, 0.0))
    return (e0 * cap + r0) * ROW_GROUP, (e1 * cap + r1) * ROW_GROUP, wcol, carry


def _zero_segment_padding(cnt_ref, cap, zero_ref, xs_ref, sem):
    copies = []
    for ex in range(N_EXPERTS):
        cnt = cnt_ref[ex, 0]
        seg = ex * cap
        pos = seg + cnt
        end8 = seg + ((cnt + SUBLANES - 1) // SUBLANES) * SUBLANES
        for j in range(SUBLANES - 1):
            copies.append((pos + j < end8, _tile_copy(zero_ref, 0, xs_ref, (pos + j) * ROW_GROUP, sem)))
        seg_end = seg + ((cnt + FFN_ROWS - 1) // FFN_ROWS) * FFN_ROWS
        groups = (seg_end - end8) // SUBLANES
        at = end8
        bit = FFN_ROWS // (2 * SUBLANES)
        while bit >= 1:
            rows = bit * SUBLANES
            pred = (groups & bit) != 0
            copies.append((pred, pltpu.make_async_copy(
                zero_ref.at[pl.ds(0, rows * ROW_GROUP)],
                xs_ref.at[pl.ds(pl.multiple_of(at * ROW_GROUP, SUBLANES), rows * ROW_GROUP)], sem)))
            at = at + jnp.where(pred, rows, 0)
            bit //= 2
    for pred, cp in copies:
        pl.when(pred)(cp.start)
    for pred, cp in copies:
        pl.when(pred)(cp.wait)


N_MIXER_INPUTS = 19
N_MIXER_OUTPUTS = 5


def _mixer_kernel(*refs, cap, fuse_prev):
    ins, rest = refs[:N_MIXER_INPUTS], refs[N_MIXER_INPUTS:]
    if fuse_prev:
        dp0_ref, dp1_ref, dp0n_ref, dp1n_ref, wp_ref, modp_ref, ysp_ref = rest[:7]
        rest = rest[7:]
    outs, scratch = rest[:N_MIXER_OUTPUTS], rest[N_MIXER_OUTPUTS:]
    (x_ref, mod_ref, g1_ref, wina_ref, winb_ref, winc_ref, wout_ref, lng_ref, lnb_ref, ws_ref, bsm_ref,
     wg_ref, bg_ref, glag_ref, cw_ref, cb_ref, g2_ref, rwt_ref, rb_ref) = ins
    o_ref, dest_ref, w_ref, cnt_ref, xs_ref = outs
    (h_ref, pa_ref, pb_ref, y0_ref, y1_ref, y2_ref, y3_ref, s_ref, zc_ref, wsm_ref, h2_ref,
     carry_ref, dvm_ref, d0s_ref, d1s_ref, cvm_ref, csm_ref, zero_ref, sem, sem_s, sem_z,
     ybuf, xin_ref, sem_g) = scratch

    bi, ti = pl.program_id(0), pl.program_id(1)
    tm = x_ref.shape[0]
    n_blk = tm // GMLP_BLOCK
    step = bi * pl.num_programs(1) + ti
    last_step = step == pl.num_programs(0) * pl.num_programs(1) - 1
    slot = step % 2
    spare_row0 = N_EXPERTS * cap

    def drain_rows(src_ref, dst_ref, s, copy=_row_copy):
        def body(g, c):
            for _ in range(TOP_K * SUBLANES):
                copy(src_ref, 0, dst_ref, 0, s).wait()
            return c
        lax.fori_loop(0, tm // SUBLANES, body, 0)

    @pl.when(step > 0)
    def _():
        pltpu.make_async_copy(dvm_ref.at[0], d0s_ref, sem_s).wait()
        pltpu.make_async_copy(dvm_ref.at[1], d1s_ref, sem_s).wait()

    if fuse_prev:
        @pl.when(step == 0)
        def _():
            def issue(g, c):
                for jj in range(SUBLANES):
                    i = pl.multiple_of(g * SUBLANES, SUBLANES) + jj
                    _tile_copy(ysp_ref, dp0_ref[i], ybuf.at[0, 0], i * ROW_GROUP, sem_g.at[0]).start()
                    _tile_copy(ysp_ref, dp1_ref[i], ybuf.at[0, 1], i * ROW_GROUP, sem_g.at[0]).start()
                return c
            lax.fori_loop(0, tm // SUBLANES, issue, 0)

        drain_rows(ysp_ref, ybuf.at[0, 0], sem_g.at[slot], _tile_copy)
        wp = wp_ref[...]
        xin_ref[...] = x_ref[...] + modp_ref[...][5:6] * (
            wp[:, 0:1] * _load_rows_from_tiles(ybuf, (slot, 0), tm)
            + wp[:, 1:2] * _load_rows_from_tiles(ybuf, (slot, 1), tm))
        x_in = xin_ref
    else:
        x_in = x_ref

    @pl.when(step == 0)
    def _():
        carry_ref[...] = jnp.zeros_like(carry_ref)
        h2_ref[...] = jnp.zeros_like(h2_ref)

        def fill(i, c):
            d0s_ref[i] = (spare_row0 + i) * ROW_GROUP
            d1s_ref[i] = (spare_row0 + tm + i) * ROW_GROUP
            return c

        lax.fori_loop(0, tm, fill, 0)
        tt = lax.broadcasted_iota(jnp.int32, (GMLP_BLOCK, GMLP_BLOCK), 0) // CHUNK
        ss = lax.broadcasted_iota(jnp.int32, (GMLP_BLOCK, GMLP_BLOCK), 1) // CHUNK
        for h in range(A_HEADS):
            wsm_ref[h] = _bf(jnp.where(tt >= ss, ws_ref[h], 0.0))

    @pl.when(ti == 0)
    def _():
        s_ref[...] = jnp.zeros_like(s_ref)
        zc_ref[...] = jnp.zeros_like(zc_ref)

    m = mod_ref[...]
    sh1, sc1, gt1, sh2, sc2 = m[0:1], m[1:2], m[2:3], m[3:4], m[4:5]
    h_ref[...] = _bf((_rms(x_in[...]) * g1_ref[...]) * (1.0 + sc1) + sh1)

    lane256 = lax.broadcasted_iota(jnp.int32, (CHUNK, B_KDIM), 1)
    head_of_lane = lane256 // B_DK
    r64 = lax.broadcasted_iota(jnp.int32, (CHUNK, CHUNK), 0)
    c64 = lax.broadcasted_iota(jnp.int32, (CHUNK, CHUNK), 1)
    tri = _bf(jnp.where(r64 >= c64, 1.0, 0.0))
    causal4 = jnp.concatenate([r64 >= c64] * B_HEADS, axis=0)
    eye256 = (lax.broadcasted_iota(jnp.int32, (B_KDIM, B_KDIM), 0)
              == lax.broadcasted_iota(jnp.int32, (B_KDIM, B_KDIM), 1))
    a_head_of_lane = lax.broadcasted_iota(jnp.int32, (GMLP_BLOCK, A_DIM), 1) // (A_DIM // A_HEADS)
    row128 = lax.broadcasted_iota(jnp.int32, (GMLP_BLOCK, C_DIM), 0)
    sqrt_half = 0.7071067811865476

    def gelu(v):
        return 0.5 * v * (1.0 + lax.erf(v * sqrt_half))

    h2_prev = h2_ref.at[1 - slot]
    row_copies = []
    for i in range(tm):
        row_copies.append(
            lambda i=i: _tile_copy(h2_prev, i * ROW_GROUP, xs_ref, d0s_ref[i], sem).start(priority=0))
        row_copies.append(
            lambda i=i: _tile_copy(h2_prev, i * ROW_GROUP, xs_ref, d1s_ref[i], sem).start(priority=1))
        if fuse_prev:
            row_copies.append(lambda i=i: _tile_copy(
                ysp_ref, dp0n_ref[i], ybuf.at[1 - slot, 0], i * ROW_GROUP, sem_g.at[1 - slot]).start(priority=0))
            row_copies.append(lambda i=i: _tile_copy(
                ysp_ref, dp1n_ref[i], ybuf.at[1 - slot, 1], i * ROW_GROUP, sem_g.at[1 - slot]).start(priority=1))
    row_copies.reverse()
    copies_per_point = -(-len(row_copies) // (n_blk * COPY_POINTS_PER_BLOCK))

    def start_row_copies():
        for _ in range(min(copies_per_point, len(row_copies))):
            row_copies.pop()()

    p_halves = (pa_ref, pb_ref)
    y_blocks = (y0_ref, y1_ref, y2_ref, y3_ref)
    assert n_blk == len(y_blocks)

    def project(half):
        rows = slice(half * (tm // 2), (half + 1) * (tm // 2))
        hb = h_ref[rows, :]
        p_halves[half][:, 0:OFF_OG] = _dot(hb, wina_ref[...])
        p_halves[half][:, OFF_OG:OFF_GLR] = _dot(hb, winb_ref[...])
        p_halves[half][:, OFF_GLR:P_PAD] = _dot(hb, winc_ref[...])

    def block(j, state, zprev):
        p_ref, y_ref = p_halves[j // (n_blk // 2)], y_blocks[j]
        r0 = (j % (n_blk // 2)) * GMLP_BLOCK
        rows = slice(r0, r0 + GMLP_BLOCK)

        u = gelu(p_ref[rows, OFF_AU:OFF_AU + A_DIM])
        vv = gelu(p_ref[rows, OFF_AV:OFF_AV + A_DIM])
        mu = jnp.mean(vv, axis=-1, keepdims=True)
        var = jnp.mean((vv - mu) ** 2, axis=-1, keepdims=True)
        vn = _bf((vv - mu) * lax.rsqrt(var + EPS) * lng_ref[...] + lnb_ref[...])
        sv = jnp.zeros((GMLP_BLOCK, A_DIM), jnp.float32)
        for hh in range(A_HEADS):
            sv = jnp.where(a_head_of_lane == hh, _dot(wsm_ref[hh], vn), sv)
        y_ref[:, 0:A_DIM] = _bf(u * (sv + bsm_ref[...]))
        start_row_copies()

        z = _dot(_bf(p_ref[rows, OFF_GLR:OFF_GLR + LANES]), wg_ref[...]) + bg_ref[...]
        glog = (jnp.minimum(z, 0.0) - jnp.log1p(jnp.exp(-jnp.abs(z)))) / GLA_TAU
        def chunk_head(c):
            rc = slice(r0 + c * CHUNK, r0 + (c + 1) * CHUNK)
            g = glog[c * CHUNK:(c + 1) * CHUNK]
            g_hi = _bf(g)
            g_r1 = g - g_hi.astype(jnp.float32)
            g_mid = _bf(g_r1)
            g_lo = _bf(g_r1 - g_mid.astype(jnp.float32))
            b = _dot(tri, g_hi) + _dot(tri, g_mid) + _dot(tri, g_lo)
            b_mid = b[CHUNK // 2:CHUNK // 2 + 1]
            b_last = b[CHUNK - 1:CHUNK]
            q = p_ref[rc, OFF_Q:OFF_Q + B_KDIM] * (B_DK ** -0.5)
            k = p_ref[rc, OFF_K:OFF_K + B_KDIM]
            vb = _bf(p_ref[rc, OFF_V:OFF_V + B_VDIM])
            qs = q * jnp.exp(b - b_mid)
            ks = _bf(k * jnp.exp(b_mid - b))
            kd = _bf(k * jnp.exp(b_last - b))
            qb = q * jnp.exp(b)
            qs_st = _bf(jnp.concatenate(
                [jnp.where(head_of_lane == hh, qs, 0.0) for hh in range(B_HEADS)], axis=0))
            qb_st = _bf(jnp.concatenate(
                [jnp.where(head_of_lane == hh, qb, 0.0) for hh in range(B_HEADS)], axis=0))
            scores = _bf(jnp.where(causal4, _dot(qs_st, ks, _NT), 0.0))
            start_row_copies()
            kv_all = _dot(kd, vb, _TN)
            start_row_copies()
            o_intra, kvs = [], []
            for hh in range(B_HEADS):
                rs = slice(hh * CHUNK, (hh + 1) * CHUNK)
                cs = slice(hh * B_DV, (hh + 1) * B_DV)
                o_intra.append(_dot(scores[rs], vb[:, cs]))
                kvs.append(kv_all[rs, cs])
                start_row_copies()
            decay_row = jnp.broadcast_to(jnp.exp(b_last), (B_KDIM, B_KDIM))
            decay_col = jnp.sum(jnp.where(eye256, decay_row, 0.0), axis=1, keepdims=True)
            og = p_ref[rc, OFF_OG:OFF_OG + B_VDIM]
            return qb_st, o_intra, jnp.concatenate(kvs, axis=0), decay_col, og * jax.nn.sigmoid(og)

        heads = [chunk_head(c) for c in range(GMLP_BLOCK // CHUNK)]
        for c, (qb_st, o_intra, kv, decay_col, out_gate) in enumerate(heads):
            o_inter = _dot(qb_st, _bf(state))
            state = decay_col * state + kv
            outs = [_rms(o_intra[hh] + o_inter[hh * CHUNK:(hh + 1) * CHUNK]) for hh in range(B_HEADS)]
            on = jnp.concatenate(outs, axis=1) * glag_ref[...]
            y_ref[c * CHUNK:(c + 1) * CHUNK, A_DIM:A_DIM + B_VDIM] = _bf(on * out_gate)
            start_row_copies()

        zz = p_ref[rows, OFF_CC:OFF_CC + C_DIM] * p_ref[rows, OFF_CX:OFF_CX + C_DIM]
        z1 = jnp.where(row128 == 0, zprev[7:8], pltpu.roll(zz, 1, 0))
        z2 = jnp.where(row128 == 0, zprev[6:7], jnp.where(row128 == 1, zprev[7:8], pltpu.roll(zz, 2, 0)))
        cw = cw_ref[...]
        yc = cb_ref[...] + cw[0:1] * z2
        yc = yc + cw[1:2] * z1
        yc = yc + cw[2:3] * zz
        y_ref[:, A_DIM + B_VDIM:A_DIM + B_VDIM + C_DIM] = _bf(p_ref[rows, OFF_CB:OFF_CB + C_DIM] * yc)
        start_row_copies()
        return state, zz[GMLP_BLOCK - SUBLANES:GMLP_BLOCK]

    state, zprev, carry = s_ref[...], zc_ref[...], carry_ref[...]
    project(0)
    for j in range(n_blk):
        if j == 0:
            project(1)
        state, zprev = block(j, state, zprev)
        rows = slice(j * GMLP_BLOCK, (j + 1) * GMLP_BLOCK)
        xo = x_in[rows, :] + gt1 * _dot(y_blocks[j][...], wout_ref[...])
        o_ref[rows, :] = xo
        start_row_copies()
        h2 = (_rms(xo) * g2_ref[...]) * (1.0 + sc2) + sh2
        _store_rows_as_tiles(
            h2_ref.at[slot, pl.ds(j * GMLP_BLOCK * ROW_GROUP, GMLP_BLOCK * ROW_GROUP)], _pack_rows(h2))
        d0, d1, wcol, carry = _route(h2, rwt_ref, rb_ref, carry, cap)
        dvm_ref[0:1, rows] = d0
        dvm_ref[1:2, rows] = d1
        w_ref[rows, :] = wcol
        start_row_copies()
    while row_copies:
        row_copies.pop()()
    s_ref[...] = state
    zc_ref[...] = zprev
    carry_ref[...] = carry
    cnt = jnp.broadcast_to(carry, cnt_ref.shape).astype(jnp.int32)
    cnt_ref[...] = cnt
    dest_ref[...] = dvm_ref[0:TOP_K, :]

    drain_rows(h2_prev, xs_ref, sem, _tile_copy)

    to_smem = [pltpu.make_async_copy(dvm_ref.at[0], d0s_ref, sem_s),
               pltpu.make_async_copy(dvm_ref.at[1], d1s_ref, sem_s)]
    for cp in to_smem:
        cp.start()

    @pl.when(last_step)
    def _():
        for cp in to_smem:
            cp.wait()
        h2_last = h2_ref.at[slot]

        def issue(g, c):
            for jj in range(SUBLANES):
                i = pl.multiple_of(g * SUBLANES, SUBLANES) + jj
                _tile_copy(h2_last, i * ROW_GROUP, xs_ref, d0s_ref[i], sem).start()
                _tile_copy(h2_last, i * ROW_GROUP, xs_ref, d1s_ref[i], sem).start()
            return c

        lax.fori_loop(0, tm // SUBLANES, issue, 0)
        drain_rows(h2_last, xs_ref, sem, _tile_copy)
        if fuse_prev:
            drain_rows(ysp_ref, ybuf.at[0, 0], sem_g.at[1 - slot], _tile_copy)
        cvm_ref[...] = cnt
        zero_ref[...] = jnp.zeros_like(zero_ref)
        cp = pltpu.make_async_copy(cvm_ref, csm_ref, sem_s)
        cp.start()
        cp.wait()
        _zero_segment_padding(csm_ref, cap, zero_ref, xs_ref, sem_z)


def _mixer(x, mod, g1, g2, w_in, w_out, ln_g, ln_b, w_s, b_s, w_gate, b_gate, gla_g, conv_w, conv_b,
           router_w, router_b, prev=None):
    bn, t, d = x.shape
    n = bn * t
    tm = min(MIX_ROWS, t)
    nt = t // tm
    cap = n
    glr0 = 2 * A_DIM + 2 * B_KDIM + B_VDIM
    w_in_a = _bf(w_in[:, :glr0])
    w_in_b = _bf(w_in[:, glr0 + GLA_RANK:])
    w_in_c = jnp.concatenate(
        [_bf(w_in[:, glr0:glr0 + GLA_RANK]), jnp.zeros((d, LANES - GLA_RANK), jnp.bfloat16)], axis=1)
    w_gate_p = jnp.concatenate(
        [_bf(w_gate), jnp.zeros((LANES - GLA_RANK, B_KDIM), jnp.bfloat16)], axis=0)
    bsm = jnp.repeat(b_s.T, A_DIM // A_HEADS, axis=1)

    def whole(shape):
        return pl.BlockSpec(shape, lambda b, i: (0,) * len(shape))

    prev_specs, prev_args = [], ()
    if prev is not None:
        n_tiles = bn * nt

        def this_tile(b, i):
            return (b * nt + i,)

        def next_tile(b, i):
            return (jnp.minimum(b * nt + i + 1, n_tiles - 1),)

        d0p, d1p, wcolp, modp, ysp = prev
        prev_specs = [
            pl.BlockSpec((tm,), this_tile, memory_space=pltpu.SMEM),
            pl.BlockSpec((tm,), this_tile, memory_space=pltpu.SMEM),
            pl.BlockSpec((tm,), next_tile, memory_space=pltpu.SMEM),
            pl.BlockSpec((tm,), next_tile, memory_space=pltpu.SMEM),
            pl.BlockSpec((tm, 8), lambda b, i: (b * nt + i, 0)),
            pl.BlockSpec((None, 6, d), lambda b, i: (b, 0, 0)),
            pl.BlockSpec(memory_space=pl.ANY),
        ]
        prev_args = (d0p, d1p, d0p, d1p, wcolp, modp, ysp)

    return pl.pallas_call(
        functools.partial(_mixer_kernel, cap=cap, fuse_prev=prev is not None),
        grid=(bn, nt),
        in_specs=[
            pl.BlockSpec((None, tm, d), lambda b, i: (b, i, 0)),
            pl.BlockSpec((None, 6, d), lambda b, i: (b, 0, 0)),
            whole((1, d)),
            whole((d, OFF_OG)),
            whole((d, OFF_GLR - OFF_OG)),
            whole((d, LANES)),
            whole((d, d)),
            whole((1, A_DIM)),
            whole((1, A_DIM)),
            whole((A_HEADS, GMLP_BLOCK, GMLP_BLOCK)),
            whole((GMLP_BLOCK, A_DIM)),
            whole((LANES, B_KDIM)),
            whole((1, B_KDIM)),
            whole((1, B_VDIM)),
            whole((3, C_DIM)),
            whole((1, C_DIM)),
            whole((1, d)),
            whole((N_EXPERTS, d)),
            whole((N_EXPERTS, 1)),
        ] + prev_specs,
        out_specs=[
            pl.BlockSpec((None, tm, d), lambda b, i: (b, i, 0)),
            pl.BlockSpec((TOP_K, tm), lambda b, i: (0, b * nt + i)),
            pl.BlockSpec((tm, 8), lambda b, i: (b * nt + i, 0)),
            whole((N_EXPERTS, LANES)),
            pl.BlockSpec(memory_space=pl.ANY),
        ],
        out_shape=[
            jax.ShapeDtypeStruct(x.shape, x.dtype),
            jax.ShapeDtypeStruct((TOP_K, n), jnp.int32),
            jax.ShapeDtypeStruct((n, 8), jnp.float32),
            jax.ShapeDtypeStruct((N_EXPERTS, LANES), jnp.int32),
            jax.ShapeDtypeStruct(((N_EXPERTS * cap + TOP_K * tm) * ROW_GROUP, LANES), jnp.uint32),
        ],
        scratch_shapes=[
            pltpu.VMEM((tm, d), jnp.bfloat16),
            pltpu.VMEM((tm // 2, P_PAD), jnp.float32),
            pltpu.VMEM((tm // 2, P_PAD), jnp.float32),
            pltpu.VMEM((GMLP_BLOCK, d), jnp.bfloat16),
            pltpu.VMEM((GMLP_BLOCK, d), jnp.bfloat16),
            pltpu.VMEM((GMLP_BLOCK, d), jnp.bfloat16),
            pltpu.VMEM((GMLP_BLOCK, d), jnp.bfloat16),
            pltpu.VMEM((B_KDIM, B_DV), jnp.float32),
            pltpu.VMEM((SUBLANES, C_DIM), jnp.float32),
            pltpu.VMEM((A_HEADS, GMLP_BLOCK, GMLP_BLOCK), jnp.bfloat16),
            pltpu.VMEM((2, tm * ROW_GROUP, LANES), jnp.uint32),
            pltpu.VMEM((N_EXPERTS, 1), jnp.float32),
            pltpu.VMEM((SUBLANES, tm), jnp.int32),
            pltpu.SMEM((tm,), jnp.int32),
            pltpu.SMEM((tm,), jnp.int32),
            pltpu.VMEM((N_EXPERTS, LANES), jnp.int32),
            pltpu.SMEM((N_EXPERTS, LANES), jnp.int32),
            pltpu.VMEM((FFN_ROWS // 2 * ROW_GROUP, LANES), jnp.uint32),
            pltpu.SemaphoreType.DMA,
            pltpu.SemaphoreType.DMA,
            pltpu.SemaphoreType.DMA,
            pltpu.VMEM((2, TOP_K, tm * ROW_GROUP, LANES), jnp.uint32),
            pltpu.VMEM((tm, d), jnp.float32),
            pltpu.SemaphoreType.DMA((2,)),
        ],
        compiler_params=pltpu.CompilerParams(
            dimension_semantics=("arbitrary", "arbitrary"), vmem_limit_bytes=VMEM_LIMIT),
        name="mixer",
    )(x, mod, g1.reshape(1, d), w_in_a, w_in_b, w_in_c, _bf(w_out), ln_g.reshape(1, -1), ln_b.reshape(1, -1), w_s, bsm,
      w_gate_p, b_gate.reshape(1, -1), gla_g.reshape(1, -1), conv_w, conv_b.reshape(1, -1),
      g2.reshape(1, d), router_w.T, router_b.reshape(N_EXPERTS, 1), *prev_args)


def _ffn_kernel(be_ref, rb_ref, vr_ref, nb_ref, xs_ref, w1_ref, w3_ref, w2_ref, ys_ref, w1b_ref, w3b_ref, w2b_ref):
    del rb_ref
    i = pl.program_id(0)
    used = i < nb_ref[0]
    new_expert = jnp.logical_or(i == 0, be_ref[i] != be_ref[jnp.maximum(i - 1, 0)])

    @pl.when(jnp.logical_and(used, new_expert))
    def _():
        w1b_ref[...] = _bf(w1_ref[...])
        w3b_ref[...] = _bf(w3_ref[...])
        w2b_ref[...] = _bf(w2_ref[...])

    def ffn_rows(first, rows):
        span = pl.ds(first * ROW_GROUP, rows * ROW_GROUP)
        x_a, x_b = _unpack_rows(_load_packed_rows(xs_ref.at[span], (), rows))
        half = x_a.shape[1]
        a = _dot(x_a, w1b_ref[0:half, :]) + _dot(x_b, w1b_ref[half:, :])
        g = _dot(x_a, w3b_ref[0:half, :]) + _dot(x_b, w3b_ref[half:, :])
        hm = _bf((a * jax.nn.sigmoid(a)) * g)
        _store_rows_as_tiles(ys_ref.at[span], _pack_rows(_dot(hm, w2b_ref[...])))

    short = vr_ref[i] <= FFN_ROWS // 2

    @pl.when(jnp.logical_and(used, jnp.logical_not(short)))
    def _():
        ffn_rows(0, FFN_ROWS)

    @pl.when(jnp.logical_and(used, short))
    def _():
        ffn_rows(0, FFN_ROWS // 2)


def _expert_ffn(layer, block_e, row_block, valid_rows, n_used, xs, w1, w3, w2):
    n_rows = xs.shape[0] // ROW_GROUP
    n_blocks = block_e.shape[0]
    d, f = w1.shape[-2], w1.shape[-1]
    return pl.pallas_call(
        _ffn_kernel,
        grid_spec=pltpu.PrefetchScalarGridSpec(
            num_scalar_prefetch=4,
            grid=(n_blocks,),
            in_specs=[
                pl.BlockSpec((FFN_ROWS * ROW_GROUP, LANES), lambda i, be, rb, vr, nb: (rb[i], 0)),
                pl.BlockSpec((None, None, d, f), lambda i, be, rb, vr, nb: (layer, be[i], 0, 0)),
                pl.BlockSpec((None, None, d, f), lambda i, be, rb, vr, nb: (layer, be[i], 0, 0)),
                pl.BlockSpec((None, None, f, d), lambda i, be, rb, vr, nb: (layer, be[i], 0, 0)),
            ],
            out_specs=pl.BlockSpec((FFN_ROWS * ROW_GROUP, LANES), lambda i, be, rb, vr, nb: (rb[i], 0)),
            scratch_shapes=[pltpu.VMEM((d, f), jnp.bfloat16), pltpu.VMEM((d, f), jnp.bfloat16),
                            pltpu.VMEM((f, d), jnp.bfloat16)],
        ),
        out_shape=jax.ShapeDtypeStruct((n_rows * ROW_GROUP, LANES), jnp.uint32),
        compiler_params=pltpu.CompilerParams(
            dimension_semantics=("arbitrary",), vmem_limit_bytes=VMEM_LIMIT),
        name="expert_ffn",
    )(block_e, row_block, valid_rows, n_used, xs, w1, w3, w2)


def _combine_kernel(d0_ref, d1_ref, d0n_ref, d1n_ref, x_ref, mod_ref, w_ref, gf_ref, ys_ref, o_ref, ybuf, sems,
                    *, final_norm):
    tm = x_ref.shape[0]
    step = pl.program_id(0) * pl.num_programs(1) + pl.program_id(1)
    last = step == pl.num_programs(0) * pl.num_programs(1) - 1
    slot = step % 2

    def drain(sem):
        def body(g, c):
            for _ in range(TOP_K * SUBLANES):
                _tile_copy(ys_ref, 0, ybuf.at[0, 0], 0, sem).wait()
            return c
        lax.fori_loop(0, tm // SUBLANES, body, 0)

    @pl.when(step == 0)
    def _():
        def issue(g, c):
            for jj in range(SUBLANES):
                i = pl.multiple_of(g * SUBLANES, SUBLANES) + jj
                _tile_copy(ys_ref, d0_ref[i], ybuf.at[0, 0], i * ROW_GROUP, sems.at[0]).start()
                _tile_copy(ys_ref, d1_ref[i], ybuf.at[0, 1], i * ROW_GROUP, sems.at[0]).start()
            return c
        lax.fori_loop(0, tm // SUBLANES, issue, 0)

    nxt, sem_nxt = ybuf.at[1 - slot], sems.at[1 - slot]
    for i in range(tm):
        _tile_copy(ys_ref, d0n_ref[i], nxt.at[0], i * ROW_GROUP, sem_nxt).start(priority=0)
        _tile_copy(ys_ref, d1n_ref[i], nxt.at[1], i * ROW_GROUP, sem_nxt).start(priority=1)

    drain(sems.at[slot])
    w = w_ref[...]
    gt2 = mod_ref[...][5:6]
    out = x_ref[...] + gt2 * (w[:, 0:1] * _load_rows_from_tiles(ybuf, (slot, 0), tm)
                              + w[:, 1:2] * _load_rows_from_tiles(ybuf, (slot, 1), tm))
    if final_norm:
        out = _rms(out) * gf_ref[...]
    o_ref[...] = out

    @pl.when(last)
    def _():
        drain(sems.at[1 - slot])


def _combine(d0, d1, x, mod, wcol, g_final, ys, final_norm):
    bn, t, d = x.shape
    tm = min(MOVE_ROWS, t)
    nt = t // tm
    n_tiles = bn * nt

    def this_tile(b, i):
        return (b * nt + i,)

    def next_tile(b, i):
        return (jnp.minimum(b * nt + i + 1, n_tiles - 1),)

    return pl.pallas_call(
        functools.partial(_combine_kernel, final_norm=final_norm),
        grid=(bn, nt),
        in_specs=[
            pl.BlockSpec((tm,), this_tile, memory_space=pltpu.SMEM),
            pl.BlockSpec((tm,), this_tile, memory_space=pltpu.SMEM),
            pl.BlockSpec((tm,), next_tile, memory_space=pltpu.SMEM),
            pl.BlockSpec((tm,), next_tile, memory_space=pltpu.SMEM),
            pl.BlockSpec((None, tm, d), lambda b, i: (b, i, 0)),
            pl.BlockSpec((None, 6, d), lambda b, i: (b, 0, 0)),
            pl.BlockSpec((tm, 8), lambda b, i: (b * nt + i, 0)),
            pl.BlockSpec((1, d), lambda b, i: (0, 0)),
            pl.BlockSpec(memory_space=pl.ANY),
        ],
        out_specs=pl.BlockSpec((None, tm, d), lambda b, i: (b, i, 0)),
        out_shape=jax.ShapeDtypeStruct(x.shape, x.dtype),
        scratch_shapes=[pltpu.VMEM((2, TOP_K, tm * ROW_GROUP, LANES), jnp.uint32), pltpu.SemaphoreType.DMA((2,))],
        compiler_params=pltpu.CompilerParams(
            dimension_semantics=("arbitrary", "arbitrary"), vmem_limit_bytes=VMEM_LIMIT),
        name="combine",
    )(d0, d1, d0, d1, x, mod, wcol, g_final.reshape(1, d), ys)


def _block_map(counts, cap, n_blocks):
    blocks_per_expert = (counts + FFN_ROWS - 1) // FFN_ROWS
    ends = jnp.cumsum(blocks_per_expert)
    n_used = ends[-1:]
    step = jnp.minimum(jnp.arange(n_blocks, dtype=jnp.int32), n_used - 1)
    block_e = jnp.minimum(jnp.sum((ends[None, :] <= step[:, None]).astype(jnp.int32), axis=1), N_EXPERTS - 1)
    first = (ends - blocks_per_expert)[block_e]
    row_block = block_e * (cap // FFN_ROWS) + (step - first)
    valid_rows = jnp.minimum(counts[block_e] - (step - first) * FFN_ROWS, FFN_ROWS)
    return (block_e.astype(jnp.int32), row_block.astype(jnp.int32), valid_rows.astype(jnp.int32),
            n_used.astype(jnp.int32))


def kernel(x, c, w_ada, b_ada, g_norm1, g_norm2, w_in, w_out, gmlp_ln_g, gmlp_ln_b, gmlp_ws, gmlp_bs,
           gla_w_gate, gla_b_gate, gla_norm_g, conv_w, conv_b, router_w, router_b, exp_w1, exp_w3, exp_w2,
           g_final):
    depth = w_ada.shape[0]
    bn, t, d = x.shape
    n = bn * t
    n_blocks = -(-(n * TOP_K) // FFN_ROWS) + N_EXPERTS
    mod_all = _modulation(c, w_ada, b_ada).reshape(depth, bn, 6, d)
    prev = None
    for l in range(depth):
        mod = mod_all[l]
        x, dest, wcol, cnt, xs = _mixer(
            x, mod, g_norm1[l], g_norm2[l], w_in[l], w_out[l], gmlp_ln_g[l], gmlp_ln_b[l], gmlp_ws[l],
            gmlp_bs[l], gla_w_gate[l], gla_b_gate[l], gla_norm_g[l], conv_w[l], conv_b[l], router_w, router_b,
            prev=prev)
        block_e, row_block, valid_rows, n_used = _block_map(cnt[:, 0], n, n_blocks)
        ys = _expert_ffn(l, block_e, row_block, valid_rows, n_used, xs, exp_w1, exp_w3, exp_w2)
        prev = (dest[0], dest[1], wcol, mod, ys)
    d0, d1, wcol, mod, ys = prev
    return _combine(d0, d1, x, mod, wcol, g_final, ys, final_norm=True)
```

```python
import functools

import jax
import jax.numpy as jnp
from jax import lax
from jax.experimental import pallas as pl
from jax.experimental.pallas import tpu as pltpu

CHUNK = 64
GMLP_BLOCK = 128
A_DIM = 256
A_HEADS = 4
B_HEADS = 4
B_DK = 64
B_DV = 128
B_KDIM = B_HEADS * B_DK
B_VDIM = B_HEADS * B_DV
GLA_RANK = 16
GLA_TAU = 16.0
C_DIM = 256
N_EXPERTS = 32
N_GROUPS = 4
EXPERTS_PER_GROUP = N_EXPERTS // N_GROUPS
TOP_K = 2
EPS = 1e-6

LANES = 128
SUBLANES = 8
OFF_AU, OFF_AV, OFF_Q, OFF_K, OFF_V = 0, 256, 512, 768, 1024
OFF_OG, OFF_CB, OFF_CC, OFF_CX, OFF_GLR = 1536, 2048, 2304, 2560, 2816
P_PAD = OFF_GLR + LANES

MIX_ROWS = 512
COPY_POINTS_PER_BLOCK = 18
MOVE_ROWS = 512
FFN_ROWS = 512
VMEM_LIMIT = 56 * 1024 * 1024

_NT = (((1,), (1,)), ((), ()))
_TN = (((0,), (0,)), ((), ()))


def _dot(a, b, dims=None):
    if dims is None:
        return jnp.dot(a, b, preferred_element_type=jnp.float32)
    return lax.dot_general(a, b, dims, preferred_element_type=jnp.float32)


def _bf(x):
    return x.astype(jnp.bfloat16)


def _split_bf16(x):
    hi = _bf(x)
    lo = _bf(x - hi.astype(jnp.float32))
    return hi, lo


def _rms(x):
    return x * lax.rsqrt(jnp.mean(x * x, axis=-1, keepdims=True) + EPS)


def _pack_rows(v):
    half = v.shape[1] // 2
    hi = pltpu.bitcast(_bf(v[:, :half]).astype(jnp.float32), jnp.uint32)
    lo = pltpu.bitcast(_bf(v[:, half:]).astype(jnp.float32), jnp.uint32)
    return hi | (lo >> 16)


def _unpack_rows(w):
    hi = pltpu.bitcast(w & jnp.uint32(0xFFFF0000), jnp.float32)
    lo = pltpu.bitcast(w << 16, jnp.float32)
    return _bf(hi), _bf(lo)


def _row_copy(src_ref, src_row, dst_ref, dst_row, sem):
    return pltpu.make_async_copy(src_ref.at[pl.ds(src_row, 1)], dst_ref.at[pl.ds(dst_row, 1)], sem)


ROW_GROUP = 4

def _aligned(row):
    return row if isinstance(row, int) else pl.multiple_of(row, ROW_GROUP)


def _tile_copy(src_ref, src_row, dst_ref, dst_row, sem):
    return pltpu.make_async_copy(src_ref.at[pl.ds(_aligned(src_row), ROW_GROUP)],
                                 dst_ref.at[pl.ds(_aligned(dst_row), ROW_GROUP)], sem)


def _store_rows_as_tiles(ref, value):
    rows = value.shape[0]
    assert value.shape[1] == ROW_GROUP * LANES
    for c in range(ROW_GROUP):
        ref[pl.ds(c, rows, stride=ROW_GROUP), :] = value[:, c * LANES:(c + 1) * LANES]


def _load_packed_rows(ref, idx, rows):
    return jnp.concatenate(
        [ref[idx + (pl.ds(c, rows, stride=ROW_GROUP), slice(None))] for c in range(ROW_GROUP)], axis=1)


def _load_rows_from_tiles(ref, idx, rows):
    hi, lo = _unpack_rows(_load_packed_rows(ref, idx, rows))
    return jnp.concatenate([hi.astype(jnp.float32), lo.astype(jnp.float32)], axis=1)


def _mod_kernel(c_ref, w_ref, b_ref, o_ref):
    c = c_ref[...]
    s = c * jax.nn.sigmoid(c)
    s_hi, s_lo = _split_bf16(s)
    w_hi, w_lo = _split_bf16(w_ref[...])
    acc = _dot(s_hi, w_hi) + _dot(s_hi, w_lo) + _dot(s_lo, w_hi)
    o_ref[...] = acc + b_ref[...]


def _modulation(c, w_ada, b_ada):
    depth, d, six_d = w_ada.shape
    bn = c.shape[0]
    cb = 2048
    return pl.pallas_call(
        _mod_kernel,
        grid=(depth, six_d // cb),
        in_specs=[
            pl.BlockSpec((bn, d), lambda l, j: (0, 0)),
            pl.BlockSpec((None, d, cb), lambda l, j: (l, 0, j)),
            pl.BlockSpec((None, 1, cb), lambda l, j: (l, 0, j)),
        ],
        out_specs=pl.BlockSpec((None, bn, cb), lambda l, j: (l, 0, j)),
        out_shape=jax.ShapeDtypeStruct((depth, bn, six_d), jnp.float32),
        compiler_params=pltpu.CompilerParams(
            dimension_semantics=("arbitrary", "arbitrary"), vmem_limit_bytes=VMEM_LIMIT),
        name="adaln_mod",
    )(c, w_ada, b_ada.reshape(depth, 1, six_d))


def _route(h2, rwt_ref, rb_ref, carry, cap):
    tr = h2.shape[0]
    h_hi, h_lo = _split_bf16(h2)
    w_hi, w_lo = _split_bf16(rwt_ref[...])
    logits = (_dot(w_hi, h_hi, _NT) + _dot(w_hi, h_lo, _NT) + _dot(w_lo, h_hi, _NT)) + rb_ref[...]
    ex = jnp.exp(logits - jnp.max(logits, axis=0, keepdims=True))
    probs = ex / jnp.sum(ex, axis=0, keepdims=True)

    idx8 = lax.broadcasted_iota(jnp.int32, (EXPERTS_PER_GROUP, tr), 0)
    best = None
    for g in range(N_GROUPS):
        pg = probs[g * EXPERTS_PER_GROUP:(g + 1) * EXPERTS_PER_GROUP]
        m1 = jnp.max(pg, axis=0, keepdims=True)
        i1 = jnp.min(jnp.where(pg == m1, idx8, EXPERTS_PER_GROUP), axis=0, keepdims=True)
        pg2 = jnp.where(idx8 == i1, -1.0, pg)
        m2 = jnp.max(pg2, axis=0, keepdims=True)
        i2 = jnp.min(jnp.where(pg2 == m2, idx8, EXPERTS_PER_GROUP), axis=0, keepdims=True)
        cand = (m1 + m2, m1, m2, i1 + g * EXPERTS_PER_GROUP, i2 + g * EXPERTS_PER_GROUP)
        if best is None:
            best = cand
        else:
            better = cand[0] > best[0]
            best = tuple(jnp.where(better, a, b) for a, b in zip(cand, best))
    _, p1, p2, e0, e1 = best
    denom = p1 + p2
    w0, w1 = p1 / denom, p2 / denom

    eidx = lax.broadcasted_iota(jnp.int32, (N_EXPERTS, tr), 0)
    hit0, hit1 = eidx == e0, eidx == e1
    onehot = jnp.where(jnp.logical_or(hit0, hit1), 1.0, 0.0)
    before = (lax.broadcasted_iota(jnp.int32, (tr, tr), 0) < lax.broadcasted_iota(jnp.int32, (tr, tr), 1))
    rank = _dot(_bf(onehot), _bf(jnp.where(before, 1.0, 0.0))) + carry
    r0 = jnp.sum(jnp.where(hit0, rank, 0.0), axis=0, keepdims=True).astype(jnp.int32)
    r1 = jnp.sum(jnp.where(hit1, rank, 0.0), axis=0, keepdims=True).astype(jnp.int32)
    carry = carry + jnp.sum(onehot, axis=1, keepdims=True)

    eye = lax.broadcasted_iota(jnp.int32, (tr, tr), 0) == lax.broadcasted_iota(jnp.int32, (tr, tr), 1)
    w0c = jnp.sum(jnp.where(eye, jnp.broadcast_to(w0, (tr, tr)), 0.0), axis=1, keepdims=True)
    w1c = jnp.sum(jnp.where(eye, jnp.broadcast_to(w1, (tr, tr)), 0.0), axis=1, keepdims=True)
    lane8 = lax.broadcasted_iota(jnp.int32, (tr, 8), 1)
    wcol = jnp.where(lane8 == 0, w0c, jnp.where(lane8 == 1, w1c, 0.0))
    return (e0 * cap + r0) * ROW_GROUP, (e1 * cap + r1) * ROW_GROUP, wcol, carry


def _zero_segment_padding(cnt_ref, cap, zero_ref, xs_ref, sem):
    copies = []
    for ex in range(N_EXPERTS):
        cnt = cnt_ref[ex, 0]
        seg = ex * cap
        pos = seg + cnt
        end8 = seg + ((cnt + SUBLANES - 1) // SUBLANES) * SUBLANES
        for j in range(SUBLANES - 1):
            copies.append((pos + j < end8, _tile_copy(zero_ref, 0, xs_ref, (pos + j) * ROW_GROUP, sem)))
        seg_end = seg + ((cnt + FFN_ROWS - 1) // FFN_ROWS) * FFN_ROWS
        groups = (seg_end - end8) // SUBLANES
        at = end8
        bit = FFN_ROWS // (2 * SUBLANES)
        while bit >= 1:
            rows = bit * SUBLANES
            pred = (groups & bit) != 0
            copies.append((pred, pltpu.make_async_copy(
                zero_ref.at[pl.ds(0, rows * ROW_GROUP)],
                xs_ref.at[pl.ds(pl.multiple_of(at * ROW_GROUP, SUBLANES), rows * ROW_GROUP)], sem)))
            at = at + jnp.where(pred, rows, 0)
            bit //= 2
    for pred, cp in copies:
        pl.when(pred)(cp.start)
    for pred, cp in copies:
        pl.when(pred)(cp.wait)


N_MIXER_INPUTS = 19
N_MIXER_OUTPUTS = 5


def _mixer_kernel(*refs, cap, fuse_prev):
    ins, rest = refs[:N_MIXER_INPUTS], refs[N_MIXER_INPUTS:]
    if fuse_prev:
        dp0_ref, dp1_ref, dp0n_ref, dp1n_ref, wp_ref, modp_ref, ysp_ref = rest[:7]
        rest = rest[7:]
    outs, scratch = rest[:N_MIXER_OUTPUTS], rest[N_MIXER_OUTPUTS:]
    (x_ref, mod_ref, g1_ref, wina_ref, winb_ref, winc_ref, wout_ref, lng_ref, lnb_ref, ws_ref, bsm_ref,
     wg_ref, bg_ref, glag_ref, cw_ref, cb_ref, g2_ref, rwt_ref, rb_ref) = ins
    o_ref, dest_ref, w_ref, cnt_ref, xs_ref = outs
    (h_ref, pa_ref, pb_ref, y0_ref, y1_ref, y2_ref, y3_ref, s_ref, zc_ref, wsm_ref, h2_ref,
     carry_ref, dvm_ref, d0s_ref, d1s_ref, cvm_ref, csm_ref, zero_ref, sem, sem_s, sem_z,
     ybuf, xin_ref, sem_g) = scratch

    bi, ti = pl.program_id(0), pl.program_id(1)
    tm = x_ref.shape[0]
    n_blk = tm // GMLP_BLOCK
    step = bi * pl.num_programs(1) + ti
    last_step = step == pl.num_programs(0) * pl.num_programs(1) - 1
    slot = step % 2
    spare_row0 = N_EXPERTS * cap

    def drain_rows(src_ref, dst_ref, s, copy=_row_copy):
        def body(g, c):
            for _ in range(TOP_K * SUBLANES):
                copy(src_ref, 0, dst_ref, 0, s).wait()
            return c
        lax.fori_loop(0, tm // SUBLANES, body, 0)

    @pl.when(step > 0)
    def _():
        pltpu.make_async_copy(dvm_ref.at[0], d0s_ref, sem_s).wait()
        pltpu.make_async_copy(dvm_ref.at[1], d1s_ref, sem_s).wait()

    if fuse_prev:
        @pl.when(step == 0)
        def _():
            def issue(g, c):
                for jj in range(SUBLANES):
                    i = pl.multiple_of(g * SUBLANES, SUBLANES) + jj
                    _tile_copy(ysp_ref, dp0_ref[i], ybuf.at[0, 0], i * ROW_GROUP, sem_g.at[0]).start()
                    _tile_copy(ysp_ref, dp1_ref[i], ybuf.at[0, 1], i * ROW_GROUP, sem_g.at[0]).start()
                return c
            lax.fori_loop(0, tm // SUBLANES, issue, 0)

        drain_rows(ysp_ref, ybuf.at[0, 0], sem_g.at[slot], _tile_copy)
        wp = wp_ref[...]
        xin_ref[...] = x_ref[...] + modp_ref[...][5:6] * (
            wp[:, 0:1] * _load_rows_from_tiles(ybuf, (slot, 0), tm)
            + wp[:, 1:2] * _load_rows_from_tiles(ybuf, (slot, 1), tm))
        x_in = xin_ref
    else:
        x_in = x_ref

    @pl.when(step == 0)
    def _():
        carry_ref[...] = jnp.zeros_like(carry_ref)
        h2_ref[...] = jnp.zeros_like(h2_ref)

        def fill(i, c):
            d0s_ref[i] = (spare_row0 + i) * ROW_GROUP
            d1s_ref[i] = (spare_row0 + tm + i) * ROW_GROUP
            return c

        lax.fori_loop(0, tm, fill, 0)
        tt = lax.broadcasted_iota(jnp.int32, (GMLP_BLOCK, GMLP_BLOCK), 0) // CHUNK
        ss = lax.broadcasted_iota(jnp.int32, (GMLP_BLOCK, GMLP_BLOCK), 1) // CHUNK
        for h in range(A_HEADS):
            wsm_ref[h] = _bf(jnp.where(tt >= ss, ws_ref[h], 0.0))

    @pl.when(ti == 0)
    def _():
        s_ref[...] = jnp.zeros_like(s_ref)
        zc_ref[...] = jnp.zeros_like(zc_ref)

    m = mod_ref[...]
    sh1, sc1, gt1, sh2, sc2 = m[0:1], m[1:2], m[2:3], m[3:4], m[4:5]
    h_ref[...] = _bf((_rms(x_in[...]) * g1_ref[...]) * (1.0 + sc1) + sh1)

    lane256 = lax.broadcasted_iota(jnp.int32, (CHUNK, B_KDIM), 1)
    head_of_lane = lane256 // B_DK
    r64 = lax.broadcasted_iota(jnp.int32, (CHUNK, CHUNK), 0)
    c64 = lax.broadcasted_iota(jnp.int32, (CHUNK, CHUNK), 1)
    tri = _bf(jnp.where(r64 >= c64, 1.0, 0.0))
    causal4 = jnp.concatenate([r64 >= c64] * B_HEADS, axis=0)
    eye256 = (lax.broadcasted_iota(jnp.int32, (B_KDIM, B_KDIM), 0)
              == lax.broadcasted_iota(jnp.int32, (B_KDIM, B_KDIM), 1))
    a_head_of_lane = lax.broadcasted_iota(jnp.int32, (GMLP_BLOCK, A_DIM), 1) // (A_DIM // A_HEADS)
    row128 = lax.broadcasted_iota(jnp.int32, (GMLP_BLOCK, C_DIM), 0)
    sqrt_half = 0.7071067811865476

    def gelu(v):
        return 0.5 * v * (1.0 + lax.erf(v * sqrt_half))

    h2_prev = h2_ref.at[1 - slot]
    row_copies = []
    for i in range(tm):
        row_copies.append(
            lambda i=i: _tile_copy(h2_prev, i * ROW_GROUP, xs_ref, d0s_ref[i], sem).start(priority=0))
        row_copies.append(
            lambda i=i: _tile_copy(h2_prev, i * ROW_GROUP, xs_ref, d1s_ref[i], sem).start(priority=1))
        if fuse_prev:
            row_copies.append(lambda i=i: _tile_copy(
                ysp_ref, dp0n_ref[i], ybuf.at[1 - slot, 0], i * ROW_GROUP, sem_g.at[1 - slot]).start(priority=0))
            row_copies.append(lambda i=i: _tile_copy(
                ysp_ref, dp1n_ref[i], ybuf.at[1 - slot, 1], i * ROW_GROUP, sem_g.at[1 - slot]).start(priority=1))
    row_copies.reverse()
    copies_per_point = -(-len(row_copies) // (n_blk * COPY_POINTS_PER_BLOCK))

    def start_row_copies():
        for _ in range(min(copies_per_point, len(row_copies))):
            row_copies.pop()()

    p_halves = (pa_ref, pb_ref)
    y_blocks = (y0_ref, y1_ref, y2_ref, y3_ref)
    assert n_blk == len(y_blocks)

    def project(half):
        rows = slice(half * (tm // 2), (half + 1) * (tm // 2))
        hb = h_ref[rows, :]
        p_halves[half][:, 0:OFF_OG] = _dot(hb, wina_ref[...])
        p_halves[half][:, OFF_OG:OFF_GLR] = _dot(hb, winb_ref[...])
        p_halves[half][:, OFF_GLR:P_PAD] = _dot(hb, winc_ref[...])

    def block(j, state, zprev):
        p_ref, y_ref = p_halves[j // (n_blk // 2)], y_blocks[j]
        r0 = (j % (n_blk // 2)) * GMLP_BLOCK
        rows = slice(r0, r0 + GMLP_BLOCK)

        u = gelu(p_ref[rows, OFF_AU:OFF_AU + A_DIM])
        vv = gelu(p_ref[rows, OFF_AV:OFF_AV + A_DIM])
        mu = jnp.mean(vv, axis=-1, keepdims=True)
        var = jnp.mean((vv - mu) ** 2, axis=-1, keepdims=True)
        vn = _bf((vv - mu) * lax.rsqrt(var + EPS) * lng_ref[...] + lnb_ref[...])
        sv = jnp.zeros((GMLP_BLOCK, A_DIM), jnp.float32)
        for hh in range(A_HEADS):
            sv = jnp.where(a_head_of_lane == hh, _dot(wsm_ref[hh], vn), sv)
        y_ref[:, 0:A_DIM] = _bf(u * (sv + bsm_ref[...]))
        start_row_copies()

        z = _dot(_bf(p_ref[rows, OFF_GLR:OFF_GLR + LANES]), wg_ref[...]) + bg_ref[...]
        glog = (jnp.minimum(z, 0.0) - jnp.log1p(jnp.exp(-jnp.abs(z)))) / GLA_TAU
        def chunk_head(c):
            rc = slice(r0 + c * CHUNK, r0 + (c + 1) * CHUNK)
            g = glog[c * CHUNK:(c + 1) * CHUNK]
            g_hi = _bf(g)
            g_r1 = g - g_hi.astype(jnp.float32)
            g_mid = _bf(g_r1)
            g_lo = _bf(g_r1 - g_mid.astype(jnp.float32))
            b = _dot(tri, g_hi) + _dot(tri, g_mid) + _dot(tri, g_lo)
            b_mid = b[CHUNK // 2:CHUNK // 2 + 1]
            b_last = b[CHUNK - 1:CHUNK]
            q = p_ref[rc, OFF_Q:OFF_Q + B_KDIM] * (B_DK ** -0.5)
            k = p_ref[rc, OFF_K:OFF_K + B_KDIM]
            vb = _bf(p_ref[rc, OFF_V:OFF_V + B_VDIM])
            qs = q * jnp.exp(b - b_mid)
            ks = _bf(k * jnp.exp(b_mid - b))
            kd = _bf(k * jnp.exp(b_last - b))
            qb = q * jnp.exp(b)
            qs_st = _bf(jnp.concatenate(
                [jnp.where(head_of_lane == hh, qs, 0.0) for hh in range(B_HEADS)], axis=0))
            qb_st = _bf(jnp.concatenate(
                [jnp.where(head_of_lane == hh, qb, 0.0) for hh in range(B_HEADS)], axis=0))
            scores = _bf(jnp.where(causal4, _dot(qs_st, ks, _NT), 0.0))
            start_row_copies()
            kv_all = _dot(kd, vb, _TN)
            start_row_copies()
            o_intra, kvs = [], []
            for hh in range(B_HEADS):
                rs = slice(hh * CHUNK, (hh + 1) * CHUNK)
                cs = slice(hh * B_DV, (hh + 1) * B_DV)
                o_intra.append(_dot(scores[rs], vb[:, cs]))
                kvs.append(kv_all[rs, cs])
                start_row_copies()
            decay_row = jnp.broadcast_to(jnp.exp(b_last), (B_KDIM, B_KDIM))
            decay_col = jnp.sum(jnp.where(eye256, decay_row, 0.0), axis=1, keepdims=True)
            og = p_ref[rc, OFF_OG:OFF_OG + B_VDIM]
            return qb_st, o_intra, jnp.concatenate(kvs, axis=0), decay_col, og * jax.nn.sigmoid(og)

        heads = [chunk_head(c) for c in range(GMLP_BLOCK // CHUNK)]
        for c, (qb_st, o_intra, kv, decay_col, out_gate) in enumerate(heads):
            o_inter = _dot(qb_st, _bf(state))
            state = decay_col * state + kv
            outs = [_rms(o_intra[hh] + o_inter[hh * CHUNK:(hh + 1) * CHUNK]) for hh in range(B_HEADS)]
            on = jnp.concatenate(outs, axis=1) * glag_ref[...]
            y_ref[c * CHUNK:(c + 1) * CHUNK, A_DIM:A_DIM + B_VDIM] = _bf(on * out_gate)
            start_row_copies()

        zz = p_ref[rows, OFF_CC:OFF_CC + C_DIM] * p_ref[rows, OFF_CX:OFF_CX + C_DIM]
        z1 = jnp.where(row128 == 0, zprev[7:8], pltpu.roll(zz, 1, 0))
        z2 = jnp.where(row128 == 0, zprev[6:7], jnp.where(row128 == 1, zprev[7:8], pltpu.roll(zz, 2, 0)))
        cw = cw_ref[...]
        yc = cb_ref[...] + cw[0:1] * z2
        yc = yc + cw[1:2] * z1
        yc = yc + cw[2:3] * zz
        y_ref[:, A_DIM + B_VDIM:A_DIM + B_VDIM + C_DIM] = _bf(p_ref[rows, OFF_CB:OFF_CB + C_DIM] * yc)
        start_row_copies()
        return state, zz[GMLP_BLOCK - SUBLANES:GMLP_BLOCK]

    state, zprev, carry = s_ref[...], zc_ref[...], carry_ref[...]
    project(0)
    for j in range(n_blk):
        if j == 0:
            project(1)
        state, zprev = block(j, state, zprev)
        rows = slice(j * GMLP_BLOCK, (j + 1) * GMLP_BLOCK)
        xo = x_in[rows, :] + gt1 * _dot(y_blocks[j][...], wout_ref[...])
        o_ref[rows, :] = xo
        start_row_copies()
        h2 = (_rms(xo) * g2_ref[...]) * (1.0 + sc2) + sh2
        _store_rows_as_tiles(
            h2_ref.at[slot, pl.ds(j * GMLP_BLOCK * ROW_GROUP, GMLP_BLOCK * ROW_GROUP)], _pack_rows(h2))
        d0, d1, wcol, carry = _route(h2, rwt_ref, rb_ref, carry, cap)
        dvm_ref[0:1, rows] = d0
        dvm_ref[1:2, rows] = d1
        w_ref[rows, :] = wcol
        start_row_copies()
    while row_copies:
        row_copies.pop()()
    s_ref[...] = state
    zc_ref[...] = zprev
    carry_ref[...] = carry
    cnt = jnp.broadcast_to(carry, cnt_ref.shape).astype(jnp.int32)
    cnt_ref[...] = cnt
    dest_ref[...] = dvm_ref[0:TOP_K, :]

    drain_rows(h2_prev, xs_ref, sem, _tile_copy)

    to_smem = [pltpu.make_async_copy(dvm_ref.at[0], d0s_ref, sem_s),
               pltpu.make_async_copy(dvm_ref.at[1], d1s_ref, sem_s)]
    for cp in to_smem:
        cp.start()

    @pl.when(last_step)
    def _():
        for cp in to_smem:
            cp.wait()
        h2_last = h2_ref.at[slot]

        def issue(g, c):
            for jj in range(SUBLANES):
                i = pl.multiple_of(g * SUBLANES, SUBLANES) + jj
                _tile_copy(h2_last, i * ROW_GROUP, xs_ref, d0s_ref[i], sem).start()
                _tile_copy(h2_last, i * ROW_GROUP, xs_ref, d1s_ref[i], sem).start()
            return c

        lax.fori_loop(0, tm // SUBLANES, issue, 0)
        drain_rows(h2_last, xs_ref, sem, _tile_copy)
        if fuse_prev:
            drain_rows(ysp_ref, ybuf.at[0, 0], sem_g.at[1 - slot], _tile_copy)
        cvm_ref[...] = cnt
        zero_ref[...] = jnp.zeros_like(zero_ref)
        cp = pltpu.make_async_copy(cvm_ref, csm_ref, sem_s)
        cp.start()
        cp.wait()
        _zero_segment_padding(csm_ref, cap, zero_ref, xs_ref, sem_z)


def _mixer(x, mod, g1, g2, w_in, w_out, ln_g, ln_b, w_s, b_s, w_gate, b_gate, gla_g, conv_w, conv_b,
           router_w, router_b, prev=None):
    bn, t, d = x.shape
    n = bn * t
    tm = min(MIX_ROWS, t)
    nt = t // tm
    cap = n
    glr0 = 2 * A_DIM + 2 * B_KDIM + B_VDIM
    w_in_a = _bf(w_in[:, :glr0])
    w_in_b = _bf(w_in[:, glr0 + GLA_RANK:])
    w_in_c = jnp.concatenate(
        [_bf(w_in[:, glr0:glr0 + GLA_RANK]), jnp.zeros((d, LANES - GLA_RANK), jnp.bfloat16)], axis=1)
    w_gate_p = jnp.concatenate(
        [_bf(w_gate), jnp.zeros((LANES - GLA_RANK, B_KDIM), jnp.bfloat16)], axis=0)
    bsm = jnp.repeat(b_s.T, A_DIM // A_HEADS, axis=1)

    def whole(shape):
        return pl.BlockSpec(shape, lambda b, i: (0,) * len(shape))

    prev_specs, prev_args = [], ()
    if prev is not None:
        n_tiles = bn * nt

        def this_tile(b, i):
            return (b * nt + i,)

        def next_tile(b, i):
            return (jnp.minimum(b * nt + i + 1, n_tiles - 1),)

        d0p, d1p, wcolp, modp, ysp = prev
        prev_specs = [
            pl.BlockSpec((tm,), this_tile, memory_space=pltpu.SMEM),
            pl.BlockSpec((tm,), this_tile, memory_space=pltpu.SMEM),
            pl.BlockSpec((tm,), next_tile, memory_space=pltpu.SMEM),
            pl.BlockSpec((tm,), next_tile, memory_space=pltpu.SMEM),
            pl.BlockSpec((tm, 8), lambda b, i: (b * nt + i, 0)),
            pl.BlockSpec((None, 6, d), lambda b, i: (b, 0, 0)),
            pl.BlockSpec(memory_space=pl.ANY),
        ]
        prev_args = (d0p, d1p, d0p, d1p, wcolp, modp, ysp)

    return pl.pallas_call(
        functools.partial(_mixer_kernel, cap=cap, fuse_prev=prev is not None),
        grid=(bn, nt),
        in_specs=[
            pl.BlockSpec((None, tm, d), lambda b, i: (b, i, 0)),
            pl.BlockSpec((None, 6, d), lambda b, i: (b, 0, 0)),
            whole((1, d)),
            whole((d, OFF_OG)),
            whole((d, OFF_GLR - OFF_OG)),
            whole((d, LANES)),
            whole((d, d)),
            whole((1, A_DIM)),
            whole((1, A_DIM)),
            whole((A_HEADS, GMLP_BLOCK, GMLP_BLOCK)),
            whole((GMLP_BLOCK, A_DIM)),
            whole((LANES, B_KDIM)),
            whole((1, B_KDIM)),
            whole((1, B_VDIM)),
            whole((3, C_DIM)),
            whole((1, C_DIM)),
            whole((1, d)),
            whole((N_EXPERTS, d)),
            whole((N_EXPERTS, 1)),
        ] + prev_specs,
        out_specs=[
            pl.BlockSpec((None, tm, d), lambda b, i: (b, i, 0)),
            pl.BlockSpec((TOP_K, tm), lambda b, i: (0, b * nt + i)),
            pl.BlockSpec((tm, 8), lambda b, i: (b * nt + i, 0)),
            whole((N_EXPERTS, LANES)),
            pl.BlockSpec(memory_space=pl.ANY),
        ],
        out_shape=[
            jax.ShapeDtypeStruct(x.shape, x.dtype),
            jax.ShapeDtypeStruct((TOP_K, n), jnp.int32),
            jax.ShapeDtypeStruct((n, 8), jnp.float32),
            jax.ShapeDtypeStruct((N_EXPERTS, LANES), jnp.int32),
            jax.ShapeDtypeStruct(((N_EXPERTS * cap + TOP_K * tm) * ROW_GROUP, LANES), jnp.uint32),
        ],
        scratch_shapes=[
            pltpu.VMEM((tm, d), jnp.bfloat16),
            pltpu.VMEM((tm // 2, P_PAD), jnp.float32),
            pltpu.VMEM((tm // 2, P_PAD), jnp.float32),
            pltpu.VMEM((GMLP_BLOCK, d), jnp.bfloat16),
            pltpu.VMEM((GMLP_BLOCK, d), jnp.bfloat16),
            pltpu.VMEM((GMLP_BLOCK, d), jnp.bfloat16),
            pltpu.VMEM((GMLP_BLOCK, d), jnp.bfloat16),
            pltpu.VMEM((B_KDIM, B_DV), jnp.float32),
            pltpu.VMEM((SUBLANES, C_DIM), jnp.float32),
            pltpu.VMEM((A_HEADS, GMLP_BLOCK, GMLP_BLOCK), jnp.bfloat16),
            pltpu.VMEM((2, tm * ROW_GROUP, LANES), jnp.uint32),
            pltpu.VMEM((N_EXPERTS, 1), jnp.float32),
            pltpu.VMEM((SUBLANES, tm), jnp.int32),
            pltpu.SMEM((tm,), jnp.int32),
            pltpu.SMEM((tm,), jnp.int32),
            pltpu.VMEM((N_EXPERTS, LANES), jnp.int32),
            pltpu.SMEM((N_EXPERTS, LANES), jnp.int32),
            pltpu.VMEM((FFN_ROWS // 2 * ROW_GROUP, LANES), jnp.uint32),
            pltpu.SemaphoreType.DMA,
            pltpu.SemaphoreType.DMA,
            pltpu.SemaphoreType.DMA,
            pltpu.VMEM((2, TOP_K, tm * ROW_GROUP, LANES), jnp.uint32),
            pltpu.VMEM((tm, d), jnp.float32),
            pltpu.SemaphoreType.DMA((2,)),
        ],
        compiler_params=pltpu.CompilerParams(
            dimension_semantics=("arbitrary", "arbitrary"), vmem_limit_bytes=VMEM_LIMIT),
        name="mixer",
    )(x, mod, g1.reshape(1, d), w_in_a, w_in_b, w_in_c, _bf(w_out), ln_g.reshape(1, -1), ln_b.reshape(1, -1), w_s, bsm,
      w_gate_p, b_gate.reshape(1, -1), gla_g.reshape(1, -1), conv_w, conv_b.reshape(1, -1),
      g2.reshape(1, d), router_w.T, router_b.reshape(N_EXPERTS, 1), *prev_args)


def _ffn_kernel(be_ref, rb_ref, nb_ref, xs_ref, w1_ref, w3_ref, w2_ref, ys_ref, w1b_ref, w3b_ref, w2b_ref):
    del rb_ref
    i = pl.program_id(0)
    used = i < nb_ref[0]
    new_expert = jnp.logical_or(i == 0, be_ref[i] != be_ref[jnp.maximum(i - 1, 0)])

    @pl.when(jnp.logical_and(used, new_expert))
    def _():
        w1b_ref[...] = _bf(w1_ref[...])
        w3b_ref[...] = _bf(w3_ref[...])
        w2b_ref[...] = _bf(w2_ref[...])

    @pl.when(used)
    def _():
        x_a, x_b = _unpack_rows(_load_packed_rows(xs_ref, (), FFN_ROWS))
        half = x_a.shape[1]
        a = _dot(x_a, w1b_ref[0:half, :]) + _dot(x_b, w1b_ref[half:, :])
        g = _dot(x_a, w3b_ref[0:half, :]) + _dot(x_b, w3b_ref[half:, :])
        hm = _bf((a * jax.nn.sigmoid(a)) * g)
        _store_rows_as_tiles(ys_ref, _pack_rows(_dot(hm, w2b_ref[...])))


def _expert_ffn(layer, block_e, row_block, n_used, xs, w1, w3, w2):
    n_rows = xs.shape[0] // ROW_GROUP
    n_blocks = block_e.shape[0]
    d, f = w1.shape[-2], w1.shape[-1]
    return pl.pallas_call(
        _ffn_kernel,
        grid_spec=pltpu.PrefetchScalarGridSpec(
            num_scalar_prefetch=3,
            grid=(n_blocks,),
            in_specs=[
                pl.BlockSpec((FFN_ROWS * ROW_GROUP, LANES), lambda i, be, rb, nb: (rb[i], 0)),
                pl.BlockSpec((None, None, d, f), lambda i, be, rb, nb: (layer, be[i], 0, 0)),
                pl.BlockSpec((None, None, d, f), lambda i, be, rb, nb: (layer, be[i], 0, 0)),
                pl.BlockSpec((None, None, f, d), lambda i, be, rb, nb: (layer, be[i], 0, 0)),
            ],
            out_specs=pl.BlockSpec((FFN_ROWS * ROW_GROUP, LANES), lambda i, be, rb, nb: (rb[i], 0)),
            scratch_shapes=[pltpu.VMEM((d, f), jnp.bfloat16), pltpu.VMEM((d, f), jnp.bfloat16),
                            pltpu.VMEM((f, d), jnp.bfloat16)],
        ),
        out_shape=jax.ShapeDtypeStruct((n_rows * ROW_GROUP, LANES), jnp.uint32),
        compiler_params=pltpu.CompilerParams(
            dimension_semantics=("arbitrary",), vmem_limit_bytes=VMEM_LIMIT),
        name="expert_ffn",
    )(block_e, row_block, n_used, xs, w1, w3, w2)


def _combine_kernel(d0_ref, d1_ref, d0n_ref, d1n_ref, x_ref, mod_ref, w_ref, gf_ref, ys_ref, o_ref, ybuf, sems,
                    *, final_norm):
    tm = x_ref.shape[0]
    step = pl.program_id(0) * pl.num_programs(1) + pl.program_id(1)
    last = step == pl.num_programs(0) * pl.num_programs(1) - 1
    slot = step % 2

    def drain(sem):
        def body(g, c):
            for _ in range(TOP_K * SUBLANES):
                _tile_copy(ys_ref, 0, ybuf.at[0, 0], 0, sem).wait()
            return c
        lax.fori_loop(0, tm // SUBLANES, body, 0)

    @pl.when(step == 0)
    def _():
        def issue(g, c):
            for jj in range(SUBLANES):
                i = pl.multiple_of(g * SUBLANES, SUBLANES) + jj
                _tile_copy(ys_ref, d0_ref[i], ybuf.at[0, 0], i * ROW_GROUP, sems.at[0]).start()
                _tile_copy(ys_ref, d1_ref[i], ybuf.at[0, 1], i * ROW_GROUP, sems.at[0]).start()
            return c
        lax.fori_loop(0, tm // SUBLANES, issue, 0)

    nxt, sem_nxt = ybuf.at[1 - slot], sems.at[1 - slot]
    drain(sems.at[slot])
    gt2 = mod_ref[...][5:6]
    slab = GMLP_BLOCK
    for r0 in range(0, tm, slab):
        for i in range(r0, r0 + slab):
            _tile_copy(ys_ref, d0n_ref[i], nxt.at[0], i * ROW_GROUP, sem_nxt).start(priority=0)
            _tile_copy(ys_ref, d1n_ref[i], nxt.at[1], i * ROW_GROUP, sem_nxt).start(priority=1)
        rows = slice(r0, r0 + slab)
        span = pl.ds(r0 * ROW_GROUP, slab * ROW_GROUP)
        w = w_ref[rows, :]
        out = x_ref[rows, :] + gt2 * (w[:, 0:1] * _load_rows_from_tiles(ybuf.at[slot, 0, span], (), slab)
                                      + w[:, 1:2] * _load_rows_from_tiles(ybuf.at[slot, 1, span], (), slab))
        if final_norm:
            out = _rms(out) * gf_ref[...]
        o_ref[rows, :] = out

    @pl.when(last)
    def _():
        drain(sems.at[1 - slot])


def _combine(d0, d1, x, mod, wcol, g_final, ys, final_norm):
    bn, t, d = x.shape
    tm = min(MOVE_ROWS, t)
    nt = t // tm
    n_tiles = bn * nt

    def this_tile(b, i):
        return (b * nt + i,)

    def next_tile(b, i):
        return (jnp.minimum(b * nt + i + 1, n_tiles - 1),)

    return pl.pallas_call(
        functools.partial(_combine_kernel, final_norm=final_norm),
        grid=(bn, nt),
        in_specs=[
            pl.BlockSpec((tm,), this_tile, memory_space=pltpu.SMEM),
            pl.BlockSpec((tm,), this_tile, memory_space=pltpu.SMEM),
            pl.BlockSpec((tm,), next_tile, memory_space=pltpu.SMEM),
            pl.BlockSpec((tm,), next_tile, memory_space=pltpu.SMEM),
            pl.BlockSpec((None, tm, d), lambda b, i: (b, i, 0)),
            pl.BlockSpec((None, 6, d), lambda b, i: (b, 0, 0)),
            pl.BlockSpec((tm, 8), lambda b, i: (b * nt + i, 0)),
            pl.BlockSpec((1, d), lambda b, i: (0, 0)),
            pl.BlockSpec(memory_space=pl.ANY),
        ],
        out_specs=pl.BlockSpec((None, tm, d), lambda b, i: (b, i, 0)),
        out_shape=jax.ShapeDtypeStruct(x.shape, x.dtype),
        scratch_shapes=[pltpu.VMEM((2, TOP_K, tm * ROW_GROUP, LANES), jnp.uint32), pltpu.SemaphoreType.DMA((2,))],
        compiler_params=pltpu.CompilerParams(
            dimension_semantics=("arbitrary", "arbitrary"), vmem_limit_bytes=VMEM_LIMIT),
        name="combine",
    )(d0, d1, d0, d1, x, mod, wcol, g_final.reshape(1, d), ys)


def _block_map(counts, cap, n_blocks):
    blocks_per_expert = (counts + FFN_ROWS - 1) // FFN_ROWS
    ends = jnp.cumsum(blocks_per_expert)
    n_used = ends[-1:]
    step = jnp.minimum(jnp.arange(n_blocks, dtype=jnp.int32), n_used - 1)
    block_e = jnp.minimum(jnp.sum((ends[None, :] <= step[:, None]).astype(jnp.int32), axis=1), N_EXPERTS - 1)
    first = (ends - blocks_per_expert)[block_e]
    row_block = block_e * (cap // FFN_ROWS) + (step - first)
    return block_e.astype(jnp.int32), row_block.astype(jnp.int32), n_used.astype(jnp.int32)


def kernel(x, c, w_ada, b_ada, g_norm1, g_norm2, w_in, w_out, gmlp_ln_g, gmlp_ln_b, gmlp_ws, gmlp_bs,
           gla_w_gate, gla_b_gate, gla_norm_g, conv_w, conv_b, router_w, router_b, exp_w1, exp_w3, exp_w2,
           g_final):
    depth = w_ada.shape[0]
    bn, t, d = x.shape
    n = bn * t
    n_blocks = -(-(n * TOP_K) // FFN_ROWS) + N_EXPERTS
    mod_all = _modulation(c, w_ada, b_ada).reshape(depth, bn, 6, d)
    prev = None
    for l in range(depth):
        mod = mod_all[l]
        x, dest, wcol, cnt, xs = _mixer(
            x, mod, g_norm1[l], g_norm2[l], w_in[l], w_out[l], gmlp_ln_g[l], gmlp_ln_b[l], gmlp_ws[l],
            gmlp_bs[l], gla_w_gate[l], gla_b_gate[l], gla_norm_g[l], conv_w[l], conv_b[l], router_w, router_b,
            prev=prev)
        block_e, row_block, n_used = _block_map(cnt[:, 0], n, n_blocks)
        ys = _expert_ffn(l, block_e, row_block, n_used, xs, exp_w1, exp_w3, exp_w2)
        prev = (dest[0], dest[1], wcol, mod, ys)
    d0, d1, wcol, mod, ys = prev
    return _combine(d0, d1, x, mod, wcol, g_final, ys, final_norm=True)
```

```python
import functools

import jax
import jax.numpy as jnp
from jax import lax
from jax.experimental import pallas as pl
from jax.experimental.pallas import tpu as pltpu

CHUNK = 64
GMLP_BLOCK = 128
A_DIM = 256
A_HEADS = 4
B_HEADS = 4
B_DK = 64
B_DV = 128
B_KDIM = B_HEADS * B_DK
B_VDIM = B_HEADS * B_DV
GLA_RANK = 16
GLA_TAU = 16.0
C_DIM = 256
N_EXPERTS = 32
N_GROUPS = 4
EXPERTS_PER_GROUP = N_EXPERTS // N_GROUPS
TOP_K = 2
EPS = 1e-6

LANES = 128
SUBLANES = 8
OFF_AU, OFF_AV, OFF_Q, OFF_K, OFF_V = 0, 256, 512, 768, 1024
OFF_OG, OFF_CB, OFF_CC, OFF_CX, OFF_GLR = 1536, 2048, 2304, 2560, 2816
P_PAD = OFF_GLR + LANES

MIX_ROWS = 512
COPY_POINTS_PER_BLOCK = 18
MOVE_ROWS = 512
FFN_ROWS = 512
VMEM_LIMIT = 56 * 1024 * 1024

_NT = (((1,), (1,)), ((), ()))
_TN = (((0,), (0,)), ((), ()))


def _dot(a, b, dims=None):
    if dims is None:
        return jnp.dot(a, b, preferred_element_type=jnp.float32)
    return lax.dot_general(a, b, dims, preferred_element_type=jnp.float32)


def _bf(x):
    return x.astype(jnp.bfloat16)


def _split_bf16(x):
    hi = _bf(x)
    lo = _bf(x - hi.astype(jnp.float32))
    return hi, lo


def _rms(x):
    return x * lax.rsqrt(jnp.mean(x * x, axis=-1, keepdims=True) + EPS)


def _pack_rows(v):
    half = v.shape[1] // 2
    hi = pltpu.bitcast(_bf(v[:, :half]).astype(jnp.float32), jnp.uint32)
    lo = pltpu.bitcast(_bf(v[:, half:]).astype(jnp.float32), jnp.uint32)
    return hi | (lo >> 16)


def _unpack_rows(w):
    hi = pltpu.bitcast(w & jnp.uint32(0xFFFF0000), jnp.float32)
    lo = pltpu.bitcast(w << 16, jnp.float32)
    return _bf(hi), _bf(lo)


def _row_copy(src_ref, src_row, dst_ref, dst_row, sem):
    return pltpu.make_async_copy(src_ref.at[pl.ds(src_row, 1)], dst_ref.at[pl.ds(dst_row, 1)], sem)


ROW_GROUP = 4

def _aligned(row):
    return row if isinstance(row, int) else pl.multiple_of(row, ROW_GROUP)


def _tile_copy(src_ref, src_row, dst_ref, dst_row, sem):
    return pltpu.make_async_copy(src_ref.at[pl.ds(_aligned(src_row), ROW_GROUP)],
                                 dst_ref.at[pl.ds(_aligned(dst_row), ROW_GROUP)], sem)


def _store_rows_as_tiles(ref, value):
    rows = value.shape[0]
    assert value.shape[1] == ROW_GROUP * LANES
    for c in range(ROW_GROUP):
        ref[pl.ds(c, rows, stride=ROW_GROUP), :] = value[:, c * LANES:(c + 1) * LANES]


def _load_packed_rows(ref, idx, rows):
    return jnp.concatenate(
        [ref[idx + (pl.ds(c, rows, stride=ROW_GROUP), slice(None))] for c in range(ROW_GROUP)], axis=1)


def _load_rows_from_tiles(ref, idx, rows):
    hi, lo = _unpack_rows(_load_packed_rows(ref, idx, rows))
    return jnp.concatenate([hi.astype(jnp.float32), lo.astype(jnp.float32)], axis=1)


def _mod_kernel(c_ref, w_ref, b_ref, o_ref):
    c = c_ref[...]
    s = c * jax.nn.sigmoid(c)
    s_hi, s_lo = _split_bf16(s)
    w_hi, w_lo = _split_bf16(w_ref[...])
    acc = _dot(s_hi, w_hi) + _dot(s_hi, w_lo) + _dot(s_lo, w_hi)
    o_ref[...] = acc + b_ref[...]


def _modulation(c, w_ada, b_ada):
    depth, d, six_d = w_ada.shape
    bn = c.shape[0]
    cb = 2048
    return pl.pallas_call(
        _mod_kernel,
        grid=(depth, six_d // cb),
        in_specs=[
            pl.BlockSpec((bn, d), lambda l, j: (0, 0)),
            pl.BlockSpec((None, d, cb), lambda l, j: (l, 0, j)),
            pl.BlockSpec((None, 1, cb), lambda l, j: (l, 0, j)),
        ],
        out_specs=pl.BlockSpec((None, bn, cb), lambda l, j: (l, 0, j)),
        out_shape=jax.ShapeDtypeStruct((depth, bn, six_d), jnp.float32),
        compiler_params=pltpu.CompilerParams(
            dimension_semantics=("arbitrary", "arbitrary"), vmem_limit_bytes=VMEM_LIMIT),
        name="adaln_mod",
    )(c, w_ada, b_ada.reshape(depth, 1, six_d))


def _route(h2, rwt_ref, rb_ref, carry, cap):
    tr = h2.shape[0]
    h_hi, h_lo = _split_bf16(h2)
    w_hi, w_lo = _split_bf16(rwt_ref[...])
    logits = (_dot(w_hi, h_hi, _NT) + _dot(w_hi, h_lo, _NT) + _dot(w_lo, h_hi, _NT)) + rb_ref[...]
    ex = jnp.exp(logits - jnp.max(logits, axis=0, keepdims=True))
    probs = ex / jnp.sum(ex, axis=0, keepdims=True)

    idx8 = lax.broadcasted_iota(jnp.int32, (EXPERTS_PER_GROUP, tr), 0)
    best = None
    for g in range(N_GROUPS):
        pg = probs[g * EXPERTS_PER_GROUP:(g + 1) * EXPERTS_PER_GROUP]
        m1 = jnp.max(pg, axis=0, keepdims=True)
        i1 = jnp.min(jnp.where(pg == m1, idx8, EXPERTS_PER_GROUP), axis=0, keepdims=True)
        pg2 = jnp.where(idx8 == i1, -1.0, pg)
        m2 = jnp.max(pg2, axis=0, keepdims=True)
        i2 = jnp.min(jnp.where(pg2 == m2, idx8, EXPERTS_PER_GROUP), axis=0, keepdims=True)
        cand = (m1 + m2, m1, m2, i1 + g * EXPERTS_PER_GROUP, i2 + g * EXPERTS_PER_GROUP)
        if best is None:
            best = cand
        else:
            better = cand[0] > best[0]
            best = tuple(jnp.where(better, a, b) for a, b in zip(cand, best))
    _, p1, p2, e0, e1 = best
    denom = p1 + p2
    w0, w1 = p1 / denom, p2 / denom

    eidx = lax.broadcasted_iota(jnp.int32, (N_EXPERTS, tr), 0)
    hit0, hit1 = eidx == e0, eidx == e1
    onehot = jnp.where(jnp.logical_or(hit0, hit1), 1.0, 0.0)
    before = (lax.broadcasted_iota(jnp.int32, (tr, tr), 0) < lax.broadcasted_iota(jnp.int32, (tr, tr), 1))
    rank = _dot(_bf(onehot), _bf(jnp.where(before, 1.0, 0.0))) + carry
    r0 = jnp.sum(jnp.where(hit0, rank, 0.0), axis=0, keepdims=True).astype(jnp.int32)
    r1 = jnp.sum(jnp.where(hit1, rank, 0.0), axis=0, keepdims=True).astype(jnp.int32)
    carry = carry + jnp.sum(onehot, axis=1, keepdims=True)

    eye = lax.broadcasted_iota(jnp.int32, (tr, tr), 0) == lax.broadcasted_iota(jnp.int32, (tr, tr), 1)
    w0c = jnp.sum(jnp.where(eye, jnp.broadcast_to(w0, (tr, tr)), 0.0), axis=1, keepdims=True)
    w1c = jnp.sum(jnp.where(eye, jnp.broadcast_to(w1, (tr, tr)), 0.0), axis=1, keepdims=True)
    lane8 = lax.broadcasted_iota(jnp.int32, (tr, 8), 1)
    wcol = jnp.where(lane8 == 0, w0c, jnp.where(lane8 == 1, w1c, 0.0))
    return (e0 * cap + r0) * ROW_GROUP, (e1 * cap + r1) * ROW_GROUP, wcol, carry


def _zero_segment_padding(cnt_ref, cap, zero_ref, xs_ref, sem):
    copies = []
    for ex in range(N_EXPERTS):
        cnt = cnt_ref[ex, 0]
        seg = ex * cap
        pos = seg + cnt
        end8 = seg + ((cnt + SUBLANES - 1) // SUBLANES) * SUBLANES
        for j in range(SUBLANES - 1):
            copies.append((pos + j < end8, _tile_copy(zero_ref, 0, xs_ref, (pos + j) * ROW_GROUP, sem)))
        seg_end = seg + ((cnt + FFN_ROWS - 1) // FFN_ROWS) * FFN_ROWS
        groups = (seg_end - end8) // SUBLANES
        at = end8
        bit = FFN_ROWS // (2 * SUBLANES)
        while bit >= 1:
            rows = bit * SUBLANES
            pred = (groups & bit) != 0
            copies.append((pred, pltpu.make_async_copy(
                zero_ref.at[pl.ds(0, rows * ROW_GROUP)],
                xs_ref.at[pl.ds(pl.multiple_of(at * ROW_GROUP, SUBLANES), rows * ROW_GROUP)], sem)))
            at = at + jnp.where(pred, rows, 0)
            bit //= 2
    for pred, cp in copies:
        pl.when(pred)(cp.start)
    for pred, cp in copies:
        pl.when(pred)(cp.wait)


N_MIXER_INPUTS = 19
N_MIXER_OUTPUTS = 5


def _mixer_kernel(*refs, cap, fuse_prev):
    ins, rest = refs[:N_MIXER_INPUTS], refs[N_MIXER_INPUTS:]
    if fuse_prev:
        dp0_ref, dp1_ref, dp0n_ref, dp1n_ref, wp_ref, modp_ref, ysp_ref = rest[:7]
        rest = rest[7:]
    outs, scratch = rest[:N_MIXER_OUTPUTS], rest[N_MIXER_OUTPUTS:]
    (x_ref, mod_ref, g1_ref, wina_ref, winb_ref, winc_ref, wout_ref, lng_ref, lnb_ref, ws_ref, bsm_ref,
     wg_ref, bg_ref, glag_ref, cw_ref, cb_ref, g2_ref, rwt_ref, rb_ref) = ins
    o_ref, dest_ref, w_ref, cnt_ref, xs_ref = outs
    (h_ref, pa_ref, pb_ref, y0_ref, y1_ref, y2_ref, y3_ref, s_ref, zc_ref, wsm_ref, h2_ref,
     carry_ref, dvm_ref, d0s_ref, d1s_ref, cvm_ref, csm_ref, zero_ref, sem, sem_s, sem_z,
     ybuf, xin_ref, sem_g) = scratch

    bi, ti = pl.program_id(0), pl.program_id(1)
    tm = x_ref.shape[0]
    n_blk = tm // GMLP_BLOCK
    step = bi * pl.num_programs(1) + ti
    last_step = step == pl.num_programs(0) * pl.num_programs(1) - 1
    slot = step % 2
    spare_row0 = N_EXPERTS * cap

    def drain_rows(src_ref, dst_ref, s, copy=_row_copy):
        def body(g, c):
            for _ in range(TOP_K * SUBLANES):
                copy(src_ref, 0, dst_ref, 0, s).wait()
            return c
        lax.fori_loop(0, tm // SUBLANES, body, 0)

    @pl.when(step > 0)
    def _():
        pltpu.make_async_copy(dvm_ref.at[0], d0s_ref, sem_s).wait()
        pltpu.make_async_copy(dvm_ref.at[1], d1s_ref, sem_s).wait()

    if fuse_prev:
        @pl.when(step == 0)
        def _():
            def issue(g, c):
                for jj in range(SUBLANES):
                    i = pl.multiple_of(g * SUBLANES, SUBLANES) + jj
                    _tile_copy(ysp_ref, dp0_ref[i], ybuf.at[0, 0], i * ROW_GROUP, sem_g.at[0]).start()
                    _tile_copy(ysp_ref, dp1_ref[i], ybuf.at[0, 1], i * ROW_GROUP, sem_g.at[0]).start()
                return c
            lax.fori_loop(0, tm // SUBLANES, issue, 0)

        drain_rows(ysp_ref, ybuf.at[0, 0], sem_g.at[slot], _tile_copy)
        wp = wp_ref[...]
        xin_ref[...] = x_ref[...] + modp_ref[...][5:6] * (
            wp[:, 0:1] * _load_rows_from_tiles(ybuf, (slot, 0), tm)
            + wp[:, 1:2] * _load_rows_from_tiles(ybuf, (slot, 1), tm))
        x_in = xin_ref
    else:
        x_in = x_ref

    @pl.when(step == 0)
    def _():
        carry_ref[...] = jnp.zeros_like(carry_ref)
        h2_ref[...] = jnp.zeros_like(h2_ref)

        def fill(i, c):
            d0s_ref[i] = (spare_row0 + i) * ROW_GROUP
            d1s_ref[i] = (spare_row0 + tm + i) * ROW_GROUP
            return c

        lax.fori_loop(0, tm, fill, 0)
        tt = lax.broadcasted_iota(jnp.int32, (GMLP_BLOCK, GMLP_BLOCK), 0) // CHUNK
        ss = lax.broadcasted_iota(jnp.int32, (GMLP_BLOCK, GMLP_BLOCK), 1) // CHUNK
        for h in range(A_HEADS):
            wsm_ref[h] = _bf(jnp.where(tt >= ss, ws_ref[h], 0.0))

    @pl.when(ti == 0)
    def _():
        s_ref[...] = jnp.zeros_like(s_ref)
        zc_ref[...] = jnp.zeros_like(zc_ref)

    m = mod_ref[...]
    sh1, sc1, gt1, sh2, sc2 = m[0:1], m[1:2], m[2:3], m[3:4], m[4:5]
    h_ref[...] = _bf((_rms(x_in[...]) * g1_ref[...]) * (1.0 + sc1) + sh1)

    lane256 = lax.broadcasted_iota(jnp.int32, (CHUNK, B_KDIM), 1)
    head_of_lane = lane256 // B_DK
    r64 = lax.broadcasted_iota(jnp.int32, (CHUNK, CHUNK), 0)
    c64 = lax.broadcasted_iota(jnp.int32, (CHUNK, CHUNK), 1)
    tri = _bf(jnp.where(r64 >= c64, 1.0, 0.0))
    causal4 = jnp.concatenate([r64 >= c64] * B_HEADS, axis=0)
    eye256 = (lax.broadcasted_iota(jnp.int32, (B_KDIM, B_KDIM), 0)
              == lax.broadcasted_iota(jnp.int32, (B_KDIM, B_KDIM), 1))
    a_head_of_lane = lax.broadcasted_iota(jnp.int32, (GMLP_BLOCK, A_DIM), 1) // (A_DIM // A_HEADS)
    row128 = lax.broadcasted_iota(jnp.int32, (GMLP_BLOCK, C_DIM), 0)
    sqrt_half = 0.7071067811865476

    def gelu(v):
        return 0.5 * v * (1.0 + lax.erf(v * sqrt_half))

    h2_prev = h2_ref.at[1 - slot]
    row_copies = []
    for i in range(tm):
        row_copies.append(
            lambda i=i: _tile_copy(h2_prev, i * ROW_GROUP, xs_ref, d0s_ref[i], sem).start(priority=1))
        row_copies.append(
            lambda i=i: _tile_copy(h2_prev, i * ROW_GROUP, xs_ref, d1s_ref[i], sem).start(priority=1))
        if fuse_prev:
            row_copies.append(lambda i=i: _tile_copy(
                ysp_ref, dp0n_ref[i], ybuf.at[1 - slot, 0], i * ROW_GROUP, sem_g.at[1 - slot]).start(priority=0))
            row_copies.append(lambda i=i: _tile_copy(
                ysp_ref, dp1n_ref[i], ybuf.at[1 - slot, 1], i * ROW_GROUP, sem_g.at[1 - slot]).start(priority=0))
    row_copies.reverse()
    copies_per_point = -(-len(row_copies) // (n_blk * COPY_POINTS_PER_BLOCK))

    def start_row_copies():
        for _ in range(min(copies_per_point, len(row_copies))):
            row_copies.pop()()

    p_halves = (pa_ref, pb_ref)
    y_blocks = (y0_ref, y1_ref, y2_ref, y3_ref)
    assert n_blk == len(y_blocks)

    def project(half):
        rows = slice(half * (tm // 2), (half + 1) * (tm // 2))
        hb = h_ref[rows, :]
        p_halves[half][:, 0:OFF_OG] = _dot(hb, wina_ref[...])
        p_halves[half][:, OFF_OG:OFF_GLR] = _dot(hb, winb_ref[...])
        p_halves[half][:, OFF_GLR:P_PAD] = _dot(hb, winc_ref[...])

    def block(j, state, zprev):
        p_ref, y_ref = p_halves[j // (n_blk // 2)], y_blocks[j]
        r0 = (j % (n_blk // 2)) * GMLP_BLOCK
        rows = slice(r0, r0 + GMLP_BLOCK)

        u = gelu(p_ref[rows, OFF_AU:OFF_AU + A_DIM])
        vv = gelu(p_ref[rows, OFF_AV:OFF_AV + A_DIM])
        mu = jnp.mean(vv, axis=-1, keepdims=True)
        var = jnp.mean((vv - mu) ** 2, axis=-1, keepdims=True)
        vn = _bf((vv - mu) * lax.rsqrt(var + EPS) * lng_ref[...] + lnb_ref[...])
        sv = jnp.zeros((GMLP_BLOCK, A_DIM), jnp.float32)
        for hh in range(A_HEADS):
            sv = jnp.where(a_head_of_lane == hh, _dot(wsm_ref[hh], vn), sv)
        y_ref[:, 0:A_DIM] = _bf(u * (sv + bsm_ref[...]))
        start_row_copies()

        z = _dot(_bf(p_ref[rows, OFF_GLR:OFF_GLR + LANES]), wg_ref[...]) + bg_ref[...]
        glog = (jnp.minimum(z, 0.0) - jnp.log1p(jnp.exp(-jnp.abs(z)))) / GLA_TAU
        def chunk_head(c):
            rc = slice(r0 + c * CHUNK, r0 + (c + 1) * CHUNK)
            g = glog[c * CHUNK:(c + 1) * CHUNK]
            g_hi = _bf(g)
            g_r1 = g - g_hi.astype(jnp.float32)
            g_mid = _bf(g_r1)
            g_lo = _bf(g_r1 - g_mid.astype(jnp.float32))
            b = _dot(tri, g_hi) + _dot(tri, g_mid) + _dot(tri, g_lo)
            b_mid = b[CHUNK // 2:CHUNK // 2 + 1]
            b_last = b[CHUNK - 1:CHUNK]
            q = p_ref[rc, OFF_Q:OFF_Q + B_KDIM] * (B_DK ** -0.5)
            k = p_ref[rc, OFF_K:OFF_K + B_KDIM]
            vb = _bf(p_ref[rc, OFF_V:OFF_V + B_VDIM])
            qs = q * jnp.exp(b - b_mid)
            ks = _bf(k * jnp.exp(b_mid - b))
            kd = _bf(k * jnp.exp(b_last - b))
            qb = q * jnp.exp(b)
            qs_st = _bf(jnp.concatenate(
                [jnp.where(head_of_lane == hh, qs, 0.0) for hh in range(B_HEADS)], axis=0))
            qb_st = _bf(jnp.concatenate(
                [jnp.where(head_of_lane == hh, qb, 0.0) for hh in range(B_HEADS)], axis=0))
            scores = _bf(jnp.where(causal4, _dot(qs_st, ks, _NT), 0.0))
            start_row_copies()
            kv_all = _dot(kd, vb, _TN)
            start_row_copies()
            o_intra, kvs = [], []
            for hh in range(B_HEADS):
                rs = slice(hh * CHUNK, (hh + 1) * CHUNK)
                cs = slice(hh * B_DV, (hh + 1) * B_DV)
                o_intra.append(_dot(scores[rs], vb[:, cs]))
                kvs.append(kv_all[rs, cs])
                start_row_copies()
            decay_row = jnp.broadcast_to(jnp.exp(b_last), (B_KDIM, B_KDIM))
            decay_col = jnp.sum(jnp.where(eye256, decay_row, 0.0), axis=1, keepdims=True)
            og = p_ref[rc, OFF_OG:OFF_OG + B_VDIM]
            return qb_st, o_intra, jnp.concatenate(kvs, axis=0), decay_col, og * jax.nn.sigmoid(og)

        heads = [chunk_head(c) for c in range(GMLP_BLOCK // CHUNK)]
        for c, (qb_st, o_intra, kv, decay_col, out_gate) in enumerate(heads):
            o_inter = _dot(qb_st, _bf(state))
            state = decay_col * state + kv
            outs = [_rms(o_intra[hh] + o_inter[hh * CHUNK:(hh + 1) * CHUNK]) for hh in range(B_HEADS)]
            on = jnp.concatenate(outs, axis=1) * glag_ref[...]
            y_ref[c * CHUNK:(c + 1) * CHUNK, A_DIM:A_DIM + B_VDIM] = _bf(on * out_gate)
            start_row_copies()

        zz = p_ref[rows, OFF_CC:OFF_CC + C_DIM] * p_ref[rows, OFF_CX:OFF_CX + C_DIM]
        z1 = jnp.where(row128 == 0, zprev[7:8], pltpu.roll(zz, 1, 0))
        z2 = jnp.where(row128 == 0, zprev[6:7], jnp.where(row128 == 1, zprev[7:8], pltpu.roll(zz, 2, 0)))
        cw = cw_ref[...]
        yc = cb_ref[...] + cw[0:1] * z2
        yc = yc + cw[1:2] * z1
        yc = yc + cw[2:3] * zz
        y_ref[:, A_DIM + B_VDIM:A_DIM + B_VDIM + C_DIM] = _bf(p_ref[rows, OFF_CB:OFF_CB + C_DIM] * yc)
        start_row_copies()
        return state, zz[GMLP_BLOCK - SUBLANES:GMLP_BLOCK]

    state, zprev, carry = s_ref[...], zc_ref[...], carry_ref[...]
    project(0)
    for j in range(n_blk):
        if j == 0:
            project(1)
        state, zprev = block(j, state, zprev)
        rows = slice(j * GMLP_BLOCK, (j + 1) * GMLP_BLOCK)
        xo = x_in[rows, :] + gt1 * _dot(y_blocks[j][...], wout_ref[...])
        o_ref[rows, :] = xo
        start_row_copies()
        h2 = (_rms(xo) * g2_ref[...]) * (1.0 + sc2) + sh2
        _store_rows_as_tiles(
            h2_ref.at[slot, pl.ds(j * GMLP_BLOCK * ROW_GROUP, GMLP_BLOCK * ROW_GROUP)], _pack_rows(h2))
        d0, d1, wcol, carry = _route(h2, rwt_ref, rb_ref, carry, cap)
        dvm_ref[0:1, rows] = d0
        dvm_ref[1:2, rows] = d1
        w_ref[rows, :] = wcol
        start_row_copies()
    while row_copies:
        row_copies.pop()()
    s_ref[...] = state
    zc_ref[...] = zprev
    carry_ref[...] = carry
    cnt = jnp.broadcast_to(carry, cnt_ref.shape).astype(jnp.int32)
    cnt_ref[...] = cnt
    dest_ref[...] = dvm_ref[0:TOP_K, :]

    drain_rows(h2_prev, xs_ref, sem, _tile_copy)

    to_smem = [pltpu.make_async_copy(dvm_ref.at[0], d0s_ref, sem_s),
               pltpu.make_async_copy(dvm_ref.at[1], d1s_ref, sem_s)]
    for cp in to_smem:
        cp.start()

    @pl.when(last_step)
    def _():
        for cp in to_smem:
            cp.wait()
        h2_last = h2_ref.at[slot]

        def issue(g, c):
            for jj in range(SUBLANES):
                i = pl.multiple_of(g * SUBLANES, SUBLANES) + jj
                _tile_copy(h2_last, i * ROW_GROUP, xs_ref, d0s_ref[i], sem).start()
                _tile_copy(h2_last, i * ROW_GROUP, xs_ref, d1s_ref[i], sem).start()
            return c

        lax.fori_loop(0, tm // SUBLANES, issue, 0)
        drain_rows(h2_last, xs_ref, sem, _tile_copy)
        if fuse_prev:
            drain_rows(ysp_ref, ybuf.at[0, 0], sem_g.at[1 - slot], _tile_copy)
        cvm_ref[...] = cnt
        zero_ref[...] = jnp.zeros_like(zero_ref)
        cp = pltpu.make_async_copy(cvm_ref, csm_ref, sem_s)
        cp.start()
        cp.wait()
        _zero_segment_padding(csm_ref, cap, zero_ref, xs_ref, sem_z)


def _mixer(x, mod, g1, g2, w_in, w_out, ln_g, ln_b, w_s, b_s, w_gate, b_gate, gla_g, conv_w, conv_b,
           router_w, router_b, prev=None):
    bn, t, d = x.shape
    n = bn * t
    tm = min(MIX_ROWS, t)
    nt = t // tm
    cap = n
    glr0 = 2 * A_DIM + 2 * B_KDIM + B_VDIM
    w_in_a = _bf(w_in[:, :glr0])
    w_in_b = _bf(w_in[:, glr0 + GLA_RANK:])
    w_in_c = jnp.concatenate(
        [_bf(w_in[:, glr0:glr0 + GLA_RANK]), jnp.zeros((d, LANES - GLA_RANK), jnp.bfloat16)], axis=1)
    w_gate_p = jnp.concatenate(
        [_bf(w_gate), jnp.zeros((LANES - GLA_RANK, B_KDIM), jnp.bfloat16)], axis=0)
    bsm = jnp.repeat(b_s.T, A_DIM // A_HEADS, axis=1)

    def whole(shape):
        return pl.BlockSpec(shape, lambda b, i: (0,) * len(shape))

    prev_specs, prev_args = [], ()
    if prev is not None:
        n_tiles = bn * nt

        def this_tile(b, i):
            return (b * nt + i,)

        def next_tile(b, i):
            return (jnp.minimum(b * nt + i + 1, n_tiles - 1),)

        d0p, d1p, wcolp, modp, ysp = prev
        prev_specs = [
            pl.BlockSpec((tm,), this_tile, memory_space=pltpu.SMEM),
            pl.BlockSpec((tm,), this_tile, memory_space=pltpu.SMEM),
            pl.BlockSpec((tm,), next_tile, memory_space=pltpu.SMEM),
            pl.BlockSpec((tm,), next_tile, memory_space=pltpu.SMEM),
            pl.BlockSpec((tm, 8), lambda b, i: (b * nt + i, 0)),
            pl.BlockSpec((None, 6, d), lambda b, i: (b, 0, 0)),
            pl.BlockSpec(memory_space=pl.ANY),
        ]
        prev_args = (d0p, d1p, d0p, d1p, wcolp, modp, ysp)

    return pl.pallas_call(
        functools.partial(_mixer_kernel, cap=cap, fuse_prev=prev is not None),
        grid=(bn, nt),
        in_specs=[
            pl.BlockSpec((None, tm, d), lambda b, i: (b, i, 0)),
            pl.BlockSpec((None, 6, d), lambda b, i: (b, 0, 0)),
            whole((1, d)),
            whole((d, OFF_OG)),
            whole((d, OFF_GLR - OFF_OG)),
            whole((d, LANES)),
            whole((d, d)),
            whole((1, A_DIM)),
            whole((1, A_DIM)),
            whole((A_HEADS, GMLP_BLOCK, GMLP_BLOCK)),
            whole((GMLP_BLOCK, A_DIM)),
            whole((LANES, B_KDIM)),
            whole((1, B_KDIM)),
            whole((1, B_VDIM)),
            whole((3, C_DIM)),
            whole((1, C_DIM)),
            whole((1, d)),
            whole((N_EXPERTS, d)),
            whole((N_EXPERTS, 1)),
        ] + prev_specs,
        out_specs=[
            pl.BlockSpec((None, tm, d), lambda b, i: (b, i, 0)),
            pl.BlockSpec((TOP_K, tm), lambda b, i: (0, b * nt + i)),
            pl.BlockSpec((tm, 8), lambda b, i: (b * nt + i, 0)),
            whole((N_EXPERTS, LANES)),
            pl.BlockSpec(memory_space=pl.ANY),
        ],
        out_shape=[
            jax.ShapeDtypeStruct(x.shape, x.dtype),
            jax.ShapeDtypeStruct((TOP_K, n), jnp.int32),
            jax.ShapeDtypeStruct((n, 8), jnp.float32),
            jax.ShapeDtypeStruct((N_EXPERTS, LANES), jnp.int32),
            jax.ShapeDtypeStruct(((N_EXPERTS * cap + TOP_K * tm) * ROW_GROUP, LANES), jnp.uint32),
        ],
        scratch_shapes=[
            pltpu.VMEM((tm, d), jnp.bfloat16),
            pltpu.VMEM((tm // 2, P_PAD), jnp.float32),
            pltpu.VMEM((tm // 2, P_PAD), jnp.float32),
            pltpu.VMEM((GMLP_BLOCK, d), jnp.bfloat16),
            pltpu.VMEM((GMLP_BLOCK, d), jnp.bfloat16),
            pltpu.VMEM((GMLP_BLOCK, d), jnp.bfloat16),
            pltpu.VMEM((GMLP_BLOCK, d), jnp.bfloat16),
            pltpu.VMEM((B_KDIM, B_DV), jnp.float32),
            pltpu.VMEM((SUBLANES, C_DIM), jnp.float32),
            pltpu.VMEM((A_HEADS, GMLP_BLOCK, GMLP_BLOCK), jnp.bfloat16),
            pltpu.VMEM((2, tm * ROW_GROUP, LANES), jnp.uint32),
            pltpu.VMEM((N_EXPERTS, 1), jnp.float32),
            pltpu.VMEM((SUBLANES, tm), jnp.int32),
            pltpu.SMEM((tm,), jnp.int32),
            pltpu.SMEM((tm,), jnp.int32),
            pltpu.VMEM((N_EXPERTS, LANES), jnp.int32),
            pltpu.SMEM((N_EXPERTS, LANES), jnp.int32),
            pltpu.VMEM((FFN_ROWS // 2 * ROW_GROUP, LANES), jnp.uint32),
            pltpu.SemaphoreType.DMA,
            pltpu.SemaphoreType.DMA,
            pltpu.SemaphoreType.DMA,
            pltpu.VMEM((2, TOP_K, tm * ROW_GROUP, LANES), jnp.uint32),
            pltpu.VMEM((tm, d), jnp.float32),
            pltpu.SemaphoreType.DMA((2,)),
        ],
        compiler_params=pltpu.CompilerParams(
            dimension_semantics=("arbitrary", "arbitrary"), vmem_limit_bytes=VMEM_LIMIT),
        name="mixer",
    )(x, mod, g1.reshape(1, d), w_in_a, w_in_b, w_in_c, _bf(w_out), ln_g.reshape(1, -1), ln_b.reshape(1, -1), w_s, bsm,
      w_gate_p, b_gate.reshape(1, -1), gla_g.reshape(1, -1), conv_w, conv_b.reshape(1, -1),
      g2.reshape(1, d), router_w.T, router_b.reshape(N_EXPERTS, 1), *prev_args)


def _ffn_kernel(be_ref, rb_ref, nb_ref, xs_ref, w1_ref, w3_ref, w2_ref, ys_ref, w1b_ref, w3b_ref, w2b_ref):
    del rb_ref
    i = pl.program_id(0)
    used = i < nb_ref[0]
    new_expert = jnp.logical_or(i == 0, be_ref[i] != be_ref[jnp.maximum(i - 1, 0)])

    @pl.when(jnp.logical_and(used, new_expert))
    def _():
        w1b_ref[...] = _bf(w1_ref[...])
        w3b_ref[...] = _bf(w3_ref[...])
        w2b_ref[...] = _bf(w2_ref[...])

    @pl.when(used)
    def _():
        x_a, x_b = _unpack_rows(_load_packed_rows(xs_ref, (), FFN_ROWS))
        half = x_a.shape[1]
        a = _dot(x_a, w1b_ref[0:half, :]) + _dot(x_b, w1b_ref[half:, :])
        g = _dot(x_a, w3b_ref[0:half, :]) + _dot(x_b, w3b_ref[half:, :])
        hm = _bf((a * jax.nn.sigmoid(a)) * g)
        _store_rows_as_tiles(ys_ref, _pack_rows(_dot(hm, w2b_ref[...])))


def _expert_ffn(layer, block_e, row_block, n_used, xs, w1, w3, w2):
    n_rows = xs.shape[0] // ROW_GROUP
    n_blocks = block_e.shape[0]
    d, f = w1.shape[-2], w1.shape[-1]
    return pl.pallas_call(
        _ffn_kernel,
        grid_spec=pltpu.PrefetchScalarGridSpec(
            num_scalar_prefetch=3,
            grid=(n_blocks,),
            in_specs=[
                pl.BlockSpec((FFN_ROWS * ROW_GROUP, LANES), lambda i, be, rb, nb: (rb[i], 0)),
                pl.BlockSpec((None, None, d, f), lambda i, be, rb, nb: (layer, be[i], 0, 0)),
                pl.BlockSpec((None, None, d, f), lambda i, be, rb, nb: (layer, be[i], 0, 0)),
                pl.BlockSpec((None, None, f, d), lambda i, be, rb, nb: (layer, be[i], 0, 0)),
            ],
            out_specs=pl.BlockSpec((FFN_ROWS * ROW_GROUP, LANES), lambda i, be, rb, nb: (rb[i], 0)),
            scratch_shapes=[pltpu.VMEM((d, f), jnp.bfloat16), pltpu.VMEM((d, f), jnp.bfloat16),
                            pltpu.VMEM((f, d), jnp.bfloat16)],
        ),
        out_shape=jax.ShapeDtypeStruct((n_rows * ROW_GROUP, LANES), jnp.uint32),
        compiler_params=pltpu.CompilerParams(
            dimension_semantics=("arbitrary",), vmem_limit_bytes=VMEM_LIMIT),
        name="expert_ffn",
    )(block_e, row_block, n_used, xs, w1, w3, w2)


def _combine_kernel(d0_ref, d1_ref, d0n_ref, d1n_ref, x_ref, mod_ref, w_ref, gf_ref, ys_ref, o_ref, ybuf, sems,
                    *, final_norm):
    tm = x_ref.shape[0]
    step = pl.program_id(0) * pl.num_programs(1) + pl.program_id(1)
    last = step == pl.num_programs(0) * pl.num_programs(1) - 1
    slot = step % 2

    def drain(sem):
        def body(g, c):
            for _ in range(TOP_K * SUBLANES):
                _tile_copy(ys_ref, 0, ybuf.at[0, 0], 0, sem).wait()
            return c
        lax.fori_loop(0, tm // SUBLANES, body, 0)

    @pl.when(step == 0)
    def _():
        def issue(g, c):
            for jj in range(SUBLANES):
                i = pl.multiple_of(g * SUBLANES, SUBLANES) + jj
                _tile_copy(ys_ref, d0_ref[i], ybuf.at[0, 0], i * ROW_GROUP, sems.at[0]).start()
                _tile_copy(ys_ref, d1_ref[i], ybuf.at[0, 1], i * ROW_GROUP, sems.at[0]).start()
            return c
        lax.fori_loop(0, tm // SUBLANES, issue, 0)

    nxt, sem_nxt = ybuf.at[1 - slot], sems.at[1 - slot]
    drain(sems.at[slot])
    gt2 = mod_ref[...][5:6]
    slab = GMLP_BLOCK
    for r0 in range(0, tm, slab):
        for i in range(r0, r0 + slab):
            _tile_copy(ys_ref, d0n_ref[i], nxt.at[0], i * ROW_GROUP, sem_nxt).start(priority=0)
            _tile_copy(ys_ref, d1n_ref[i], nxt.at[1], i * ROW_GROUP, sem_nxt).start(priority=1)
        rows = slice(r0, r0 + slab)
        span = pl.ds(r0 * ROW_GROUP, slab * ROW_GROUP)
        w = w_ref[rows, :]
        out = x_ref[rows, :] + gt2 * (w[:, 0:1] * _load_rows_from_tiles(ybuf.at[slot, 0, span], (), slab)
                                      + w[:, 1:2] * _load_rows_from_tiles(ybuf.at[slot, 1, span], (), slab))
        if final_norm:
            out = _rms(out) * gf_ref[...]
        o_ref[rows, :] = out

    @pl.when(last)
    def _():
        drain(sems.at[1 - slot])


def _combine(d0, d1, x, mod, wcol, g_final, ys, final_norm):
    bn, t, d = x.shape
    tm = min(MOVE_ROWS, t)
    nt = t // tm
    n_tiles = bn * nt

    def this_tile(b, i):
        return (b * nt + i,)

    def next_tile(b, i):
        return (jnp.minimum(b * nt + i + 1, n_tiles - 1),)

    return pl.pallas_call(
        functools.partial(_combine_kernel, final_norm=final_norm),
        grid=(bn, nt),
        in_specs=[
            pl.BlockSpec((tm,), this_tile, memory_space=pltpu.SMEM),
            pl.BlockSpec((tm,), this_tile, memory_space=pltpu.SMEM),
            pl.BlockSpec((tm,), next_tile, memory_space=pltpu.SMEM),
            pl.BlockSpec((tm,), next_tile, memory_space=pltpu.SMEM),
            pl.BlockSpec((None, tm, d), lambda b, i: (b, i, 0)),
            pl.BlockSpec((None, 6, d), lambda b, i: (b, 0, 0)),
            pl.BlockSpec((tm, 8), lambda b, i: (b * nt + i, 0)),
            pl.BlockSpec((1, d), lambda b, i: (0, 0)),
            pl.BlockSpec(memory_space=pl.ANY),
        ],
        out_specs=pl.BlockSpec((None, tm, d), lambda b, i: (b, i, 0)),
        out_shape=jax.ShapeDtypeStruct(x.shape, x.dtype),
        scratch_shapes=[pltpu.VMEM((2, TOP_K, tm * ROW_GROUP, LANES), jnp.uint32), pltpu.SemaphoreType.DMA((2,))],
        compiler_params=pltpu.CompilerParams(
            dimension_semantics=("arbitrary", "arbitrary"), vmem_limit_bytes=VMEM_LIMIT),
        name="combine",
    )(d0, d1, d0, d1, x, mod, wcol, g_final.reshape(1, d), ys)


def _block_map(counts, cap, n_blocks):
    blocks_per_expert = (counts + FFN_ROWS - 1) // FFN_ROWS
    ends = jnp.cumsum(blocks_per_expert)
    n_used = ends[-1:]
    step = jnp.minimum(jnp.arange(n_blocks, dtype=jnp.int32), n_used - 1)
    block_e = jnp.minimum(jnp.sum((ends[None, :] <= step[:, None]).astype(jnp.int32), axis=1), N_EXPERTS - 1)
    first = (ends - blocks_per_expert)[block_e]
    row_block = block_e * (cap // FFN_ROWS) + (step - first)
    return block_e.astype(jnp.int32), row_block.astype(jnp.int32), n_used.astype(jnp.int32)


def kernel(x, c, w_ada, b_ada, g_norm1, g_norm2, w_in, w_out, gmlp_ln_g, gmlp_ln_b, gmlp_ws, gmlp_bs,
           gla_w_gate, gla_b_gate, gla_norm_g, conv_w, conv_b, router_w, router_b, exp_w1, exp_w3, exp_w2,
           g_final):
    depth = w_ada.shape[0]
    bn, t, d = x.shape
    n = bn * t
    n_blocks = -(-(n * TOP_K) // FFN_ROWS) + N_EXPERTS
    mod_all = _modulation(c, w_ada, b_ada).reshape(depth, bn, 6, d)
    prev = None
    for l in range(depth):
        mod = mod_all[l]
        x, dest, wcol, cnt, xs = _mixer(
            x, mod, g_norm1[l], g_norm2[l], w_in[l], w_out[l], gmlp_ln_g[l], gmlp_ln_b[l], gmlp_ws[l],
            gmlp_bs[l], gla_w_gate[l], gla_b_gate[l], gla_norm_g[l], conv_w[l], conv_b[l], router_w, router_b,
            prev=prev)
        block_e, row_block, n_used = _block_map(cnt[:, 0], n, n_blocks)
        ys = _expert_ffn(l, block_e, row_block, n_used, xs, exp_w1, exp_w3, exp_w2)
        prev = (dest[0], dest[1], wcol, mod, ys)
    d0, d1, wcol, mod, ys = prev
    return _combine(d0, d1, x, mod, wcol, g_final, ys, final_norm=True)
```

```python
import functools

import jax
import jax.numpy as jnp
from jax import lax
from jax.experimental import pallas as pl
from jax.experimental.pallas import tpu as pltpu

CHUNK = 64
GMLP_BLOCK = 128
A_DIM = 256
A_HEADS = 4
B_HEADS = 4
B_DK = 64
B_DV = 128
B_KDIM = B_HEADS * B_DK
B_VDIM = B_HEADS * B_DV
GLA_RANK = 16
GLA_TAU = 16.0
C_DIM = 256
N_EXPERTS = 32
N_GROUPS = 4
EXPERTS_PER_GROUP = N_EXPERTS // N_GROUPS
TOP_K = 2
EPS = 1e-6

LANES = 128
SUBLANES = 8
OFF_AU, OFF_AV, OFF_Q, OFF_K, OFF_V = 0, 256, 512, 768, 1024
OFF_OG, OFF_CB, OFF_CC, OFF_CX, OFF_GLR = 1536, 2048, 2304, 2560, 2816
P_PAD = OFF_GLR + LANES

MIX_ROWS = 512
COPY_POINTS_PER_BLOCK = 18
MOVE_ROWS = 512
FFN_ROWS = 512
VMEM_LIMIT = 56 * 1024 * 1024

_NT = (((1,), (1,)), ((), ()))
_TN = (((0,), (0,)), ((), ()))


def _dot(a, b, dims=None):
    if dims is None:
        return jnp.dot(a, b, preferred_element_type=jnp.float32)
    return lax.dot_general(a, b, dims, preferred_element_type=jnp.float32)


def _bf(x):
    return x.astype(jnp.bfloat16)


def _split_bf16(x):
    hi = _bf(x)
    lo = _bf(x - hi.astype(jnp.float32))
    return hi, lo


def _rms(x):
    return x * lax.rsqrt(jnp.mean(x * x, axis=-1, keepdims=True) + EPS)


def _pack_rows(v):
    half = v.shape[1] // 2
    hi = pltpu.bitcast(_bf(v[:, :half]).astype(jnp.float32), jnp.uint32)
    lo = pltpu.bitcast(_bf(v[:, half:]).astype(jnp.float32), jnp.uint32)
    return hi | (lo >> 16)


def _unpack_rows(w):
    hi = pltpu.bitcast(w & jnp.uint32(0xFFFF0000), jnp.float32)
    lo = pltpu.bitcast(w << 16, jnp.float32)
    return _bf(hi), _bf(lo)


def _row_copy(src_ref, src_row, dst_ref, dst_row, sem):
    return pltpu.make_async_copy(src_ref.at[pl.ds(src_row, 1)], dst_ref.at[pl.ds(dst_row, 1)], sem)


ROW_GROUP = 4

def _aligned(row):
    return row if isinstance(row, int) else pl.multiple_of(row, ROW_GROUP)


def _tile_copy(src_ref, src_row, dst_ref, dst_row, sem):
    return pltpu.make_async_copy(src_ref.at[pl.ds(_aligned(src_row), ROW_GROUP)],
                                 dst_ref.at[pl.ds(_aligned(dst_row), ROW_GROUP)], sem)


def _store_rows_as_tiles(ref, value):
    rows = value.shape[0]
    assert value.shape[1] == ROW_GROUP * LANES
    for c in range(ROW_GROUP):
        ref[pl.ds(c, rows, stride=ROW_GROUP), :] = value[:, c * LANES:(c + 1) * LANES]


def _load_packed_rows(ref, idx, rows):
    return jnp.concatenate(
        [ref[idx + (pl.ds(c, rows, stride=ROW_GROUP), slice(None))] for c in range(ROW_GROUP)], axis=1)


def _load_rows_from_tiles(ref, idx, rows):
    hi, lo = _unpack_rows(_load_packed_rows(ref, idx, rows))
    return jnp.concatenate([hi.astype(jnp.float32), lo.astype(jnp.float32)], axis=1)


def _mod_kernel(c_ref, w_ref, b_ref, o_ref):
    c = c_ref[...]
    s = c * jax.nn.sigmoid(c)
    s_hi, s_lo = _split_bf16(s)
    w_hi, w_lo = _split_bf16(w_ref[...])
    acc = _dot(s_hi, w_hi) + _dot(s_hi, w_lo) + _dot(s_lo, w_hi)
    o_ref[...] = acc + b_ref[...]


def _modulation(c, w_ada, b_ada):
    depth, d, six_d = w_ada.shape
    bn = c.shape[0]
    cb = 2048
    return pl.pallas_call(
        _mod_kernel,
        grid=(depth, six_d // cb),
        in_specs=[
            pl.BlockSpec((bn, d), lambda l, j: (0, 0)),
            pl.BlockSpec((None, d, cb), lambda l, j: (l, 0, j)),
            pl.BlockSpec((None, 1, cb), lambda l, j: (l, 0, j)),
        ],
        out_specs=pl.BlockSpec((None, bn, cb), lambda l, j: (l, 0, j)),
        out_shape=jax.ShapeDtypeStruct((depth, bn, six_d), jnp.float32),
        compiler_params=pltpu.CompilerParams(
            dimension_semantics=("arbitrary", "arbitrary"), vmem_limit_bytes=VMEM_LIMIT),
        name="adaln_mod",
    )(c, w_ada, b_ada.reshape(depth, 1, six_d))


def _route(h2, rwt_ref, rb_ref, carry, cap):
    tr = h2.shape[0]
    h_hi, h_lo = _split_bf16(h2)
    w_hi, w_lo = _split_bf16(rwt_ref[...])
    logits = (_dot(w_hi, h_hi, _NT) + _dot(w_hi, h_lo, _NT) + _dot(w_lo, h_hi, _NT)) + rb_ref[...]
    ex = jnp.exp(logits - jnp.max(logits, axis=0, keepdims=True))
    probs = ex / jnp.sum(ex, axis=0, keepdims=True)

    idx8 = lax.broadcasted_iota(jnp.int32, (EXPERTS_PER_GROUP, tr), 0)
    best = None
    for g in range(N_GROUPS):
        pg = probs[g * EXPERTS_PER_GROUP:(g + 1) * EXPERTS_PER_GROUP]
        m1 = jnp.max(pg, axis=0, keepdims=True)
        i1 = jnp.min(jnp.where(pg == m1, idx8, EXPERTS_PER_GROUP), axis=0, keepdims=True)
        pg2 = jnp.where(idx8 == i1, -1.0, pg)
        m2 = jnp.max(pg2, axis=0, keepdims=True)
        i2 = jnp.min(jnp.where(pg2 == m2, idx8, EXPERTS_PER_GROUP), axis=0, keepdims=True)
        cand = (m1 + m2, m1, m2, i1 + g * EXPERTS_PER_GROUP, i2 + g * EXPERTS_PER_GROUP)
        if best is None:
            best = cand
        else:
            better = cand[0] > best[0]
            best = tuple(jnp.where(better, a, b) for a, b in zip(cand, best))
    _, p1, p2, e0, e1 = best
    denom = p1 + p2
    w0, w1 = p1 / denom, p2 / denom

    eidx = lax.broadcasted_iota(jnp.int32, (N_EXPERTS, tr), 0)
    hit0, hit1 = eidx == e0, eidx == e1
    onehot = jnp.where(jnp.logical_or(hit0, hit1), 1.0, 0.0)
    before = (lax.broadcasted_iota(jnp.int32, (tr, tr), 0) < lax.broadcasted_iota(jnp.int32, (tr, tr), 1))
    rank = _dot(_bf(onehot), _bf(jnp.where(before, 1.0, 0.0))) + carry
    r0 = jnp.sum(jnp.where(hit0, rank, 0.0), axis=0, keepdims=True).astype(jnp.int32)
    r1 = jnp.sum(jnp.where(hit1, rank, 0.0), axis=0, keepdims=True).astype(jnp.int32)
    carry = carry + jnp.sum(onehot, axis=1, keepdims=True)

    eye = lax.broadcasted_iota(jnp.int32, (tr, tr), 0) == lax.broadcasted_iota(jnp.int32, (tr, tr), 1)
    w0c = jnp.sum(jnp.where(eye, jnp.broadcast_to(w0, (tr, tr)), 0.0), axis=1, keepdims=True)
    w1c = jnp.sum(jnp.where(eye, jnp.broadcast_to(w1, (tr, tr)), 0.0), axis=1, keepdims=True)
    lane8 = lax.broadcasted_iota(jnp.int32, (tr, 8), 1)
    wcol = jnp.where(lane8 == 0, w0c, jnp.where(lane8 == 1, w1c, 0.0))
    return (e0 * cap + r0) * ROW_GROUP, (e1 * cap + r1) * ROW_GROUP, wcol, carry


def _zero_segment_padding(cnt_ref, cap, zero_ref, xs_ref, sem):
    copies = []
    for ex in range(N_EXPERTS):
        cnt = cnt_ref[ex, 0]
        seg = ex * cap
        pos = seg + cnt
        end8 = seg + ((cnt + SUBLANES - 1) // SUBLANES) * SUBLANES
        for j in range(SUBLANES - 1):
            copies.append((pos + j < end8, _tile_copy(zero_ref, 0, xs_ref, (pos + j) * ROW_GROUP, sem)))
        seg_end = seg + ((cnt + FFN_ROWS - 1) // FFN_ROWS) * FFN_ROWS
        groups = (seg_end - end8) // SUBLANES
        at = end8
        bit = FFN_ROWS // (2 * SUBLANES)
        while bit >= 1:
            rows = bit * SUBLANES
            pred = (groups & bit) != 0
            copies.append((pred, pltpu.make_async_copy(
                zero_ref.at[pl.ds(0, rows * ROW_GROUP)],
                xs_ref.at[pl.ds(pl.multiple_of(at * ROW_GROUP, SUBLANES), rows * ROW_GROUP)], sem)))
            at = at + jnp.where(pred, rows, 0)
            bit //= 2
    for pred, cp in copies:
        pl.when(pred)(cp.start)
    for pred, cp in copies:
        pl.when(pred)(cp.wait)


N_MIXER_INPUTS = 19
N_MIXER_OUTPUTS = 5


def _mixer_kernel(*refs, cap, fuse_prev):
    ins, rest = refs[:N_MIXER_INPUTS], refs[N_MIXER_INPUTS:]
    if fuse_prev:
        dp0_ref, dp1_ref, dp0n_ref, dp1n_ref, wp_ref, modp_ref, ysp_ref = rest[:7]
        rest = rest[7:]
    outs, scratch = rest[:N_MIXER_OUTPUTS], rest[N_MIXER_OUTPUTS:]
    (x_ref, mod_ref, g1_ref, wina_ref, winb_ref, winc_ref, wout_ref, lng_ref, lnb_ref, ws_ref, bsm_ref,
     wg_ref, bg_ref, glag_ref, cw_ref, cb_ref, g2_ref, rwt_ref, rb_ref) = ins
    o_ref, dest_ref, w_ref, cnt_ref, xs_ref = outs
    (h_ref, pa_ref, pb_ref, y0_ref, y1_ref, y2_ref, y3_ref, s_ref, zc_ref, wsm_ref, h2_ref,
     carry_ref, dvm_ref, d0s_ref, d1s_ref, cvm_ref, csm_ref, zero_ref, sem, sem_s, sem_z,
     ybuf, xin_ref, sem_g) = scratch

    bi, ti = pl.program_id(0), pl.program_id(1)
    tm = x_ref.shape[0]
    n_blk = tm // GMLP_BLOCK
    step = bi * pl.num_programs(1) + ti
    last_step = step == pl.num_programs(0) * pl.num_programs(1) - 1
    slot = step % 2
    spare_row0 = N_EXPERTS * cap

    def drain_rows(src_ref, dst_ref, s, copy=_row_copy):
        def body(g, c):
            for _ in range(TOP_K * SUBLANES):
                copy(src_ref, 0, dst_ref, 0, s).wait()
            return c
        lax.fori_loop(0, tm // SUBLANES, body, 0)

    @pl.when(step > 0)
    def _():
        pltpu.make_async_copy(dvm_ref.at[0], d0s_ref, sem_s).wait()
        pltpu.make_async_copy(dvm_ref.at[1], d1s_ref, sem_s).wait()

    if fuse_prev:
        @pl.when(step == 0)
        def _():
            def issue(g, c):
                for jj in range(SUBLANES):
                    i = pl.multiple_of(g * SUBLANES, SUBLANES) + jj
                    _tile_copy(ysp_ref, dp0_ref[i], ybuf.at[0, 0], i * ROW_GROUP, sem_g.at[0]).start()
                    _tile_copy(ysp_ref, dp1_ref[i], ybuf.at[0, 1], i * ROW_GROUP, sem_g.at[0]).start()
                return c
            lax.fori_loop(0, tm // SUBLANES, issue, 0)

        drain_rows(ysp_ref, ybuf.at[0, 0], sem_g.at[slot], _tile_copy)
        wp = wp_ref[...]
        xin_ref[...] = x_ref[...] + modp_ref[...][5:6] * (
            wp[:, 0:1] * _load_rows_from_tiles(ybuf, (slot, 0), tm)
            + wp[:, 1:2] * _load_rows_from_tiles(ybuf, (slot, 1), tm))
        x_in = xin_ref
    else:
        x_in = x_ref

    @pl.when(step == 0)
    def _():
        carry_ref[...] = jnp.zeros_like(carry_ref)
        h2_ref[...] = jnp.zeros_like(h2_ref)

        def fill(i, c):
            d0s_ref[i] = (spare_row0 + i) * ROW_GROUP
            d1s_ref[i] = (spare_row0 + tm + i) * ROW_GROUP
            return c

        lax.fori_loop(0, tm, fill, 0)
        tt = lax.broadcasted_iota(jnp.int32, (GMLP_BLOCK, GMLP_BLOCK), 0) // CHUNK
        ss = lax.broadcasted_iota(jnp.int32, (GMLP_BLOCK, GMLP_BLOCK), 1) // CHUNK
        for h in range(A_HEADS):
            wsm_ref[h] = _bf(jnp.where(tt >= ss, ws_ref[h], 0.0))

    @pl.when(ti == 0)
    def _():
        s_ref[...] = jnp.zeros_like(s_ref)
        zc_ref[...] = jnp.zeros_like(zc_ref)

    m = mod_ref[...]
    sh1, sc1, gt1, sh2, sc2 = m[0:1], m[1:2], m[2:3], m[3:4], m[4:5]
    h_ref[...] = _bf((_rms(x_in[...]) * g1_ref[...]) * (1.0 + sc1) + sh1)

    lane256 = lax.broadcasted_iota(jnp.int32, (CHUNK, B_KDIM), 1)
    head_of_lane = lane256 // B_DK
    r64 = lax.broadcasted_iota(jnp.int32, (CHUNK, CHUNK), 0)
    c64 = lax.broadcasted_iota(jnp.int32, (CHUNK, CHUNK), 1)
    tri = _bf(jnp.where(r64 >= c64, 1.0, 0.0))
    causal4 = jnp.concatenate([r64 >= c64] * B_HEADS, axis=0)
    eye256 = (lax.broadcasted_iota(jnp.int32, (B_KDIM, B_KDIM), 0)
              == lax.broadcasted_iota(jnp.int32, (B_KDIM, B_KDIM), 1))
    a_head_of_lane = lax.broadcasted_iota(jnp.int32, (GMLP_BLOCK, A_DIM), 1) // (A_DIM // A_HEADS)
    row128 = lax.broadcasted_iota(jnp.int32, (GMLP_BLOCK, C_DIM), 0)
    sqrt_half = 0.7071067811865476

    def gelu(v):
        return 0.5 * v * (1.0 + lax.erf(v * sqrt_half))

    h2_prev = h2_ref.at[1 - slot]
    row_copies = []
    for i in range(tm):
        row_copies.append(
            lambda i=i: _tile_copy(h2_prev, i * ROW_GROUP, xs_ref, d0s_ref[i], sem).start(priority=0))
        row_copies.append(
            lambda i=i: _tile_copy(h2_prev, i * ROW_GROUP, xs_ref, d1s_ref[i], sem).start(priority=1))
        if fuse_prev:
            row_copies.append(lambda i=i: _tile_copy(
                ysp_ref, dp0n_ref[i], ybuf.at[1 - slot, 0], i * ROW_GROUP, sem_g.at[1 - slot]).start(priority=1))
            row_copies.append(lambda i=i: _tile_copy(
                ysp_ref, dp1n_ref[i], ybuf.at[1 - slot, 1], i * ROW_GROUP, sem_g.at[1 - slot]).start(priority=0))
    row_copies.reverse()
    copies_per_point = -(-len(row_copies) // (n_blk * COPY_POINTS_PER_BLOCK))

    def start_row_copies():
        for _ in range(min(copies_per_point, len(row_copies))):
            row_copies.pop()()

    p_halves = (pa_ref, pb_ref)
    y_blocks = (y0_ref, y1_ref, y2_ref, y3_ref)
    assert n_blk == len(y_blocks)

    def project(half):
        rows = slice(half * (tm // 2), (half + 1) * (tm // 2))
        hb = h_ref[rows, :]
        p_halves[half][:, 0:OFF_OG] = _dot(hb, wina_ref[...])
        p_halves[half][:, OFF_OG:OFF_GLR] = _dot(hb, winb_ref[...])
        p_halves[half][:, OFF_GLR:P_PAD] = _dot(hb, winc_ref[...])

    def block(j, state, zprev):
        p_ref, y_ref = p_halves[j // (n_blk // 2)], y_blocks[j]
        r0 = (j % (n_blk // 2)) * GMLP_BLOCK
        rows = slice(r0, r0 + GMLP_BLOCK)

        u = gelu(p_ref[rows, OFF_AU:OFF_AU + A_DIM])
        vv = gelu(p_ref[rows, OFF_AV:OFF_AV + A_DIM])
        mu = jnp.mean(vv, axis=-1, keepdims=True)
        var = jnp.mean((vv - mu) ** 2, axis=-1, keepdims=True)
        vn = _bf((vv - mu) * lax.rsqrt(var + EPS) * lng_ref[...] + lnb_ref[...])
        sv = jnp.zeros((GMLP_BLOCK, A_DIM), jnp.float32)
        for hh in range(A_HEADS):
            sv = jnp.where(a_head_of_lane == hh, _dot(wsm_ref[hh], vn), sv)
        y_ref[:, 0:A_DIM] = _bf(u * (sv + bsm_ref[...]))
        start_row_copies()

        z = _dot(_bf(p_ref[rows, OFF_GLR:OFF_GLR + LANES]), wg_ref[...]) + bg_ref[...]
        glog = (jnp.minimum(z, 0.0) - jnp.log1p(jnp.exp(-jnp.abs(z)))) / GLA_TAU
        def chunk_head(c):
            rc = slice(r0 + c * CHUNK, r0 + (c + 1) * CHUNK)
            g = glog[c * CHUNK:(c + 1) * CHUNK]
            g_hi = _bf(g)
            g_r1 = g - g_hi.astype(jnp.float32)
            g_mid = _bf(g_r1)
            g_lo = _bf(g_r1 - g_mid.astype(jnp.float32))
            b = _dot(tri, g_hi) + _dot(tri, g_mid) + _dot(tri, g_lo)
            b_mid = b[CHUNK // 2:CHUNK // 2 + 1]
            b_last = b[CHUNK - 1:CHUNK]
            q = p_ref[rc, OFF_Q:OFF_Q + B_KDIM] * (B_DK ** -0.5)
            k = p_ref[rc, OFF_K:OFF_K + B_KDIM]
            vb = _bf(p_ref[rc, OFF_V:OFF_V + B_VDIM])
            qs = q * jnp.exp(b - b_mid)
            ks = _bf(k * jnp.exp(b_mid - b))
            kd = _bf(k * jnp.exp(b_last - b))
            qb = q * jnp.exp(b)
            qs_st = _bf(jnp.concatenate(
                [jnp.where(head_of_lane == hh, qs, 0.0) for hh in range(B_HEADS)], axis=0))
            qb_st = _bf(jnp.concatenate(
                [jnp.where(head_of_lane == hh, qb, 0.0) for hh in range(B_HEADS)], axis=0))
            scores = _bf(jnp.where(causal4, _dot(qs_st, ks, _NT), 0.0))
            start_row_copies()
            kv_all = _dot(kd, vb, _TN)
            start_row_copies()
            o_intra, kvs = [], []
            for hh in range(B_HEADS):
                rs = slice(hh * CHUNK, (hh + 1) * CHUNK)
                cs = slice(hh * B_DV, (hh + 1) * B_DV)
                o_intra.append(_dot(scores[rs], vb[:, cs]))
                kvs.append(kv_all[rs, cs])
                start_row_copies()
            decay_row = jnp.broadcast_to(jnp.exp(b_last), (B_KDIM, B_KDIM))
            decay_col = jnp.sum(jnp.where(eye256, decay_row, 0.0), axis=1, keepdims=True)
            og = p_ref[rc, OFF_OG:OFF_OG + B_VDIM]
            return qb_st, o_intra, jnp.concatenate(kvs, axis=0), decay_col, og * jax.nn.sigmoid(og)

        heads = [chunk_head(c) for c in range(GMLP_BLOCK // CHUNK)]
        for c, (qb_st, o_intra, kv, decay_col, out_gate) in enumerate(heads):
            o_inter = _dot(qb_st, _bf(state))
            state = decay_col * state + kv
            outs = [_rms(o_intra[hh] + o_inter[hh * CHUNK:(hh + 1) * CHUNK]) for hh in range(B_HEADS)]
            on = jnp.concatenate(outs, axis=1) * glag_ref[...]
            y_ref[c * CHUNK:(c + 1) * CHUNK, A_DIM:A_DIM + B_VDIM] = _bf(on * out_gate)
            start_row_copies()

        zz = p_ref[rows, OFF_CC:OFF_CC + C_DIM] * p_ref[rows, OFF_CX:OFF_CX + C_DIM]
        z1 = jnp.where(row128 == 0, zprev[7:8], pltpu.roll(zz, 1, 0))
        z2 = jnp.where(row128 == 0, zprev[6:7], jnp.where(row128 == 1, zprev[7:8], pltpu.roll(zz, 2, 0)))
        cw = cw_ref[...]
        yc = cb_ref[...] + cw[0:1] * z2
        yc = yc + cw[1:2] * z1
        yc = yc + cw[2:3] * zz
        y_ref[:, A_DIM + B_VDIM:A_DIM + B_VDIM + C_DIM] = _bf(p_ref[rows, OFF_CB:OFF_CB + C_DIM] * yc)
        start_row_copies()
        return state, zz[GMLP_BLOCK - SUBLANES:GMLP_BLOCK]

    state, zprev, carry = s_ref[...], zc_ref[...], carry_ref[...]
    project(0)
    for j in range(n_blk):
        if j == 0:
            project(1)
        state, zprev = block(j, state, zprev)
        rows = slice(j * GMLP_BLOCK, (j + 1) * GMLP_BLOCK)
        xo = x_in[rows, :] + gt1 * _dot(y_blocks[j][...], wout_ref[...])
        o_ref[rows, :] = xo
        start_row_copies()
        h2 = (_rms(xo) * g2_ref[...]) * (1.0 + sc2) + sh2
        _store_rows_as_tiles(
            h2_ref.at[slot, pl.ds(j * GMLP_BLOCK * ROW_GROUP, GMLP_BLOCK * ROW_GROUP)], _pack_rows(h2))
        d0, d1, wcol, carry = _route(h2, rwt_ref, rb_ref, carry, cap)
        dvm_ref[0:1, rows] = d0
        dvm_ref[1:2, rows] = d1
        w_ref[rows, :] = wcol
        start_row_copies()
    while row_copies:
        row_copies.pop()()
    s_ref[...] = state
    zc_ref[...] = zprev
    carry_ref[...] = carry
    cnt = jnp.broadcast_to(carry, cnt_ref.shape).astype(jnp.int32)
    cnt_ref[...] = cnt
    dest_ref[...] = dvm_ref[0:TOP_K, :]

    drain_rows(h2_prev, xs_ref, sem, _tile_copy)

    to_smem = [pltpu.make_async_copy(dvm_ref.at[0], d0s_ref, sem_s),
               pltpu.make_async_copy(dvm_ref.at[1], d1s_ref, sem_s)]
    for cp in to_smem:
        cp.start()

    @pl.when(last_step)
    def _():
        for cp in to_smem:
            cp.wait()
        h2_last = h2_ref.at[slot]

        def issue(g, c):
            for jj in range(SUBLANES):
                i = pl.multiple_of(g * SUBLANES, SUBLANES) + jj
                _tile_copy(h2_last, i * ROW_GROUP, xs_ref, d0s_ref[i], sem).start()
                _tile_copy(h2_last, i * ROW_GROUP, xs_ref, d1s_ref[i], sem).start()
            return c

        lax.fori_loop(0, tm // SUBLANES, issue, 0)
        drain_rows(h2_last, xs_ref, sem, _tile_copy)
        if fuse_prev:
            drain_rows(ysp_ref, ybuf.at[0, 0], sem_g.at[1 - slot], _tile_copy)
        cvm_ref[...] = cnt
        zero_ref[...] = jnp.zeros_like(zero_ref)
        cp = pltpu.make_async_copy(cvm_ref, csm_ref, sem_s)
        cp.start()
        cp.wait()
        _zero_segment_padding(csm_ref, cap, zero_ref, xs_ref, sem_z)


def _mixer(x, mod, g1, g2, w_in, w_out, ln_g, ln_b, w_s, b_s, w_gate, b_gate, gla_g, conv_w, conv_b,
           router_w, router_b, prev=None):
    bn, t, d = x.shape
    n = bn * t
    tm = min(MIX_ROWS, t)
    nt = t // tm
    cap = n
    glr0 = 2 * A_DIM + 2 * B_KDIM + B_VDIM
    w_in_a = _bf(w_in[:, :glr0])
    w_in_b = _bf(w_in[:, glr0 + GLA_RANK:])
    w_in_c = jnp.concatenate(
        [_bf(w_in[:, glr0:glr0 + GLA_RANK]), jnp.zeros((d, LANES - GLA_RANK), jnp.bfloat16)], axis=1)
    w_gate_p = jnp.concatenate(
        [_bf(w_gate), jnp.zeros((LANES - GLA_RANK, B_KDIM), jnp.bfloat16)], axis=0)
    bsm = jnp.repeat(b_s.T, A_DIM // A_HEADS, axis=1)

    def whole(shape):
        return pl.BlockSpec(shape, lambda b, i: (0,) * len(shape))

    prev_specs, prev_args = [], ()
    if prev is not None:
        n_tiles = bn * nt

        def this_tile(b, i):
            return (b * nt + i,)

        def next_tile(b, i):
            return (jnp.minimum(b * nt + i + 1, n_tiles - 1),)

        d0p, d1p, wcolp, modp, ysp = prev
        prev_specs = [
            pl.BlockSpec((tm,), this_tile, memory_space=pltpu.SMEM),
            pl.BlockSpec((tm,), this_tile, memory_space=pltpu.SMEM),
            pl.BlockSpec((tm,), next_tile, memory_space=pltpu.SMEM),
            pl.BlockSpec((tm,), next_tile, memory_space=pltpu.SMEM),
            pl.BlockSpec((tm, 8), lambda b, i: (b * nt + i, 0)),
            pl.BlockSpec((None, 6, d), lambda b, i: (b, 0, 0)),
            pl.BlockSpec(memory_space=pl.ANY),
        ]
        prev_args = (d0p, d1p, d0p, d1p, wcolp, modp, ysp)

    return pl.pallas_call(
        functools.partial(_mixer_kernel, cap=cap, fuse_prev=prev is not None),
        grid=(bn, nt),
        in_specs=[
            pl.BlockSpec((None, tm, d), lambda b, i: (b, i, 0)),
            pl.BlockSpec((None, 6, d), lambda b, i: (b, 0, 0)),
            whole((1, d)),
            whole((d, OFF_OG)),
            whole((d, OFF_GLR - OFF_OG)),
            whole((d, LANES)),
            whole((d, d)),
            whole((1, A_DIM)),
            whole((1, A_DIM)),
            whole((A_HEADS, GMLP_BLOCK, GMLP_BLOCK)),
            whole((GMLP_BLOCK, A_DIM)),
            whole((LANES, B_KDIM)),
            whole((1, B_KDIM)),
            whole((1, B_VDIM)),
            whole((3, C_DIM)),
            whole((1, C_DIM)),
            whole((1, d)),
            whole((N_EXPERTS, d)),
            whole((N_EXPERTS, 1)),
        ] + prev_specs,
        out_specs=[
            pl.BlockSpec((None, tm, d), lambda b, i: (b, i, 0)),
            pl.BlockSpec((TOP_K, tm), lambda b, i: (0, b * nt + i)),
            pl.BlockSpec((tm, 8), lambda b, i: (b * nt + i, 0)),
            whole((N_EXPERTS, LANES)),
            pl.BlockSpec(memory_space=pl.ANY),
        ],
        out_shape=[
            jax.ShapeDtypeStruct(x.shape, x.dtype),
            jax.ShapeDtypeStruct((TOP_K, n), jnp.int32),
            jax.ShapeDtypeStruct((n, 8), jnp.float32),
            jax.ShapeDtypeStruct((N_EXPERTS, LANES), jnp.int32),
            jax.ShapeDtypeStruct(((N_EXPERTS * cap + TOP_K * tm) * ROW_GROUP, LANES), jnp.uint32),
        ],
        scratch_shapes=[
            pltpu.VMEM((tm, d), jnp.bfloat16),
            pltpu.VMEM((tm // 2, P_PAD), jnp.float32),
            pltpu.VMEM((tm // 2, P_PAD), jnp.float32),
            pltpu.VMEM((GMLP_BLOCK, d), jnp.bfloat16),
            pltpu.VMEM((GMLP_BLOCK, d), jnp.bfloat16),
            pltpu.VMEM((GMLP_BLOCK, d), jnp.bfloat16),
            pltpu.VMEM((GMLP_BLOCK, d), jnp.bfloat16),
            pltpu.VMEM((B_KDIM, B_DV), jnp.float32),
            pltpu.VMEM((SUBLANES, C_DIM), jnp.float32),
            pltpu.VMEM((A_HEADS, GMLP_BLOCK, GMLP_BLOCK), jnp.bfloat16),
            pltpu.VMEM((2, tm * ROW_GROUP, LANES), jnp.uint32),
            pltpu.VMEM((N_EXPERTS, 1), jnp.float32),
            pltpu.VMEM((SUBLANES, tm), jnp.int32),
            pltpu.SMEM((tm,), jnp.int32),
            pltpu.SMEM((tm,), jnp.int32),
            pltpu.VMEM((N_EXPERTS, LANES), jnp.int32),
            pltpu.SMEM((N_EXPERTS, LANES), jnp.int32),
            pltpu.VMEM((FFN_ROWS // 2 * ROW_GROUP, LANES), jnp.uint32),
            pltpu.SemaphoreType.DMA,
            pltpu.SemaphoreType.DMA,
            pltpu.SemaphoreType.DMA,
            pltpu.VMEM((2, TOP_K, tm * ROW_GROUP, LANES), jnp.uint32),
            pltpu.VMEM((tm, d), jnp.float32),
            pltpu.SemaphoreType.DMA((2,)),
        ],
        compiler_params=pltpu.CompilerParams(
            dimension_semantics=("arbitrary", "arbitrary"), vmem_limit_bytes=VMEM_LIMIT),
        name="mixer",
    )(x, mod, g1.reshape(1, d), w_in_a, w_in_b, w_in_c, _bf(w_out), ln_g.reshape(1, -1), ln_b.reshape(1, -1), w_s, bsm,
      w_gate_p, b_gate.reshape(1, -1), gla_g.reshape(1, -1), conv_w, conv_b.reshape(1, -1),
      g2.reshape(1, d), router_w.T, router_b.reshape(N_EXPERTS, 1), *prev_args)


def _ffn_kernel(be_ref, rb_ref, nb_ref, xs_ref, w1_ref, w3_ref, w2_ref, ys_ref, w1b_ref, w3b_ref, w2b_ref):
    del rb_ref
    i = pl.program_id(0)
    used = i < nb_ref[0]
    new_expert = jnp.logical_or(i == 0, be_ref[i] != be_ref[jnp.maximum(i - 1, 0)])

    @pl.when(jnp.logical_and(used, new_expert))
    def _():
        w1b_ref[...] = _bf(w1_ref[...])
        w3b_ref[...] = _bf(w3_ref[...])
        w2b_ref[...] = _bf(w2_ref[...])

    @pl.when(used)
    def _():
        x_a, x_b = _unpack_rows(_load_packed_rows(xs_ref, (), FFN_ROWS))
        half = x_a.shape[1]
        a = _dot(x_a, w1b_ref[0:half, :]) + _dot(x_b, w1b_ref[half:, :])
        g = _dot(x_a, w3b_ref[0:half, :]) + _dot(x_b, w3b_ref[half:, :])
        hm = _bf((a * jax.nn.sigmoid(a)) * g)
        _store_rows_as_tiles(ys_ref, _pack_rows(_dot(hm, w2b_ref[...])))


def _expert_ffn(layer, block_e, row_block, n_used, xs, w1, w3, w2):
    n_rows = xs.shape[0] // ROW_GROUP
    n_blocks = block_e.shape[0]
    d, f = w1.shape[-2], w1.shape[-1]
    return pl.pallas_call(
        _ffn_kernel,
        grid_spec=pltpu.PrefetchScalarGridSpec(
            num_scalar_prefetch=3,
            grid=(n_blocks,),
            in_specs=[
                pl.BlockSpec((FFN_ROWS * ROW_GROUP, LANES), lambda i, be, rb, nb: (rb[i], 0)),
                pl.BlockSpec((None, None, d, f), lambda i, be, rb, nb: (layer, be[i], 0, 0)),
                pl.BlockSpec((None, None, d, f), lambda i, be, rb, nb: (layer, be[i], 0, 0)),
                pl.BlockSpec((None, None, f, d), lambda i, be, rb, nb: (layer, be[i], 0, 0)),
            ],
            out_specs=pl.BlockSpec((FFN_ROWS * ROW_GROUP, LANES), lambda i, be, rb, nb: (rb[i], 0)),
            scratch_shapes=[pltpu.VMEM((d, f), jnp.bfloat16), pltpu.VMEM((d, f), jnp.bfloat16),
                            pltpu.VMEM((f, d), jnp.bfloat16)],
        ),
        out_shape=jax.ShapeDtypeStruct((n_rows * ROW_GROUP, LANES), jnp.uint32),
        compiler_params=pltpu.CompilerParams(
            dimension_semantics=("arbitrary",), vmem_limit_bytes=VMEM_LIMIT),
        name="expert_ffn",
    )(block_e, row_block, n_used, xs, w1, w3, w2)


def _combine_kernel(d0_ref, d1_ref, d0n_ref, d1n_ref, x_ref, mod_ref, w_ref, gf_ref, ys_ref, o_ref, ybuf, sems,
                    *, final_norm):
    tm = x_ref.shape[0]
    step = pl.program_id(0) * pl.num_programs(1) + pl.program_id(1)
    last = step == pl.num_programs(0) * pl.num_programs(1) - 1
    slot = step % 2

    def drain(sem):
        def body(g, c):
            for _ in range(TOP_K * SUBLANES):
                _tile_copy(ys_ref, 0, ybuf.at[0, 0], 0, sem).wait()
            return c
        lax.fori_loop(0, tm // SUBLANES, body, 0)

    @pl.when(step == 0)
    def _():
        def issue(g, c):
            for jj in range(SUBLANES):
                i = pl.multiple_of(g * SUBLANES, SUBLANES) + jj
                _tile_copy(ys_ref, d0_ref[i], ybuf.at[0, 0], i * ROW_GROUP, sems.at[0]).start()
                _tile_copy(ys_ref, d1_ref[i], ybuf.at[0, 1], i * ROW_GROUP, sems.at[0]).start()
            return c
        lax.fori_loop(0, tm // SUBLANES, issue, 0)

    nxt, sem_nxt = ybuf.at[1 - slot], sems.at[1 - slot]
    drain(sems.at[slot])
    gt2 = mod_ref[...][5:6]
    slab = GMLP_BLOCK
    for r0 in range(0, tm, slab):
        for i in range(r0, r0 + slab):
            _tile_copy(ys_ref, d0n_ref[i], nxt.at[0], i * ROW_GROUP, sem_nxt).start(priority=0)
            _tile_copy(ys_ref, d1n_ref[i], nxt.at[1], i * ROW_GROUP, sem_nxt).start(priority=1)
        rows = slice(r0, r0 + slab)
        span = pl.ds(r0 * ROW_GROUP, slab * ROW_GROUP)
        w = w_ref[rows, :]
        out = x_ref[rows, :] + gt2 * (w[:, 0:1] * _load_rows_from_tiles(ybuf.at[slot, 0, span], (), slab)
                                      + w[:, 1:2] * _load_rows_from_tiles(ybuf.at[slot, 1, span], (), slab))
        if final_norm:
            out = _rms(out) * gf_ref[...]
        o_ref[rows, :] = out

    @pl.when(last)
    def _():
        drain(sems.at[1 - slot])


def _combine(d0, d1, x, mod, wcol, g_final, ys, final_norm):
    bn, t, d = x.shape
    tm = min(MOVE_ROWS, t)
    nt = t // tm
    n_tiles = bn * nt

    def this_tile(b, i):
        return (b * nt + i,)

    def next_tile(b, i):
        return (jnp.minimum(b * nt + i + 1, n_tiles - 1),)

    return pl.pallas_call(
        functools.partial(_combine_kernel, final_norm=final_norm),
        grid=(bn, nt),
        in_specs=[
            pl.BlockSpec((tm,), this_tile, memory_space=pltpu.SMEM),
            pl.BlockSpec((tm,), this_tile, memory_space=pltpu.SMEM),
            pl.BlockSpec((tm,), next_tile, memory_space=pltpu.SMEM),
            pl.BlockSpec((tm,), next_tile, memory_space=pltpu.SMEM),
            pl.BlockSpec((None, tm, d), lambda b, i: (b, i, 0)),
            pl.BlockSpec((None, 6, d), lambda b, i: (b, 0, 0)),
            pl.BlockSpec((tm, 8), lambda b, i: (b * nt + i, 0)),
            pl.BlockSpec((1, d), lambda b, i: (0, 0)),
            pl.BlockSpec(memory_space=pl.ANY),
        ],
        out_specs=pl.BlockSpec((None, tm, d), lambda b, i: (b, i, 0)),
        out_shape=jax.ShapeDtypeStruct(x.shape, x.dtype),
        scratch_shapes=[pltpu.VMEM((2, TOP_K, tm * ROW_GROUP, LANES), jnp.uint32), pltpu.SemaphoreType.DMA((2,))],
        compiler_params=pltpu.CompilerParams(
            dimension_semantics=("arbitrary", "arbitrary"), vmem_limit_bytes=VMEM_LIMIT),
        name="combine",
    )(d0, d1, d0, d1, x, mod, wcol, g_final.reshape(1, d), ys)


def _block_map(counts, cap, n_blocks):
    blocks_per_expert = (counts + FFN_ROWS - 1) // FFN_ROWS
    ends = jnp.cumsum(blocks_per_expert)
    n_used = ends[-1:]
    step = jnp.minimum(jnp.arange(n_blocks, dtype=jnp.int32), n_used - 1)
    block_e = jnp.minimum(jnp.sum((ends[None, :] <= step[:, None]).astype(jnp.int32), axis=1), N_EXPERTS - 1)
    first = (ends - blocks_per_expert)[block_e]
    row_block = block_e * (cap // FFN_ROWS) + (step - first)
    return block_e.astype(jnp.int32), row_block.astype(jnp.int32), n_used.astype(jnp.int32)


def kernel(x, c, w_ada, b_ada, g_norm1, g_norm2, w_in, w_out, gmlp_ln_g, gmlp_ln_b, gmlp_ws, gmlp_bs,
           gla_w_gate, gla_b_gate, gla_norm_g, conv_w, conv_b, router_w, router_b, exp_w1, exp_w3, exp_w2,
           g_final):
    depth = w_ada.shape[0]
    bn, t, d = x.shape
    n = bn * t
    n_blocks = -(-(n * TOP_K) // FFN_ROWS) + N_EXPERTS
    mod_all = _modulation(c, w_ada, b_ada).reshape(depth, bn, 6, d)
    prev = None
    for l in range(depth):
        mod = mod_all[l]
        x, dest, wcol, cnt, xs = _mixer(
            x, mod, g_norm1[l], g_norm2[l], w_in[l], w_out[l], gmlp_ln_g[l], gmlp_ln_b[l], gmlp_ws[l],
            gmlp_bs[l], gla_w_gate[l], gla_b_gate[l], gla_norm_g[l], conv_w[l], conv_b[l], router_w, router_b,
            prev=prev)
        block_e, row_block, n_used = _block_map(cnt[:, 0], n, n_blocks)
        ys = _expert_ffn(l, block_e, row_block, n_used, xs, exp_w1, exp_w3, exp_w2)
        prev = (dest[0], dest[1], wcol, mod, ys)
    d0, d1, wcol, mod, ys = prev
    return _combine(d0, d1, x, mod, wcol, g_final, ys, final_norm=True)
```
